```python
import jax, jax.numpy as jnp
from jax import lax
import numpy as np

D_MODEL = 2048
BATCH = 8
SEQ = 2048
DEPTH = 2
DEC_BATCH = 128
DEC_SEQ = 8
PAST_LEN = 8192
PAGE_SIZE = 128

D_A = 512
CONV_A = 31
HEAD_DIM = 64
N_HEADS = 16
N_KV = 4
GQA = N_HEADS // N_KV
WINDOW = 128
ROPE_THETA = 10000.0
D_C = 512
CONV_C = 3
D_D = 512
CHUNK = 128
N_GROUPS_D = 4
GROUP_D = D_D // N_GROUPS_D
N_BRANCH = 4
Q_WIDTH = N_HEADS * HEAD_DIM
KV_WIDTH = N_KV * HEAD_DIM
IN_SIZES = (D_A, D_A, Q_WIDTH, KV_WIDTH, KV_WIDTH, D_C, D_C, D_C, D_D, D_D, N_BRANCH * D_MODEL)
N_IN = 2 * D_A + Q_WIDTH + 2 * KV_WIDTH + 3 * D_C + 2 * D_D + N_BRANCH * D_MODEL
D_FF = 5632
N_EXPERTS = 8
TOP_K = 2
N_DENSE = (DEPTH + 1) // 2
N_MOE = DEPTH // 2
ALPHA = (2 * DEPTH) ** 0.25
BETA = (8 * DEPTH) ** -0.25
LN_EPS = 1e-5

kernel_name = 'hybrid_gated_conv_swa_chunkmlp_decoder_step'


def layer_norm(x, g, b):
    xf = x.astype(jnp.float32)
    mu = jnp.mean(xf, axis=-1, keepdims=True)
    var = jnp.mean(jnp.square(xf - mu), axis=-1, keepdims=True)
    y = (xf - mu) * lax.rsqrt(var + LN_EPS) * g.astype(jnp.float32) + b.astype(jnp.float32)
    return y.astype(x.dtype)


def rope(x, pos):
    half = HEAD_DIM // 2
    inv_freq = jnp.power(ROPE_THETA, -jnp.arange(half, dtype=jnp.float32) * (2.0 / HEAD_DIM))
    ang = pos.astype(jnp.float32)[:, None] * inv_freq[None, :]
    cos = jnp.cos(ang)[:, None, :]
    sin = jnp.sin(ang)[:, None, :]
    xf = x.astype(jnp.float32)
    x1, x2 = xf[..., :half], xf[..., half:]
    return jnp.concatenate([x1 * cos - x2 * sin, x2 * cos + x1 * sin], axis=-1).astype(x.dtype)


def causal_depthwise_conv(xh, w):
    return lax.conv_general_dilated(
        xh, w[:, None, :].astype(xh.dtype), window_strides=(1,), padding='VALID',
        dimension_numbers=('NWC', 'WIO', 'NWC'), feature_group_count=xh.shape[-1])


def split_columns(h):
    parts = []
    off = 0
    for n in IN_SIZES:
        parts.append(h[..., off:off + n])
        off += n
    return parts


def sink_softmax(s, sinks_g):
    sk = sinks_g.astype(jnp.float32)[:, :, None, None]
    m = jnp.maximum(jnp.max(s, axis=-1, keepdims=True), sk)
    e = jnp.exp(s - m)
    return e / (jnp.sum(e, axis=-1, keepdims=True) + jnp.exp(sk - m))


def swa_banded(q, k, v, sinks_g):
    B, T = q.shape[0], q.shape[1]
    nb = T // WINDOW
    qb = q.reshape(B, nb, WINDOW, N_KV, GQA, HEAD_DIM)
    pad = ((0, 0), (WINDOW, 0), (0, 0), (0, 0))
    kp = jnp.pad(k, pad).reshape(B, nb + 1, WINDOW, N_KV, HEAD_DIM)
    vp = jnp.pad(v, pad).reshape(B, nb + 1, WINDOW, N_KV, HEAD_DIM)
    kb = jnp.concatenate([kp[:, :-1], kp[:, 1:]], axis=2)
    vb = jnp.concatenate([vp[:, :-1], vp[:, 1:]], axis=2)
    s = jnp.einsum('bnqhgd,bnkhd->bnhgqk', qb, kb, preferred_element_type=jnp.float32) * (HEAD_DIM ** -0.5)
    blk = jnp.arange(nb)[:, None] * WINDOW
    qpos = blk + jnp.arange(WINDOW)[None, :]
    kpos = blk - WINDOW + jnp.arange(2 * WINDOW)[None, :]
    diff = qpos[:, :, None] - kpos[:, None, :]
    mask = (diff >= 0) & (diff < WINDOW) & (kpos[:, None, :] >= 0)
    s = jnp.where(mask[None, :, None, None], s, -jnp.inf)
    p = sink_softmax(s, sinks_g).astype(v.dtype)
    o = jnp.einsum('bnhgqk,bnkhd->bnqhgd', p, vb)
    return o.reshape(B, T, Q_WIDTH)


def swa_buffered(q, kc, vc, sinks_g):
    B, S = q.shape[0], q.shape[1]
    L = kc.shape[1]
    qg = q.reshape(B, S, N_KV, GQA, HEAD_DIM)
    s = jnp.einsum('bqhgd,bkhd->bhgqk', qg, kc, preferred_element_type=jnp.float32) * (HEAD_DIM ** -0.5)
    qpos = (L - S) + jnp.arange(S)
    diff = qpos[:, None] - jnp.arange(L)[None, :]
    mask = (diff >= 0) & (diff < WINDOW)
    s = jnp.where(mask, s, -jnp.inf)
    p = sink_softmax(s, sinks_g).astype(vc.dtype)
    o = jnp.einsum('bhgqk,bkhd->bqhgd', p, vc)
    return o.reshape(B, S, Q_WIDTH)


def chunk_spatial_gating(d_u, d_v, ln_g, ln_b, sgu_w, sgu_b):
    B, T = d_u.shape[0], d_u.shape[1]
    u = jax.nn.gelu(d_u)
    vn = layer_norm(jax.nn.gelu(d_v), ln_g, ln_b)
    lc = min(T, CHUNK)
    vr = vn.reshape(B, T // lc, lc, N_GROUPS_D, GROUP_D)
    w = jnp.tril(sgu_w[:, :lc, :lc])
    bias = jnp.transpose(sgu_b[:, :lc])[:, :, None]
    s = jnp.einsum('gij,bnjgc->bnigc', w, vr) + bias
    return u * s.reshape(B, T, D_D), vn


def mixer_sublayer(x, pos, hist_a, hist_c, k_buf, v_buf,
                   w_in, b_gate, conv_a_w, conv_a_b, ln_a_g, ln_a_b, w_branch_a,
                   sinks, w_branch_b, conv_c_w, w_branch_c,
                   ln_d_g, ln_d_b, sgu_w, sgu_b, w_branch_d, w_out):
    B, T = x.shape[0], x.shape[1]
    h = jnp.einsum('btd,dn->btn', x, w_in)
    a_val, a_gate, q, k, v, c_b, c_c, c_x, d_u, d_v, g = split_columns(h)
    glu = a_val * jax.nn.sigmoid(a_gate)
    xa = jnp.concatenate([hist_a, glu], axis=1)
    ya = jax.nn.silu(layer_norm(causal_depthwise_conv(xa, conv_a_w) + conv_a_b, ln_a_g, ln_a_b))
    new_hist_a = xa[:, -(CONV_A - 1):]
    q = rope(q.reshape(B, T, N_HEADS, HEAD_DIM), pos)
    k = rope(k.reshape(B, T, N_KV, HEAD_DIM), pos)
    v = v.reshape(B, T, N_KV, HEAD_DIM)
    sinks_g = sinks.reshape(N_KV, GQA)
    if k_buf is None:
        yb = swa_banded(q, k, v, sinks_g)
        kc, vc = k, v
    else:
        kc = jnp.concatenate([k_buf, k], axis=1)
        vc = jnp.concatenate([v_buf, v], axis=1)
        yb = swa_buffered(q, kc, vc, sinks_g)
    new_k, new_v = kc[:, -WINDOW:], vc[:, -WINDOW:]
    xc = jnp.concatenate([hist_c, c_c * c_x], axis=1)
    yc = c_b * causal_depthwise_conv(xc, conv_c_w)
    new_hist_c = xc[:, -(CONV_C - 1):]
    yd, vd = chunk_spatial_gating(d_u, d_v, ln_d_g, ln_d_b, sgu_w, sgu_b)
    gates = jax.nn.sigmoid(g.reshape(B, T, N_BRANCH, D_MODEL) + b_gate)
    merged = (gates[:, :, 0] * (ya @ w_branch_a) + gates[:, :, 1] * (yb @ w_branch_b)
              + gates[:, :, 2] * (yc @ w_branch_c) + gates[:, :, 3] * (yd @ w_branch_d))
    out = merged @ w_out
    return out, new_hist_a, new_k, new_v, new_hist_c, vd


def swiglu(x, wg, wu, wd):
    return (jax.nn.silu(x @ wg) * (x @ wu)) @ wd


def moe_swiglu(x, router_w, router_b, wg, wu, wd):
    logits = jnp.einsum('btd,de->bte', x, router_w, preferred_element_type=jnp.float32) + router_b.astype(jnp.float32)
    top_v, top_i = lax.top_k(logits, TOP_K)
    top_p = jax.nn.softmax(top_v, axis=-1)
    gate = jnp.sum(jax.nn.one_hot(top_i, N_EXPERTS, dtype=jnp.float32) * top_p[..., None], axis=-2)
    y = jnp.zeros_like(x)
    for e in range(N_EXPERTS):
        y = y + gate[..., e:e + 1].astype(x.dtype) * swiglu(x, wg[e], wu[e], wd[e])
    return y


def setup_inputs(seed: int = 0) -> dict:
    key = jax.random.key(seed)
    keys = iter(jax.random.split(key, 48))

    def nrm(shape, scale):
        return jax.random.normal(next(keys), shape, jnp.float32) * scale

    d = D_MODEL
    return {
        'x_prompt': nrm((BATCH, SEQ, d), 1.0),
        'x_sample': nrm((DEC_BATCH, DEC_SEQ, d), 1.0),
        'state_conv_a': nrm((DEPTH, DEC_BATCH, CONV_A - 1, D_A), 0.5),
        'cache_swa_k': nrm((DEPTH, DEC_BATCH, WINDOW, N_KV, HEAD_DIM), 1.0),
        'cache_swa_v': nrm((DEPTH, DEC_BATCH, WINDOW, N_KV, HEAD_DIM), 1.0),
        'state_conv_c': nrm((DEPTH, DEC_BATCH, CONV_C - 1, D_C), 0.5),
        'w_in': nrm((DEPTH, d, N_IN), d ** -0.5),
        'b_gate': nrm((DEPTH, N_BRANCH, d), 0.01),
        'conv_a_w': nrm((DEPTH, CONV_A, D_A), CONV_A ** -0.5),
        'conv_a_b': nrm((DEPTH, D_A), 0.01),
        'ln_a_g': 1.0 + nrm((DEPTH, D_A), 0.01),
        'ln_a_b': nrm((DEPTH, D_A), 0.01),
        'w_branch_a': nrm((DEPTH, D_A, d), D_A ** -0.5),
        'sinks': nrm((DEPTH, N_HEADS), 1.0),
        'w_branch_b': nrm((DEPTH, Q_WIDTH, d), Q_WIDTH ** -0.5),
        'conv_c_w': nrm((DEPTH, CONV_C, D_C), CONV_C ** -0.5),
        'w_branch_c': nrm((DEPTH, D_C, d), D_C ** -0.5),
        'ln_d_g': 1.0 + nrm((DEPTH, D_D), 0.01),
        'ln_d_b': nrm((DEPTH, D_D), 0.01),
        'sgu_w': nrm((DEPTH, N_GROUPS_D, CHUNK, CHUNK), CHUNK ** -0.5),
        'sgu_b': 1.0 + nrm((DEPTH, N_GROUPS_D, CHUNK), 0.01),
        'w_branch_d': nrm((DEPTH, D_D, d), D_D ** -0.5),
        'w_out': nrm((DEPTH, d, d), (d ** -0.5) * BETA),
        'ln1_g': 1.0 + nrm((DEPTH, d), 0.01),
        'ln1_b': nrm((DEPTH, d), 0.01),
        'ffn_w_gate': nrm((N_DENSE, d, D_FF), d ** -0.5),
        'ffn_w_up': nrm((N_DENSE, d, D_FF), d ** -0.5),
        'ffn_w_down': nrm((N_DENSE, D_FF, d), (D_FF ** -0.5) * BETA),
        'router_w': nrm((N_MOE, d, N_EXPERTS), d ** -0.5),
        'router_b': nrm((N_MOE, N_EXPERTS), 0.01),
        'exp_w_gate': nrm((N_MOE, N_EXPERTS, d, D_FF), d ** -0.5),
        'exp_w_up': nrm((N_MOE, N_EXPERTS, d, D_FF), d ** -0.5),
        'exp_w_down': nrm((N_MOE, N_EXPERTS, D_FF, d), (D_FF ** -0.5) * BETA),
        'ln2_g': 1.0 + nrm((DEPTH, d), 0.01),
        'ln2_b': nrm((DEPTH, d), 0.01),
    }


def reference(x_prompt, x_sample, state_conv_a, cache_swa_k, cache_swa_v, state_conv_c,
              w_in, b_gate, conv_a_w, conv_a_b, ln_a_g, ln_a_b, w_branch_a,
              sinks, w_branch_b, conv_c_w, w_branch_c,
              ln_d_g, ln_d_b, sgu_w, sgu_b, w_branch_d, w_out, ln1_g, ln1_b,
              ffn_w_gate, ffn_w_up, ffn_w_down, router_w, router_b,
              exp_w_gate, exp_w_up, exp_w_down, ln2_g, ln2_b):
    xp, xs = x_prompt, x_sample
    bp = xp.shape[0]
    pos_p = jnp.arange(xp.shape[1])
    pos_s = PAST_LEN + jnp.arange(xs.shape[1])
    p_a, p_k, p_v, p_c = [], [], [], []
    s_a, s_k, s_v, s_c, s_d = [], [], [], [], []
    for l in range(DEPTH):
        mix_w = (w_in[l], b_gate[l], conv_a_w[l], conv_a_b[l], ln_a_g[l], ln_a_b[l], w_branch_a[l],
                 sinks[l], w_branch_b[l], conv_c_w[l], w_branch_c[l],
                 ln_d_g[l], ln_d_b[l], sgu_w[l], sgu_b[l], w_branch_d[l], w_out[l])
        hist_a0 = jnp.zeros((bp, CONV_A - 1, D_A), xp.dtype)
        hist_c0 = jnp.zeros((bp, CONV_C - 1, D_C), xp.dtype)
        mp, pa, pk, pv, pc, _ = mixer_sublayer(xp, pos_p, hist_a0, hist_c0, None, None, *mix_w)
        ms, sa, sk, sv, sc, sd = mixer_sublayer(xs, pos_s, state_conv_a[l], state_conv_c[l],
                                                cache_swa_k[l], cache_swa_v[l], *mix_w)
        xp = layer_norm(ALPHA * xp + mp, ln1_g[l], ln1_b[l])
        xs = layer_norm(ALPHA * xs + ms, ln1_g[l], ln1_b[l])
        i = l // 2
        if l % 2 == 0:
            fp = swiglu(xp, ffn_w_gate[i], ffn_w_up[i], ffn_w_down[i])
            fs = swiglu(xs, ffn_w_gate[i], ffn_w_up[i], ffn_w_down[i])
        else:
            fp = moe_swiglu(xp, router_w[i], router_b[i], exp_w_gate[i], exp_w_up[i], exp_w_down[i])
            fs = moe_swiglu(xs, router_w[i], router_b[i], exp_w_gate[i], exp_w_up[i], exp_w_down[i])
        xp = layer_norm(ALPHA * xp + fp, ln2_g[l], ln2_b[l])
        xs = layer_norm(ALPHA * xs + fs, ln2_g[l], ln2_b[l])
        p_a.append(pa)
        p_k.append(pk)
        p_v.append(pv)
        p_c.append(pc)
        s_a.append(sa)
        s_k.append(sk)
        s_v.append(sv)
        s_c.append(sc)
        s_d.append(sd)
    prompt_conv_a = jnp.stack(p_a)
    prompt_swa_k = jnp.stack(p_k)
    prompt_swa_v = jnp.stack(p_v)
    prompt_conv_c = jnp.stack(p_c)
    sample_conv_a = jnp.stack(s_a)
    sample_swa_k = jnp.stack(s_k)
    sample_swa_v = jnp.stack(s_v)
    sample_conv_c = jnp.stack(s_c)
    sample_chunk_v = jnp.stack(s_d)
    return (xp, xs, prompt_conv_a, prompt_swa_k, prompt_swa_v, prompt_conv_c,
            sample_conv_a, sample_swa_k, sample_swa_v, sample_conv_c, sample_chunk_v)
```

```python
import functools

import jax
import jax.numpy as jnp
import numpy as np
from jax import lax
from jax.experimental import pallas as pl
from jax.experimental.pallas import tpu as pltpu

D_MODEL = 2048
PAST_LEN = 8192
D_A = 512
CONV_A = 31
HEAD_DIM = 64
N_HEADS = 16
N_KV = 4
GQA = N_HEADS // N_KV
WINDOW = 128
ROPE_THETA = 10000.0
D_C = 512
CONV_C = 3
D_D = 512
CHUNK = 128
N_GROUPS_D = 4
GROUP_D = D_D // N_GROUPS_D
N_BRANCH = 4
Q_WIDTH = N_HEADS * HEAD_DIM
KV_WIDTH = N_KV * HEAD_DIM
D_FF = 5632
N_EXPERTS = 8
ALPHA = 4.0 ** 0.25
LN_EPS = 1e-5

OFF_A_VAL = 0
OFF_A_GATE = D_A
OFF_Q = 2 * D_A
OFF_K = OFF_Q + Q_WIDTH
OFF_V = OFF_K + KV_WIDTH
OFF_C_B = OFF_V + KV_WIDTH
OFF_C_C = OFF_C_B + D_C
OFF_C_X = OFF_C_C + D_C
OFF_D_U = OFF_C_X + D_C
OFF_D_V = OFF_D_U + D_D
OFF_GATES = OFF_D_V + D_D

LANES = 128
SUBLANES = 8
HALO = 32
VMEM_LIMIT = 48 * 1024 * 1024

BF16 = jnp.bfloat16
F32 = jnp.float32


def _tile(n, pref):
    if n <= pref:
        return n
    for t in range(pref, 7, -1):
        if n % t == 0 and t % 8 == 0:
            return t
    return n


def _params(*sem):
    return pltpu.CompilerParams(dimension_semantics=sem, vmem_limit_bytes=VMEM_LIMIT)


def _layer_norm(x, g, b):
    mu = jnp.mean(x, axis=-1, keepdims=True)
    xc = x - mu
    var = jnp.mean(xc * xc, axis=-1, keepdims=True)
    return xc * lax.rsqrt(var + LN_EPS) * g + b


def _silu(x):
    return x * jax.nn.sigmoid(x)


def _gelu(x):
    return jax.nn.gelu(x, approximate=True)


def _mm_kernel(x_ref, w_ref, o_ref):
    o_ref[...] = jnp.dot(x_ref[...], w_ref[...], preferred_element_type=F32).astype(o_ref.dtype)


def _in_proj(xb, w_in_b):
    m, k = xb.shape
    n = OFF_GATES
    tm = _tile(m, 1024)
    tn = 1024
    return pl.pallas_call(
        _mm_kernel,
        grid=(m // tm, n // tn),
        in_specs=[pl.BlockSpec((tm, k), lambda i, j: (i, 0)),
                  pl.BlockSpec((k, tn), lambda i, j: (0, j))],
        out_specs=pl.BlockSpec((tm, tn), lambda i, j: (i, j)),
        out_shape=jax.ShapeDtypeStruct((m, n), F32),
        compiler_params=_params("parallel", "arbitrary"),
        name="in_proj",
    )(xb, w_in_b)


def _ac_prompt_kernel(av_ref, ag_ref, cb_ref, cc_ref, cx_ref,
                      avp_ref, agp_ref, ccp_ref, cxp_ref,
                      wa_ref, ba_ref, lg_ref, lb_ref, wc_ref,
                      ya_ref, yc_ref, ha_ref, hc_ref, ga_s, gc_s):
    t = pl.program_id(1)
    tt = av_ref.shape[0]
    first = t == 0
    glu_prev = avp_ref[...] * jax.nn.sigmoid(agp_ref[...])
    ga_s[0:HALO, :] = jnp.where(first, 0.0, glu_prev)
    ga_s[HALO:HALO + tt, :] = av_ref[...] * jax.nn.sigmoid(ag_ref[...])
    acc = jnp.zeros((tt, D_A), F32) + ba_ref[...]
    base = HALO - (CONV_A - 1)
    for j in range(CONV_A):
        acc = acc + wa_ref[j:j + 1, :] * ga_s[base + j:base + j + tt, :]
    ya = _silu(_layer_norm(acc, lg_ref[...], lb_ref[...]))
    ya_ref[...] = ya.astype(ya_ref.dtype)
    ha_ref[0] = ga_s[HALO + tt - (CONV_A - 1):HALO + tt, :]

    gc_s[0:HALO, :] = jnp.where(first, 0.0, ccp_ref[...] * cxp_ref[...])
    gc_s[HALO:HALO + tt, :] = cc_ref[...] * cx_ref[...]
    base_c = HALO - (CONV_C - 1)
    yc = jnp.zeros((tt, D_C), F32)
    for j in range(CONV_C):
        yc = yc + wc_ref[j:j + 1, :] * gc_s[base_c + j:base_c + j + tt, :]
    yc_ref[...] = (cb_ref[...] * yc).astype(yc_ref.dtype)
    hc_ref[0] = gc_s[HALO + tt - (CONV_C - 1):HALO + tt, :]


def _ac_prompt(h, nb, t_len, wa, ba, lg, lb, wc):
    tt = _tile(t_len, 256)
    nt = t_len // tt
    cw = D_A

    def cur(col):
        return pl.BlockSpec((tt, cw), lambda b, t: (b * nt + t, col // cw))

    def prev(col):
        return pl.BlockSpec(
            (HALO, cw),
            lambda b, t: (jnp.maximum((b * t_len + t * tt) // HALO - 1, 0), col // cw))

    def full(shape):
        return pl.BlockSpec(shape, lambda b, t: (0,) * len(shape))

    mp = nb * t_len
    return pl.pallas_call(
        _ac_prompt_kernel,
        grid=(nb, nt),
        in_specs=[cur(OFF_A_VAL), cur(OFF_A_GATE), cur(OFF_C_B), cur(OFF_C_C), cur(OFF_C_X),
                  prev(OFF_A_VAL), prev(OFF_A_GATE), prev(OFF_C_C), prev(OFF_C_X),
                  full((CONV_A, D_A)), full((1, D_A)), full((1, D_A)), full((1, D_A)),
                  full((CONV_C, D_C))],
        out_specs=[pl.BlockSpec((tt, D_A), lambda b, t: (b * nt + t, 0)),
                   pl.BlockSpec((tt, D_C), lambda b, t: (b * nt + t, 0)),
                   pl.BlockSpec((1, CONV_A - 1, D_A), lambda b, t: (b, 0, 0)),
                   pl.BlockSpec((1, CONV_C - 1, D_C), lambda b, t: (b, 0, 0))],
        out_shape=[jax.ShapeDtypeStruct((mp, D_A), BF16),
                   jax.ShapeDtypeStruct((mp, D_C), BF16),
                   jax.ShapeDtypeStruct((nb, CONV_A - 1, D_A), F32),
                   jax.ShapeDtypeStruct((nb, CONV_C - 1, D_C), F32)],
        scratch_shapes=[pltpu.VMEM((HALO + tt, D_A), F32), pltpu.VMEM((HALO + tt, D_C), F32)],
        compiler_params=_params("parallel", "arbitrary"),
        name="ac_prompt",
    )(h, h, h, h, h, h, h, h, h, wa, ba, lg, lb, wc)


def _ac_sample_kernel(av_ref, ag_ref, cb_ref, cc_ref, cx_ref, hista_ref, histc_ref,
                      wa_ref, ba_ref, lg_ref, lb_ref, wc_ref,
                      ya_ref, yc_ref, ha_ref, hc_ref, xa_s, xc_s):
    sb, s_len = hista_ref.shape[0], av_ref.shape[0] // hista_ref.shape[0]
    ka, kc = CONV_A - 1, CONV_C - 1
    glu = av_ref[...] * jax.nn.sigmoid(ag_ref[...])
    xa_s[:, 0:ka, :] = hista_ref[...]
    xa_s[:, ka:ka + s_len, :] = glu.reshape(sb, s_len, D_A)
    acc = jnp.zeros((sb, s_len, D_A), F32) + ba_ref[...][None]
    for j in range(CONV_A):
        acc = acc + wa_ref[j:j + 1, :][None] * xa_s[:, j:j + s_len, :]
    ya = _silu(_layer_norm(acc, lg_ref[...][None], lb_ref[...][None]))
    ya_ref[...] = ya.reshape(sb * s_len, D_A).astype(ya_ref.dtype)
    ha_ref[...] = xa_s[:, s_len:s_len + ka, :]

    xc_s[:, 0:kc, :] = histc_ref[...]
    xc_s[:, kc:kc + s_len, :] = (cc_ref[...] * cx_ref[...]).reshape(sb, s_len, D_C)
    yc = jnp.zeros((sb, s_len, D_C), F32)
    for j in range(CONV_C):
        yc = yc + wc_ref[j:j + 1, :][None] * xc_s[:, j:j + s_len, :]
    yc = cb_ref[...] * yc.reshape(sb * s_len, D_C)
    yc_ref[...] = yc.astype(yc_ref.dtype)
    hc_ref[...] = xc_s[:, s_len:s_len + kc, :]


def _ac_sample(h, row0, nseq, s_len, hist_a, hist_c, wa, ba, lg, lb, wc):
    sb = _tile(nseq, 16)
    rows = sb * s_len
    blk0 = row0 // rows
    cw = D_A

    def cur(col):
        return pl.BlockSpec((rows, cw), lambda i: (blk0 + i, col // cw))

    def full(shape):
        return pl.BlockSpec(shape, lambda i: (0,) * len(shape))

    ms = nseq * s_len
    ka, kc = CONV_A - 1, CONV_C - 1
    return pl.pallas_call(
        _ac_sample_kernel,
        grid=(nseq // sb,),
        in_specs=[cur(OFF_A_VAL), cur(OFF_A_GATE), cur(OFF_C_B), cur(OFF_C_C), cur(OFF_C_X),
                  pl.BlockSpec((sb, ka, D_A), lambda i: (i, 0, 0)),
                  pl.BlockSpec((sb, kc, D_C), lambda i: (i, 0, 0)),
                  full((CONV_A, D_A)), full((1, D_A)), full((1, D_A)), full((1, D_A)),
                  full((CONV_C, D_C))],
        out_specs=[pl.BlockSpec((rows, D_A), lambda i: (i, 0)),
                   pl.BlockSpec((rows, D_C), lambda i: (i, 0)),
                   pl.BlockSpec((sb, ka, D_A), lambda i: (i, 0, 0)),
                   pl.BlockSpec((sb, kc, D_C), lambda i: (i, 0, 0))],
        out_shape=[jax.ShapeDtypeStruct((ms, D_A), BF16),
                   jax.ShapeDtypeStruct((ms, D_C), BF16),
                   jax.ShapeDtypeStruct((nseq, ka, D_A), F32),
                   jax.ShapeDtypeStruct((nseq, kc, D_C), F32)],
        scratch_shapes=[pltpu.VMEM((sb, ka + s_len + 2, D_A), F32),
                        pltpu.VMEM((sb, kc + s_len + 6, D_C), F32)],
        compiler_params=_params("parallel"),
        name="ac_sample",
    )(h, h, h, h, h, hist_a, hist_c, wa, ba, lg, lb, wc)


def _rope_tables(pos):
    half = HEAD_DIM // 2
    inv_freq = jnp.power(ROPE_THETA, -jnp.arange(half, dtype=F32) * (2.0 / HEAD_DIM))
    ang = pos.astype(F32)[:, None] * inv_freq[None, :]
    cos, sin = jnp.cos(ang), jnp.sin(ang)
    cos_t = jnp.concatenate([cos, cos, cos, cos], axis=1)
    sin_t = jnp.concatenate([-sin, sin, -sin, sin], axis=1)
    return cos_t, sin_t


def _rope(x, cos_t, sin_t):
    half = HEAD_DIM // 2
    axis = x.ndim - 1
    shape = x.shape[:-1] + (LANES,)
    lane = lax.broadcasted_iota(jnp.int32, shape, axis)
    first = (lane % HEAD_DIM) < half
    out = []
    for i in range(x.shape[-1] // LANES):
        xi = x[..., LANES * i:LANES * (i + 1)]
        partner = jnp.where(first, pltpu.roll(xi, LANES - half, axis), pltpu.roll(xi, half, axis))
        out.append(xi * cos_t + partner * sin_t)
    return out


def _head_halves(t, axis, lo_valid):
    lane = lax.broadcasted_iota(jnp.int32, t.shape, axis)
    if lo_valid:
        lo = jnp.where(lane < HEAD_DIM, t, jnp.zeros_like(t))
        hi = pltpu.roll(lo, HEAD_DIM, axis)
    else:
        hi = jnp.where(lane >= HEAD_DIM, t, jnp.zeros_like(t))
        lo = pltpu.roll(hi, HEAD_DIM, axis)
    return lo, hi


def _attn_prompt_kernel(sinks_ref, q_ref, kc_ref, vc_ref, kp_ref, vp_ref,
                        cos_ref, sin_ref, cosp_ref, sinp_ref,
                        yb_ref, nk_ref, nv_ref):
    n = pl.program_id(1)
    w = WINDOW
    cos_c, sin_c = cos_ref[...], sin_ref[...]
    q_tiles = _rope(q_ref[...] * (HEAD_DIM ** -0.5), cos_c, sin_c)
    kc_tiles = _rope(kc_ref[...], cos_c, sin_c)
    kp_tiles = _rope(kp_ref[...], cosp_ref[...], sinp_ref[...])
    for i in range(KV_WIDTH // LANES):
        nk_ref[0, :, LANES * i:LANES * (i + 1)] = kc_tiles[i]
    nv_ref[0] = vc_ref[...]

    qi = lax.broadcasted_iota(jnp.int32, (w, 2 * w), 0)
    ci = lax.broadcasted_iota(jnp.int32, (w, 2 * w), 1)
    mask = (ci > qi) & (ci <= qi + w) & ((n > 0) | (ci >= w))
    mask2 = jnp.concatenate([mask, mask], axis=0)
    row_top = lax.broadcasted_iota(jnp.int32, (2 * w, 1), 0) < w

    for h in range(N_KV):
        tile, lo_valid = h // 2, (h % 2 == 0)
        kcat = jnp.concatenate([kp_tiles[tile], kc_tiles[tile]], axis=0)
        vcat = jnp.concatenate([vp_ref[:, LANES * tile:LANES * (tile + 1)],
                                vc_ref[:, LANES * tile:LANES * (tile + 1)]], axis=0)
        k_lo, k_hi = _head_halves(kcat.astype(BF16), 1, lo_valid)
        v_lo, v_hi = _head_halves(vcat.astype(BF16), 1, lo_valid)
        qs = jnp.concatenate([q_tiles[2 * h], q_tiles[2 * h + 1]], axis=0).astype(BF16)
        probs = []
        for half_idx, k_half in enumerate((k_lo, k_hi)):
            s = lax.dot_general(qs, k_half, (((1,), (1,)), ((), ())), preferred_element_type=F32)
            s = jnp.where(mask2, s, -jnp.inf)
            sk = jnp.where(row_top, sinks_ref[4 * h + half_idx], sinks_ref[4 * h + 2 + half_idx])
            m = jnp.maximum(jnp.max(s, axis=1, keepdims=True), sk)
            e = jnp.exp(s - m)
            den = jnp.sum(e, axis=1, keepdims=True) + jnp.exp(sk - m)
            probs.append((e / den).astype(BF16))
        o = (jnp.dot(probs[0], v_lo, preferred_element_type=F32)
             + jnp.dot(probs[1], v_hi, preferred_element_type=F32))
        yb_ref[:, LANES * (2 * h):LANES * (2 * h + 1)] = o[0:w].astype(yb_ref.dtype)
        yb_ref[:, LANES * (2 * h + 1):LANES * (2 * h + 2)] = o[w:2 * w].astype(yb_ref.dtype)


def _attn_prompt(h, nb, t_len, sinks, cos_t, sin_t):
    w = WINDOW
    nblk = t_len // w
    mp = nb * t_len

    def rows(b, n):
        return b * nblk + n

    def rows_prev(b, n):
        return b * nblk + jnp.maximum(n - 1, 0)

    return pl.pallas_call(
        _attn_prompt_kernel,
        grid=(nb, nblk),
        in_specs=[pl.BlockSpec(memory_space=pltpu.SMEM),
                  pl.BlockSpec((w, Q_WIDTH), lambda b, n: (rows(b, n), OFF_Q // Q_WIDTH)),
                  pl.BlockSpec((w, KV_WIDTH), lambda b, n: (rows(b, n), OFF_K // KV_WIDTH)),
                  pl.BlockSpec((w, KV_WIDTH), lambda b, n: (rows(b, n), OFF_V // KV_WIDTH)),
                  pl.BlockSpec((w, KV_WIDTH), lambda b, n: (rows_prev(b, n), OFF_K // KV_WIDTH)),
                  pl.BlockSpec((w, KV_WIDTH), lambda b, n: (rows_prev(b, n), OFF_V // KV_WIDTH)),
                  pl.BlockSpec((w, LANES), lambda b, n: (n, 0)),
                  pl.BlockSpec((w, LANES), lambda b, n: (n, 0)),
                  pl.BlockSpec((w, LANES), lambda b, n: (jnp.maximum(n - 1, 0), 0)),
                  pl.BlockSpec((w, LANES), lambda b, n: (jnp.maximum(n - 1, 0), 0))],
        out_specs=[pl.BlockSpec((w, Q_WIDTH), lambda b, n: (rows(b, n), 0)),
                   pl.BlockSpec((1, w, KV_WIDTH), lambda b, n: (b, 0, 0)),
                   pl.BlockSpec((1, w, KV_WIDTH), lambda b, n: (b, 0, 0))],
        out_shape=[jax.ShapeDtypeStruct((mp, Q_WIDTH), BF16),
                   jax.ShapeDtypeStruct((nb, w, KV_WIDTH), F32),
                   jax.ShapeDtypeStruct((nb, w, KV_WIDTH), F32)],
        compiler_params=_params("parallel", "arbitrary"),
        name="attn_prompt",
    )(sinks, h, h, h, h, h, cos_t, sin_t, cos_t, sin_t)


def _attn_sample_kernel(sinks_ref, q_ref, kn_ref, vn_ref, kbuf_ref, vbuf_ref, cos_ref, sin_ref,
                        yb_ref, nk_ref, nv_ref):
    sb, w = kbuf_ref.shape[0], kbuf_ref.shape[1]
    s_len = q_ref.shape[0] // sb
    cos_t, sin_t = cos_ref[...], sin_ref[...]
    q_tiles = _rope(q_ref[...] * (HEAD_DIM ** -0.5), cos_t, sin_t)
    kn_tiles = _rope(kn_ref[...], cos_t, sin_t)
    nk_ref[:, 0:w - s_len, :] = kbuf_ref[:, s_len:w, :]
    nv_ref[:, 0:w - s_len, :] = vbuf_ref[:, s_len:w, :]
    for i in range(KV_WIDTH // LANES):
        nk_ref[:, w - s_len:w, LANES * i:LANES * (i + 1)] = kn_tiles[i].reshape(sb, s_len, LANES)
    nv_ref[:, w - s_len:w, :] = vn_ref[...].reshape(sb, s_len, KV_WIDTH)

    qi = lax.broadcasted_iota(jnp.int32, (2 * s_len, w), 0) % s_len
    ci = lax.broadcasted_iota(jnp.int32, (2 * s_len, w), 1)
    mask_buf = (ci > qi)[None]
    qn = lax.broadcasted_iota(jnp.int32, (2 * s_len, s_len), 0) % s_len
    cn = lax.broadcasted_iota(jnp.int32, (2 * s_len, s_len), 1)
    mask_new = (cn <= qn)[None]
    row_top = (lax.broadcasted_iota(jnp.int32, (2 * s_len, 1), 0) < s_len)[None]
    bqk = (((2,), (2,)), ((0,), (0,)))
    bkd = (((2,), (1,)), ((0,), (0,)))

    for h in range(N_KV):
        tile, lo_valid = h // 2, (h % 2 == 0)
        sl = slice(LANES * tile, LANES * (tile + 1))
        kb_lo, kb_hi = _head_halves(kbuf_ref[:, :, sl].astype(BF16), 2, lo_valid)
        vb_lo, vb_hi = _head_halves(vbuf_ref[:, :, sl].astype(BF16), 2, lo_valid)
        kn3 = kn_tiles[tile].reshape(sb, s_len, LANES).astype(BF16)
        vn3 = vn_ref[:, sl].reshape(sb, s_len, LANES).astype(BF16)
        kn_lo, kn_hi = _head_halves(kn3, 2, lo_valid)
        vn_lo, vn_hi = _head_halves(vn3, 2, lo_valid)
        qs = jnp.concatenate([q_tiles[2 * h].reshape(sb, s_len, LANES),
                              q_tiles[2 * h + 1].reshape(sb, s_len, LANES)], axis=1).astype(BF16)
        o = jnp.zeros((sb, 2 * s_len, LANES), F32)
        for half_idx, (kb, kn, vb, vn) in enumerate(((kb_lo, kn_lo, vb_lo, vn_lo),
                                                     (kb_hi, kn_hi, vb_hi, vn_hi))):
            s_b = lax.dot_general(qs, kb, bqk, preferred_element_type=F32)
            s_n = lax.dot_general(qs, kn, bqk, preferred_element_type=F32)
            s_b = jnp.where(mask_buf, s_b, -jnp.inf)
            s_n = jnp.where(mask_new, s_n, -jnp.inf)
            sk = jnp.where(row_top, sinks_ref[4 * h + half_idx], sinks_ref[4 * h + 2 + half_idx])
            m = jnp.maximum(jnp.maximum(jnp.max(s_b, axis=2, keepdims=True),
                                        jnp.max(s_n, axis=2, keepdims=True)), sk)
            e_b = jnp.exp(s_b - m)
            e_n = jnp.exp(s_n - m)
            den = (jnp.sum(e_b, axis=2, keepdims=True) + jnp.sum(e_n, axis=2, keepdims=True)
                   + jnp.exp(sk - m))
            o = o + lax.dot_general((e_b / den).astype(BF16), vb, bkd, preferred_element_type=F32)
            o = o + lax.dot_general((e_n / den).astype(BF16), vn, bkd, preferred_element_type=F32)
        yb_ref[:, LANES * (2 * h):LANES * (2 * h + 1)] = (
            o[:, 0:s_len, :].reshape(sb * s_len, LANES).astype(yb_ref.dtype))
        yb_ref[:, LANES * (2 * h + 1):LANES * (2 * h + 2)] = (
            o[:, s_len:2 * s_len, :].reshape(sb * s_len, LANES).astype(yb_ref.dtype))


def _attn_sample(h, row0, nseq, s_len, sinks, k_buf, v_buf, cos_t, sin_t):
    sb = _tile(nseq, 8)
    rows = sb * s_len
    blk0 = row0 // rows
    w = k_buf.shape[1]
    ms = nseq * s_len
    cos_rows = jnp.tile(cos_t, (sb, 1))
    sin_rows = jnp.tile(sin_t, (sb, 1))
    return pl.pallas_call(
        _attn_sample_kernel,
        grid=(nseq // sb,),
        in_specs=[pl.BlockSpec(memory_space=pltpu.SMEM),
                  pl.BlockSpec((rows, Q_WIDTH), lambda i: (blk0 + i, OFF_Q // Q_WIDTH)),
                  pl.BlockSpec((rows, KV_WIDTH), lambda i: (blk0 + i, OFF_K // KV_WIDTH)),
                  pl.BlockSpec((rows, KV_WIDTH), lambda i: (blk0 + i, OFF_V // KV_WIDTH)),
                  pl.BlockSpec((sb, w, KV_WIDTH), lambda i: (i, 0, 0)),
                  pl.BlockSpec((sb, w, KV_WIDTH), lambda i: (i, 0, 0)),
                  pl.BlockSpec((rows, LANES), lambda i: (0, 0)),
                  pl.BlockSpec((rows, LANES), lambda i: (0, 0))],
        out_specs=[pl.BlockSpec((rows, Q_WIDTH), lambda i: (i, 0)),
                   pl.BlockSpec((sb, w, KV_WIDTH), lambda i: (i, 0, 0)),
                   pl.BlockSpec((sb, w, KV_WIDTH), lambda i: (i, 0, 0))],
        out_shape=[jax.ShapeDtypeStruct((ms, Q_WIDTH), BF16),
                   jax.ShapeDtypeStruct((nseq, w, KV_WIDTH), F32),
                   jax.ShapeDtypeStruct((nseq, w, KV_WIDTH), F32)],
        compiler_params=_params("parallel"),
        name="attn_sample",
    )(sinks, h, h, h, k_buf, v_buf, cos_rows, sin_rows)


def _d_prompt_kernel(du_ref, dv_ref, lg_ref, lb_ref, w_ref, bt_ref, yd_ref):
    u = _gelu(du_ref[...])
    vn = _layer_norm(_gelu(dv_ref[...]), lg_ref[...], lb_ref[...]).astype(BF16)
    c = du_ref.shape[0]
    ri = lax.broadcasted_iota(jnp.int32, (c, c), 0)
    cj = lax.broadcasted_iota(jnp.int32, (c, c), 1)
    causal = cj <= ri
    for g in range(N_GROUPS_D):
        sl = slice(GROUP_D * g, GROUP_D * (g + 1))
        wg = jnp.where(causal, w_ref[g], 0.0).astype(BF16)
        s = jnp.dot(wg, vn[:, sl], preferred_element_type=F32) + bt_ref[:, g:g + 1]
        yd_ref[:, sl] = (u[:, sl] * s).astype(yd_ref.dtype)


def _d_prompt(h, mp, ln_g, ln_b, sgu_w, sgu_bt):
    c = CHUNK
    return pl.pallas_call(
        _d_prompt_kernel,
        grid=(mp // c,),
        in_specs=[pl.BlockSpec((c, D_D), lambda i: (i, OFF_D_U // D_D)),
                  pl.BlockSpec((c, D_D), lambda i: (i, OFF_D_V // D_D)),
                  pl.BlockSpec((1, D_D), lambda i: (0, 0)),
                  pl.BlockSpec((1, D_D), lambda i: (0, 0)),
                  pl.BlockSpec((N_GROUPS_D, c, c), lambda i: (0, 0, 0)),
                  pl.BlockSpec((c, N_GROUPS_D), lambda i: (0, 0))],
        out_specs=pl.BlockSpec((c, D_D), lambda i: (i, 0)),
        out_shape=jax.ShapeDtypeStruct((mp, D_D), BF16),
        compiler_params=_params("parallel"),
        name="d_prompt",
    )(h, h, ln_g, ln_b, sgu_w, sgu_bt)


def _d_sample_kernel(du_ref, dv_ref, lg_ref, lb_ref, wt_ref, bt_ref, yd_ref, vd_ref):
    sb, s_len = vd_ref.shape[0], vd_ref.shape[1]
    u = _gelu(du_ref[...])
    vn = _layer_norm(_gelu(dv_ref[...]), lg_ref[...], lb_ref[...])
    vn3 = vn.reshape(sb, s_len, D_D)
    vd_ref[...] = vn3
    ii = lax.broadcasted_iota(jnp.int32, (s_len, D_D), 0)
    s = jnp.zeros((sb, s_len, D_D), F32) + bt_ref[...][None]
    for j in range(s_len):
        wj = jnp.where(ii >= j, wt_ref[j], 0.0)
        s = s + wj[None] * vn3[:, j:j + 1, :]
    yd_ref[...] = (u * s.reshape(sb * s_len, D_D)).astype(yd_ref.dtype)


def _d_sample(h, row0, nseq, s_len, ln_g, ln_b, sgu_wt, sgu_bs):
    sb = _tile(nseq, 16)
    rows = sb * s_len
    blk0 = row0 // rows
    ms = nseq * s_len
    return pl.pallas_call(
        _d_sample_kernel,
        grid=(nseq // sb,),
        in_specs=[pl.BlockSpec((rows, D_D), lambda i: (blk0 + i, OFF_D_U // D_D)),
                  pl.BlockSpec((rows, D_D), lambda i: (blk0 + i, OFF_D_V // D_D)),
                  pl.BlockSpec((1, D_D), lambda i: (0, 0)),
                  pl.BlockSpec((1, D_D), lambda i: (0, 0)),
                  pl.BlockSpec((s_len, s_len, D_D), lambda i: (0, 0, 0)),
                  pl.BlockSpec((s_len, D_D), lambda i: (0, 0))],
        out_specs=[pl.BlockSpec((rows, D_D), lambda i: (i, 0)),
                   pl.BlockSpec((sb, s_len, D_D), lambda i: (i, 0, 0))],
        out_shape=[jax.ShapeDtypeStruct((ms, D_D), BF16),
                   jax.ShapeDtypeStruct((nseq, s_len, D_D), F32)],
        compiler_params=_params("parallel"),
        name="d_sample",
    )(h, h, ln_g, ln_b, sgu_wt, sgu_bs)


def _merge_kernel(x_ref, ya_ref, yb_ref, yc_ref, yd_ref, g0_ref, g1_ref, g2_ref, g3_ref,
                  bg_ref, pa_ref, pb_ref, pc_ref, pd_ref, o_ref):
    x = x_ref[...]
    acc = None
    branches = ((ya_ref, g0_ref, pa_ref), (yb_ref, g1_ref, pb_ref),
                (yc_ref, g2_ref, pc_ref), (yd_ref, g3_ref, pd_ref))
    for i, (y_ref, g_ref, p_ref) in enumerate(branches):
        gate = jax.nn.sigmoid(jnp.dot(x, g_ref[...], preferred_element_type=F32) + bg_ref[i:i + 1, :])
        term = gate * jnp.dot(y_ref[...], p_ref[...], preferred_element_type=F32)
        acc = term if acc is None else acc + term
    o_ref[...] = acc.astype(o_ref.dtype)


def _merge(xb, ya, yb, yc, yd, w_in_b, b_gate, pa, pb, pc, pd):
    m = xb.shape[0]
    tm = _tile(m, 512)
    tn = 512
    nn = D_MODEL // tn

    def act(width):
        return pl.BlockSpec((tm, width), lambda j, i: (i, 0))

    def gate_w(b):
        return pl.BlockSpec((D_MODEL, tn), lambda j, i: (0, (OFF_GATES + b * D_MODEL) // tn + j))

    def proj_w(k):
        return pl.BlockSpec((k, tn), lambda j, i: (0, j))

    return pl.pallas_call(
        _merge_kernel,
        grid=(nn, m // tm),
        in_specs=[act(D_MODEL), act(D_A), act(Q_WIDTH), act(D_C), act(D_D),
                  gate_w(0), gate_w(1), gate_w(2), gate_w(3),
                  pl.BlockSpec((N_BRANCH, tn), lambda j, i: (0, j)),
                  proj_w(D_A), proj_w(Q_WIDTH), proj_w(D_C), proj_w(D_D)],
        out_specs=pl.BlockSpec((tm, tn), lambda j, i: (i, j)),
        out_shape=jax.ShapeDtypeStruct((m, D_MODEL), BF16),
        compiler_params=_params("parallel", "arbitrary"),
        name="merge",
    )(xb, ya, yb, yc, yd, w_in_b, w_in_b, w_in_b, w_in_b, b_gate, pa, pb, pc, pd)


def _out_ln_kernel(mg_ref, x_ref, w_ref, g_ref, b_ref, o_ref, ob_ref):
    y = ALPHA * x_ref[...] + jnp.dot(mg_ref[...], w_ref[...], preferred_element_type=F32)
    y = _layer_norm(y, g_ref[...], b_ref[...])
    o_ref[...] = y
    ob_ref[...] = y.astype(ob_ref.dtype)


def _out_ln(merged, x, w_out_b, g, b):
    m = x.shape[0]
    tm = _tile(m, 512)
    return pl.pallas_call(
        _out_ln_kernel,
        grid=(m // tm,),
        in_specs=[pl.BlockSpec((tm, D_MODEL), lambda i: (i, 0)),
                  pl.BlockSpec((tm, D_MODEL), lambda i: (i, 0)),
                  pl.BlockSpec((D_MODEL, D_MODEL), lambda i: (0, 0)),
                  pl.BlockSpec((1, D_MODEL), lambda i: (0, 0)),
                  pl.BlockSpec((1, D_MODEL), lambda i: (0, 0))],
        out_specs=[pl.BlockSpec((tm, D_MODEL), lambda i: (i, 0)),
                   pl.BlockSpec((tm, D_MODEL), lambda i: (i, 0))],
        out_shape=[jax.ShapeDtypeStruct((m, D_MODEL), F32),
                   jax.ShapeDtypeStruct((m, D_MODEL), BF16)],
        compiler_params=_params("parallel"),
        name="out_ln",
    )(merged, x, w_out_b, g, b)


def _ffn_kernel(x_ref, xb_ref, wg_ref, wu_ref, wd_ref, g_ref, b_ref, o_ref, ob_ref, acc_s):
    f = pl.program_id(1)

    @pl.when(f == 0)
    def _():
        acc_s[...] = jnp.zeros_like(acc_s)

    xb = xb_ref[...]
    hid = (_silu(jnp.dot(xb, wg_ref[...], preferred_element_type=F32))
           * jnp.dot(xb, wu_ref[...], preferred_element_type=F32))
    acc_s[...] += jnp.dot(hid.astype(BF16), wd_ref[...], preferred_element_type=F32)

    @pl.when(f == pl.num_programs(1) - 1)
    def _():
        y = _layer_norm(ALPHA * x_ref[...] + acc_s[...], g_ref[...], b_ref[...])
        o_ref[...] = y
        ob_ref[...] = y.astype(ob_ref.dtype)


def _ffn_ln(x, xb, wg, wu, wd, g, b):
    m = x.shape[0]
    tm = _tile(m, 512)
    tf = 512
    return pl.pallas_call(
        _ffn_kernel,
        grid=(m // tm, D_FF // tf),
        in_specs=[pl.BlockSpec((tm, D_MODEL), lambda i, f: (i, 0)),
                  pl.BlockSpec((tm, D_MODEL), lambda i, f: (i, 0)),
                  pl.BlockSpec((D_MODEL, tf), lambda i, f: (0, f)),
                  pl.BlockSpec((D_MODEL, tf), lambda i, f: (0, f)),
                  pl.BlockSpec((tf, D_MODEL), lambda i, f: (f, 0)),
                  pl.BlockSpec((1, D_MODEL), lambda i, f: (0, 0)),
                  pl.BlockSpec((1, D_MODEL), lambda i, f: (0, 0))],
        out_specs=[pl.BlockSpec((tm, D_MODEL), lambda i, f: (i, 0)),
                   pl.BlockSpec((tm, D_MODEL), lambda i, f: (i, 0))],
        out_shape=[jax.ShapeDtypeStruct((m, D_MODEL), F32),
                   jax.ShapeDtypeStruct((m, D_MODEL), BF16)],
        scratch_shapes=[pltpu.VMEM((tm, D_MODEL), F32)],
        compiler_params=_params("parallel", "arbitrary"),
        name="ffn_ln",
    )(x, xb, wg, wu, wd, g, b)


def _router_kernel(x_ref, w_ref, b_ref, gate_ref):
    logits = jnp.dot(x_ref[...], w_ref[...], preferred_element_type=F32,
                     precision=lax.Precision.HIGHEST) + b_ref[...]
    idx = lax.broadcasted_iota(jnp.int32, logits.shape, 1)
    v1 = jnp.max(logits, axis=1, keepdims=True)
    i1 = jnp.min(jnp.where(logits == v1, idx, N_EXPERTS), axis=1, keepdims=True)
    rest = jnp.where(idx == i1, -jnp.inf, logits)
    v2 = jnp.max(rest, axis=1, keepdims=True)
    i2 = jnp.min(jnp.where(rest == v2, idx, N_EXPERTS), axis=1, keepdims=True)
    e2 = jnp.exp(v2 - v1)
    den = 1.0 + e2
    gate_ref[...] = jnp.where(idx == i1, 1.0 / den, 0.0) + jnp.where(idx == i2, e2 / den, 0.0)


def _router(x, router_w, router_b):
    m = x.shape[0]
    tm = _tile(m, 512)
    return pl.pallas_call(
        _router_kernel,
        grid=(m // tm,),
        in_specs=[pl.BlockSpec((tm, D_MODEL), lambda i: (i, 0)),
                  pl.BlockSpec((D_MODEL, N_EXPERTS), lambda i: (0, 0)),
                  pl.BlockSpec((1, N_EXPERTS), lambda i: (0, 0))],
        out_specs=pl.BlockSpec((tm, N_EXPERTS), lambda i: (i, 0)),
        out_shape=jax.ShapeDtypeStruct((m, N_EXPERTS), F32),
        compiler_params=_params("parallel"),
        name="router",
    )(x, router_w, router_b)


def _moe_kernel(x_ref, xb_ref, gate_ref, wg_ref, wu_ref, wd_ref, g_ref, b_ref, o_ref, ob_ref, acc_s):
    e = pl.program_id(1)
    f = pl.program_id(2)

    @pl.when((e == 0) & (f == 0))
    def _():
        acc_s[...] = jnp.zeros_like(acc_s)

    xb = xb_ref[...]
    gate = gate_ref[...]
    idx = lax.broadcasted_iota(jnp.int32, gate.shape, 1)
    gate_e = jnp.sum(jnp.where(idx == e, gate, 0.0), axis=1, keepdims=True)
    hid = (_silu(jnp.dot(xb, wg_ref[...], preferred_element_type=F32))
           * jnp.dot(xb, wu_ref[...], preferred_element_type=F32))
    acc_s[...] += gate_e * jnp.dot(hid.astype(BF16), wd_ref[...], preferred_element_type=F32)

    @pl.when((e == pl.num_programs(1) - 1) & (f == pl.num_programs(2) - 1))
    def _():
        y = _layer_norm(ALPHA * x_ref[...] + acc_s[...], g_ref[...], b_ref[...])
        o_ref[...] = y
        ob_ref[...] = y.astype(ob_ref.dtype)


def _moe_ln(x, xb, gate, wg, wu, wd, g, b):
    m = x.shape[0]
    tm = _tile(m, 512)
    tf = 512
    return pl.pallas_call(
        _moe_kernel,
        grid=(m // tm, N_EXPERTS, D_FF // tf),
        in_specs=[pl.BlockSpec((tm, D_MODEL), lambda i, e, f: (i, 0)),
                  pl.BlockSpec((tm, D_MODEL), lambda i, e, f: (i, 0)),
                  pl.BlockSpec((tm, N_EXPERTS), lambda i, e, f: (i, 0)),
                  pl.BlockSpec((None, D_MODEL, tf), lambda i, e, f: (e, 0, f)),
                  pl.BlockSpec((None, D_MODEL, tf), lambda i, e, f: (e, 0, f)),
                  pl.BlockSpec((None, tf, D_MODEL), lambda i, e, f: (e, f, 0)),
                  pl.BlockSpec((1, D_MODEL), lambda i, e, f: (0, 0)),
                  pl.BlockSpec((1, D_MODEL), lambda i, e, f: (0, 0))],
        out_specs=[pl.BlockSpec((tm, D_MODEL), lambda i, e, f: (i, 0)),
                   pl.BlockSpec((tm, D_MODEL), lambda i, e, f: (i, 0))],
        out_shape=[jax.ShapeDtypeStruct((m, D_MODEL), F32),
                   jax.ShapeDtypeStruct((m, D_MODEL), BF16)],
        scratch_shapes=[pltpu.VMEM((tm, D_MODEL), F32)],
        compiler_params=_params("parallel", "arbitrary", "arbitrary"),
        name="moe_ln",
    )(x, xb, gate, wg, wu, wd, g, b)


def kernel(x_prompt, x_sample, state_conv_a, cache_swa_k, cache_swa_v, state_conv_c, w_in, b_gate, conv_a_w, conv_a_b, ln_a_g, ln_a_b, w_branch_a, sinks, w_branch_b, conv_c_w, w_branch_c, ln_d_g, ln_d_b, sgu_w, sgu_b, w_branch_d, w_out, ln1_g, ln1_b, ffn_w_gate, ffn_w_up, ffn_w_down, router_w, router_b, exp_w_gate, exp_w_up, exp_w_down, ln2_g, ln2_b):
    nb, t_len, d = x_prompt.shape
    nseq, s_len, _ = x_sample.shape
    depth = w_in.shape[0]
    mp, ms = nb * t_len, nseq * s_len
    w = WINDOW

    x = jnp.concatenate([x_prompt.reshape(mp, d), x_sample.reshape(ms, d)], axis=0)
    xb = x.astype(BF16)

    cos_p, sin_p = _rope_tables(jnp.arange(t_len))
    cos_s, sin_s = _rope_tables(PAST_LEN + jnp.arange(s_len))

    def row(v):
        return v.reshape(1, -1)

    outs = {k: [] for k in ("pa", "pk", "pv", "pc", "sa", "sk", "sv", "sc", "sd")}
    for l in range(depth):
        w_in_b = w_in[l].astype(BF16)
        h = _in_proj(xb, w_in_b)

        wa, ba = conv_a_w[l], row(conv_a_b[l])
        lag, lab = row(ln_a_g[l]), row(ln_a_b[l])
        ya_p, yc_p, pa, pc = _ac_prompt(h, nb, t_len, wa, ba, lag, lab, conv_c_w[l])
        ya_s, yc_s, sa, sc = _ac_sample(h, mp, nseq, s_len, state_conv_a[l], state_conv_c[l],
                                        wa, ba, lag, lab, conv_c_w[l])

        yb_p, pk, pv = _attn_prompt(h, nb, t_len, sinks[l], cos_p, sin_p)
        yb_s, sk, sv = _attn_sample(h, mp, nseq, s_len, sinks[l],
                                    cache_swa_k[l].reshape(nseq, w, KV_WIDTH),
                                    cache_swa_v[l].reshape(nseq, w, KV_WIDTH), cos_s, sin_s)

        ldg, ldb = row(ln_d_g[l]), row(ln_d_b[l])
        yd_p = _d_prompt(h, mp, ldg, ldb, sgu_w[l], jnp.transpose(sgu_b[l]))
        sgu_wt = jnp.repeat(jnp.transpose(sgu_w[l][:, :s_len, :s_len], (2, 1, 0)), GROUP_D, axis=2)
        sgu_bs = jnp.repeat(jnp.transpose(sgu_b[l][:, :s_len]), GROUP_D, axis=1)
        yd_s, sd = _d_sample(h, mp, nseq, s_len, ldg, ldb, sgu_wt, sgu_bs)

        ya = jnp.concatenate([ya_p, ya_s], axis=0)
        yb = jnp.concatenate([yb_p, yb_s], axis=0)
        yc = jnp.concatenate([yc_p, yc_s], axis=0)
        yd = jnp.concatenate([yd_p, yd_s], axis=0)
        merged = _merge(xb, ya, yb, yc, yd, w_in_b, b_gate[l],
                        w_branch_a[l].astype(BF16), w_branch_b[l].astype(BF16),
                        w_branch_c[l].astype(BF16), w_branch_d[l].astype(BF16))
        x, xb = _out_ln(merged, x, w_out[l].astype(BF16), row(ln1_g[l]), row(ln1_b[l]))

        i = l // 2
        if l % 2 == 0:
            x, xb = _ffn_ln(x, xb, ffn_w_gate[i].astype(BF16), ffn_w_up[i].astype(BF16),
                            ffn_w_down[i].astype(BF16), row(ln2_g[l]), row(ln2_b[l]))
        else:
            gate = _router(x, router_w[i], row(router_b[i]))
            x, xb = _moe_ln(x, xb, gate, exp_w_gate[i].astype(BF16), exp_w_up[i].astype(BF16),
                            exp_w_down[i].astype(BF16), row(ln2_g[l]), row(ln2_b[l]))

        outs["pa"].append(pa)
        outs["pk"].append(pk.reshape(nb, w, N_KV, HEAD_DIM))
        outs["pv"].append(pv.reshape(nb, w, N_KV, HEAD_DIM))
        outs["pc"].append(pc)
        outs["sa"].append(sa)
        outs["sk"].append(sk.reshape(nseq, w, N_KV, HEAD_DIM))
        outs["sv"].append(sv.reshape(nseq, w, N_KV, HEAD_DIM))
        outs["sc"].append(sc)
        outs["sd"].append(sd)

    y_prompt = x[:mp].reshape(nb, t_len, d)
    y_sample = x[mp:].reshape(nseq, s_len, d)
    st = {k: jnp.stack(v) for k, v in outs.items()}
    return (y_prompt, y_sample, st["pa"], st["pk"], st["pv"], st["pc"],
            st["sa"], st["sk"], st["sv"], st["sc"], st["sd"])
```

```python
import functools

import jax
import jax.numpy as jnp
import numpy as np
from jax import lax
from jax.experimental import pallas as pl
from jax.experimental.pallas import tpu as pltpu

D_MODEL = 2048
PAST_LEN = 8192
D_A = 512
CONV_A = 31
HEAD_DIM = 64
N_HEADS = 16
N_KV = 4
GQA = N_HEADS // N_KV
WINDOW = 128
ROPE_THETA = 10000.0
D_C = 512
CONV_C = 3
D_D = 512
CHUNK = 128
N_GROUPS_D = 4
GROUP_D = D_D // N_GROUPS_D
N_BRANCH = 4
Q_WIDTH = N_HEADS * HEAD_DIM
KV_WIDTH = N_KV * HEAD_DIM
D_FF = 5632
N_EXPERTS = 8
ALPHA = 4.0 ** 0.25
LN_EPS = 1e-5

OFF_A_VAL = 0
OFF_A_GATE = D_A
OFF_Q = 2 * D_A
OFF_K = OFF_Q + Q_WIDTH
OFF_V = OFF_K + KV_WIDTH
OFF_C_B = OFF_V + KV_WIDTH
OFF_C_C = OFF_C_B + D_C
OFF_C_X = OFF_C_C + D_C
OFF_D_U = OFF_C_X + D_C
OFF_D_V = OFF_D_U + D_D
OFF_GATES = OFF_D_V + D_D

LANES = 128
SUBLANES = 8
HALO = 32
VMEM_LIMIT = 48 * 1024 * 1024

BF16 = jnp.bfloat16
F32 = jnp.float32


def _tile(n, pref):
    if n <= pref:
        return n
    for t in range(pref, 7, -1):
        if n % t == 0 and t % 8 == 0:
            return t
    return n


def _params(*sem):
    return pltpu.CompilerParams(dimension_semantics=sem, vmem_limit_bytes=VMEM_LIMIT)


def _layer_norm(x, g, b):
    mu = jnp.mean(x, axis=-1, keepdims=True)
    xc = x - mu
    var = jnp.mean(xc * xc, axis=-1, keepdims=True)
    return xc * lax.rsqrt(var + LN_EPS) * g + b


def _silu(x):
    return x * jax.nn.sigmoid(x)


def _gelu(x):
    return jax.nn.gelu(x, approximate=True)


def _mm_kernel(x_ref, w_ref, o_ref):
    o_ref[...] = jnp.dot(x_ref[...], w_ref[...], preferred_element_type=F32).astype(o_ref.dtype)


def _in_proj(xb, w_in_b):
    m, k = xb.shape
    n = OFF_GATES
    tm = _tile(m, 1024)
    tn = 1024
    return pl.pallas_call(
        _mm_kernel,
        grid=(m // tm, n // tn),
        in_specs=[pl.BlockSpec((tm, k), lambda i, j: (i, 0)),
                  pl.BlockSpec((k, tn), lambda i, j: (0, j))],
        out_specs=pl.BlockSpec((tm, tn), lambda i, j: (i, j)),
        out_shape=jax.ShapeDtypeStruct((m, n), F32),
        compiler_params=_params("parallel", "arbitrary"),
        name="in_proj",
    )(xb, w_in_b)


def _ac_prompt_kernel(av_ref, ag_ref, cb_ref, cc_ref, cx_ref,
                      avp_ref, agp_ref, ccp_ref, cxp_ref,
                      wa_ref, ba_ref, lg_ref, lb_ref, wc_ref,
                      ya_ref, yc_ref, ha_ref, hc_ref, ga_s, gc_s):
    t = pl.program_id(1)
    tt = av_ref.shape[0]
    first = t == 0
    glu_prev = avp_ref[...] * jax.nn.sigmoid(agp_ref[...])
    ga_s[0:HALO, :] = jnp.where(first, 0.0, glu_prev)
    ga_s[HALO:HALO + tt, :] = av_ref[...] * jax.nn.sigmoid(ag_ref[...])
    acc = jnp.zeros((tt, D_A), F32) + ba_ref[...]
    base = HALO - (CONV_A - 1)
    for j in range(CONV_A):
        acc = acc + wa_ref[j:j + 1, :] * ga_s[base + j:base + j + tt, :]
    ya = _silu(_layer_norm(acc, lg_ref[...], lb_ref[...]))
    ya_ref[...] = ya.astype(ya_ref.dtype)
    ha_ref[0] = ga_s[HALO + tt - (CONV_A - 1):HALO + tt, :]

    gc_s[0:HALO, :] = jnp.where(first, 0.0, ccp_ref[...] * cxp_ref[...])
    gc_s[HALO:HALO + tt, :] = cc_ref[...] * cx_ref[...]
    base_c = HALO - (CONV_C - 1)
    yc = jnp.zeros((tt, D_C), F32)
    for j in range(CONV_C):
        yc = yc + wc_ref[j:j + 1, :] * gc_s[base_c + j:base_c + j + tt, :]
    yc_ref[...] = (cb_ref[...] * yc).astype(yc_ref.dtype)
    hc_ref[0] = gc_s[HALO + tt - (CONV_C - 1):HALO + tt, :]


def _ac_prompt(h, nb, t_len, wa, ba, lg, lb, wc):
    tt = _tile(t_len, 256)
    nt = t_len // tt
    cw = D_A

    def cur(col):
        return pl.BlockSpec((tt, cw), lambda b, t: (b * nt + t, col // cw))

    def prev(col):
        return pl.BlockSpec(
            (HALO, cw),
            lambda b, t: (jnp.maximum((b * t_len + t * tt) // HALO - 1, 0), col // cw))

    def full(shape):
        return pl.BlockSpec(shape, lambda b, t: (0,) * len(shape))

    mp = nb * t_len
    return pl.pallas_call(
        _ac_prompt_kernel,
        grid=(nb, nt),
        in_specs=[cur(OFF_A_VAL), cur(OFF_A_GATE), cur(OFF_C_B), cur(OFF_C_C), cur(OFF_C_X),
                  prev(OFF_A_VAL), prev(OFF_A_GATE), prev(OFF_C_C), prev(OFF_C_X),
                  full((CONV_A, D_A)), full((1, D_A)), full((1, D_A)), full((1, D_A)),
                  full((CONV_C, D_C))],
        out_specs=[pl.BlockSpec((tt, D_A), lambda b, t: (b * nt + t, 0)),
                   pl.BlockSpec((tt, D_C), lambda b, t: (b * nt + t, 0)),
                   pl.BlockSpec((1, CONV_A - 1, D_A), lambda b, t: (b, 0, 0)),
                   pl.BlockSpec((1, CONV_C - 1, D_C), lambda b, t: (b, 0, 0))],
        out_shape=[jax.ShapeDtypeStruct((mp, D_A), BF16),
                   jax.ShapeDtypeStruct((mp, D_C), BF16),
                   jax.ShapeDtypeStruct((nb, CONV_A - 1, D_A), F32),
                   jax.ShapeDtypeStruct((nb, CONV_C - 1, D_C), F32)],
        scratch_shapes=[pltpu.VMEM((HALO + tt, D_A), F32), pltpu.VMEM((HALO + tt, D_C), F32)],
        compiler_params=_params("parallel", "arbitrary"),
        name="ac_prompt",
    )(h, h, h, h, h, h, h, h, h, wa, ba, lg, lb, wc)


def _ac_sample_kernel(av_ref, ag_ref, cb_ref, cc_ref, cx_ref, hista_ref, histc_ref,
                      wa_ref, ba_ref, lg_ref, lb_ref, wc_ref,
                      ya_ref, yc_ref, ha_ref, hc_ref, xa_s, xc_s):
    sb, s_len = hista_ref.shape[0], av_ref.shape[0] // hista_ref.shape[0]
    ka, kc = CONV_A - 1, CONV_C - 1
    glu = av_ref[...] * jax.nn.sigmoid(ag_ref[...])
    xa_s[:, 0:ka, :] = hista_ref[...]
    xa_s[:, ka:ka + s_len, :] = glu.reshape(sb, s_len, D_A)
    acc = jnp.zeros((sb, s_len, D_A), F32) + ba_ref[...][None]
    for j in range(CONV_A):
        acc = acc + wa_ref[j:j + 1, :][None] * xa_s[:, j:j + s_len, :]
    ya = _silu(_layer_norm(acc, lg_ref[...][None], lb_ref[...][None]))
    ya_ref[...] = ya.reshape(sb * s_len, D_A).astype(ya_ref.dtype)
    ha_ref[...] = xa_s[:, s_len:s_len + ka, :]

    xc_s[:, 0:kc, :] = histc_ref[...]
    xc_s[:, kc:kc + s_len, :] = (cc_ref[...] * cx_ref[...]).reshape(sb, s_len, D_C)
    yc = jnp.zeros((sb, s_len, D_C), F32)
    for j in range(CONV_C):
        yc = yc + wc_ref[j:j + 1, :][None] * xc_s[:, j:j + s_len, :]
    yc = cb_ref[...] * yc.reshape(sb * s_len, D_C)
    yc_ref[...] = yc.astype(yc_ref.dtype)
    hc_ref[...] = xc_s[:, s_len:s_len + kc, :]


def _ac_sample(h, row0, nseq, s_len, hist_a, hist_c, wa, ba, lg, lb, wc):
    sb = _tile(nseq, 16)
    rows = sb * s_len
    blk0 = row0 // rows
    cw = D_A

    def cur(col):
        return pl.BlockSpec((rows, cw), lambda i: (blk0 + i, col // cw))

    def full(shape):
        return pl.BlockSpec(shape, lambda i: (0,) * len(shape))

    ms = nseq * s_len
    ka, kc = CONV_A - 1, CONV_C - 1
    return pl.pallas_call(
        _ac_sample_kernel,
        grid=(nseq // sb,),
        in_specs=[cur(OFF_A_VAL), cur(OFF_A_GATE), cur(OFF_C_B), cur(OFF_C_C), cur(OFF_C_X),
                  pl.BlockSpec((sb, ka, D_A), lambda i: (i, 0, 0)),
                  pl.BlockSpec((sb, kc, D_C), lambda i: (i, 0, 0)),
                  full((CONV_A, D_A)), full((1, D_A)), full((1, D_A)), full((1, D_A)),
                  full((CONV_C, D_C))],
        out_specs=[pl.BlockSpec((rows, D_A), lambda i: (i, 0)),
                   pl.BlockSpec((rows, D_C), lambda i: (i, 0)),
                   pl.BlockSpec((sb, ka, D_A), lambda i: (i, 0, 0)),
                   pl.BlockSpec((sb, kc, D_C), lambda i: (i, 0, 0))],
        out_shape=[jax.ShapeDtypeStruct((ms, D_A), BF16),
                   jax.ShapeDtypeStruct((ms, D_C), BF16),
                   jax.ShapeDtypeStruct((nseq, ka, D_A), F32),
                   jax.ShapeDtypeStruct((nseq, kc, D_C), F32)],
        scratch_shapes=[pltpu.VMEM((sb, ka + s_len + 2, D_A), F32),
                        pltpu.VMEM((sb, kc + s_len + 6, D_C), F32)],
        compiler_params=_params("parallel"),
        name="ac_sample",
    )(h, h, h, h, h, hist_a, hist_c, wa, ba, lg, lb, wc)


def _rope_tables(pos):
    half = HEAD_DIM // 2
    inv_freq = jnp.power(ROPE_THETA, -jnp.arange(half, dtype=F32) * (2.0 / HEAD_DIM))
    ang = pos.astype(F32)[:, None] * inv_freq[None, :]
    cos, sin = jnp.cos(ang), jnp.sin(ang)
    cos_t = jnp.concatenate([cos, cos, cos, cos], axis=1)
    sin_t = jnp.concatenate([-sin, sin, -sin, sin], axis=1)
    return cos_t, sin_t


def _rope(x, cos_t, sin_t):
    half = HEAD_DIM // 2
    axis = x.ndim - 1
    shape = x.shape[:-1] + (LANES,)
    lane = lax.broadcasted_iota(jnp.int32, shape, axis)
    first = (lane % HEAD_DIM) < half
    out = []
    for i in range(x.shape[-1] // LANES):
        xi = x[..., LANES * i:LANES * (i + 1)]
        partner = jnp.where(first, pltpu.roll(xi, LANES - half, axis), pltpu.roll(xi, half, axis))
        out.append(xi * cos_t + partner * sin_t)
    return out


def _head_halves(t, axis, lo_valid):
    lane = lax.broadcasted_iota(jnp.int32, t.shape, axis)
    if lo_valid:
        lo = jnp.where(lane < HEAD_DIM, t, jnp.zeros_like(t))
        hi = pltpu.roll(lo, HEAD_DIM, axis)
    else:
        hi = jnp.where(lane >= HEAD_DIM, t, jnp.zeros_like(t))
        lo = pltpu.roll(hi, HEAD_DIM, axis)
    return lo, hi


def _attn_prompt_kernel(sinks_ref, q_ref, kc_ref, vc_ref, kp_ref, vp_ref,
                        cos_ref, sin_ref, cosp_ref, sinp_ref,
                        yb_ref, nk_ref, nv_ref):
    n = pl.program_id(1)
    w = WINDOW
    cos_c, sin_c = cos_ref[...], sin_ref[...]
    q_tiles = _rope(q_ref[...] * (HEAD_DIM ** -0.5), cos_c, sin_c)
    kc_tiles = _rope(kc_ref[...], cos_c, sin_c)
    kp_tiles = _rope(kp_ref[...], cosp_ref[...], sinp_ref[...])
    for i in range(KV_WIDTH // LANES):
        nk_ref[0, :, LANES * i:LANES * (i + 1)] = kc_tiles[i]
    nv_ref[0] = vc_ref[...]

    qi = lax.broadcasted_iota(jnp.int32, (w, 2 * w), 0)
    ci = lax.broadcasted_iota(jnp.int32, (w, 2 * w), 1)
    mask = (ci > qi) & (ci <= qi + w) & ((n > 0) | (ci >= w))
    mask2 = jnp.concatenate([mask, mask], axis=0)
    row_top = lax.broadcasted_iota(jnp.int32, (2 * w, 1), 0) < w

    for h in range(N_KV):
        tile, lo_valid = h // 2, (h % 2 == 0)
        kcat = jnp.concatenate([kp_tiles[tile], kc_tiles[tile]], axis=0)
        vcat = jnp.concatenate([vp_ref[:, LANES * tile:LANES * (tile + 1)],
                                vc_ref[:, LANES * tile:LANES * (tile + 1)]], axis=0)
        k_lo, k_hi = _head_halves(kcat.astype(BF16), 1, lo_valid)
        v_lo, v_hi = _head_halves(vcat.astype(BF16), 1, lo_valid)
        qs = jnp.concatenate([q_tiles[2 * h], q_tiles[2 * h + 1]], axis=0).astype(BF16)
        probs = []
        for half_idx, k_half in enumerate((k_lo, k_hi)):
            s = lax.dot_general(qs, k_half, (((1,), (1,)), ((), ())), preferred_element_type=F32)
            s = jnp.where(mask2, s, -jnp.inf)
            sk = jnp.where(row_top, sinks_ref[4 * h + half_idx], sinks_ref[4 * h + 2 + half_idx])
            m = jnp.maximum(jnp.max(s, axis=1, keepdims=True), sk)
            e = jnp.exp(s - m)
            den = jnp.sum(e, axis=1, keepdims=True) + jnp.exp(sk - m)
            probs.append((e / den).astype(BF16))
        o = (jnp.dot(probs[0], v_lo, preferred_element_type=F32)
             + jnp.dot(probs[1], v_hi, preferred_element_type=F32))
        yb_ref[:, LANES * (2 * h):LANES * (2 * h + 1)] = o[0:w].astype(yb_ref.dtype)
        yb_ref[:, LANES * (2 * h + 1):LANES * (2 * h + 2)] = o[w:2 * w].astype(yb_ref.dtype)


def _attn_prompt(h, nb, t_len, sinks, cos_t, sin_t):
    w = WINDOW
    nblk = t_len // w
    mp = nb * t_len

    def rows(b, n):
        return b * nblk + n

    def rows_prev(b, n):
        return b * nblk + jnp.maximum(n - 1, 0)

    return pl.pallas_call(
        _attn_prompt_kernel,
        grid=(nb, nblk),
        in_specs=[pl.BlockSpec(memory_space=pltpu.SMEM),
                  pl.BlockSpec((w, Q_WIDTH), lambda b, n: (rows(b, n), OFF_Q // Q_WIDTH)),
                  pl.BlockSpec((w, KV_WIDTH), lambda b, n: (rows(b, n), OFF_K // KV_WIDTH)),
                  pl.BlockSpec((w, KV_WIDTH), lambda b, n: (rows(b, n), OFF_V // KV_WIDTH)),
                  pl.BlockSpec((w, KV_WIDTH), lambda b, n: (rows_prev(b, n), OFF_K // KV_WIDTH)),
                  pl.BlockSpec((w, KV_WIDTH), lambda b, n: (rows_prev(b, n), OFF_V // KV_WIDTH)),
                  pl.BlockSpec((w, LANES), lambda b, n: (n, 0)),
                  pl.BlockSpec((w, LANES), lambda b, n: (n, 0)),
                  pl.BlockSpec((w, LANES), lambda b, n: (jnp.maximum(n - 1, 0), 0)),
                  pl.BlockSpec((w, LANES), lambda b, n: (jnp.maximum(n - 1, 0), 0))],
        out_specs=[pl.BlockSpec((w, Q_WIDTH), lambda b, n: (rows(b, n), 0)),
                   pl.BlockSpec((1, w, KV_WIDTH), lambda b, n: (b, 0, 0)),
                   pl.BlockSpec((1, w, KV_WIDTH), lambda b, n: (b, 0, 0))],
        out_shape=[jax.ShapeDtypeStruct((mp, Q_WIDTH), BF16),
                   jax.ShapeDtypeStruct((nb, w, KV_WIDTH), F32),
                   jax.ShapeDtypeStruct((nb, w, KV_WIDTH), F32)],
        compiler_params=_params("parallel", "arbitrary"),
        name="attn_prompt",
    )(sinks, h, h, h, h, h, cos_t, sin_t, cos_t, sin_t)


def _attn_sample_kernel(sinks_ref, q_ref, kn_ref, vn_ref, kbuf_ref, vbuf_ref, cos_ref, sin_ref,
                        yb_ref, nk_ref, nv_ref):
    sb, w = kbuf_ref.shape[0], kbuf_ref.shape[1]
    s_len = q_ref.shape[0] // sb
    cos_t, sin_t = cos_ref[...], sin_ref[...]
    q_tiles = _rope(q_ref[...] * (HEAD_DIM ** -0.5), cos_t, sin_t)
    kn_tiles = _rope(kn_ref[...], cos_t, sin_t)
    nk_ref[:, 0:w - s_len, :] = kbuf_ref[:, s_len:w, :]
    nv_ref[:, 0:w - s_len, :] = vbuf_ref[:, s_len:w, :]
    for i in range(KV_WIDTH // LANES):
        nk_ref[:, w - s_len:w, LANES * i:LANES * (i + 1)] = kn_tiles[i].reshape(sb, s_len, LANES)
    nv_ref[:, w - s_len:w, :] = vn_ref[...].reshape(sb, s_len, KV_WIDTH)

    qi = lax.broadcasted_iota(jnp.int32, (2 * s_len, w), 0) % s_len
    ci = lax.broadcasted_iota(jnp.int32, (2 * s_len, w), 1)
    mask_buf = (ci > qi)[None]
    qn = lax.broadcasted_iota(jnp.int32, (2 * s_len, s_len), 0) % s_len
    cn = lax.broadcasted_iota(jnp.int32, (2 * s_len, s_len), 1)
    mask_new = (cn <= qn)[None]
    row_top = (lax.broadcasted_iota(jnp.int32, (2 * s_len, 1), 0) < s_len)[None]
    bqk = (((2,), (2,)), ((0,), (0,)))
    bkd = (((2,), (1,)), ((0,), (0,)))

    for h in range(N_KV):
        tile, lo_valid = h // 2, (h % 2 == 0)
        sl = slice(LANES * tile, LANES * (tile + 1))
        kb_lo, kb_hi = _head_halves(kbuf_ref[:, :, sl].astype(BF16), 2, lo_valid)
        vb_lo, vb_hi = _head_halves(vbuf_ref[:, :, sl].astype(BF16), 2, lo_valid)
        kn3 = kn_tiles[tile].reshape(sb, s_len, LANES).astype(BF16)
        vn3 = vn_ref[:, sl].reshape(sb, s_len, LANES).astype(BF16)
        kn_lo, kn_hi = _head_halves(kn3, 2, lo_valid)
        vn_lo, vn_hi = _head_halves(vn3, 2, lo_valid)
        qs = jnp.concatenate([q_tiles[2 * h].reshape(sb, s_len, LANES),
                              q_tiles[2 * h + 1].reshape(sb, s_len, LANES)], axis=1).astype(BF16)
        o = jnp.zeros((sb, 2 * s_len, LANES), F32)
        for half_idx, (kb, kn, vb, vn) in enumerate(((kb_lo, kn_lo, vb_lo, vn_lo),
                                                     (kb_hi, kn_hi, vb_hi, vn_hi))):
            s_b = lax.dot_general(qs, kb, bqk, preferred_element_type=F32)
            s_n = lax.dot_general(qs, kn, bqk, preferred_element_type=F32)
            s_b = jnp.where(mask_buf, s_b, -jnp.inf)
            s_n = jnp.where(mask_new, s_n, -jnp.inf)
            sk = jnp.where(row_top, sinks_ref[4 * h + half_idx], sinks_ref[4 * h + 2 + half_idx])
            m = jnp.maximum(jnp.maximum(jnp.max(s_b, axis=2, keepdims=True),
                                        jnp.max(s_n, axis=2, keepdims=True)), sk)
            e_b = jnp.exp(s_b - m)
            e_n = jnp.exp(s_n - m)
            den = (jnp.sum(e_b, axis=2, keepdims=True) + jnp.sum(e_n, axis=2, keepdims=True)
                   + jnp.exp(sk - m))
            o = o + lax.dot_general((e_b / den).astype(BF16), vb, bkd, preferred_element_type=F32)
            o = o + lax.dot_general((e_n / den).astype(BF16), vn, bkd, preferred_element_type=F32)
        yb_ref[:, LANES * (2 * h):LANES * (2 * h + 1)] = (
            o[:, 0:s_len, :].reshape(sb * s_len, LANES).astype(yb_ref.dtype))
        yb_ref[:, LANES * (2 * h + 1):LANES * (2 * h + 2)] = (
            o[:, s_len:2 * s_len, :].reshape(sb * s_len, LANES).astype(yb_ref.dtype))


def _attn_sample(h, row0, nseq, s_len, sinks, k_buf, v_buf, cos_t, sin_t):
    sb = _tile(nseq, 8)
    rows = sb * s_len
    blk0 = row0 // rows
    w = k_buf.shape[1]
    ms = nseq * s_len
    cos_rows = jnp.tile(cos_t, (sb, 1))
    sin_rows = jnp.tile(sin_t, (sb, 1))
    return pl.pallas_call(
        _attn_sample_kernel,
        grid=(nseq // sb,),
        in_specs=[pl.BlockSpec(memory_space=pltpu.SMEM),
                  pl.BlockSpec((rows, Q_WIDTH), lambda i: (blk0 + i, OFF_Q // Q_WIDTH)),
                  pl.BlockSpec((rows, KV_WIDTH), lambda i: (blk0 + i, OFF_K // KV_WIDTH)),
                  pl.BlockSpec((rows, KV_WIDTH), lambda i: (blk0 + i, OFF_V // KV_WIDTH)),
                  pl.BlockSpec((sb, w, KV_WIDTH), lambda i: (i, 0, 0)),
                  pl.BlockSpec((sb, w, KV_WIDTH), lambda i: (i, 0, 0)),
                  pl.BlockSpec((rows, LANES), lambda i: (0, 0)),
                  pl.BlockSpec((rows, LANES), lambda i: (0, 0))],
        out_specs=[pl.BlockSpec((rows, Q_WIDTH), lambda i: (i, 0)),
                   pl.BlockSpec((sb, w, KV_WIDTH), lambda i: (i, 0, 0)),
                   pl.BlockSpec((sb, w, KV_WIDTH), lambda i: (i, 0, 0))],
        out_shape=[jax.ShapeDtypeStruct((ms, Q_WIDTH), BF16),
                   jax.ShapeDtypeStruct((nseq, w, KV_WIDTH), F32),
                   jax.ShapeDtypeStruct((nseq, w, KV_WIDTH), F32)],
        compiler_params=_params("parallel"),
        name="attn_sample",
    )(sinks, h, h, h, k_buf, v_buf, cos_rows, sin_rows)


def _d_prompt_kernel(du_ref, dv_ref, lg_ref, lb_ref, w_ref, bt_ref, yd_ref):
    u = _gelu(du_ref[...])
    vn = _layer_norm(_gelu(dv_ref[...]), lg_ref[...], lb_ref[...]).astype(BF16)
    c = du_ref.shape[0]
    ri = lax.broadcasted_iota(jnp.int32, (c, c), 0)
    cj = lax.broadcasted_iota(jnp.int32, (c, c), 1)
    causal = cj <= ri
    for g in range(N_GROUPS_D):
        sl = slice(GROUP_D * g, GROUP_D * (g + 1))
        wg = jnp.where(causal, w_ref[g], 0.0).astype(BF16)
        s = jnp.dot(wg, vn[:, sl], preferred_element_type=F32) + bt_ref[:, g:g + 1]
        yd_ref[:, sl] = (u[:, sl] * s).astype(yd_ref.dtype)


def _d_prompt(h, mp, ln_g, ln_b, sgu_w, sgu_bt):
    c = CHUNK
    return pl.pallas_call(
        _d_prompt_kernel,
        grid=(mp // c,),
        in_specs=[pl.BlockSpec((c, D_D), lambda i: (i, OFF_D_U // D_D)),
                  pl.BlockSpec((c, D_D), lambda i: (i, OFF_D_V // D_D)),
                  pl.BlockSpec((1, D_D), lambda i: (0, 0)),
                  pl.BlockSpec((1, D_D), lambda i: (0, 0)),
                  pl.BlockSpec((N_GROUPS_D, c, c), lambda i: (0, 0, 0)),
                  pl.BlockSpec((c, N_GROUPS_D), lambda i: (0, 0))],
        out_specs=pl.BlockSpec((c, D_D), lambda i: (i, 0)),
        out_shape=jax.ShapeDtypeStruct((mp, D_D), BF16),
        compiler_params=_params("parallel"),
        name="d_prompt",
    )(h, h, ln_g, ln_b, sgu_w, sgu_bt)


def _d_sample_kernel(du_ref, dv_ref, lg_ref, lb_ref, wt_ref, bt_ref, yd_ref, vd_ref):
    sb, s_len = vd_ref.shape[0], vd_ref.shape[1]
    u = _gelu(du_ref[...])
    vn = _layer_norm(_gelu(dv_ref[...]), lg_ref[...], lb_ref[...])
    vn3 = vn.reshape(sb, s_len, D_D)
    vd_ref[...] = vn3
    ii = lax.broadcasted_iota(jnp.int32, (s_len, D_D), 0)
    s = jnp.zeros((sb, s_len, D_D), F32) + bt_ref[...][None]
    for j in range(s_len):
        wj = jnp.where(ii >= j, wt_ref[j], 0.0)
        s = s + wj[None] * vn3[:, j:j + 1, :]
    yd_ref[...] = (u * s.reshape(sb * s_len, D_D)).astype(yd_ref.dtype)


def _d_sample(h, row0, nseq, s_len, ln_g, ln_b, sgu_wt, sgu_bs):
    sb = _tile(nseq, 16)
    rows = sb * s_len
    blk0 = row0 // rows
    ms = nseq * s_len
    return pl.pallas_call(
        _d_sample_kernel,
        grid=(nseq // sb,),
        in_specs=[pl.BlockSpec((rows, D_D), lambda i: (blk0 + i, OFF_D_U // D_D)),
                  pl.BlockSpec((rows, D_D), lambda i: (blk0 + i, OFF_D_V // D_D)),
                  pl.BlockSpec((1, D_D), lambda i: (0, 0)),
                  pl.BlockSpec((1, D_D), lambda i: (0, 0)),
                  pl.BlockSpec((s_len, s_len, D_D), lambda i: (0, 0, 0)),
                  pl.BlockSpec((s_len, D_D), lambda i: (0, 0))],
        out_specs=[pl.BlockSpec((rows, D_D), lambda i: (i, 0)),
                   pl.BlockSpec((sb, s_len, D_D), lambda i: (i, 0, 0))],
        out_shape=[jax.ShapeDtypeStruct((ms, D_D), BF16),
                   jax.ShapeDtypeStruct((nseq, s_len, D_D), F32)],
        compiler_params=_params("parallel"),
        name="d_sample",
    )(h, h, ln_g, ln_b, sgu_wt, sgu_bs)


def _merge_kernel(x_ref, yap_ref, ybp_ref, ycp_ref, ydp_ref, yas_ref, ybs_ref, ycs_ref, yds_ref,
                  g0_ref, g1_ref, g2_ref, g3_ref, bg_ref, pa_ref, pb_ref, pc_ref, pd_ref, o_ref,
                  *, prompt_tiles):
    x = x_ref[...]
    is_prompt = pl.program_id(1) < prompt_tiles
    acc = None
    branches = ((yap_ref, yas_ref, g0_ref, pa_ref), (ybp_ref, ybs_ref, g1_ref, pb_ref),
                (ycp_ref, ycs_ref, g2_ref, pc_ref), (ydp_ref, yds_ref, g3_ref, pd_ref))
    for i, (yp_ref, ys_ref, g_ref, p_ref) in enumerate(branches):
        y = jnp.where(is_prompt, yp_ref[...], ys_ref[...])
        gate = jax.nn.sigmoid(jnp.dot(x, g_ref[...], preferred_element_type=F32) + bg_ref[i:i + 1, :])
        term = gate * jnp.dot(y, p_ref[...], preferred_element_type=F32)
        acc = term if acc is None else acc + term
    o_ref[...] = acc.astype(o_ref.dtype)


def _merge(xb, y_prompt, y_sample, w_in_b, b_gate, projs):
    m = xb.shape[0]
    mp, ms = y_prompt[0].shape[0], y_sample[0].shape[0]
    tm = _tile(np.gcd(mp, ms), 512)
    npt = mp // tm
    tn = 512
    nn = D_MODEL // tn
    widths = (D_A, Q_WIDTH, D_C, D_D)

    def act_p(width):
        return pl.BlockSpec((tm, width), lambda j, i: (jnp.minimum(i, npt - 1), 0))

    def act_s(width):
        return pl.BlockSpec((tm, width), lambda j, i: (jnp.maximum(i - npt, 0), 0))

    def gate_w(b):
        return pl.BlockSpec((D_MODEL, tn), lambda j, i: (0, (OFF_GATES + b * D_MODEL) // tn + j))

    def proj_w(k):
        return pl.BlockSpec((k, tn), lambda j, i: (0, j))

    return pl.pallas_call(
        functools.partial(_merge_kernel, prompt_tiles=npt),
        grid=(nn, m // tm),
        in_specs=[pl.BlockSpec((tm, D_MODEL), lambda j, i: (i, 0))]
                 + [act_p(wd) for wd in widths] + [act_s(wd) for wd in widths]
                 + [gate_w(0), gate_w(1), gate_w(2), gate_w(3),
                    pl.BlockSpec((N_BRANCH, tn), lambda j, i: (0, j))]
                 + [proj_w(wd) for wd in widths],
        out_specs=pl.BlockSpec((tm, tn), lambda j, i: (i, j)),
        out_shape=jax.ShapeDtypeStruct((m, D_MODEL), BF16),
        compiler_params=_params("parallel", "arbitrary"),
        name="merge",
    )(xb, *y_prompt, *y_sample, w_in_b, w_in_b, w_in_b, w_in_b, b_gate, *projs)


def _out_ln_kernel(mg_ref, x_ref, w_ref, g_ref, b_ref, o_ref, ob_ref):
    y = ALPHA * x_ref[...] + jnp.dot(mg_ref[...], w_ref[...], preferred_element_type=F32)
    y = _layer_norm(y, g_ref[...], b_ref[...])
    o_ref[...] = y
    ob_ref[...] = y.astype(ob_ref.dtype)


def _out_ln(merged, x, w_out_b, g, b):
    m = x.shape[0]
    tm = _tile(m, 512)
    return pl.pallas_call(
        _out_ln_kernel,
        grid=(m // tm,),
        in_specs=[pl.BlockSpec((tm, D_MODEL), lambda i: (i, 0)),
                  pl.BlockSpec((tm, D_MODEL), lambda i: (i, 0)),
                  pl.BlockSpec((D_MODEL, D_MODEL), lambda i: (0, 0)),
                  pl.BlockSpec((1, D_MODEL), lambda i: (0, 0)),
                  pl.BlockSpec((1, D_MODEL), lambda i: (0, 0))],
        out_specs=[pl.BlockSpec((tm, D_MODEL), lambda i: (i, 0)),
                   pl.BlockSpec((tm, D_MODEL), lambda i: (i, 0))],
        out_shape=[jax.ShapeDtypeStruct((m, D_MODEL), F32),
                   jax.ShapeDtypeStruct((m, D_MODEL), BF16)],
        compiler_params=_params("parallel"),
        name="out_ln",
    )(merged, x, w_out_b, g, b)


def _ffn_kernel(x_ref, xb_ref, wg_ref, wu_ref, wd_ref, g_ref, b_ref, o_ref, ob_ref, acc_s):
    f = pl.program_id(1)

    @pl.when(f == 0)
    def _():
        acc_s[...] = jnp.zeros_like(acc_s)

    xb = xb_ref[...]
    hid = (_silu(jnp.dot(xb, wg_ref[...], preferred_element_type=F32))
           * jnp.dot(xb, wu_ref[...], preferred_element_type=F32))
    acc_s[...] += jnp.dot(hid.astype(BF16), wd_ref[...], preferred_element_type=F32)

    @pl.when(f == pl.num_programs(1) - 1)
    def _():
        y = _layer_norm(ALPHA * x_ref[...] + acc_s[...], g_ref[...], b_ref[...])
        o_ref[...] = y
        ob_ref[...] = y.astype(ob_ref.dtype)


def _ffn_ln(x, xb, wg, wu, wd, g, b):
    m = x.shape[0]
    tm = _tile(m, 512)
    tf = 512
    return pl.pallas_call(
        _ffn_kernel,
        grid=(m // tm, D_FF // tf),
        in_specs=[pl.BlockSpec((tm, D_MODEL), lambda i, f: (i, 0)),
                  pl.BlockSpec((tm, D_MODEL), lambda i, f: (i, 0)),
                  pl.BlockSpec((D_MODEL, tf), lambda i, f: (0, f)),
                  pl.BlockSpec((D_MODEL, tf), lambda i, f: (0, f)),
                  pl.BlockSpec((tf, D_MODEL), lambda i, f: (f, 0)),
                  pl.BlockSpec((1, D_MODEL), lambda i, f: (0, 0)),
                  pl.BlockSpec((1, D_MODEL), lambda i, f: (0, 0))],
        out_specs=[pl.BlockSpec((tm, D_MODEL), lambda i, f: (i, 0)),
                   pl.BlockSpec((tm, D_MODEL), lambda i, f: (i, 0))],
        out_shape=[jax.ShapeDtypeStruct((m, D_MODEL), F32),
                   jax.ShapeDtypeStruct((m, D_MODEL), BF16)],
        scratch_shapes=[pltpu.VMEM((tm, D_MODEL), F32)],
        compiler_params=_params("parallel", "arbitrary"),
        name="ffn_ln",
    )(x, xb, wg, wu, wd, g, b)


def _router_kernel(x_ref, w_ref, b_ref, idx_ref, prob_ref, rank_ref, cnt_ref, run_s):
    @pl.when(pl.program_id(0) == 0)
    def _():
        run_s[...] = jnp.zeros_like(run_s)

    logits = jnp.dot(x_ref[...], w_ref[...], preferred_element_type=F32,
                     precision=lax.Precision.HIGHEST) + b_ref[...]
    tm = logits.shape[0]
    idx = lax.broadcasted_iota(jnp.int32, logits.shape, 1)
    v1 = jnp.max(logits, axis=1, keepdims=True)
    i1 = jnp.min(jnp.where(logits == v1, idx, N_EXPERTS), axis=1, keepdims=True)
    rest = jnp.where(idx == i1, -jnp.inf, logits)
    v2 = jnp.max(rest, axis=1, keepdims=True)
    i2 = jnp.min(jnp.where(rest == v2, idx, N_EXPERTS), axis=1, keepdims=True)
    e2 = jnp.exp(v2 - v1)
    den = 1.0 + e2

    hit = ((idx == i1) | (idx == i2)).astype(F32)
    ri = lax.broadcasted_iota(jnp.int32, (tm, tm), 0)
    ci = lax.broadcasted_iota(jnp.int32, (tm, tm), 1)
    before = jnp.dot((ci < ri).astype(BF16), hit.astype(BF16), preferred_element_type=F32) + run_s[...]
    rank1 = jnp.sum(jnp.where(idx == i1, before, 0.0), axis=1, keepdims=True)
    rank2 = jnp.sum(jnp.where(idx == i2, before, 0.0), axis=1, keepdims=True)

    col = lax.broadcasted_iota(jnp.int32, (tm, 2), 1)
    idx_ref[...] = jnp.where(col == 0, i1, i2)
    prob_ref[...] = jnp.where(col == 0, 1.0 / den, e2 / den)
    rank_ref[...] = jnp.where(col == 0, rank1, rank2).astype(jnp.int32)
    total = run_s[...] + jnp.sum(hit, axis=0, keepdims=True)
    run_s[...] = total
    cnt_ref[...] = total.astype(jnp.int32)


def _router(x, router_w, router_b):
    m = x.shape[0]
    tm = _tile(m, 512)
    return pl.pallas_call(
        _router_kernel,
        grid=(m // tm,),
        in_specs=[pl.BlockSpec((tm, D_MODEL), lambda i: (i, 0)),
                  pl.BlockSpec((D_MODEL, N_EXPERTS), lambda i: (0, 0)),
                  pl.BlockSpec((1, N_EXPERTS), lambda i: (0, 0))],
        out_specs=[pl.BlockSpec((tm, 2), lambda i: (i, 0)),
                   pl.BlockSpec((tm, 2), lambda i: (i, 0)),
                   pl.BlockSpec((tm, 2), lambda i: (i, 0)),
                   pl.BlockSpec((1, N_EXPERTS), lambda i: (0, 0))],
        out_shape=[jax.ShapeDtypeStruct((m, 2), jnp.int32),
                   jax.ShapeDtypeStruct((m, 2), F32),
                   jax.ShapeDtypeStruct((m, 2), jnp.int32),
                   jax.ShapeDtypeStruct((1, N_EXPERTS), jnp.int32)],
        scratch_shapes=[pltpu.VMEM((1, N_EXPERTS), F32)],
        compiler_params=_params("arbitrary"),
        name="router",
    )(x, router_w, router_b)


def _route_tables(idx, prob, rank, counts, tm, n_tiles):
    m = idx.shape[0]
    padded = ((counts + tm - 1) // tm) * tm
    ends = jnp.cumsum(padded)
    base = ends - padded
    pos = base[idx] + rank
    n_valid = ends[-1] // tm
    tiles = jnp.arange(n_tiles, dtype=jnp.int32)
    tile_valid = (tiles < n_valid).astype(jnp.int32)
    owner = jnp.searchsorted(ends, jnp.minimum(tiles, n_valid - 1) * tm, side="right")
    tile_expert = jnp.minimum(owner, N_EXPERTS - 1).astype(jnp.int32)
    flat = pos.reshape(-1)
    token = jnp.repeat(jnp.arange(m, dtype=jnp.int32), 2)
    src = jnp.zeros((n_tiles * tm,), jnp.int32).at[flat].set(token)
    row_prob = jnp.zeros((n_tiles * tm,), F32).at[flat].set(prob.reshape(-1))
    return pos, src, row_prob, tile_expert, tile_valid


def _row_copy(src_hbm, row, dst_vmem, dst_row, sem):
    return pltpu.make_async_copy(src_hbm.at[pl.ds(row, 1), :], dst_vmem.at[pl.ds(dst_row, 1), :], sem)


def _moe_group_kernel(te_ref, tv_ref, src_ref, src_next_ref, x_hbm, p_ref, wg_ref, wu_ref, wd_ref,
                      o_ref, xg_s, xb_s, sem):
    i, f = pl.program_id(0), pl.program_id(1)
    nt, nf = pl.num_programs(0), pl.num_programs(1)
    tm = xb_s.shape[0]
    slot = i % 2
    valid = tv_ref[i] == 1

    def issue(idx_ref, s):
        def body(r, c):
            _row_copy(x_hbm, idx_ref[0, r], xg_s.at[s], r, sem.at[s]).start()
            return c
        lax.fori_loop(0, tm, body, 0, unroll=8)

    @pl.when((i == 0) & (f == 0))
    def _():
        issue(src_ref, 0)

    @pl.when(f == 0)
    def _():
        o_ref[...] = jnp.zeros_like(o_ref)

    @pl.when(valid & (f == 0))
    def _():
        pltpu.make_async_copy(xg_s.at[slot], xg_s.at[slot], sem.at[slot]).wait()
        xb_s[...] = xg_s[slot].astype(BF16)

    @pl.when((f == 1) & (tv_ref[jnp.minimum(i + 1, nt - 1)] == 1) & (i + 1 < nt))
    def _():
        issue(src_next_ref, 1 - slot)

    @pl.when(valid)
    def _():
        xb = xb_s[...]
        hid = (_silu(jnp.dot(xb, wg_ref[...], preferred_element_type=F32))
               * jnp.dot(xb, wu_ref[...], preferred_element_type=F32))
        o_ref[...] += jnp.dot(hid.astype(BF16), wd_ref[...], preferred_element_type=F32)

    @pl.when(valid & (f == nf - 1))
    def _():
        o_ref[...] = o_ref[...] * p_ref[...]


def _moe_group(x, src, row_prob, tile_expert, tile_valid, wg, wu, wd, tm, n_tiles):
    tf = 512
    nf = D_FF // tf

    def w_col(i, f, te, tv):
        return (te[i], 0, jnp.where(tv[i] == 1, f, nf - 1))

    def w_row(i, f, te, tv):
        return (te[i], jnp.where(tv[i] == 1, f, nf - 1), 0)

    grid_spec = pltpu.PrefetchScalarGridSpec(
        num_scalar_prefetch=2,
        grid=(n_tiles, nf),
        in_specs=[pl.BlockSpec((None, 1, tm), lambda i, f, te, tv: (i, 0, 0), memory_space=pltpu.SMEM),
                  pl.BlockSpec((None, 1, tm), lambda i, f, te, tv: (jnp.minimum(i + 1, n_tiles - 1), 0, 0),
                               memory_space=pltpu.SMEM),
                  pl.BlockSpec(memory_space=pl.ANY),
                  pl.BlockSpec((tm, 1), lambda i, f, te, tv: (i, 0)),
                  pl.BlockSpec((None, D_MODEL, tf), w_col),
                  pl.BlockSpec((None, D_MODEL, tf), w_col),
                  pl.BlockSpec((None, tf, D_MODEL), w_row)],
        out_specs=pl.BlockSpec((tm, D_MODEL), lambda i, f, te, tv: (i, 0)),
        scratch_shapes=[pltpu.VMEM((2, tm, D_MODEL), F32),
                        pltpu.VMEM((tm, D_MODEL), BF16),
                        pltpu.SemaphoreType.DMA((2,))],
    )
    src3 = src.reshape(n_tiles, 1, tm)
    return pl.pallas_call(
        _moe_group_kernel,
        grid_spec=grid_spec,
        out_shape=jax.ShapeDtypeStruct((n_tiles * tm, D_MODEL), F32),
        compiler_params=_params("arbitrary", "arbitrary"),
        name="moe_group",
    )(tile_expert, tile_valid, src3, src3, x, row_prob.reshape(-1, 1), wg, wu, wd)


def _combine_kernel(pos_ref, pos_next_ref, x_ref, y_hbm, g_ref, b_ref, op_ref, os_ref, ybuf, sem,
                    *, prompt_tiles):
    i, nt = pl.program_id(0), pl.num_programs(0)
    tm = x_ref.shape[0]
    slot = i % 2

    def issue(idx_ref, s):
        def body(r, c):
            _row_copy(y_hbm, idx_ref[0, 2 * r], ybuf.at[s, 0], r, sem.at[s]).start()
            _row_copy(y_hbm, idx_ref[0, 2 * r + 1], ybuf.at[s, 1], r, sem.at[s]).start()
            return c
        lax.fori_loop(0, tm, body, 0, unroll=4)

    @pl.when(i == 0)
    def _():
        issue(pos_ref, 0)

    @pl.when(i + 1 < nt)
    def _():
        issue(pos_next_ref, 1 - slot)

    pltpu.make_async_copy(ybuf.at[slot], ybuf.at[slot], sem.at[slot]).wait()
    y = _layer_norm(ALPHA * x_ref[...] + (ybuf[slot, 0] + ybuf[slot, 1]), g_ref[...], b_ref[...])

    @pl.when(i < prompt_tiles)
    def _():
        op_ref[...] = y

    @pl.when(i >= prompt_tiles)
    def _():
        os_ref[...] = y


def _combine_ln(x, y_sorted, pos, g, b, mp):
    m = x.shape[0]
    ms = m - mp
    tm = _tile(np.gcd(mp, ms), 256)
    nt, npt = m // tm, mp // tm
    pos3 = pos.reshape(nt, 1, 2 * tm)
    return pl.pallas_call(
        functools.partial(_combine_kernel, prompt_tiles=npt),
        grid=(nt,),
        in_specs=[pl.BlockSpec((None, 1, 2 * tm), lambda i: (i, 0, 0), memory_space=pltpu.SMEM),
                  pl.BlockSpec((None, 1, 2 * tm), lambda i: (jnp.minimum(i + 1, nt - 1), 0, 0),
                               memory_space=pltpu.SMEM),
                  pl.BlockSpec((tm, D_MODEL), lambda i: (i, 0)),
                  pl.BlockSpec(memory_space=pl.ANY),
                  pl.BlockSpec((1, D_MODEL), lambda i: (0, 0)),
                  pl.BlockSpec((1, D_MODEL), lambda i: (0, 0))],
        out_specs=[pl.BlockSpec((tm, D_MODEL), lambda i: (jnp.minimum(i, npt - 1), 0)),
                   pl.BlockSpec((tm, D_MODEL), lambda i: (jnp.maximum(i - npt, 0), 0))],
        out_shape=[jax.ShapeDtypeStruct((mp, D_MODEL), F32),
                   jax.ShapeDtypeStruct((ms, D_MODEL), F32)],
        scratch_shapes=[pltpu.VMEM((2, 2, tm, D_MODEL), F32), pltpu.SemaphoreType.DMA((2,))],
        compiler_params=_params("arbitrary"),
        name="combine_ln",
    )(pos3, pos3, x, y_sorted, g, b)


def _moe_ln(x, mp, router_w, router_b, wg, wu, wd, g, b):
    m = x.shape[0]
    tm = _tile(2 * m, 512)
    n_tiles = (2 * m) // tm + N_EXPERTS
    idx, prob, rank, counts = _router(x, router_w, router_b)
    pos, src, row_prob, tile_expert, tile_valid = _route_tables(idx, prob, rank, counts[0], tm, n_tiles)
    y_sorted = _moe_group(x, src, row_prob, tile_expert, tile_valid, wg, wu, wd, tm, n_tiles)
    return _combine_ln(x, y_sorted, pos, g, b, mp)


def kernel(x_prompt, x_sample, state_conv_a, cache_swa_k, cache_swa_v, state_conv_c, w_in, b_gate, conv_a_w, conv_a_b, ln_a_g, ln_a_b, w_branch_a, sinks, w_branch_b, conv_c_w, w_branch_c, ln_d_g, ln_d_b, sgu_w, sgu_b, w_branch_d, w_out, ln1_g, ln1_b, ffn_w_gate, ffn_w_up, ffn_w_down, router_w, router_b, exp_w_gate, exp_w_up, exp_w_down, ln2_g, ln2_b):
    nb, t_len, d = x_prompt.shape
    nseq, s_len, _ = x_sample.shape
    depth = w_in.shape[0]
    mp, ms = nb * t_len, nseq * s_len
    w = WINDOW

    x = jnp.concatenate([x_prompt.reshape(mp, d), x_sample.reshape(ms, d)], axis=0)
    xb = x.astype(BF16)

    cos_p, sin_p = _rope_tables(jnp.arange(t_len))
    cos_s, sin_s = _rope_tables(PAST_LEN + jnp.arange(s_len))

    def row(v):
        return v.reshape(1, -1)

    outs = {k: [] for k in ("pa", "pk", "pv", "pc", "sa", "sk", "sv", "sc", "sd")}
    for l in range(depth):
        w_in_b = w_in[l].astype(BF16)
        h = _in_proj(xb, w_in_b)

        wa, ba = conv_a_w[l], row(conv_a_b[l])
        lag, lab = row(ln_a_g[l]), row(ln_a_b[l])
        ya_p, yc_p, pa, pc = _ac_prompt(h, nb, t_len, wa, ba, lag, lab, conv_c_w[l])
        ya_s, yc_s, sa, sc = _ac_sample(h, mp, nseq, s_len, state_conv_a[l], state_conv_c[l],
                                        wa, ba, lag, lab, conv_c_w[l])

        yb_p, pk, pv = _attn_prompt(h, nb, t_len, sinks[l], cos_p, sin_p)
        yb_s, sk, sv = _attn_sample(h, mp, nseq, s_len, sinks[l],
                                    cache_swa_k[l].reshape(nseq, w, KV_WIDTH),
                                    cache_swa_v[l].reshape(nseq, w, KV_WIDTH), cos_s, sin_s)

        ldg, ldb = row(ln_d_g[l]), row(ln_d_b[l])
        yd_p = _d_prompt(h, mp, ldg, ldb, sgu_w[l], jnp.transpose(sgu_b[l]))
        sgu_wt = jnp.repeat(jnp.transpose(sgu_w[l][:, :s_len, :s_len], (2, 1, 0)), GROUP_D, axis=2)
        sgu_bs = jnp.repeat(jnp.transpose(sgu_b[l][:, :s_len]), GROUP_D, axis=1)
        yd_s, sd = _d_sample(h, mp, nseq, s_len, ldg, ldb, sgu_wt, sgu_bs)

        merged = _merge(xb, (ya_p, yb_p, yc_p, yd_p), (ya_s, yb_s, yc_s, yd_s), w_in_b, b_gate[l],
                        (w_branch_a[l].astype(BF16), w_branch_b[l].astype(BF16),
                         w_branch_c[l].astype(BF16), w_branch_d[l].astype(BF16)))
        x, xb = _out_ln(merged, x, w_out[l].astype(BF16), row(ln1_g[l]), row(ln1_b[l]))

        i = l // 2
        if l % 2 == 0:
            x, xb = _ffn_ln(x, xb, ffn_w_gate[i].astype(BF16), ffn_w_up[i].astype(BF16),
                            ffn_w_down[i].astype(BF16), row(ln2_g[l]), row(ln2_b[l]))
            x_p, x_s = x[:mp], x[mp:]
        else:
            x_p, x_s = _moe_ln(x, mp, router_w[i], row(router_b[i]), exp_w_gate[i].astype(BF16),
                               exp_w_up[i].astype(BF16), exp_w_down[i].astype(BF16),
                               row(ln2_g[l]), row(ln2_b[l]))
            if l + 1 < depth:
                x = jnp.concatenate([x_p, x_s], axis=0)
                xb = x.astype(BF16)

        outs["pa"].append(pa)
        outs["pk"].append(pk.reshape(nb, w, N_KV, HEAD_DIM))
        outs["pv"].append(pv.reshape(nb, w, N_KV, HEAD_DIM))
        outs["pc"].append(pc)
        outs["sa"].append(sa)
        outs["sk"].append(sk.reshape(nseq, w, N_KV, HEAD_DIM))
        outs["sv"].append(sv.reshape(nseq, w, N_KV, HEAD_DIM))
        outs["sc"].append(sc)
        outs["sd"].append(sd)

    y_prompt = x_p.reshape(nb, t_len, d)
    y_sample = x_s.reshape(nseq, s_len, d)
    st = {k: jnp.stack(v) for k, v in outs.items()}
    return (y_prompt, y_sample, st["pa"], st["pk"], st["pv"], st["pc"],
            st["sa"], st["sk"], st["sv"], st["sc"], st["sd"])
```

```python
import functools

import jax
import jax.numpy as jnp
import numpy as np
from jax import lax
from jax.experimental import pallas as pl
from jax.experimental.pallas import tpu as pltpu

D_MODEL = 2048
PAST_LEN = 8192
D_A = 512
CONV_A = 31
HEAD_DIM = 64
N_HEADS = 16
N_KV = 4
GQA = N_HEADS // N_KV
WINDOW = 128
ROPE_THETA = 10000.0
D_C = 512
CONV_C = 3
D_D = 512
CHUNK = 128
N_GROUPS_D = 4
GROUP_D = D_D // N_GROUPS_D
N_BRANCH = 4
Q_WIDTH = N_HEADS * HEAD_DIM
KV_WIDTH = N_KV * HEAD_DIM
D_FF = 5632
N_EXPERTS = 8
ALPHA = 4.0 ** 0.25
LN_EPS = 1e-5

OFF_A_VAL = 0
OFF_A_GATE = D_A
OFF_Q = 2 * D_A
OFF_K = OFF_Q + Q_WIDTH
OFF_V = OFF_K + KV_WIDTH
OFF_C_B = OFF_V + KV_WIDTH
OFF_C_C = OFF_C_B + D_C
OFF_C_X = OFF_C_C + D_C
OFF_D_U = OFF_C_X + D_C
OFF_D_V = OFF_D_U + D_D
OFF_GATES = OFF_D_V + D_D

LANES = 128
SUBLANES = 8
HALO = 32
VMEM_LIMIT = 48 * 1024 * 1024
MOE_F_STEPS = 11
MOE_ROWS = 64 * (MOE_F_STEPS - 1)

BF16 = jnp.bfloat16
F32 = jnp.float32


def _tile(n, pref):
    if n <= pref:
        return n
    for t in range(pref, 7, -1):
        if n % t == 0 and t % 8 == 0:
            return t
    return n


def _params(*sem):
    return pltpu.CompilerParams(dimension_semantics=sem, vmem_limit_bytes=VMEM_LIMIT)


def _layer_norm(x, g, b):
    mu = jnp.mean(x, axis=-1, keepdims=True)
    xc = x - mu
    var = jnp.mean(xc * xc, axis=-1, keepdims=True)
    return xc * lax.rsqrt(var + LN_EPS) * g + b


def _silu(x):
    return x * jax.nn.sigmoid(x)


def _gelu(x):
    return jax.nn.gelu(x, approximate=True)


def _mm_kernel(x_ref, w_ref, o_ref):
    o_ref[...] = jnp.dot(x_ref[...], w_ref[...], preferred_element_type=F32).astype(o_ref.dtype)


def _in_proj(xb, w_in_b):
    m, k = xb.shape
    n = OFF_GATES
    tm = _tile(m, 1024)
    tn = 1024
    return pl.pallas_call(
        _mm_kernel,
        grid=(m // tm, n // tn),
        in_specs=[pl.BlockSpec((tm, k), lambda i, j: (i, 0)),
                  pl.BlockSpec((k, tn), lambda i, j: (0, j))],
        out_specs=pl.BlockSpec((tm, tn), lambda i, j: (i, j)),
        out_shape=jax.ShapeDtypeStruct((m, n), F32),
        compiler_params=_params("parallel", "arbitrary"),
        name="in_proj",
    )(xb, w_in_b)


def _ac_prompt_kernel(av_ref, ag_ref, cb_ref, cc_ref, cx_ref,
                      avp_ref, agp_ref, ccp_ref, cxp_ref,
                      wa_ref, ba_ref, lg_ref, lb_ref, wc_ref,
                      ya_ref, yc_ref, ha_ref, hc_ref, ga_s, gc_s):
    t = pl.program_id(1)
    tt = av_ref.shape[0]
    first = t == 0
    glu_prev = avp_ref[...] * jax.nn.sigmoid(agp_ref[...])
    ga_s[0:HALO, :] = jnp.where(first, 0.0, glu_prev)
    ga_s[HALO:HALO + tt, :] = av_ref[...] * jax.nn.sigmoid(ag_ref[...])
    acc = jnp.zeros((tt, D_A), F32) + ba_ref[...]
    base = HALO - (CONV_A - 1)
    for j in range(CONV_A):
        acc = acc + wa_ref[j:j + 1, :] * ga_s[base + j:base + j + tt, :]
    ya = _silu(_layer_norm(acc, lg_ref[...], lb_ref[...]))
    ya_ref[...] = ya.astype(ya_ref.dtype)
    ha_ref[0] = ga_s[HALO + tt - (CONV_A - 1):HALO + tt, :]

    gc_s[0:HALO, :] = jnp.where(first, 0.0, ccp_ref[...] * cxp_ref[...])
    gc_s[HALO:HALO + tt, :] = cc_ref[...] * cx_ref[...]
    base_c = HALO - (CONV_C - 1)
    yc = jnp.zeros((tt, D_C), F32)
    for j in range(CONV_C):
        yc = yc + wc_ref[j:j + 1, :] * gc_s[base_c + j:base_c + j + tt, :]
    yc_ref[...] = (cb_ref[...] * yc).astype(yc_ref.dtype)
    hc_ref[0] = gc_s[HALO + tt - (CONV_C - 1):HALO + tt, :]


def _ac_prompt(h, nb, t_len, wa, ba, lg, lb, wc):
    tt = _tile(t_len, 256)
    nt = t_len // tt
    cw = D_A

    def cur(col):
        return pl.BlockSpec((tt, cw), lambda b, t: (b * nt + t, col // cw))

    def prev(col):
        return pl.BlockSpec(
            (HALO, cw),
            lambda b, t: (jnp.maximum((b * t_len + t * tt) // HALO - 1, 0), col // cw))

    def full(shape):
        return pl.BlockSpec(shape, lambda b, t: (0,) * len(shape))

    mp = nb * t_len
    return pl.pallas_call(
        _ac_prompt_kernel,
        grid=(nb, nt),
        in_specs=[cur(OFF_A_VAL), cur(OFF_A_GATE), cur(OFF_C_B), cur(OFF_C_C), cur(OFF_C_X),
                  prev(OFF_A_VAL), prev(OFF_A_GATE), prev(OFF_C_C), prev(OFF_C_X),
                  full((CONV_A, D_A)), full((1, D_A)), full((1, D_A)), full((1, D_A)),
                  full((CONV_C, D_C))],
        out_specs=[pl.BlockSpec((tt, D_A), lambda b, t: (b * nt + t, 0)),
                   pl.BlockSpec((tt, D_C), lambda b, t: (b * nt + t, 0)),
                   pl.BlockSpec((1, CONV_A - 1, D_A), lambda b, t: (b, 0, 0)),
                   pl.BlockSpec((1, CONV_C - 1, D_C), lambda b, t: (b, 0, 0))],
        out_shape=[jax.ShapeDtypeStruct((mp, D_A), BF16),
                   jax.ShapeDtypeStruct((mp, D_C), BF16),
                   jax.ShapeDtypeStruct((nb, CONV_A - 1, D_A), F32),
                   jax.ShapeDtypeStruct((nb, CONV_C - 1, D_C), F32)],
        scratch_shapes=[pltpu.VMEM((HALO + tt, D_A), F32), pltpu.VMEM((HALO + tt, D_C), F32)],
        compiler_params=_params("parallel", "arbitrary"),
        name="ac_prompt",
    )(h, h, h, h, h, h, h, h, h, wa, ba, lg, lb, wc)


def _ac_sample_kernel(av_ref, ag_ref, cb_ref, cc_ref, cx_ref, hista_ref, histc_ref,
                      wa_ref, ba_ref, lg_ref, lb_ref, wc_ref,
                      ya_ref, yc_ref, ha_ref, hc_ref, xa_s, xc_s):
    sb, s_len = hista_ref.shape[0], av_ref.shape[0] // hista_ref.shape[0]
    ka, kc = CONV_A - 1, CONV_C - 1
    glu = av_ref[...] * jax.nn.sigmoid(ag_ref[...])
    xa_s[:, 0:ka, :] = hista_ref[...]
    xa_s[:, ka:ka + s_len, :] = glu.reshape(sb, s_len, D_A)
    acc = jnp.zeros((sb, s_len, D_A), F32) + ba_ref[...][None]
    for j in range(CONV_A):
        acc = acc + wa_ref[j:j + 1, :][None] * xa_s[:, j:j + s_len, :]
    ya = _silu(_layer_norm(acc, lg_ref[...][None], lb_ref[...][None]))
    ya_ref[...] = ya.reshape(sb * s_len, D_A).astype(ya_ref.dtype)
    ha_ref[...] = xa_s[:, s_len:s_len + ka, :]

    xc_s[:, 0:kc, :] = histc_ref[...]
    xc_s[:, kc:kc + s_len, :] = (cc_ref[...] * cx_ref[...]).reshape(sb, s_len, D_C)
    yc = jnp.zeros((sb, s_len, D_C), F32)
    for j in range(CONV_C):
        yc = yc + wc_ref[j:j + 1, :][None] * xc_s[:, j:j + s_len, :]
    yc = cb_ref[...] * yc.reshape(sb * s_len, D_C)
    yc_ref[...] = yc.astype(yc_ref.dtype)
    hc_ref[...] = xc_s[:, s_len:s_len + kc, :]


def _ac_sample(h, row0, nseq, s_len, hist_a, hist_c, wa, ba, lg, lb, wc):
    sb = _tile(nseq, 16)
    rows = sb * s_len
    blk0 = row0 // rows
    cw = D_A

    def cur(col):
        return pl.BlockSpec((rows, cw), lambda i: (blk0 + i, col // cw))

    def full(shape):
        return pl.BlockSpec(shape, lambda i: (0,) * len(shape))

    ms = nseq * s_len
    ka, kc = CONV_A - 1, CONV_C - 1
    return pl.pallas_call(
        _ac_sample_kernel,
        grid=(nseq // sb,),
        in_specs=[cur(OFF_A_VAL), cur(OFF_A_GATE), cur(OFF_C_B), cur(OFF_C_C), cur(OFF_C_X),
                  pl.BlockSpec((sb, ka, D_A), lambda i: (i, 0, 0)),
                  pl.BlockSpec((sb, kc, D_C), lambda i: (i, 0, 0)),
                  full((CONV_A, D_A)), full((1, D_A)), full((1, D_A)), full((1, D_A)),
                  full((CONV_C, D_C))],
        out_specs=[pl.BlockSpec((rows, D_A), lambda i: (i, 0)),
                   pl.BlockSpec((rows, D_C), lambda i: (i, 0)),
                   pl.BlockSpec((sb, ka, D_A), lambda i: (i, 0, 0)),
                   pl.BlockSpec((sb, kc, D_C), lambda i: (i, 0, 0))],
        out_shape=[jax.ShapeDtypeStruct((ms, D_A), BF16),
                   jax.ShapeDtypeStruct((ms, D_C), BF16),
                   jax.ShapeDtypeStruct((nseq, ka, D_A), F32),
                   jax.ShapeDtypeStruct((nseq, kc, D_C), F32)],
        scratch_shapes=[pltpu.VMEM((sb, ka + s_len + 2, D_A), F32),
                        pltpu.VMEM((sb, kc + s_len + 6, D_C), F32)],
        compiler_params=_params("parallel"),
        name="ac_sample",
    )(h, h, h, h, h, hist_a, hist_c, wa, ba, lg, lb, wc)


def _rope_tables(pos):
    half = HEAD_DIM // 2
    inv_freq = jnp.power(ROPE_THETA, -jnp.arange(half, dtype=F32) * (2.0 / HEAD_DIM))
    ang = pos.astype(F32)[:, None] * inv_freq[None, :]
    cos, sin = jnp.cos(ang), jnp.sin(ang)
    cos_t = jnp.concatenate([cos, cos, cos, cos], axis=1)
    sin_t = jnp.concatenate([-sin, sin, -sin, sin], axis=1)
    return cos_t, sin_t


def _rope(x, cos_t, sin_t):
    half = HEAD_DIM // 2
    axis = x.ndim - 1
    shape = x.shape[:-1] + (LANES,)
    lane = lax.broadcasted_iota(jnp.int32, shape, axis)
    first = (lane % HEAD_DIM) < half
    out = []
    for i in range(x.shape[-1] // LANES):
        xi = x[..., LANES * i:LANES * (i + 1)]
        partner = jnp.where(first, pltpu.roll(xi, LANES - half, axis), pltpu.roll(xi, half, axis))
        out.append(xi * cos_t + partner * sin_t)
    return out


def _head_halves(t, axis, lo_valid):
    lane = lax.broadcasted_iota(jnp.int32, t.shape, axis)
    if lo_valid:
        lo = jnp.where(lane < HEAD_DIM, t, jnp.zeros_like(t))
        hi = pltpu.roll(lo, HEAD_DIM, axis)
    else:
        hi = jnp.where(lane >= HEAD_DIM, t, jnp.zeros_like(t))
        lo = pltpu.roll(hi, HEAD_DIM, axis)
    return lo, hi


def _attn_prompt_kernel(sinks_ref, q_ref, kc_ref, vc_ref, kp_ref, vp_ref,
                        cos_ref, sin_ref, cosp_ref, sinp_ref,
                        yb_ref, nk_ref, nv_ref):
    n = pl.program_id(1)
    w = WINDOW
    cos_c, sin_c = cos_ref[...], sin_ref[...]
    q_tiles = _rope(q_ref[...] * (HEAD_DIM ** -0.5), cos_c, sin_c)
    kc_tiles = _rope(kc_ref[...], cos_c, sin_c)
    kp_tiles = _rope(kp_ref[...], cosp_ref[...], sinp_ref[...])
    for i in range(KV_WIDTH // LANES):
        nk_ref[0, :, LANES * i:LANES * (i + 1)] = kc_tiles[i]
    nv_ref[0] = vc_ref[...]

    qi = lax.broadcasted_iota(jnp.int32, (w, 2 * w), 0)
    ci = lax.broadcasted_iota(jnp.int32, (w, 2 * w), 1)
    mask = (ci > qi) & (ci <= qi + w) & ((n > 0) | (ci >= w))
    mask2 = jnp.concatenate([mask, mask], axis=0)
    row_top = lax.broadcasted_iota(jnp.int32, (2 * w, 1), 0) < w

    kt_tiles = [jnp.concatenate([kp_tiles[t], kc_tiles[t]], axis=0).T.astype(BF16)
                for t in range(KV_WIDTH // LANES)]
    zeros_t = jnp.zeros((HEAD_DIM, 2 * w), BF16)

    for h in range(N_KV):
        tile, lo_valid = h // 2, (h % 2 == 0)
        kt = kt_tiles[tile][0:HEAD_DIM] if lo_valid else kt_tiles[tile][HEAD_DIM:2 * HEAD_DIM]
        kt_lo = jnp.concatenate([kt, zeros_t], axis=0)
        kt_hi = jnp.concatenate([zeros_t, kt], axis=0)
        vcat = jnp.concatenate([vp_ref[:, LANES * tile:LANES * (tile + 1)],
                                vc_ref[:, LANES * tile:LANES * (tile + 1)]], axis=0)
        v_lo, v_hi = _head_halves(vcat.astype(BF16), 1, lo_valid)
        qs = jnp.concatenate([q_tiles[2 * h], q_tiles[2 * h + 1]], axis=0).astype(BF16)
        probs = []
        for half_idx, kt_half in enumerate((kt_lo, kt_hi)):
            s = jnp.dot(qs, kt_half, preferred_element_type=F32)
            s = jnp.where(mask2, s, -jnp.inf)
            sk = jnp.where(row_top, sinks_ref[4 * h + half_idx], sinks_ref[4 * h + 2 + half_idx])
            m = jnp.maximum(jnp.max(s, axis=1, keepdims=True), sk)
            e = jnp.exp(s - m)
            den = jnp.sum(e, axis=1, keepdims=True) + jnp.exp(sk - m)
            probs.append((e / den).astype(BF16))
        o = (jnp.dot(probs[0], v_lo, preferred_element_type=F32)
             + jnp.dot(probs[1], v_hi, preferred_element_type=F32))
        yb_ref[:, LANES * (2 * h):LANES * (2 * h + 1)] = o[0:w].astype(yb_ref.dtype)
        yb_ref[:, LANES * (2 * h + 1):LANES * (2 * h + 2)] = o[w:2 * w].astype(yb_ref.dtype)


def _attn_prompt(h, nb, t_len, sinks, cos_t, sin_t):
    w = WINDOW
    nblk = t_len // w
    mp = nb * t_len

    def rows(b, n):
        return b * nblk + n

    def rows_prev(b, n):
        return b * nblk + jnp.maximum(n - 1, 0)

    return pl.pallas_call(
        _attn_prompt_kernel,
        grid=(nb, nblk),
        in_specs=[pl.BlockSpec(memory_space=pltpu.SMEM),
                  pl.BlockSpec((w, Q_WIDTH), lambda b, n: (rows(b, n), OFF_Q // Q_WIDTH)),
                  pl.BlockSpec((w, KV_WIDTH), lambda b, n: (rows(b, n), OFF_K // KV_WIDTH)),
                  pl.BlockSpec((w, KV_WIDTH), lambda b, n: (rows(b, n), OFF_V // KV_WIDTH)),
                  pl.BlockSpec((w, KV_WIDTH), lambda b, n: (rows_prev(b, n), OFF_K // KV_WIDTH)),
                  pl.BlockSpec((w, KV_WIDTH), lambda b, n: (rows_prev(b, n), OFF_V // KV_WIDTH)),
                  pl.BlockSpec((w, LANES), lambda b, n: (n, 0)),
                  pl.BlockSpec((w, LANES), lambda b, n: (n, 0)),
                  pl.BlockSpec((w, LANES), lambda b, n: (jnp.maximum(n - 1, 0), 0)),
                  pl.BlockSpec((w, LANES), lambda b, n: (jnp.maximum(n - 1, 0), 0))],
        out_specs=[pl.BlockSpec((w, Q_WIDTH), lambda b, n: (rows(b, n), 0)),
                   pl.BlockSpec((1, w, KV_WIDTH), lambda b, n: (b, 0, 0)),
                   pl.BlockSpec((1, w, KV_WIDTH), lambda b, n: (b, 0, 0))],
        out_shape=[jax.ShapeDtypeStruct((mp, Q_WIDTH), BF16),
                   jax.ShapeDtypeStruct((nb, w, KV_WIDTH), F32),
                   jax.ShapeDtypeStruct((nb, w, KV_WIDTH), F32)],
        compiler_params=_params("parallel", "arbitrary"),
        name="attn_prompt",
    )(sinks, h, h, h, h, h, cos_t, sin_t, cos_t, sin_t)


def _attn_sample_kernel(sinks_ref, q_ref, kn_ref, vn_ref, kbuf_ref, vbuf_ref, cos_ref, sin_ref,
                        yb_ref, nk_ref, nv_ref):
    sb, w = kbuf_ref.shape[0], kbuf_ref.shape[1]
    s_len = q_ref.shape[0] // sb
    cos_t, sin_t = cos_ref[...], sin_ref[...]
    q_tiles = _rope(q_ref[...] * (HEAD_DIM ** -0.5), cos_t, sin_t)
    kn_tiles = _rope(kn_ref[...], cos_t, sin_t)
    nk_ref[:, 0:w - s_len, :] = kbuf_ref[:, s_len:w, :]
    nv_ref[:, 0:w - s_len, :] = vbuf_ref[:, s_len:w, :]
    for i in range(KV_WIDTH // LANES):
        nk_ref[:, w - s_len:w, LANES * i:LANES * (i + 1)] = kn_tiles[i].reshape(sb, s_len, LANES)
    nv_ref[:, w - s_len:w, :] = vn_ref[...].reshape(sb, s_len, KV_WIDTH)

    qi = lax.broadcasted_iota(jnp.int32, (2 * s_len, w), 0) % s_len
    ci = lax.broadcasted_iota(jnp.int32, (2 * s_len, w), 1)
    mask_buf = (ci > qi)[None]
    qn = lax.broadcasted_iota(jnp.int32, (2 * s_len, s_len), 0) % s_len
    cn = lax.broadcasted_iota(jnp.int32, (2 * s_len, s_len), 1)
    mask_new = (cn <= qn)[None]
    row_top = (lax.broadcasted_iota(jnp.int32, (2 * s_len, 1), 0) < s_len)[None]
    bqk = (((2,), (2,)), ((0,), (0,)))
    bkd = (((2,), (1,)), ((0,), (0,)))

    for h in range(N_KV):
        tile, lo_valid = h // 2, (h % 2 == 0)
        sl = slice(LANES * tile, LANES * (tile + 1))
        kb_lo, kb_hi = _head_halves(kbuf_ref[:, :, sl].astype(BF16), 2, lo_valid)
        vb_lo, vb_hi = _head_halves(vbuf_ref[:, :, sl].astype(BF16), 2, lo_valid)
        kn3 = kn_tiles[tile].reshape(sb, s_len, LANES).astype(BF16)
        vn3 = vn_ref[:, sl].reshape(sb, s_len, LANES).astype(BF16)
        kn_lo, kn_hi = _head_halves(kn3, 2, lo_valid)
        vn_lo, vn_hi = _head_halves(vn3, 2, lo_valid)
        qs = jnp.concatenate([q_tiles[2 * h].reshape(sb, s_len, LANES),
                              q_tiles[2 * h + 1].reshape(sb, s_len, LANES)], axis=1).astype(BF16)
        o = jnp.zeros((sb, 2 * s_len, LANES), F32)
        for half_idx, (kb, kn, vb, vn) in enumerate(((kb_lo, kn_lo, vb_lo, vn_lo),
                                                     (kb_hi, kn_hi, vb_hi, vn_hi))):
            s_b = lax.dot_general(qs, kb, bqk, preferred_element_type=F32)
            s_n = lax.dot_general(qs, kn, bqk, preferred_element_type=F32)
            s_b = jnp.where(mask_buf, s_b, -jnp.inf)
            s_n = jnp.where(mask_new, s_n, -jnp.inf)
            sk = jnp.where(row_top, sinks_ref[4 * h + half_idx], sinks_ref[4 * h + 2 + half_idx])
            m = jnp.maximum(jnp.maximum(jnp.max(s_b, axis=2, keepdims=True),
                                        jnp.max(s_n, axis=2, keepdims=True)), sk)
            e_b = jnp.exp(s_b - m)
            e_n = jnp.exp(s_n - m)
            den = (jnp.sum(e_b, axis=2, keepdims=True) + jnp.sum(e_n, axis=2, keepdims=True)
                   + jnp.exp(sk - m))
            o = o + lax.dot_general((e_b / den).astype(BF16), vb, bkd, preferred_element_type=F32)
            o = o + lax.dot_general((e_n / den).astype(BF16), vn, bkd, preferred_element_type=F32)
        yb_ref[:, LANES * (2 * h):LANES * (2 * h + 1)] = (
            o[:, 0:s_len, :].reshape(sb * s_len, LANES).astype(yb_ref.dtype))
        yb_ref[:, LANES * (2 * h + 1):LANES * (2 * h + 2)] = (
            o[:, s_len:2 * s_len, :].reshape(sb * s_len, LANES).astype(yb_ref.dtype))


def _attn_sample(h, row0, nseq, s_len, sinks, k_buf, v_buf, cos_t, sin_t):
    sb = _tile(nseq, 8)
    rows = sb * s_len
    blk0 = row0 // rows
    w = k_buf.shape[1]
    ms = nseq * s_len
    cos_rows = jnp.tile(cos_t, (sb, 1))
    sin_rows = jnp.tile(sin_t, (sb, 1))
    return pl.pallas_call(
        _attn_sample_kernel,
        grid=(nseq // sb,),
        in_specs=[pl.BlockSpec(memory_space=pltpu.SMEM),
                  pl.BlockSpec((rows, Q_WIDTH), lambda i: (blk0 + i, OFF_Q // Q_WIDTH)),
                  pl.BlockSpec((rows, KV_WIDTH), lambda i: (blk0 + i, OFF_K // KV_WIDTH)),
                  pl.BlockSpec((rows, KV_WIDTH), lambda i: (blk0 + i, OFF_V // KV_WIDTH)),
                  pl.BlockSpec((sb, w, KV_WIDTH), lambda i: (i, 0, 0)),
                  pl.BlockSpec((sb, w, KV_WIDTH), lambda i: (i, 0, 0)),
                  pl.BlockSpec((rows, LANES), lambda i: (0, 0)),
                  pl.BlockSpec((rows, LANES), lambda i: (0, 0))],
        out_specs=[pl.BlockSpec((rows, Q_WIDTH), lambda i: (i, 0)),
                   pl.BlockSpec((sb, w, KV_WIDTH), lambda i: (i, 0, 0)),
                   pl.BlockSpec((sb, w, KV_WIDTH), lambda i: (i, 0, 0))],
        out_shape=[jax.ShapeDtypeStruct((ms, Q_WIDTH), BF16),
                   jax.ShapeDtypeStruct((nseq, w, KV_WIDTH), F32),
                   jax.ShapeDtypeStruct((nseq, w, KV_WIDTH), F32)],
        compiler_params=_params("parallel"),
        name="attn_sample",
    )(sinks, h, h, h, k_buf, v_buf, cos_rows, sin_rows)


def _d_prompt_kernel(du_ref, dv_ref, lg_ref, lb_ref, w_ref, bt_ref, yd_ref):
    u = _gelu(du_ref[...])
    vn = _layer_norm(_gelu(dv_ref[...]), lg_ref[...], lb_ref[...]).astype(BF16)
    c = du_ref.shape[0]
    ri = lax.broadcasted_iota(jnp.int32, (c, c), 0)
    cj = lax.broadcasted_iota(jnp.int32, (c, c), 1)
    causal = cj <= ri
    for g in range(N_GROUPS_D):
        sl = slice(GROUP_D * g, GROUP_D * (g + 1))
        wg = jnp.where(causal, w_ref[g], 0.0).astype(BF16)
        s = jnp.dot(wg, vn[:, sl], preferred_element_type=F32) + bt_ref[:, g:g + 1]
        yd_ref[:, sl] = (u[:, sl] * s).astype(yd_ref.dtype)


def _d_prompt(h, mp, ln_g, ln_b, sgu_w, sgu_bt):
    c = CHUNK
    return pl.pallas_call(
        _d_prompt_kernel,
        grid=(mp // c,),
        in_specs=[pl.BlockSpec((c, D_D), lambda i: (i, OFF_D_U // D_D)),
                  pl.BlockSpec((c, D_D), lambda i: (i, OFF_D_V // D_D)),
                  pl.BlockSpec((1, D_D), lambda i: (0, 0)),
                  pl.BlockSpec((1, D_D), lambda i: (0, 0)),
                  pl.BlockSpec((N_GROUPS_D, c, c), lambda i: (0, 0, 0)),
                  pl.BlockSpec((c, N_GROUPS_D), lambda i: (0, 0))],
        out_specs=pl.BlockSpec((c, D_D), lambda i: (i, 0)),
        out_shape=jax.ShapeDtypeStruct((mp, D_D), BF16),
        compiler_params=_params("parallel"),
        name="d_prompt",
    )(h, h, ln_g, ln_b, sgu_w, sgu_bt)


def _d_sample_kernel(du_ref, dv_ref, lg_ref, lb_ref, wt_ref, bt_ref, yd_ref, vd_ref):
    sb, s_len = vd_ref.shape[0], vd_ref.shape[1]
    u = _gelu(du_ref[...])
    vn = _layer_norm(_gelu(dv_ref[...]), lg_ref[...], lb_ref[...])
    vn3 = vn.reshape(sb, s_len, D_D)
    vd_ref[...] = vn3
    ii = lax.broadcasted_iota(jnp.int32, (s_len, D_D), 0)
    s = jnp.zeros((sb, s_len, D_D), F32) + bt_ref[...][None]
    for j in range(s_len):
        wj = jnp.where(ii >= j, wt_ref[j], 0.0)
        s = s + wj[None] * vn3[:, j:j + 1, :]
    yd_ref[...] = (u * s.reshape(sb * s_len, D_D)).astype(yd_ref.dtype)


def _d_sample(h, row0, nseq, s_len, ln_g, ln_b, sgu_wt, sgu_bs):
    sb = _tile(nseq, 16)
    rows = sb * s_len
    blk0 = row0 // rows
    ms = nseq * s_len
    return pl.pallas_call(
        _d_sample_kernel,
        grid=(nseq // sb,),
        in_specs=[pl.BlockSpec((rows, D_D), lambda i: (blk0 + i, OFF_D_U // D_D)),
                  pl.BlockSpec((rows, D_D), lambda i: (blk0 + i, OFF_D_V // D_D)),
                  pl.BlockSpec((1, D_D), lambda i: (0, 0)),
                  pl.BlockSpec((1, D_D), lambda i: (0, 0)),
                  pl.BlockSpec((s_len, s_len, D_D), lambda i: (0, 0, 0)),
                  pl.BlockSpec((s_len, D_D), lambda i: (0, 0))],
        out_specs=[pl.BlockSpec((rows, D_D), lambda i: (i, 0)),
                   pl.BlockSpec((sb, s_len, D_D), lambda i: (i, 0, 0))],
        out_shape=[jax.ShapeDtypeStruct((ms, D_D), BF16),
                   jax.ShapeDtypeStruct((nseq, s_len, D_D), F32)],
        compiler_params=_params("parallel"),
        name="d_sample",
    )(h, h, ln_g, ln_b, sgu_wt, sgu_bs)


def _merge_kernel(x_ref, yap_ref, ybp_ref, ycp_ref, ydp_ref, yas_ref, ybs_ref, ycs_ref, yds_ref,
                  g0_ref, g1_ref, g2_ref, g3_ref, bg_ref, pa_ref, pb_ref, pc_ref, pd_ref, o_ref,
                  *, prompt_tiles):
    x = x_ref[...]
    is_prompt = pl.program_id(1) < prompt_tiles
    acc = None
    branches = ((yap_ref, yas_ref, g0_ref, pa_ref), (ybp_ref, ybs_ref, g1_ref, pb_ref),
                (ycp_ref, ycs_ref, g2_ref, pc_ref), (ydp_ref, yds_ref, g3_ref, pd_ref))
    for i, (yp_ref, ys_ref, g_ref, p_ref) in enumerate(branches):
        y = jnp.where(is_prompt, yp_ref[...], ys_ref[...])
        gate = jax.nn.sigmoid(jnp.dot(x, g_ref[...], preferred_element_type=F32) + bg_ref[i:i + 1, :])
        term = gate * jnp.dot(y, p_ref[...], preferred_element_type=F32)
        acc = term if acc is None else acc + term
    o_ref[...] = acc.astype(o_ref.dtype)


def _merge(xb, y_prompt, y_sample, w_in_b, b_gate, projs):
    m = xb.shape[0]
    mp, ms = y_prompt[0].shape[0], y_sample[0].shape[0]
    tm = _tile(np.gcd(mp, ms), 512)
    npt = mp // tm
    tn = 512
    nn = D_MODEL // tn
    widths = (D_A, Q_WIDTH, D_C, D_D)

    def act_p(width):
        return pl.BlockSpec((tm, width), lambda j, i: (jnp.minimum(i, npt - 1), 0))

    def act_s(width):
        return pl.BlockSpec((tm, width), lambda j, i: (jnp.maximum(i - npt, 0), 0))

    def gate_w(b):
        return pl.BlockSpec((D_MODEL, tn), lambda j, i: (0, (OFF_GATES + b * D_MODEL) // tn + j))

    def proj_w(k):
        return pl.BlockSpec((k, tn), lambda j, i: (0, j))

    return pl.pallas_call(
        functools.partial(_merge_kernel, prompt_tiles=npt),
        grid=(nn, m // tm),
        in_specs=[pl.BlockSpec((tm, D_MODEL), lambda j, i: (i, 0))]
                 + [act_p(wd) for wd in widths] + [act_s(wd) for wd in widths]
                 + [gate_w(0), gate_w(1), gate_w(2), gate_w(3),
                    pl.BlockSpec((N_BRANCH, tn), lambda j, i: (0, j))]
                 + [proj_w(wd) for wd in widths],
        out_specs=pl.BlockSpec((tm, tn), lambda j, i: (i, j)),
        out_shape=jax.ShapeDtypeStruct((m, D_MODEL), BF16),
        compiler_params=_params("parallel", "arbitrary"),
        name="merge",
    )(xb, *y_prompt, *y_sample, w_in_b, w_in_b, w_in_b, w_in_b, b_gate, *projs)


def _out_ln_kernel(mg_ref, x_ref, w_ref, g_ref, b_ref, o_ref, ob_ref):
    y = ALPHA * x_ref[...] + jnp.dot(mg_ref[...], w_ref[...], preferred_element_type=F32)
    y = _layer_norm(y, g_ref[...], b_ref[...])
    o_ref[...] = y
    ob_ref[...] = y.astype(ob_ref.dtype)


def _out_ln(merged, x, w_out_b, g, b):
    m = x.shape[0]
    tm = _tile(m, 512)
    return pl.pallas_call(
        _out_ln_kernel,
        grid=(m // tm,),
        in_specs=[pl.BlockSpec((tm, D_MODEL), lambda i: (i, 0)),
                  pl.BlockSpec((tm, D_MODEL), lambda i: (i, 0)),
                  pl.BlockSpec((D_MODEL, D_MODEL), lambda i: (0, 0)),
                  pl.BlockSpec((1, D_MODEL), lambda i: (0, 0)),
                  pl.BlockSpec((1, D_MODEL), lambda i: (0, 0))],
        out_specs=[pl.BlockSpec((tm, D_MODEL), lambda i: (i, 0)),
                   pl.BlockSpec((tm, D_MODEL), lambda i: (i, 0))],
        out_shape=[jax.ShapeDtypeStruct((m, D_MODEL), F32),
                   jax.ShapeDtypeStruct((m, D_MODEL), BF16)],
        compiler_params=_params("parallel"),
        name="out_ln",
    )(merged, x, w_out_b, g, b)


def _ffn_kernel(x_ref, xb_ref, wg_ref, wu_ref, wd_ref, g_ref, b_ref, o_ref, ob_ref, acc_s):
    f = pl.program_id(1)

    @pl.when(f == 0)
    def _():
        acc_s[...] = jnp.zeros_like(acc_s)

    xb = xb_ref[...]
    hid = (_silu(jnp.dot(xb, wg_ref[...], preferred_element_type=F32))
           * jnp.dot(xb, wu_ref[...], preferred_element_type=F32))
    acc_s[...] += jnp.dot(hid.astype(BF16), wd_ref[...], preferred_element_type=F32)

    @pl.when(f == pl.num_programs(1) - 1)
    def _():
        y = _layer_norm(ALPHA * x_ref[...] + acc_s[...], g_ref[...], b_ref[...])
        o_ref[...] = y
        ob_ref[...] = y.astype(ob_ref.dtype)


def _ffn_ln(x, xb, wg, wu, wd, g, b):
    m = x.shape[0]
    tm = _tile(m, 512)
    tf = 512
    return pl.pallas_call(
        _ffn_kernel,
        grid=(m // tm, D_FF // tf),
        in_specs=[pl.BlockSpec((tm, D_MODEL), lambda i, f: (i, 0)),
                  pl.BlockSpec((tm, D_MODEL), lambda i, f: (i, 0)),
                  pl.BlockSpec((D_MODEL, tf), lambda i, f: (0, f)),
                  pl.BlockSpec((D_MODEL, tf), lambda i, f: (0, f)),
                  pl.BlockSpec((tf, D_MODEL), lambda i, f: (f, 0)),
                  pl.BlockSpec((1, D_MODEL), lambda i, f: (0, 0)),
                  pl.BlockSpec((1, D_MODEL), lambda i, f: (0, 0))],
        out_specs=[pl.BlockSpec((tm, D_MODEL), lambda i, f: (i, 0)),
                   pl.BlockSpec((tm, D_MODEL), lambda i, f: (i, 0))],
        out_shape=[jax.ShapeDtypeStruct((m, D_MODEL), F32),
                   jax.ShapeDtypeStruct((m, D_MODEL), BF16)],
        scratch_shapes=[pltpu.VMEM((tm, D_MODEL), F32)],
        compiler_params=_params("parallel", "arbitrary"),
        name="ffn_ln",
    )(x, xb, wg, wu, wd, g, b)


def _router_kernel(x_ref, w_ref, b_ref, idx_ref, prob_ref, rank_ref, cnt_ref, run_s):
    @pl.when(pl.program_id(0) == 0)
    def _():
        run_s[...] = jnp.zeros_like(run_s)

    logits = jnp.dot(x_ref[...], w_ref[...], preferred_element_type=F32,
                     precision=lax.Precision.HIGHEST) + b_ref[...]
    tm = logits.shape[0]
    idx = lax.broadcasted_iota(jnp.int32, logits.shape, 1)
    v1 = jnp.max(logits, axis=1, keepdims=True)
    i1 = jnp.min(jnp.where(logits == v1, idx, N_EXPERTS), axis=1, keepdims=True)
    rest = jnp.where(idx == i1, -jnp.inf, logits)
    v2 = jnp.max(rest, axis=1, keepdims=True)
    i2 = jnp.min(jnp.where(rest == v2, idx, N_EXPERTS), axis=1, keepdims=True)
    e2 = jnp.exp(v2 - v1)
    den = 1.0 + e2

    hit = ((idx == i1) | (idx == i2)).astype(F32)
    ri = lax.broadcasted_iota(jnp.int32, (tm, tm), 0)
    ci = lax.broadcasted_iota(jnp.int32, (tm, tm), 1)
    before = jnp.dot((ci < ri).astype(BF16), hit.astype(BF16), preferred_element_type=F32) + run_s[...]
    rank1 = jnp.sum(jnp.where(idx == i1, before, 0.0), axis=1, keepdims=True)
    rank2 = jnp.sum(jnp.where(idx == i2, before, 0.0), axis=1, keepdims=True)

    col = lax.broadcasted_iota(jnp.int32, (tm, 2), 1)
    idx_ref[...] = jnp.where(col == 0, i1, i2)
    prob_ref[...] = jnp.where(col == 0, 1.0 / den, e2 / den)
    rank_ref[...] = jnp.where(col == 0, rank1, rank2).astype(jnp.int32)
    total = run_s[...] + jnp.sum(hit, axis=0, keepdims=True)
    run_s[...] = total
    cnt_ref[...] = total.astype(jnp.int32)


def _router(x, router_w, router_b):
    m = x.shape[0]
    tm = _tile(m, 512)
    return pl.pallas_call(
        _router_kernel,
        grid=(m // tm,),
        in_specs=[pl.BlockSpec((tm, D_MODEL), lambda i: (i, 0)),
                  pl.BlockSpec((D_MODEL, N_EXPERTS), lambda i: (0, 0)),
                  pl.BlockSpec((1, N_EXPERTS), lambda i: (0, 0))],
        out_specs=[pl.BlockSpec((tm, 2), lambda i: (i, 0)),
                   pl.BlockSpec((tm, 2), lambda i: (i, 0)),
                   pl.BlockSpec((tm, 2), lambda i: (i, 0)),
                   pl.BlockSpec((1, N_EXPERTS), lambda i: (0, 0))],
        out_shape=[jax.ShapeDtypeStruct((m, 2), jnp.int32),
                   jax.ShapeDtypeStruct((m, 2), F32),
                   jax.ShapeDtypeStruct((m, 2), jnp.int32),
                   jax.ShapeDtypeStruct((1, N_EXPERTS), jnp.int32)],
        scratch_shapes=[pltpu.VMEM((1, N_EXPERTS), F32)],
        compiler_params=_params("arbitrary"),
        name="router",
    )(x, router_w, router_b)


def _route_tables(idx, rank, counts, tm, n_tiles):
    m = idx.shape[0]
    padded = ((counts + tm - 1) // tm) * tm
    ends = jnp.cumsum(padded)
    base = ends - padded
    pos = base[idx] + rank
    n_valid = ends[-1] // tm
    tiles = jnp.arange(n_tiles, dtype=jnp.int32)
    tile_valid = (tiles < n_valid).astype(jnp.int32)
    owner = jnp.searchsorted(ends, jnp.minimum(tiles, n_valid - 1) * tm, side="right")
    tile_expert = jnp.minimum(owner, N_EXPERTS - 1).astype(jnp.int32)
    flat = pos.reshape(-1)
    token = jnp.repeat(jnp.arange(m, dtype=jnp.int32), 2)
    src = jnp.zeros((n_tiles * tm,), jnp.int32).at[flat].set(token, unique_indices=True)
    return pos, src, tile_expert, tile_valid


def _row_copy(src_hbm, row, dst_vmem, dst_row, sem):
    return pltpu.make_async_copy(src_hbm.at[pl.ds(row, 1), :], dst_vmem.at[pl.ds(dst_row, 1), :], sem)


def _moe_group_kernel(te_ref, tv_ref, src_ref, src_next_ref, x_hbm, wg_ref, wu_ref, wd_ref,
                      o_ref, xg_s, xb_s, sem):
    i, f = pl.program_id(0), pl.program_id(1)
    nf = pl.num_programs(1)
    tm = xb_s.shape[0]
    chunk = tm // (MOE_F_STEPS - 1)
    slot = i % 2
    valid = tv_ref[i] == 1

    @pl.when((i == 0) & (f == 0))
    def _():
        def body(r, c):
            _row_copy(x_hbm, src_ref[0, r], xg_s.at[0], r, sem.at[0]).start()
            return c
        lax.fori_loop(0, tm, body, 0, unroll=8)

    @pl.when(f == 0)
    def _():
        o_ref[...] = jnp.zeros_like(o_ref)

    @pl.when((f == 0) & ((i == 0) | (tv_ref[jnp.maximum(i - 1, 0)] == 1)))
    def _():
        pltpu.make_async_copy(xg_s.at[slot], xg_s.at[slot], sem.at[slot]).wait()
        xb_s[...] = xg_s[slot].astype(BF16)

    def swiglu_step(issue_next):
        if issue_next:
            for r in range(chunk):
                row = f * chunk + r
                _row_copy(x_hbm, src_next_ref[0, row], xg_s.at[1 - slot], row, sem.at[1 - slot]).start()
        xb = xb_s[...]
        hid = (_silu(jnp.dot(xb, wg_ref[...], preferred_element_type=F32))
               * jnp.dot(xb, wu_ref[...], preferred_element_type=F32))
        o_ref[...] += jnp.dot(hid.astype(BF16), wd_ref[...], preferred_element_type=F32)

    @pl.when(valid & (f < nf - 1))
    def _():
        swiglu_step(True)

    @pl.when(valid & (f == nf - 1))
    def _():
        swiglu_step(False)


def _moe_group(x, src, tile_expert, tile_valid, wg, wu, wd, tm, n_tiles):
    tf = D_FF // MOE_F_STEPS
    nf = MOE_F_STEPS

    def w_col(i, f, te, tv):
        return (te[i], 0, jnp.where(tv[i] == 1, f, nf - 1))

    def w_row(i, f, te, tv):
        return (te[i], jnp.where(tv[i] == 1, f, nf - 1), 0)

    grid_spec = pltpu.PrefetchScalarGridSpec(
        num_scalar_prefetch=2,
        grid=(n_tiles, nf),
        in_specs=[pl.BlockSpec((None, 1, tm), lambda i, f, te, tv: (i, 0, 0), memory_space=pltpu.SMEM),
                  pl.BlockSpec((None, 1, tm), lambda i, f, te, tv: (jnp.minimum(i + 1, n_tiles - 1), 0, 0),
                               memory_space=pltpu.SMEM),
                  pl.BlockSpec(memory_space=pl.ANY),
                  pl.BlockSpec((None, D_MODEL, tf), w_col),
                  pl.BlockSpec((None, D_MODEL, tf), w_col),
                  pl.BlockSpec((None, tf, D_MODEL), w_row)],
        out_specs=pl.BlockSpec((tm, D_MODEL), lambda i, f, te, tv: (i, 0)),
        scratch_shapes=[pltpu.VMEM((2, tm, D_MODEL), F32),
                        pltpu.VMEM((tm, D_MODEL), BF16),
                        pltpu.SemaphoreType.DMA((2,))],
    )
    src3 = src.reshape(n_tiles, 1, tm)
    return pl.pallas_call(
        _moe_group_kernel,
        grid_spec=grid_spec,
        out_shape=jax.ShapeDtypeStruct((n_tiles * tm, D_MODEL), F32),
        compiler_params=_params("arbitrary", "arbitrary"),
        name="moe_group",
    )(tile_expert, tile_valid, src3, src3, x, wg, wu, wd)

def _combine_kernel(pos_ref, pos_next_ref, x_ref, p_ref, y_hbm, g_ref, b_ref, op_ref, os_ref, ybuf, sem,
                    *, prompt_tiles):
    i, nt = pl.program_id(0), pl.num_programs(0)
    tm = x_ref.shape[0]
    slot = i % 2

    def issue(idx_ref, s):
        def body(r, c):
            _row_copy(y_hbm, idx_ref[0, 2 * r], ybuf.at[s, 0], r, sem.at[s]).start()
            _row_copy(y_hbm, idx_ref[0, 2 * r + 1], ybuf.at[s, 1], r, sem.at[s]).start()
            return c
        lax.fori_loop(0, tm, body, 0, unroll=4)

    @pl.when(i == 0)
    def _():
        issue(pos_ref, 0)

    @pl.when(i + 1 < nt)
    def _():
        issue(pos_next_ref, 1 - slot)

    pltpu.make_async_copy(ybuf.at[slot], ybuf.at[slot], sem.at[slot]).wait()
    moe = p_ref[:, 0:1] * ybuf[slot, 0] + p_ref[:, 1:2] * ybuf[slot, 1]
    y = _layer_norm(ALPHA * x_ref[...] + moe, g_ref[...], b_ref[...])

    @pl.when(i < prompt_tiles)
    def _():
        op_ref[...] = y

    @pl.when(i >= prompt_tiles)
    def _():
        os_ref[...] = y


def _combine_ln(x, prob, y_sorted, pos, g, b, mp):
    m = x.shape[0]
    ms = m - mp
    tm = _tile(np.gcd(mp, ms), 256)
    nt, npt = m // tm, mp // tm
    pos3 = pos.reshape(nt, 1, 2 * tm)
    return pl.pallas_call(
        functools.partial(_combine_kernel, prompt_tiles=npt),
        grid=(nt,),
        in_specs=[pl.BlockSpec((None, 1, 2 * tm), lambda i: (i, 0, 0), memory_space=pltpu.SMEM),
                  pl.BlockSpec((None, 1, 2 * tm), lambda i: (jnp.minimum(i + 1, nt - 1), 0, 0),
                               memory_space=pltpu.SMEM),
                  pl.BlockSpec((tm, D_MODEL), lambda i: (i, 0)),
                  pl.BlockSpec((tm, 2), lambda i: (i, 0)),
                  pl.BlockSpec(memory_space=pl.ANY),
                  pl.BlockSpec((1, D_MODEL), lambda i: (0, 0)),
                  pl.BlockSpec((1, D_MODEL), lambda i: (0, 0))],
        out_specs=[pl.BlockSpec((tm, D_MODEL), lambda i: (jnp.minimum(i, npt - 1), 0)),
                   pl.BlockSpec((tm, D_MODEL), lambda i: (jnp.maximum(i - npt, 0), 0))],
        out_shape=[jax.ShapeDtypeStruct((mp, D_MODEL), F32),
                   jax.ShapeDtypeStruct((ms, D_MODEL), F32)],
        scratch_shapes=[pltpu.VMEM((2, 2, tm, D_MODEL), F32), pltpu.SemaphoreType.DMA((2,))],
        compiler_params=_params("arbitrary"),
        name="combine_ln",
    )(pos3, pos3, x, prob, y_sorted, g, b)


def _moe_ln(x, mp, router_w, router_b, wg, wu, wd, g, b):
    m = x.shape[0]
    tm = MOE_ROWS
    n_tiles = -(-2 * m // tm) + N_EXPERTS
    idx, prob, rank, counts = _router(x, router_w, router_b)
    pos, src, tile_expert, tile_valid = _route_tables(idx, rank, counts[0], tm, n_tiles)
    y_sorted = _moe_group(x, src, tile_expert, tile_valid, wg, wu, wd, tm, n_tiles)
    return _combine_ln(x, prob, y_sorted, pos, g, b, mp)


def kernel(x_prompt, x_sample, state_conv_a, cache_swa_k, cache_swa_v, state_conv_c, w_in, b_gate, conv_a_w, conv_a_b, ln_a_g, ln_a_b, w_branch_a, sinks, w_branch_b, conv_c_w, w_branch_c, ln_d_g, ln_d_b, sgu_w, sgu_b, w_branch_d, w_out, ln1_g, ln1_b, ffn_w_gate, ffn_w_up, ffn_w_down, router_w, router_b, exp_w_gate, exp_w_up, exp_w_down, ln2_g, ln2_b):
    nb, t_len, d = x_prompt.shape
    nseq, s_len, _ = x_sample.shape
    depth = w_in.shape[0]
    mp, ms = nb * t_len, nseq * s_len
    w = WINDOW

    x = jnp.concatenate([x_prompt.reshape(mp, d), x_sample.reshape(ms, d)], axis=0)
    xb = x.astype(BF16)

    cos_p, sin_p = _rope_tables(jnp.arange(t_len))
    cos_s, sin_s = _rope_tables(PAST_LEN + jnp.arange(s_len))

    def row(v):
        return v.reshape(1, -1)

    outs = {k: [] for k in ("pa", "pk", "pv", "pc", "sa", "sk", "sv", "sc", "sd")}
    for l in range(depth):
        w_in_b = w_in[l].astype(BF16)
        h = _in_proj(xb, w_in_b)

        wa, ba = conv_a_w[l], row(conv_a_b[l])
        lag, lab = row(ln_a_g[l]), row(ln_a_b[l])
        ya_p, yc_p, pa, pc = _ac_prompt(h, nb, t_len, wa, ba, lag, lab, conv_c_w[l])
        ya_s, yc_s, sa, sc = _ac_sample(h, mp, nseq, s_len, state_conv_a[l], state_conv_c[l],
                                        wa, ba, lag, lab, conv_c_w[l])

        yb_p, pk, pv = _attn_prompt(h, nb, t_len, sinks[l], cos_p, sin_p)
        yb_s, sk, sv = _attn_sample(h, mp, nseq, s_len, sinks[l],
                                    cache_swa_k[l].reshape(nseq, w, KV_WIDTH),
                                    cache_swa_v[l].reshape(nseq, w, KV_WIDTH), cos_s, sin_s)

        ldg, ldb = row(ln_d_g[l]), row(ln_d_b[l])
        yd_p = _d_prompt(h, mp, ldg, ldb, sgu_w[l], jnp.transpose(sgu_b[l]))
        sgu_wt = jnp.repeat(jnp.transpose(sgu_w[l][:, :s_len, :s_len], (2, 1, 0)), GROUP_D, axis=2)
        sgu_bs = jnp.repeat(jnp.transpose(sgu_b[l][:, :s_len]), GROUP_D, axis=1)
        yd_s, sd = _d_sample(h, mp, nseq, s_len, ldg, ldb, sgu_wt, sgu_bs)

        merged = _merge(xb, (ya_p, yb_p, yc_p, yd_p), (ya_s, yb_s, yc_s, yd_s), w_in_b, b_gate[l],
                        (w_branch_a[l].astype(BF16), w_branch_b[l].astype(BF16),
                         w_branch_c[l].astype(BF16), w_branch_d[l].astype(BF16)))
        x, xb = _out_ln(merged, x, w_out[l].astype(BF16), row(ln1_g[l]), row(ln1_b[l]))

        i = l // 2
        if l % 2 == 0:
            x, xb = _ffn_ln(x, xb, ffn_w_gate[i].astype(BF16), ffn_w_up[i].astype(BF16),
                            ffn_w_down[i].astype(BF16), row(ln2_g[l]), row(ln2_b[l]))
            x_p, x_s = x[:mp], x[mp:]
        else:
            x_p, x_s = _moe_ln(x, mp, router_w[i], row(router_b[i]), exp_w_gate[i].astype(BF16),
                               exp_w_up[i].astype(BF16), exp_w_down[i].astype(BF16),
                               row(ln2_g[l]), row(ln2_b[l]))
            if l + 1 < depth:
                x = jnp.concatenate([x_p, x_s], axis=0)
                xb = x.astype(BF16)

        outs["pa"].append(pa)
        outs["pk"].append(pk.reshape(nb, w, N_KV, HEAD_DIM))
        outs["pv"].append(pv.reshape(nb, w, N_KV, HEAD_DIM))
        outs["pc"].append(pc)
        outs["sa"].append(sa)
        outs["sk"].append(sk.reshape(nseq, w, N_KV, HEAD_DIM))
        outs["sv"].append(sv.reshape(nseq, w, N_KV, HEAD_DIM))
        outs["sc"].append(sc)
        outs["sd"].append(sd)

    y_prompt = x_p.reshape(nb, t_len, d)
    y_sample = x_s.reshape(nseq, s_len, d)
    st = {k: jnp.stack(v) for k, v in outs.items()}
    return (y_prompt, y_sample, st["pa"], st["pk"], st["pv"], st["pc"],
            st["sa"], st["sk"], st["sv"], st["sc"], st["sd"])
```

```python
import functools

import jax
import jax.numpy as jnp
import numpy as np
from jax import lax
from jax.experimental import pallas as pl
from jax.experimental.pallas import tpu as pltpu

D_MODEL = 2048
PAST_LEN = 8192
D_A = 512
CONV_A = 31
HEAD_DIM = 64
N_HEADS = 16
N_KV = 4
GQA = N_HEADS // N_KV
WINDOW = 128
ROPE_THETA = 10000.0
D_C = 512
CONV_C = 3
D_D = 512
CHUNK = 128
N_GROUPS_D = 4
GROUP_D = D_D // N_GROUPS_D
N_BRANCH = 4
Q_WIDTH = N_HEADS * HEAD_DIM
KV_WIDTH = N_KV * HEAD_DIM
D_FF = 5632
N_EXPERTS = 8
ALPHA = 4.0 ** 0.25
LN_EPS = 1e-5

OFF_A_VAL = 0
OFF_A_GATE = D_A
OFF_Q = 2 * D_A
OFF_K = OFF_Q + Q_WIDTH
OFF_V = OFF_K + KV_WIDTH
OFF_C_B = OFF_V + KV_WIDTH
OFF_C_C = OFF_C_B + D_C
OFF_C_X = OFF_C_C + D_C
OFF_D_U = OFF_C_X + D_C
OFF_D_V = OFF_D_U + D_D
OFF_GATES = OFF_D_V + D_D

LANES = 128
SUBLANES = 8
HALO = 32
VMEM_LIMIT = 48 * 1024 * 1024
OUT_LN_SUBTILES = 4
MOE_F_STEPS = 11
MOE_ROWS = 64 * (MOE_F_STEPS - 1)

BF16 = jnp.bfloat16
F32 = jnp.float32


def _tile(n, pref):
    if n <= pref:
        return n
    for t in range(pref, 7, -1):
        if n % t == 0 and t % 8 == 0:
            return t
    return n


def _params(*sem):
    return pltpu.CompilerParams(dimension_semantics=sem, vmem_limit_bytes=VMEM_LIMIT)


def _layer_norm(x, g, b):
    mu = jnp.mean(x, axis=-1, keepdims=True)
    xc = x - mu
    var = jnp.mean(xc * xc, axis=-1, keepdims=True)
    return xc * lax.rsqrt(var + LN_EPS) * g + b


def _silu(x):
    return x * jax.nn.sigmoid(x)


def _gelu(x):
    return jax.nn.gelu(x, approximate=True)


def _mm_kernel(x_ref, w_ref, o_ref):
    o_ref[...] = jnp.dot(x_ref[...], w_ref[...], preferred_element_type=F32).astype(o_ref.dtype)


def _in_proj(xb, w_in_b):
    m, k = xb.shape
    n = OFF_GATES
    tm = _tile(m, 1024)
    tn = 1024
    return pl.pallas_call(
        _mm_kernel,
        grid=(m // tm, n // tn),
        in_specs=[pl.BlockSpec((tm, k), lambda i, j: (i, 0)),
                  pl.BlockSpec((k, tn), lambda i, j: (0, j))],
        out_specs=pl.BlockSpec((tm, tn), lambda i, j: (i, j)),
        out_shape=jax.ShapeDtypeStruct((m, n), F32),
        compiler_params=_params("parallel", "arbitrary"),
        name="in_proj",
    )(xb, w_in_b)


def _ac_prompt_kernel(av_ref, ag_ref, cb_ref, cc_ref, cx_ref,
                      avp_ref, agp_ref, ccp_ref, cxp_ref,
                      wa_ref, ba_ref, lg_ref, lb_ref, wc_ref,
                      ya_ref, yc_ref, ha_ref, hc_ref, ga_s, gc_s, sh_s):
    t = pl.program_id(1)
    tt = av_ref.shape[0]
    first = t == 0
    glu_prev = avp_ref[...] * jax.nn.sigmoid(agp_ref[...])
    ga_s[0:HALO, :] = jnp.where(first, 0.0, glu_prev)
    ga_s[HALO:HALO + tt, :] = av_ref[...] * jax.nn.sigmoid(ag_ref[...])
    acc = jnp.zeros((tt, D_A), F32) + ba_ref[...]
    base = HALO - (CONV_A - 1)
    for phase in range(SUBLANES):
        offs = [o for o in range(base, base + CONV_A) if o % SUBLANES == phase]
        span = offs[-1] - phase + tt
        sh_s[0:span, :] = ga_s[phase:phase + span, :]
        for o in offs:
            acc = acc + wa_ref[o - base:o - base + 1, :] * sh_s[o - phase:o - phase + tt, :]
    ya = _silu(_layer_norm(acc, lg_ref[...], lb_ref[...]))
    ya_ref[...] = ya.astype(ya_ref.dtype)
    ha_ref[0] = ga_s[HALO + tt - (CONV_A - 1):HALO + tt, :]

    gc_s[0:HALO, :] = jnp.where(first, 0.0, ccp_ref[...] * cxp_ref[...])
    gc_s[HALO:HALO + tt, :] = cc_ref[...] * cx_ref[...]
    base_c = HALO - (CONV_C - 1)
    yc = jnp.zeros((tt, D_C), F32)
    for j in range(CONV_C):
        yc = yc + wc_ref[j:j + 1, :] * gc_s[base_c + j:base_c + j + tt, :]
    yc_ref[...] = (cb_ref[...] * yc).astype(yc_ref.dtype)
    hc_ref[0] = gc_s[HALO + tt - (CONV_C - 1):HALO + tt, :]


def _ac_prompt(h, nb, t_len, wa, ba, lg, lb, wc):
    tt = _tile(t_len, 256)
    nt = t_len // tt
    cw = D_A

    def cur(col):
        return pl.BlockSpec((tt, cw), lambda b, t: (b * nt + t, col // cw))

    def prev(col):
        return pl.BlockSpec(
            (HALO, cw),
            lambda b, t: (jnp.maximum((b * t_len + t * tt) // HALO - 1, 0), col // cw))

    def full(shape):
        return pl.BlockSpec(shape, lambda b, t: (0,) * len(shape))

    mp = nb * t_len
    return pl.pallas_call(
        _ac_prompt_kernel,
        grid=(nb, nt),
        in_specs=[cur(OFF_A_VAL), cur(OFF_A_GATE), cur(OFF_C_B), cur(OFF_C_C), cur(OFF_C_X),
                  prev(OFF_A_VAL), prev(OFF_A_GATE), prev(OFF_C_C), prev(OFF_C_X),
                  full((CONV_A, D_A)), full((1, D_A)), full((1, D_A)), full((1, D_A)),
                  full((CONV_C, D_C))],
        out_specs=[pl.BlockSpec((tt, D_A), lambda b, t: (b * nt + t, 0)),
                   pl.BlockSpec((tt, D_C), lambda b, t: (b * nt + t, 0)),
                   pl.BlockSpec((1, CONV_A - 1, D_A), lambda b, t: (b, 0, 0)),
                   pl.BlockSpec((1, CONV_C - 1, D_C), lambda b, t: (b, 0, 0))],
        out_shape=[jax.ShapeDtypeStruct((mp, D_A), BF16),
                   jax.ShapeDtypeStruct((mp, D_C), BF16),
                   jax.ShapeDtypeStruct((nb, CONV_A - 1, D_A), F32),
                   jax.ShapeDtypeStruct((nb, CONV_C - 1, D_C), F32)],
        scratch_shapes=[pltpu.VMEM((HALO + tt, D_A), F32), pltpu.VMEM((HALO + tt, D_C), F32),
                        pltpu.VMEM((HALO + tt, D_A), F32)],
        compiler_params=_params("parallel", "arbitrary"),
        name="ac_prompt",
    )(h, h, h, h, h, h, h, h, h, wa, ba, lg, lb, wc)


def _ac_sample_kernel(av_ref, ag_ref, cb_ref, cc_ref, cx_ref, hista_ref, histc_ref,
                      wa_ref, ba_ref, lg_ref, lb_ref, wc_ref,
                      ya_ref, yc_ref, ha_ref, hc_ref, xa_s, xc_s):
    sb, s_len = hista_ref.shape[0], av_ref.shape[0] // hista_ref.shape[0]
    ka, kc = CONV_A - 1, CONV_C - 1
    glu = av_ref[...] * jax.nn.sigmoid(ag_ref[...])
    xa_s[:, 0:ka, :] = hista_ref[...]
    xa_s[:, ka:ka + s_len, :] = glu.reshape(sb, s_len, D_A)
    acc = jnp.zeros((sb, s_len, D_A), F32) + ba_ref[...][None]
    for j in range(CONV_A):
        acc = acc + wa_ref[j:j + 1, :][None] * xa_s[:, j:j + s_len, :]
    ya = _silu(_layer_norm(acc, lg_ref[...][None], lb_ref[...][None]))
    ya_ref[...] = ya.reshape(sb * s_len, D_A).astype(ya_ref.dtype)
    ha_ref[...] = xa_s[:, s_len:s_len + ka, :]

    xc_s[:, 0:kc, :] = histc_ref[...]
    xc_s[:, kc:kc + s_len, :] = (cc_ref[...] * cx_ref[...]).reshape(sb, s_len, D_C)
    yc = jnp.zeros((sb, s_len, D_C), F32)
    for j in range(CONV_C):
        yc = yc + wc_ref[j:j + 1, :][None] * xc_s[:, j:j + s_len, :]
    yc = cb_ref[...] * yc.reshape(sb * s_len, D_C)
    yc_ref[...] = yc.astype(yc_ref.dtype)
    hc_ref[...] = xc_s[:, s_len:s_len + kc, :]


def _ac_sample(h, row0, nseq, s_len, hist_a, hist_c, wa, ba, lg, lb, wc):
    sb = _tile(nseq, 16)
    rows = sb * s_len
    blk0 = row0 // rows
    cw = D_A

    def cur(col):
        return pl.BlockSpec((rows, cw), lambda i: (blk0 + i, col // cw))

    def full(shape):
        return pl.BlockSpec(shape, lambda i: (0,) * len(shape))

    ms = nseq * s_len
    ka, kc = CONV_A - 1, CONV_C - 1
    return pl.pallas_call(
        _ac_sample_kernel,
        grid=(nseq // sb,),
        in_specs=[cur(OFF_A_VAL), cur(OFF_A_GATE), cur(OFF_C_B), cur(OFF_C_C), cur(OFF_C_X),
                  pl.BlockSpec((sb, ka, D_A), lambda i: (i, 0, 0)),
                  pl.BlockSpec((sb, kc, D_C), lambda i: (i, 0, 0)),
                  full((CONV_A, D_A)), full((1, D_A)), full((1, D_A)), full((1, D_A)),
                  full((CONV_C, D_C))],
        out_specs=[pl.BlockSpec((rows, D_A), lambda i: (i, 0)),
                   pl.BlockSpec((rows, D_C), lambda i: (i, 0)),
                   pl.BlockSpec((sb, ka, D_A), lambda i: (i, 0, 0)),
                   pl.BlockSpec((sb, kc, D_C), lambda i: (i, 0, 0))],
        out_shape=[jax.ShapeDtypeStruct((ms, D_A), BF16),
                   jax.ShapeDtypeStruct((ms, D_C), BF16),
                   jax.ShapeDtypeStruct((nseq, ka, D_A), F32),
                   jax.ShapeDtypeStruct((nseq, kc, D_C), F32)],
        scratch_shapes=[pltpu.VMEM((sb, ka + s_len + 2, D_A), F32),
                        pltpu.VMEM((sb, kc + s_len + 6, D_C), F32)],
        compiler_params=_params("parallel"),
        name="ac_sample",
    )(h, h, h, h, h, hist_a, hist_c, wa, ba, lg, lb, wc)


def _rope_tables(pos):
    half = HEAD_DIM // 2
    inv_freq = jnp.power(ROPE_THETA, -jnp.arange(half, dtype=F32) * (2.0 / HEAD_DIM))
    ang = pos.astype(F32)[:, None] * inv_freq[None, :]
    cos, sin = jnp.cos(ang), jnp.sin(ang)
    cos_t = jnp.concatenate([cos, cos, cos, cos], axis=1)
    sin_t = jnp.concatenate([-sin, sin, -sin, sin], axis=1)
    return cos_t, sin_t


def _rope(x, cos_t, sin_t):
    half = HEAD_DIM // 2
    axis = x.ndim - 1
    shape = x.shape[:-1] + (LANES,)
    lane = lax.broadcasted_iota(jnp.int32, shape, axis)
    first = (lane % HEAD_DIM) < half
    out = []
    for i in range(x.shape[-1] // LANES):
        xi = x[..., LANES * i:LANES * (i + 1)]
        partner = jnp.where(first, pltpu.roll(xi, LANES - half, axis), pltpu.roll(xi, half, axis))
        out.append(xi * cos_t + partner * sin_t)
    return out


def _head_halves(t, axis, lo_valid):
    lane = lax.broadcasted_iota(jnp.int32, t.shape, axis)
    if lo_valid:
        lo = jnp.where(lane < HEAD_DIM, t, jnp.zeros_like(t))
        hi = pltpu.roll(lo, HEAD_DIM, axis)
    else:
        hi = jnp.where(lane >= HEAD_DIM, t, jnp.zeros_like(t))
        lo = pltpu.roll(hi, HEAD_DIM, axis)
    return lo, hi


def _attn_prompt_kernel(sinks_ref, q_ref, kc_ref, vc_ref, kp_ref, vp_ref,
                        cos_ref, sin_ref, cosp_ref, sinp_ref,
                        yb_ref, nk_ref, nv_ref):
    n = pl.program_id(1)
    w = WINDOW
    cos_c, sin_c = cos_ref[...], sin_ref[...]
    q_tiles = _rope(q_ref[...] * (HEAD_DIM ** -0.5), cos_c, sin_c)
    kc_tiles = _rope(kc_ref[...], cos_c, sin_c)
    kp_tiles = _rope(kp_ref[...], cosp_ref[...], sinp_ref[...])
    for i in range(KV_WIDTH // LANES):
        nk_ref[0, :, LANES * i:LANES * (i + 1)] = kc_tiles[i]
    nv_ref[0] = vc_ref[...]

    ci = lax.broadcasted_iota(jnp.int32, (2 * w, 2 * w), 0)
    qi = lax.broadcasted_iota(jnp.int32, (2 * w, 2 * w), 1) % w
    mask = (ci > qi) & (ci <= qi + w) & ((n > 0) | (ci >= w))
    lane_first = lax.broadcasted_iota(jnp.int32, (1, 2 * w), 1) < w

    qt_tiles = [t.T.astype(BF16) for t in q_tiles]
    zeros_t = jnp.zeros((HEAD_DIM, 2 * w), BF16)

    for h in range(N_KV):
        tile, lo_valid = h // 2, (h % 2 == 0)
        lanes = slice(LANES * tile, LANES * (tile + 1))
        kcat = jnp.concatenate([kp_tiles[tile], kc_tiles[tile]], axis=0)
        k_lo, k_hi = _head_halves(kcat.astype(BF16), 1, lo_valid)
        vt = jnp.concatenate([vp_ref[:, lanes], vc_ref[:, lanes]], axis=0).T.astype(BF16)
        vt = vt[0:HEAD_DIM] if lo_valid else vt[HEAD_DIM:2 * HEAD_DIM]
        vt_halves = (jnp.concatenate([vt, zeros_t], axis=0), jnp.concatenate([zeros_t, vt], axis=0))
        qst = jnp.concatenate([qt_tiles[2 * h], qt_tiles[2 * h + 1]], axis=1)
        ot = None
        for half_idx, k_half in enumerate((k_lo, k_hi)):
            s = jnp.dot(k_half, qst, preferred_element_type=F32)
            s = jnp.where(mask, s, -jnp.inf)
            sk = jnp.where(lane_first, sinks_ref[4 * h + half_idx], sinks_ref[4 * h + 2 + half_idx])
            m = jnp.maximum(jnp.max(s, axis=0, keepdims=True), sk)
            e = jnp.exp(s - m)
            den = jnp.sum(e, axis=0, keepdims=True) + jnp.exp(sk - m)
            p = (e / den).astype(BF16)
            term = jnp.dot(vt_halves[half_idx], p, preferred_element_type=F32)
            ot = term if ot is None else ot + term
        yb_ref[:, LANES * (2 * h):LANES * (2 * h + 1)] = ot[:, 0:w].T.astype(yb_ref.dtype)
        yb_ref[:, LANES * (2 * h + 1):LANES * (2 * h + 2)] = ot[:, w:2 * w].T.astype(yb_ref.dtype)


def _attn_prompt(h, nb, t_len, sinks, cos_t, sin_t):
    w = WINDOW
    nblk = t_len // w
    mp = nb * t_len

    def rows(b, n):
        return b * nblk + n

    def rows_prev(b, n):
        return b * nblk + jnp.maximum(n - 1, 0)

    return pl.pallas_call(
        _attn_prompt_kernel,
        grid=(nb, nblk),
        in_specs=[pl.BlockSpec(memory_space=pltpu.SMEM),
                  pl.BlockSpec((w, Q_WIDTH), lambda b, n: (rows(b, n), OFF_Q // Q_WIDTH)),
                  pl.BlockSpec((w, KV_WIDTH), lambda b, n: (rows(b, n), OFF_K // KV_WIDTH)),
                  pl.BlockSpec((w, KV_WIDTH), lambda b, n: (rows(b, n), OFF_V // KV_WIDTH)),
                  pl.BlockSpec((w, KV_WIDTH), lambda b, n: (rows_prev(b, n), OFF_K // KV_WIDTH)),
                  pl.BlockSpec((w, KV_WIDTH), lambda b, n: (rows_prev(b, n), OFF_V // KV_WIDTH)),
                  pl.BlockSpec((w, LANES), lambda b, n: (n, 0)),
                  pl.BlockSpec((w, LANES), lambda b, n: (n, 0)),
                  pl.BlockSpec((w, LANES), lambda b, n: (jnp.maximum(n - 1, 0), 0)),
                  pl.BlockSpec((w, LANES), lambda b, n: (jnp.maximum(n - 1, 0), 0))],
        out_specs=[pl.BlockSpec((w, Q_WIDTH), lambda b, n: (rows(b, n), 0)),
                   pl.BlockSpec((1, w, KV_WIDTH), lambda b, n: (b, 0, 0)),
                   pl.BlockSpec((1, w, KV_WIDTH), lambda b, n: (b, 0, 0))],
        out_shape=[jax.ShapeDtypeStruct((mp, Q_WIDTH), BF16),
                   jax.ShapeDtypeStruct((nb, w, KV_WIDTH), F32),
                   jax.ShapeDtypeStruct((nb, w, KV_WIDTH), F32)],
        compiler_params=_params("parallel", "arbitrary"),
        name="attn_prompt",
    )(sinks, h, h, h, h, h, cos_t, sin_t, cos_t, sin_t)


def _attn_sample_kernel(sinks_ref, q_ref, kn_ref, vn_ref, kbuf_ref, vbuf_ref, cos_ref, sin_ref,
                        yb_ref, nk_ref, nv_ref):
    sb, w = kbuf_ref.shape[0], kbuf_ref.shape[1]
    s_len = q_ref.shape[0] // sb
    cos_t, sin_t = cos_ref[...], sin_ref[...]
    q_tiles = _rope(q_ref[...] * (HEAD_DIM ** -0.5), cos_t, sin_t)
    kn_tiles = _rope(kn_ref[...], cos_t, sin_t)
    nk_ref[:, 0:w - s_len, :] = kbuf_ref[:, s_len:w, :]
    nv_ref[:, 0:w - s_len, :] = vbuf_ref[:, s_len:w, :]
    for i in range(KV_WIDTH // LANES):
        nk_ref[:, w - s_len:w, LANES * i:LANES * (i + 1)] = kn_tiles[i].reshape(sb, s_len, LANES)
    nv_ref[:, w - s_len:w, :] = vn_ref[...].reshape(sb, s_len, KV_WIDTH)

    qi = lax.broadcasted_iota(jnp.int32, (2 * s_len, w), 0) % s_len
    ci = lax.broadcasted_iota(jnp.int32, (2 * s_len, w), 1)
    mask_buf = (ci > qi)[None]
    qn = lax.broadcasted_iota(jnp.int32, (2 * s_len, s_len), 0) % s_len
    cn = lax.broadcasted_iota(jnp.int32, (2 * s_len, s_len), 1)
    mask_new = (cn <= qn)[None]
    row_top = (lax.broadcasted_iota(jnp.int32, (2 * s_len, 1), 0) < s_len)[None]
    bqk = (((2,), (2,)), ((0,), (0,)))
    bkd = (((2,), (1,)), ((0,), (0,)))

    for h in range(N_KV):
        tile, lo_valid = h // 2, (h % 2 == 0)
        sl = slice(LANES * tile, LANES * (tile + 1))
        kb_lo, kb_hi = _head_halves(kbuf_ref[:, :, sl].astype(BF16), 2, lo_valid)
        vb_lo, vb_hi = _head_halves(vbuf_ref[:, :, sl].astype(BF16), 2, lo_valid)
        kn3 = kn_tiles[tile].reshape(sb, s_len, LANES).astype(BF16)
        vn3 = vn_ref[:, sl].reshape(sb, s_len, LANES).astype(BF16)
        kn_lo, kn_hi = _head_halves(kn3, 2, lo_valid)
        vn_lo, vn_hi = _head_halves(vn3, 2, lo_valid)
        qs = jnp.concatenate([q_tiles[2 * h].reshape(sb, s_len, LANES),
                              q_tiles[2 * h + 1].reshape(sb, s_len, LANES)], axis=1).astype(BF16)
        o = jnp.zeros((sb, 2 * s_len, LANES), F32)
        for half_idx, (kb, kn, vb, vn) in enumerate(((kb_lo, kn_lo, vb_lo, vn_lo),
                                                     (kb_hi, kn_hi, vb_hi, vn_hi))):
            s_b = lax.dot_general(qs, kb, bqk, preferred_element_type=F32)
            s_n = lax.dot_general(qs, kn, bqk, preferred_element_type=F32)
            s_b = jnp.where(mask_buf, s_b, -jnp.inf)
            s_n = jnp.where(mask_new, s_n, -jnp.inf)
            sk = jnp.where(row_top, sinks_ref[4 * h + half_idx], sinks_ref[4 * h + 2 + half_idx])
            m = jnp.maximum(jnp.maximum(jnp.max(s_b, axis=2, keepdims=True),
                                        jnp.max(s_n, axis=2, keepdims=True)), sk)
            e_b = jnp.exp(s_b - m)
            e_n = jnp.exp(s_n - m)
            den = (jnp.sum(e_b, axis=2, keepdims=True) + jnp.sum(e_n, axis=2, keepdims=True)
                   + jnp.exp(sk - m))
            o = o + lax.dot_general((e_b / den).astype(BF16), vb, bkd, preferred_element_type=F32)
            o = o + lax.dot_general((e_n / den).astype(BF16), vn, bkd, preferred_element_type=F32)
        yb_ref[:, LANES * (2 * h):LANES * (2 * h + 1)] = (
            o[:, 0:s_len, :].reshape(sb * s_len, LANES).astype(yb_ref.dtype))
        yb_ref[:, LANES * (2 * h + 1):LANES * (2 * h + 2)] = (
            o[:, s_len:2 * s_len, :].reshape(sb * s_len, LANES).astype(yb_ref.dtype))


def _attn_sample(h, row0, nseq, s_len, sinks, k_buf, v_buf, cos_t, sin_t):
    sb = _tile(nseq, 8)
    rows = sb * s_len
    blk0 = row0 // rows
    w = k_buf.shape[1]
    ms = nseq * s_len
    cos_rows = jnp.tile(cos_t, (sb, 1))
    sin_rows = jnp.tile(sin_t, (sb, 1))
    return pl.pallas_call(
        _attn_sample_kernel,
        grid=(nseq // sb,),
        in_specs=[pl.BlockSpec(memory_space=pltpu.SMEM),
                  pl.BlockSpec((rows, Q_WIDTH), lambda i: (blk0 + i, OFF_Q // Q_WIDTH)),
                  pl.BlockSpec((rows, KV_WIDTH), lambda i: (blk0 + i, OFF_K // KV_WIDTH)),
                  pl.BlockSpec((rows, KV_WIDTH), lambda i: (blk0 + i, OFF_V // KV_WIDTH)),
                  pl.BlockSpec((sb, w, KV_WIDTH), lambda i: (i, 0, 0)),
                  pl.BlockSpec((sb, w, KV_WIDTH), lambda i: (i, 0, 0)),
                  pl.BlockSpec((rows, LANES), lambda i: (0, 0)),
                  pl.BlockSpec((rows, LANES), lambda i: (0, 0))],
        out_specs=[pl.BlockSpec((rows, Q_WIDTH), lambda i: (i, 0)),
                   pl.BlockSpec((sb, w, KV_WIDTH), lambda i: (i, 0, 0)),
                   pl.BlockSpec((sb, w, KV_WIDTH), lambda i: (i, 0, 0))],
        out_shape=[jax.ShapeDtypeStruct((ms, Q_WIDTH), BF16),
                   jax.ShapeDtypeStruct((nseq, w, KV_WIDTH), F32),
                   jax.ShapeDtypeStruct((nseq, w, KV_WIDTH), F32)],
        compiler_params=_params("parallel"),
        name="attn_sample",
    )(sinks, h, h, h, k_buf, v_buf, cos_rows, sin_rows)


def _d_prompt_kernel(du_ref, dv_ref, lg_ref, lb_ref, w_ref, bt_ref, yd_ref):
    c = CHUNK
    ri = lax.broadcasted_iota(jnp.int32, (c, c), 0)
    cj = lax.broadcasted_iota(jnp.int32, (c, c), 1)
    causal = cj <= ri
    w_causal = [jnp.where(causal, w_ref[g], 0.0).astype(BF16) for g in range(N_GROUPS_D)]
    for k in range(du_ref.shape[0] // c):
        rows = slice(k * c, (k + 1) * c)
        u = _gelu(du_ref[rows, :])
        vn = _layer_norm(_gelu(dv_ref[rows, :]), lg_ref[...], lb_ref[...]).astype(BF16)
        for g in range(N_GROUPS_D):
            sl = slice(GROUP_D * g, GROUP_D * (g + 1))
            s = jnp.dot(w_causal[g], vn[:, sl], preferred_element_type=F32) + bt_ref[:, g:g + 1]
            yd_ref[rows, sl] = (u[:, sl] * s).astype(yd_ref.dtype)


def _d_prompt(h, mp, ln_g, ln_b, sgu_w, sgu_bt):
    c = CHUNK
    rows = _tile(mp, 4 * c)
    assert rows % c == 0
    return pl.pallas_call(
        _d_prompt_kernel,
        grid=(mp // rows,),
        in_specs=[pl.BlockSpec((rows, D_D), lambda i: (i, OFF_D_U // D_D)),
                  pl.BlockSpec((rows, D_D), lambda i: (i, OFF_D_V // D_D)),
                  pl.BlockSpec((1, D_D), lambda i: (0, 0)),
                  pl.BlockSpec((1, D_D), lambda i: (0, 0)),
                  pl.BlockSpec((N_GROUPS_D, c, c), lambda i: (0, 0, 0)),
                  pl.BlockSpec((c, N_GROUPS_D), lambda i: (0, 0))],
        out_specs=pl.BlockSpec((rows, D_D), lambda i: (i, 0)),
        out_shape=jax.ShapeDtypeStruct((mp, D_D), BF16),
        compiler_params=_params("parallel"),
        name="d_prompt",
    )(h, h, ln_g, ln_b, sgu_w, sgu_bt)


def _d_sample_kernel(du_ref, dv_ref, lg_ref, lb_ref, wt_ref, bt_ref, yd_ref, vd_ref):
    sb, s_len = vd_ref.shape[0], vd_ref.shape[1]
    u = _gelu(du_ref[...])
    vn = _layer_norm(_gelu(dv_ref[...]), lg_ref[...], lb_ref[...])
    vn3 = vn.reshape(sb, s_len, D_D)
    vd_ref[...] = vn3
    ii = lax.broadcasted_iota(jnp.int32, (s_len, D_D), 0)
    s = jnp.zeros((sb, s_len, D_D), F32) + bt_ref[...][None]
    for j in range(s_len):
        wj = jnp.where(ii >= j, wt_ref[j], 0.0)
        s = s + wj[None] * vn3[:, j:j + 1, :]
    yd_ref[...] = (u * s.reshape(sb * s_len, D_D)).astype(yd_ref.dtype)


def _d_sample(h, row0, nseq, s_len, ln_g, ln_b, sgu_wt, sgu_bs):
    sb = _tile(nseq, 16)
    rows = sb * s_len
    blk0 = row0 // rows
    ms = nseq * s_len
    return pl.pallas_call(
        _d_sample_kernel,
        grid=(nseq // sb,),
        in_specs=[pl.BlockSpec((rows, D_D), lambda i: (blk0 + i, OFF_D_U // D_D)),
                  pl.BlockSpec((rows, D_D), lambda i: (blk0 + i, OFF_D_V // D_D)),
                  pl.BlockSpec((1, D_D), lambda i: (0, 0)),
                  pl.BlockSpec((1, D_D), lambda i: (0, 0)),
                  pl.BlockSpec((s_len, s_len, D_D), lambda i: (0, 0, 0)),
                  pl.BlockSpec((s_len, D_D), lambda i: (0, 0))],
        out_specs=[pl.BlockSpec((rows, D_D), lambda i: (i, 0)),
                   pl.BlockSpec((sb, s_len, D_D), lambda i: (i, 0, 0))],
        out_shape=[jax.ShapeDtypeStruct((ms, D_D), BF16),
                   jax.ShapeDtypeStruct((nseq, s_len, D_D), F32)],
        compiler_params=_params("parallel"),
        name="d_sample",
    )(h, h, ln_g, ln_b, sgu_wt, sgu_bs)


def _merge_kernel(x_ref, yap_ref, ybp_ref, ycp_ref, ydp_ref, yas_ref, ybs_ref, ycs_ref, yds_ref,
                  g0_ref, g1_ref, g2_ref, g3_ref, bg_ref, pa_ref, pb_ref, pc_ref, pd_ref, o_ref,
                  *, prompt_tiles):
    x = x_ref[...]
    is_prompt = pl.program_id(1) < prompt_tiles
    acc = None
    branches = ((yap_ref, yas_ref, g0_ref, pa_ref), (ybp_ref, ybs_ref, g1_ref, pb_ref),
                (ycp_ref, ycs_ref, g2_ref, pc_ref), (ydp_ref, yds_ref, g3_ref, pd_ref))
    for i, (yp_ref, ys_ref, g_ref, p_ref) in enumerate(branches):
        y = jnp.where(is_prompt, yp_ref[...], ys_ref[...])
        gate = jax.nn.sigmoid(jnp.dot(x, g_ref[...], preferred_element_type=F32) + bg_ref[i:i + 1, :])
        term = gate * jnp.dot(y, p_ref[...], preferred_element_type=F32)
        acc = term if acc is None else acc + term
    o_ref[...] = acc.astype(o_ref.dtype)


def _merge(xb, y_prompt, y_sample, w_in_b, b_gate, projs):
    m = xb.shape[0]
    mp, ms = y_prompt[0].shape[0], y_sample[0].shape[0]
    tm = _tile(np.gcd(mp, ms), 512)
    npt = mp // tm
    tn = 512
    nn = D_MODEL // tn
    widths = (D_A, Q_WIDTH, D_C, D_D)

    def act_p(width):
        return pl.BlockSpec((tm, width), lambda j, i: (jnp.minimum(i, npt - 1), 0))

    def act_s(width):
        return pl.BlockSpec((tm, width), lambda j, i: (jnp.maximum(i - npt, 0), 0))

    def gate_w(b):
        return pl.BlockSpec((D_MODEL, tn), lambda j, i: (0, (OFF_GATES + b * D_MODEL) // tn + j))

    def proj_w(k):
        return pl.BlockSpec((k, tn), lambda j, i: (0, j))

    return pl.pallas_call(
        functools.partial(_merge_kernel, prompt_tiles=npt),
        grid=(nn, m // tm),
        in_specs=[pl.BlockSpec((tm, D_MODEL), lambda j, i: (i, 0))]
                 + [act_p(wd) for wd in widths] + [act_s(wd) for wd in widths]
                 + [gate_w(0), gate_w(1), gate_w(2), gate_w(3),
                    pl.BlockSpec((N_BRANCH, tn), lambda j, i: (0, j))]
                 + [proj_w(wd) for wd in widths],
        out_specs=pl.BlockSpec((tm, tn), lambda j, i: (i, j)),
        out_shape=jax.ShapeDtypeStruct((m, D_MODEL), BF16),
        compiler_params=_params("parallel", "arbitrary"),
        name="merge",
    )(xb, *y_prompt, *y_sample, w_in_b, w_in_b, w_in_b, w_in_b, b_gate, *projs)


def _out_ln_kernel(mg_ref, x_ref, w_ref, g_ref, b_ref, o_ref, ob_ref):
    tm = x_ref.shape[0]
    sub = tm // OUT_LN_SUBTILES
    for k in range(OUT_LN_SUBTILES):
        rows = slice(k * sub, (k + 1) * sub)
        y = ALPHA * x_ref[rows, :] + jnp.dot(mg_ref[rows, :], w_ref[...], preferred_element_type=F32)
        y = _layer_norm(y, g_ref[...], b_ref[...])
        o_ref[rows, :] = y
        ob_ref[rows, :] = y.astype(ob_ref.dtype)


def _out_ln(merged, x, w_out_b, g, b):
    m = x.shape[0]
    tm = _tile(m, 512)
    return pl.pallas_call(
        _out_ln_kernel,
        grid=(m // tm,),
        in_specs=[pl.BlockSpec((tm, D_MODEL), lambda i: (i, 0)),
                  pl.BlockSpec((tm, D_MODEL), lambda i: (i, 0)),
                  pl.BlockSpec((D_MODEL, D_MODEL), lambda i: (0, 0)),
                  pl.BlockSpec((1, D_MODEL), lambda i: (0, 0)),
                  pl.BlockSpec((1, D_MODEL), lambda i: (0, 0))],
        out_specs=[pl.BlockSpec((tm, D_MODEL), lambda i: (i, 0)),
                   pl.BlockSpec((tm, D_MODEL), lambda i: (i, 0))],
        out_shape=[jax.ShapeDtypeStruct((m, D_MODEL), F32),
                   jax.ShapeDtypeStruct((m, D_MODEL), BF16)],
        compiler_params=_params("parallel"),
        name="out_ln",
    )(merged, x, w_out_b, g, b)


def _ffn_kernel(x_ref, xb_ref, wg_ref, wu_ref, wd_ref, g_ref, b_ref, o_ref, ob_ref, acc_s):
    f = pl.program_id(1)

    @pl.when(f == 0)
    def _():
        acc_s[...] = jnp.zeros_like(acc_s)

    xb = xb_ref[...]
    hid = (_silu(jnp.dot(xb, wg_ref[...], preferred_element_type=F32))
           * jnp.dot(xb, wu_ref[...], preferred_element_type=F32))
    acc_s[...] += jnp.dot(hid.astype(BF16), wd_ref[...], preferred_element_type=F32)

    @pl.when(f == pl.num_programs(1) - 1)
    def _():
        y = _layer_norm(ALPHA * x_ref[...] + acc_s[...], g_ref[...], b_ref[...])
        o_ref[...] = y
        ob_ref[...] = y.astype(ob_ref.dtype)


def _ffn_ln(x, xb, wg, wu, wd, g, b):
    m = x.shape[0]
    tm = _tile(m, 512)
    tf = 512
    return pl.pallas_call(
        _ffn_kernel,
        grid=(m // tm, D_FF // tf),
        in_specs=[pl.BlockSpec((tm, D_MODEL), lambda i, f: (i, 0)),
                  pl.BlockSpec((tm, D_MODEL), lambda i, f: (i, 0)),
                  pl.BlockSpec((D_MODEL, tf), lambda i, f: (0, f)),
                  pl.BlockSpec((D_MODEL, tf), lambda i, f: (0, f)),
                  pl.BlockSpec((tf, D_MODEL), lambda i, f: (f, 0)),
                  pl.BlockSpec((1, D_MODEL), lambda i, f: (0, 0)),
                  pl.BlockSpec((1, D_MODEL), lambda i, f: (0, 0))],
        out_specs=[pl.BlockSpec((tm, D_MODEL), lambda i, f: (i, 0)),
                   pl.BlockSpec((tm, D_MODEL), lambda i, f: (i, 0))],
        out_shape=[jax.ShapeDtypeStruct((m, D_MODEL), F32),
                   jax.ShapeDtypeStruct((m, D_MODEL), BF16)],
        scratch_shapes=[pltpu.VMEM((tm, D_MODEL), F32)],
        compiler_params=_params("parallel", "arbitrary"),
        name="ffn_ln",
    )(x, xb, wg, wu, wd, g, b)


def _router_kernel(x_ref, w_ref, b_ref, idx_ref, prob_ref, rank_ref, cnt_ref, run_s):
    @pl.when(pl.program_id(0) == 0)
    def _():
        run_s[...] = jnp.zeros_like(run_s)

    logits = jnp.dot(x_ref[...], w_ref[...], preferred_element_type=F32,
                     precision=lax.Precision.HIGHEST) + b_ref[...]
    tm = logits.shape[0]
    idx = lax.broadcasted_iota(jnp.int32, logits.shape, 1)
    v1 = jnp.max(logits, axis=1, keepdims=True)
    i1 = jnp.min(jnp.where(logits == v1, idx, N_EXPERTS), axis=1, keepdims=True)
    rest = jnp.where(idx == i1, -jnp.inf, logits)
    v2 = jnp.max(rest, axis=1, keepdims=True)
    i2 = jnp.min(jnp.where(rest == v2, idx, N_EXPERTS), axis=1, keepdims=True)
    e2 = jnp.exp(v2 - v1)
    den = 1.0 + e2

    hit = ((idx == i1) | (idx == i2)).astype(F32)
    ri = lax.broadcasted_iota(jnp.int32, (tm, tm), 0)
    ci = lax.broadcasted_iota(jnp.int32, (tm, tm), 1)
    before = jnp.dot((ci < ri).astype(BF16), hit.astype(BF16), preferred_element_type=F32) + run_s[...]
    rank1 = jnp.sum(jnp.where(idx == i1, before, 0.0), axis=1, keepdims=True)
    rank2 = jnp.sum(jnp.where(idx == i2, before, 0.0), axis=1, keepdims=True)

    col = lax.broadcasted_iota(jnp.int32, (tm, 2), 1)
    idx_ref[...] = jnp.where(col == 0, i1, i2)
    prob_ref[...] = jnp.where(col == 0, 1.0 / den, e2 / den)
    rank_ref[...] = jnp.where(col == 0, rank1, rank2).astype(jnp.int32)
    total = run_s[...] + jnp.sum(hit, axis=0, keepdims=True)
    run_s[...] = total
    cnt_ref[...] = total.astype(jnp.int32)


def _router(x, router_w, router_b):
    m = x.shape[0]
    tm = _tile(m, 512)
    return pl.pallas_call(
        _router_kernel,
        grid=(m // tm,),
        in_specs=[pl.BlockSpec((tm, D_MODEL), lambda i: (i, 0)),
                  pl.BlockSpec((D_MODEL, N_EXPERTS), lambda i: (0, 0)),
                  pl.BlockSpec((1, N_EXPERTS), lambda i: (0, 0))],
        out_specs=[pl.BlockSpec((tm, 2), lambda i: (i, 0)),
                   pl.BlockSpec((tm, 2), lambda i: (i, 0)),
                   pl.BlockSpec((tm, 2), lambda i: (i, 0)),
                   pl.BlockSpec((1, N_EXPERTS), lambda i: (0, 0))],
        out_shape=[jax.ShapeDtypeStruct((m, 2), jnp.int32),
                   jax.ShapeDtypeStruct((m, 2), F32),
                   jax.ShapeDtypeStruct((m, 2), jnp.int32),
                   jax.ShapeDtypeStruct((1, N_EXPERTS), jnp.int32)],
        scratch_shapes=[pltpu.VMEM((1, N_EXPERTS), F32)],
        compiler_params=_params("arbitrary"),
        name="router",
    )(x, router_w, router_b)


def _route_tables(idx, rank, counts, tm, n_tiles):
    m = idx.shape[0]
    padded = ((counts + tm - 1) // tm) * tm
    ends = jnp.cumsum(padded)
    base = ends - padded
    pos = base[idx] + rank
    n_valid = ends[-1] // tm
    tiles = jnp.arange(n_tiles, dtype=jnp.int32)
    tile_valid = (tiles < n_valid).astype(jnp.int32)
    owner = jnp.searchsorted(ends, jnp.minimum(tiles, n_valid - 1) * tm, side="right")
    tile_expert = jnp.minimum(owner, N_EXPERTS - 1).astype(jnp.int32)
    flat = pos.reshape(-1)
    token = jnp.repeat(jnp.arange(m, dtype=jnp.int32), 2)
    src = jnp.zeros((n_tiles * tm,), jnp.int32).at[flat].set(token, unique_indices=True)
    return pos, src, tile_expert, tile_valid


def _row_copy(src_hbm, row, dst_vmem, dst_row, sem):
    return pltpu.make_async_copy(src_hbm.at[pl.ds(row, 1), :], dst_vmem.at[pl.ds(dst_row, 1), :], sem)


def _moe_group_kernel(te_ref, tv_ref, src_ref, src_next_ref, x_hbm, wg_ref, wu_ref, wd_ref,
                      o_ref, xg_s, xb_s, sem):
    i, f = pl.program_id(0), pl.program_id(1)
    nf = pl.num_programs(1)
    tm = xb_s.shape[0]
    chunk = tm // (MOE_F_STEPS - 1)
    slot = i % 2
    valid = tv_ref[i] == 1

    @pl.when((i == 0) & (f == 0))
    def _():
        def body(r, c):
            _row_copy(x_hbm, src_ref[0, r], xg_s.at[0], r, sem.at[0]).start()
            return c
        lax.fori_loop(0, tm, body, 0, unroll=8)

    @pl.when(f == 0)
    def _():
        o_ref[...] = jnp.zeros_like(o_ref)

    @pl.when((f == 0) & ((i == 0) | (tv_ref[jnp.maximum(i - 1, 0)] == 1)))
    def _():
        pltpu.make_async_copy(xg_s.at[slot], xg_s.at[slot], sem.at[slot]).wait()
        xb_s[...] = xg_s[slot].astype(BF16)

    def swiglu_step(issue_next):
        if issue_next:
            for r in range(chunk):
                row = f * chunk + r
                _row_copy(x_hbm, src_next_ref[0, row], xg_s.at[1 - slot], row, sem.at[1 - slot]).start()
        xb = xb_s[...]
        hid = (_silu(jnp.dot(xb, wg_ref[...], preferred_element_type=F32))
               * jnp.dot(xb, wu_ref[...], preferred_element_type=F32))
        o_ref[...] += jnp.dot(hid.astype(BF16), wd_ref[...], preferred_element_type=F32)

    @pl.when(valid & (f < nf - 1))
    def _():
        swiglu_step(True)

    @pl.when(valid & (f == nf - 1))
    def _():
        swiglu_step(False)


def _moe_group(x, src, tile_expert, tile_valid, wg, wu, wd, tm, n_tiles):
    tf = D_FF // MOE_F_STEPS
    nf = MOE_F_STEPS

    def w_col(i, f, te, tv):
        return (te[i], jnp.where(tv[i] == 1, f, nf - 1), 0, 0)

    def w_row(i, f, te, tv):
        return (te[i], jnp.where(tv[i] == 1, f, nf - 1), 0)

    grid_spec = pltpu.PrefetchScalarGridSpec(
        num_scalar_prefetch=2,
        grid=(n_tiles, nf),
        in_specs=[pl.BlockSpec((None, 1, tm), lambda i, f, te, tv: (i, 0, 0), memory_space=pltpu.SMEM),
                  pl.BlockSpec((None, 1, tm), lambda i, f, te, tv: (jnp.minimum(i + 1, n_tiles - 1), 0, 0),
                               memory_space=pltpu.SMEM),
                  pl.BlockSpec(memory_space=pl.ANY),
                  pl.BlockSpec((None, None, D_MODEL, tf), w_col),
                  pl.BlockSpec((None, None, D_MODEL, tf), w_col),
                  pl.BlockSpec((None, tf, D_MODEL), w_row)],
        out_specs=pl.BlockSpec((tm, D_MODEL), lambda i, f, te, tv: (i, 0)),
        scratch_shapes=[pltpu.VMEM((2, tm, D_MODEL), F32),
                        pltpu.VMEM((tm, D_MODEL), BF16),
                        pltpu.SemaphoreType.DMA((2,))],
    )
    src3 = src.reshape(n_tiles, 1, tm)
    return pl.pallas_call(
        _moe_group_kernel,
        grid_spec=grid_spec,
        out_shape=jax.ShapeDtypeStruct((n_tiles * tm, D_MODEL), F32),
        compiler_params=_params("arbitrary", "arbitrary"),
        name="moe_group",
    )(tile_expert, tile_valid, src3, src3, x, wg, wu, wd)

def _combine_kernel(pos_ref, pos_next_ref, x_ref, p_ref, y_hbm, g_ref, b_ref, op_ref, os_ref, ybuf, sem,
                    *, prompt_tiles):
    i, nt = pl.program_id(0), pl.num_programs(0)
    tm = x_ref.shape[0]
    slot = i % 2

    def issue(idx_ref, s):
        def body(r, c):
            _row_copy(y_hbm, idx_ref[0, 2 * r], ybuf.at[s, 0], r, sem.at[s]).start()
            _row_copy(y_hbm, idx_ref[0, 2 * r + 1], ybuf.at[s, 1], r, sem.at[s]).start()
            return c
        lax.fori_loop(0, tm, body, 0, unroll=4)

    @pl.when(i == 0)
    def _():
        issue(pos_ref, 0)

    @pl.when(i + 1 < nt)
    def _():
        issue(pos_next_ref, 1 - slot)

    pltpu.make_async_copy(ybuf.at[slot], ybuf.at[slot], sem.at[slot]).wait()
    moe = p_ref[:, 0:1] * ybuf[slot, 0] + p_ref[:, 1:2] * ybuf[slot, 1]
    y = _layer_norm(ALPHA * x_ref[...] + moe, g_ref[...], b_ref[...])

    @pl.when(i < prompt_tiles)
    def _():
        op_ref[...] = y

    @pl.when(i >= prompt_tiles)
    def _():
        os_ref[...] = y


def _combine_ln(x, prob, y_sorted, pos, g, b, mp):
    m = x.shape[0]
    ms = m - mp
    tm = _tile(np.gcd(mp, ms), 256)
    nt, npt = m // tm, mp // tm
    pos3 = pos.reshape(nt, 1, 2 * tm)
    return pl.pallas_call(
        functools.partial(_combine_kernel, prompt_tiles=npt),
        grid=(nt,),
        in_specs=[pl.BlockSpec((None, 1, 2 * tm), lambda i: (i, 0, 0), memory_space=pltpu.SMEM),
                  pl.BlockSpec((None, 1, 2 * tm), lambda i: (jnp.minimum(i + 1, nt - 1), 0, 0),
                               memory_space=pltpu.SMEM),
                  pl.BlockSpec((tm, D_MODEL), lambda i: (i, 0)),
                  pl.BlockSpec((tm, 2), lambda i: (i, 0)),
                  pl.BlockSpec(memory_space=pl.ANY),
                  pl.BlockSpec((1, D_MODEL), lambda i: (0, 0)),
                  pl.BlockSpec((1, D_MODEL), lambda i: (0, 0))],
        out_specs=[pl.BlockSpec((tm, D_MODEL), lambda i: (jnp.minimum(i, npt - 1), 0)),
                   pl.BlockSpec((tm, D_MODEL), lambda i: (jnp.maximum(i - npt, 0), 0))],
        out_shape=[jax.ShapeDtypeStruct((mp, D_MODEL), F32),
                   jax.ShapeDtypeStruct((ms, D_MODEL), F32)],
        scratch_shapes=[pltpu.VMEM((2, 2, tm, D_MODEL), F32), pltpu.SemaphoreType.DMA((2,))],
        compiler_params=_params("arbitrary"),
        name="combine_ln",
    )(pos3, pos3, x, prob, y_sorted, g, b)


def _column_tiles(w_experts):
    e = w_experts.shape[0]
    tiled = w_experts.astype(BF16).reshape(e, D_MODEL, MOE_F_STEPS, D_FF // MOE_F_STEPS)
    return jnp.transpose(tiled, (0, 2, 1, 3))


def _moe_ln(x, mp, router_w, router_b, wg, wu, wd, g, b):
    m = x.shape[0]
    tm = MOE_ROWS
    n_tiles = -(-2 * m // tm) + N_EXPERTS
    idx, prob, rank, counts = _router(x, router_w, router_b)
    pos, src, tile_expert, tile_valid = _route_tables(idx, rank, counts[0], tm, n_tiles)
    y_sorted = _moe_group(x, src, tile_expert, tile_valid, wg, wu, wd, tm, n_tiles)
    return _combine_ln(x, prob, y_sorted, pos, g, b, mp)


def kernel(x_prompt, x_sample, state_conv_a, cache_swa_k, cache_swa_v, state_conv_c, w_in, b_gate, conv_a_w, conv_a_b, ln_a_g, ln_a_b, w_branch_a, sinks, w_branch_b, conv_c_w, w_branch_c, ln_d_g, ln_d_b, sgu_w, sgu_b, w_branch_d, w_out, ln1_g, ln1_b, ffn_w_gate, ffn_w_up, ffn_w_down, router_w, router_b, exp_w_gate, exp_w_up, exp_w_down, ln2_g, ln2_b):
    nb, t_len, d = x_prompt.shape
    nseq, s_len, _ = x_sample.shape
    depth = w_in.shape[0]
    mp, ms = nb * t_len, nseq * s_len
    w = WINDOW

    x = jnp.concatenate([x_prompt.reshape(mp, d), x_sample.reshape(ms, d)], axis=0)
    xb = x.astype(BF16)

    cos_p, sin_p = _rope_tables(jnp.arange(t_len))
    cos_s, sin_s = _rope_tables(PAST_LEN + jnp.arange(s_len))

    def row(v):
        return v.reshape(1, -1)

    outs = {k: [] for k in ("pa", "pk", "pv", "pc", "sa", "sk", "sv", "sc", "sd")}
    for l in range(depth):
        w_in_b = w_in[l].astype(BF16)
        h = _in_proj(xb, w_in_b)

        wa, ba = conv_a_w[l], row(conv_a_b[l])
        lag, lab = row(ln_a_g[l]), row(ln_a_b[l])
        ya_p, yc_p, pa, pc = _ac_prompt(h, nb, t_len, wa, ba, lag, lab, conv_c_w[l])
        ya_s, yc_s, sa, sc = _ac_sample(h, mp, nseq, s_len, state_conv_a[l], state_conv_c[l],
                                        wa, ba, lag, lab, conv_c_w[l])

        yb_p, pk, pv = _attn_prompt(h, nb, t_len, sinks[l], cos_p, sin_p)
        yb_s, sk, sv = _attn_sample(h, mp, nseq, s_len, sinks[l],
                                    cache_swa_k[l].reshape(nseq, w, KV_WIDTH),
                                    cache_swa_v[l].reshape(nseq, w, KV_WIDTH), cos_s, sin_s)

        ldg, ldb = row(ln_d_g[l]), row(ln_d_b[l])
        yd_p = _d_prompt(h, mp, ldg, ldb, sgu_w[l], jnp.transpose(sgu_b[l]))
        sgu_wt = jnp.repeat(jnp.transpose(sgu_w[l][:, :s_len, :s_len], (2, 1, 0)), GROUP_D, axis=2)
        sgu_bs = jnp.repeat(jnp.transpose(sgu_b[l][:, :s_len]), GROUP_D, axis=1)
        yd_s, sd = _d_sample(h, mp, nseq, s_len, ldg, ldb, sgu_wt, sgu_bs)

        merged = _merge(xb, (ya_p, yb_p, yc_p, yd_p), (ya_s, yb_s, yc_s, yd_s), w_in_b, b_gate[l],
                        (w_branch_a[l].astype(BF16), w_branch_b[l].astype(BF16),
                         w_branch_c[l].astype(BF16), w_branch_d[l].astype(BF16)))
        x, xb = _out_ln(merged, x, w_out[l].astype(BF16), row(ln1_g[l]), row(ln1_b[l]))

        i = l // 2
        if l % 2 == 0:
            x, xb = _ffn_ln(x, xb, ffn_w_gate[i].astype(BF16), ffn_w_up[i].astype(BF16),
                            ffn_w_down[i].astype(BF16), row(ln2_g[l]), row(ln2_b[l]))
            x_p, x_s = x[:mp], x[mp:]
        else:
            x_p, x_s = _moe_ln(x, mp, router_w[i], row(router_b[i]), _column_tiles(exp_w_gate[i]),
                               _column_tiles(exp_w_up[i]), exp_w_down[i].astype(BF16),
                               row(ln2_g[l]), row(ln2_b[l]))
            if l + 1 < depth:
                x = jnp.concatenate([x_p, x_s], axis=0)
                xb = x.astype(BF16)

        outs["pa"].append(pa)
        outs["pk"].append(pk.reshape(nb, w, N_KV, HEAD_DIM))
        outs["pv"].append(pv.reshape(nb, w, N_KV, HEAD_DIM))
        outs["pc"].append(pc)
        outs["sa"].append(sa)
        outs["sk"].append(sk.reshape(nseq, w, N_KV, HEAD_DIM))
        outs["sv"].append(sv.reshape(nseq, w, N_KV, HEAD_DIM))
        outs["sc"].append(sc)
        outs["sd"].append(sd)

    y_prompt = x_p.reshape(nb, t_len, d)
    y_sample = x_s.reshape(nseq, s_len, d)
    st = {k: jnp.stack(v) for k, v in outs.items()}
    return (y_prompt, y_sample, st["pa"], st["pk"], st["pv"], st["pc"],
            st["sa"], st["sk"], st["sv"], st["sc"], st["sd"])
```

```python
import functools

import jax
import jax.numpy as jnp
import numpy as np
from jax import lax
from jax.experimental import pallas as pl
from jax.experimental.pallas import tpu as pltpu

D_MODEL = 2048
PAST_LEN = 8192
D_A = 512
CONV_A = 31
HEAD_DIM = 64
N_HEADS = 16
N_KV = 4
GQA = N_HEADS // N_KV
WINDOW = 128
ROPE_THETA = 10000.0
D_C = 512
CONV_C = 3
D_D = 512
CHUNK = 128
N_GROUPS_D = 4
GROUP_D = D_D // N_GROUPS_D
N_BRANCH = 4
Q_WIDTH = N_HEADS * HEAD_DIM
KV_WIDTH = N_KV * HEAD_DIM
D_FF = 5632
N_EXPERTS = 8
ALPHA = 4.0 ** 0.25
LN_EPS = 1e-5

OFF_A_VAL = 0
OFF_A_GATE = D_A
OFF_Q = 2 * D_A
OFF_K = OFF_Q + Q_WIDTH
OFF_V = OFF_K + KV_WIDTH
OFF_C_B = OFF_V + KV_WIDTH
OFF_C_C = OFF_C_B + D_C
OFF_C_X = OFF_C_C + D_C
OFF_D_U = OFF_C_X + D_C
OFF_D_V = OFF_D_U + D_D
OFF_GATES = OFF_D_V + D_D

LANES = 128
SUBLANES = 8
HALO = 32
VMEM_LIMIT = 48 * 1024 * 1024
OUT_LN_SUBTILES = 4
MOE_F_STEPS = 11
MOE_ROWS = 64 * (MOE_F_STEPS - 1)

BF16 = jnp.bfloat16
F32 = jnp.float32


def _tile(n, pref):
    n = int(n)
    if n <= pref:
        return n
    for t in range(pref, 7, -1):
        if n % t == 0 and t % 8 == 0:
            return t
    return n


def _params(*sem):
    return pltpu.CompilerParams(dimension_semantics=sem, vmem_limit_bytes=VMEM_LIMIT)


def _layer_norm(x, g, b):
    mu = jnp.mean(x, axis=-1, keepdims=True)
    xc = x - mu
    var = jnp.mean(xc * xc, axis=-1, keepdims=True)
    return xc * lax.rsqrt(var + LN_EPS) * g + b


def _silu(x):
    return x * jax.nn.sigmoid(x)


def _gelu(x):
    return jax.nn.gelu(x, approximate=True)


def _mm_kernel(x_ref, w_ref, o_ref):
    o_ref[...] = jnp.dot(x_ref[...], w_ref[...], preferred_element_type=F32).astype(o_ref.dtype)


def _in_proj(xb, w_in_b):
    m, k = xb.shape
    n = OFF_GATES
    tm = _tile(m, 1024)
    tn = 1024
    return pl.pallas_call(
        _mm_kernel,
        grid=(m // tm, n // tn),
        in_specs=[pl.BlockSpec((tm, k), lambda i, j: (i, 0)),
                  pl.BlockSpec((k, tn), lambda i, j: (0, j))],
        out_specs=pl.BlockSpec((tm, tn), lambda i, j: (i, j)),
        out_shape=jax.ShapeDtypeStruct((m, n), F32),
        compiler_params=_params("parallel", "arbitrary"),
        name="in_proj",
    )(xb, w_in_b)


def _ac_prompt_kernel(av_ref, ag_ref, cb_ref, cc_ref, cx_ref,
                      avp_ref, agp_ref, ccp_ref, cxp_ref,
                      wa_ref, ba_ref, lg_ref, lb_ref, wc_ref,
                      ya_ref, yc_ref, ha_ref, hc_ref, ga_s, gc_s, sh_s):
    t = pl.program_id(1)
    tt = av_ref.shape[0]
    first = t == 0
    glu_prev = avp_ref[...] * jax.nn.sigmoid(agp_ref[...])
    ga_s[0:HALO, :] = jnp.where(first, 0.0, glu_prev)
    ga_s[HALO:HALO + tt, :] = av_ref[...] * jax.nn.sigmoid(ag_ref[...])
    acc = jnp.zeros((tt, D_A), F32) + ba_ref[...]
    base = HALO - (CONV_A - 1)
    for phase in range(SUBLANES):
        offs = [o for o in range(base, base + CONV_A) if o % SUBLANES == phase]
        span = offs[-1] - phase + tt
        sh_s[0:span, :] = ga_s[phase:phase + span, :]
        for o in offs:
            acc = acc + wa_ref[o - base:o - base + 1, :] * sh_s[o - phase:o - phase + tt, :]
    ya = _silu(_layer_norm(acc, lg_ref[...], lb_ref[...]))
    ya_ref[...] = ya.astype(ya_ref.dtype)
    ha_ref[0] = ga_s[HALO + tt - (CONV_A - 1):HALO + tt, :]

    gc_s[0:HALO, :] = jnp.where(first, 0.0, ccp_ref[...] * cxp_ref[...])
    gc_s[HALO:HALO + tt, :] = cc_ref[...] * cx_ref[...]
    base_c = HALO - (CONV_C - 1)
    yc = jnp.zeros((tt, D_C), F32)
    for j in range(CONV_C):
        yc = yc + wc_ref[j:j + 1, :] * gc_s[base_c + j:base_c + j + tt, :]
    yc_ref[...] = (cb_ref[...] * yc).astype(yc_ref.dtype)
    hc_ref[0] = gc_s[HALO + tt - (CONV_C - 1):HALO + tt, :]


def _ac_prompt(h, nb, t_len, wa, ba, lg, lb, wc):
    tt = _tile(t_len, 256)
    nt = t_len // tt
    cw = D_A

    def cur(col):
        return pl.BlockSpec((tt, cw), lambda b, t: (b * nt + t, col // cw))

    def prev(col):
        return pl.BlockSpec(
            (HALO, cw),
            lambda b, t: (jnp.maximum((b * t_len + t * tt) // HALO - 1, 0), col // cw))

    def full(shape):
        return pl.BlockSpec(shape, lambda b, t: (0,) * len(shape))

    mp = nb * t_len
    return pl.pallas_call(
        _ac_prompt_kernel,
        grid=(nb, nt),
        in_specs=[cur(OFF_A_VAL), cur(OFF_A_GATE), cur(OFF_C_B), cur(OFF_C_C), cur(OFF_C_X),
                  prev(OFF_A_VAL), prev(OFF_A_GATE), prev(OFF_C_C), prev(OFF_C_X),
                  full((CONV_A, D_A)), full((1, D_A)), full((1, D_A)), full((1, D_A)),
                  full((CONV_C, D_C))],
        out_specs=[pl.BlockSpec((tt, D_A), lambda b, t: (b * nt + t, 0)),
                   pl.BlockSpec((tt, D_C), lambda b, t: (b * nt + t, 0)),
                   pl.BlockSpec((1, CONV_A - 1, D_A), lambda b, t: (b, 0, 0)),
                   pl.BlockSpec((1, CONV_C - 1, D_C), lambda b, t: (b, 0, 0))],
        out_shape=[jax.ShapeDtypeStruct((mp, D_A), BF16),
                   jax.ShapeDtypeStruct((mp, D_C), BF16),
                   jax.ShapeDtypeStruct((nb, CONV_A - 1, D_A), F32),
                   jax.ShapeDtypeStruct((nb, CONV_C - 1, D_C), F32)],
        scratch_shapes=[pltpu.VMEM((HALO + tt, D_A), F32), pltpu.VMEM((HALO + tt, D_C), F32),
                        pltpu.VMEM((HALO + tt, D_A), F32)],
        compiler_params=_params("parallel", "arbitrary"),
        name="ac_prompt",
    )(h, h, h, h, h, h, h, h, h, wa, ba, lg, lb, wc)


def _ac_sample_kernel(av_ref, ag_ref, cb_ref, cc_ref, cx_ref, hista_ref, histc_ref,
                      wa_ref, ba_ref, lg_ref, lb_ref, wc_ref,
                      ya_ref, yc_ref, ha_ref, hc_ref, xa_s, xc_s):
    sb, s_len = hista_ref.shape[0], av_ref.shape[0] // hista_ref.shape[0]
    ka, kc = CONV_A - 1, CONV_C - 1
    glu = av_ref[...] * jax.nn.sigmoid(ag_ref[...])
    xa_s[:, 0:ka, :] = hista_ref[...]
    xa_s[:, ka:ka + s_len, :] = glu.reshape(sb, s_len, D_A)
    acc = jnp.zeros((sb, s_len, D_A), F32) + ba_ref[...][None]
    for j in range(CONV_A):
        acc = acc + wa_ref[j:j + 1, :][None] * xa_s[:, j:j + s_len, :]
    ya = _silu(_layer_norm(acc, lg_ref[...][None], lb_ref[...][None]))
    ya_ref[...] = ya.reshape(sb * s_len, D_A).astype(ya_ref.dtype)
    ha_ref[...] = xa_s[:, s_len:s_len + ka, :]

    xc_s[:, 0:kc, :] = histc_ref[...]
    xc_s[:, kc:kc + s_len, :] = (cc_ref[...] * cx_ref[...]).reshape(sb, s_len, D_C)
    yc = jnp.zeros((sb, s_len, D_C), F32)
    for j in range(CONV_C):
        yc = yc + wc_ref[j:j + 1, :][None] * xc_s[:, j:j + s_len, :]
    yc = cb_ref[...] * yc.reshape(sb * s_len, D_C)
    yc_ref[...] = yc.astype(yc_ref.dtype)
    hc_ref[...] = xc_s[:, s_len:s_len + kc, :]


def _ac_sample(h, row0, nseq, s_len, hist_a, hist_c, wa, ba, lg, lb, wc):
    sb = _tile(nseq, 16)
    rows = sb * s_len
    blk0 = row0 // rows
    cw = D_A

    def cur(col):
        return pl.BlockSpec((rows, cw), lambda i: (blk0 + i, col // cw))

    def full(shape):
        return pl.BlockSpec(shape, lambda i: (0,) * len(shape))

    ms = nseq * s_len
    ka, kc = CONV_A - 1, CONV_C - 1
    return pl.pallas_call(
        _ac_sample_kernel,
        grid=(nseq // sb,),
        in_specs=[cur(OFF_A_VAL), cur(OFF_A_GATE), cur(OFF_C_B), cur(OFF_C_C), cur(OFF_C_X),
                  pl.BlockSpec((sb, ka, D_A), lambda i: (i, 0, 0)),
                  pl.BlockSpec((sb, kc, D_C), lambda i: (i, 0, 0)),
                  full((CONV_A, D_A)), full((1, D_A)), full((1, D_A)), full((1, D_A)),
                  full((CONV_C, D_C))],
        out_specs=[pl.BlockSpec((rows, D_A), lambda i: (i, 0)),
                   pl.BlockSpec((rows, D_C), lambda i: (i, 0)),
                   pl.BlockSpec((sb, ka, D_A), lambda i: (i, 0, 0)),
                   pl.BlockSpec((sb, kc, D_C), lambda i: (i, 0, 0))],
        out_shape=[jax.ShapeDtypeStruct((ms, D_A), BF16),
                   jax.ShapeDtypeStruct((ms, D_C), BF16),
                   jax.ShapeDtypeStruct((nseq, ka, D_A), F32),
                   jax.ShapeDtypeStruct((nseq, kc, D_C), F32)],
        scratch_shapes=[pltpu.VMEM((sb, ka + s_len + 2, D_A), F32),
                        pltpu.VMEM((sb, kc + s_len + 6, D_C), F32)],
        compiler_params=_params("parallel"),
        name="ac_sample",
    )(h, h, h, h, h, hist_a, hist_c, wa, ba, lg, lb, wc)


def _rope_tables(pos):
    half = HEAD_DIM // 2
    inv_freq = jnp.power(ROPE_THETA, -jnp.arange(half, dtype=F32) * (2.0 / HEAD_DIM))
    ang = pos.astype(F32)[:, None] * inv_freq[None, :]
    cos, sin = jnp.cos(ang), jnp.sin(ang)
    cos_t = jnp.concatenate([cos, cos, cos, cos], axis=1)
    sin_t = jnp.concatenate([-sin, sin, -sin, sin], axis=1)
    return cos_t, sin_t


def _rope(x, cos_t, sin_t):
    half = HEAD_DIM // 2
    axis = x.ndim - 1
    shape = x.shape[:-1] + (LANES,)
    lane = lax.broadcasted_iota(jnp.int32, shape, axis)
    first = (lane % HEAD_DIM) < half
    out = []
    for i in range(x.shape[-1] // LANES):
        xi = x[..., LANES * i:LANES * (i + 1)]
        partner = jnp.where(first, pltpu.roll(xi, LANES - half, axis), pltpu.roll(xi, half, axis))
        out.append(xi * cos_t + partner * sin_t)
    return out


def _head_halves(t, axis, lo_valid):
    lane = lax.broadcasted_iota(jnp.int32, t.shape, axis)
    if lo_valid:
        lo = jnp.where(lane < HEAD_DIM, t, jnp.zeros_like(t))
        hi = pltpu.roll(lo, HEAD_DIM, axis)
    else:
        hi = jnp.where(lane >= HEAD_DIM, t, jnp.zeros_like(t))
        lo = pltpu.roll(hi, HEAD_DIM, axis)
    return lo, hi


def _attn_prompt_kernel(sinks_ref, q_ref, kc_ref, vc_ref, kp_ref, vp_ref,
                        cos_ref, sin_ref, cosp_ref, sinp_ref,
                        yb_ref, nk_ref, nv_ref):
    n = pl.program_id(1)
    w = WINDOW
    cos_c, sin_c = cos_ref[...], sin_ref[...]
    q_tiles = _rope(q_ref[...] * (HEAD_DIM ** -0.5), cos_c, sin_c)
    kc_tiles = _rope(kc_ref[...], cos_c, sin_c)
    kp_tiles = _rope(kp_ref[...], cosp_ref[...], sinp_ref[...])
    for i in range(KV_WIDTH // LANES):
        nk_ref[0, :, LANES * i:LANES * (i + 1)] = kc_tiles[i]
    nv_ref[0] = vc_ref[...]

    ci = lax.broadcasted_iota(jnp.int32, (2 * w, 2 * w), 0)
    qi = lax.broadcasted_iota(jnp.int32, (2 * w, 2 * w), 1) % w
    mask = (ci > qi) & (ci <= qi + w) & ((n > 0) | (ci >= w))
    lane_first = lax.broadcasted_iota(jnp.int32, (1, 2 * w), 1) < w

    qt_tiles = [t.T.astype(BF16) for t in q_tiles]
    zeros_t = jnp.zeros((HEAD_DIM, 2 * w), BF16)

    for h in range(N_KV):
        tile, lo_valid = h // 2, (h % 2 == 0)
        lanes = slice(LANES * tile, LANES * (tile + 1))
        kcat = jnp.concatenate([kp_tiles[tile], kc_tiles[tile]], axis=0)
        k_lo, k_hi = _head_halves(kcat.astype(BF16), 1, lo_valid)
        vt = jnp.concatenate([vp_ref[:, lanes], vc_ref[:, lanes]], axis=0).T.astype(BF16)
        vt = vt[0:HEAD_DIM] if lo_valid else vt[HEAD_DIM:2 * HEAD_DIM]
        vt_halves = (jnp.concatenate([vt, zeros_t], axis=0), jnp.concatenate([zeros_t, vt], axis=0))
        qst = jnp.concatenate([qt_tiles[2 * h], qt_tiles[2 * h + 1]], axis=1)
        ot = None
        for half_idx, k_half in enumerate((k_lo, k_hi)):
            s = jnp.dot(k_half, qst, preferred_element_type=F32)
            s = jnp.where(mask, s, -jnp.inf)
            sk = jnp.where(lane_first, sinks_ref[4 * h + half_idx], sinks_ref[4 * h + 2 + half_idx])
            m = jnp.maximum(jnp.max(s, axis=0, keepdims=True), sk)
            e = jnp.exp(s - m)
            den = jnp.sum(e, axis=0, keepdims=True) + jnp.exp(sk - m)
            p = (e / den).astype(BF16)
            term = jnp.dot(vt_halves[half_idx], p, preferred_element_type=F32)
            ot = term if ot is None else ot + term
        yb_ref[:, LANES * (2 * h):LANES * (2 * h + 1)] = ot[:, 0:w].T.astype(yb_ref.dtype)
        yb_ref[:, LANES * (2 * h + 1):LANES * (2 * h + 2)] = ot[:, w:2 * w].T.astype(yb_ref.dtype)


def _attn_prompt(h, nb, t_len, sinks, cos_t, sin_t):
    w = WINDOW
    nblk = t_len // w
    mp = nb * t_len

    def rows(b, n):
        return b * nblk + n

    def rows_prev(b, n):
        return b * nblk + jnp.maximum(n - 1, 0)

    return pl.pallas_call(
        _attn_prompt_kernel,
        grid=(nb, nblk),
        in_specs=[pl.BlockSpec(memory_space=pltpu.SMEM),
                  pl.BlockSpec((w, Q_WIDTH), lambda b, n: (rows(b, n), OFF_Q // Q_WIDTH)),
                  pl.BlockSpec((w, KV_WIDTH), lambda b, n: (rows(b, n), OFF_K // KV_WIDTH)),
                  pl.BlockSpec((w, KV_WIDTH), lambda b, n: (rows(b, n), OFF_V // KV_WIDTH)),
                  pl.BlockSpec((w, KV_WIDTH), lambda b, n: (rows_prev(b, n), OFF_K // KV_WIDTH)),
                  pl.BlockSpec((w, KV_WIDTH), lambda b, n: (rows_prev(b, n), OFF_V // KV_WIDTH)),
                  pl.BlockSpec((w, LANES), lambda b, n: (n, 0)),
                  pl.BlockSpec((w, LANES), lambda b, n: (n, 0)),
                  pl.BlockSpec((w, LANES), lambda b, n: (jnp.maximum(n - 1, 0), 0)),
                  pl.BlockSpec((w, LANES), lambda b, n: (jnp.maximum(n - 1, 0), 0))],
        out_specs=[pl.BlockSpec((w, Q_WIDTH), lambda b, n: (rows(b, n), 0)),
                   pl.BlockSpec((1, w, KV_WIDTH), lambda b, n: (b, 0, 0)),
                   pl.BlockSpec((1, w, KV_WIDTH), lambda b, n: (b, 0, 0))],
        out_shape=[jax.ShapeDtypeStruct((mp, Q_WIDTH), BF16),
                   jax.ShapeDtypeStruct((nb, w, KV_WIDTH), F32),
                   jax.ShapeDtypeStruct((nb, w, KV_WIDTH), F32)],
        compiler_params=_params("parallel", "arbitrary"),
        name="attn_prompt",
    )(sinks, h, h, h, h, h, cos_t, sin_t, cos_t, sin_t)


def _attn_sample_kernel(sinks_ref, q_ref, kn_ref, vn_ref, kbuf_ref, vbuf_ref, cos_ref, sin_ref,
                        yb_ref, nk_ref, nv_ref):
    sb, w = kbuf_ref.shape[0], kbuf_ref.shape[1]
    s_len = q_ref.shape[0] // sb
    cos_t, sin_t = cos_ref[...], sin_ref[...]
    q_tiles = _rope(q_ref[...] * (HEAD_DIM ** -0.5), cos_t, sin_t)
    kn_tiles = _rope(kn_ref[...], cos_t, sin_t)
    nk_ref[:, 0:w - s_len, :] = kbuf_ref[:, s_len:w, :]
    nv_ref[:, 0:w - s_len, :] = vbuf_ref[:, s_len:w, :]
    for i in range(KV_WIDTH // LANES):
        nk_ref[:, w - s_len:w, LANES * i:LANES * (i + 1)] = kn_tiles[i].reshape(sb, s_len, LANES)
    nv_ref[:, w - s_len:w, :] = vn_ref[...].reshape(sb, s_len, KV_WIDTH)

    qi = lax.broadcasted_iota(jnp.int32, (2 * s_len, w), 0) % s_len
    ci = lax.broadcasted_iota(jnp.int32, (2 * s_len, w), 1)
    mask_buf = (ci > qi)[None]
    qn = lax.broadcasted_iota(jnp.int32, (2 * s_len, s_len), 0) % s_len
    cn = lax.broadcasted_iota(jnp.int32, (2 * s_len, s_len), 1)
    mask_new = (cn <= qn)[None]
    row_top = (lax.broadcasted_iota(jnp.int32, (2 * s_len, 1), 0) < s_len)[None]
    bqk = (((2,), (2,)), ((0,), (0,)))
    bkd = (((2,), (1,)), ((0,), (0,)))

    for h in range(N_KV):
        tile, lo_valid = h // 2, (h % 2 == 0)
        sl = slice(LANES * tile, LANES * (tile + 1))
        kb_lo, kb_hi = _head_halves(kbuf_ref[:, :, sl].astype(BF16), 2, lo_valid)
        vb_lo, vb_hi = _head_halves(vbuf_ref[:, :, sl].astype(BF16), 2, lo_valid)
        kn3 = kn_tiles[tile].reshape(sb, s_len, LANES).astype(BF16)
        vn3 = vn_ref[:, sl].reshape(sb, s_len, LANES).astype(BF16)
        kn_lo, kn_hi = _head_halves(kn3, 2, lo_valid)
        vn_lo, vn_hi = _head_halves(vn3, 2, lo_valid)
        qs = jnp.concatenate([q_tiles[2 * h].reshape(sb, s_len, LANES),
                              q_tiles[2 * h + 1].reshape(sb, s_len, LANES)], axis=1).astype(BF16)
        o = jnp.zeros((sb, 2 * s_len, LANES), F32)
        for half_idx, (kb, kn, vb, vn) in enumerate(((kb_lo, kn_lo, vb_lo, vn_lo),
                                                     (kb_hi, kn_hi, vb_hi, vn_hi))):
            s_b = lax.dot_general(qs, kb, bqk, preferred_element_type=F32)
            s_n = lax.dot_general(qs, kn, bqk, preferred_element_type=F32)
            s_b = jnp.where(mask_buf, s_b, -jnp.inf)
            s_n = jnp.where(mask_new, s_n, -jnp.inf)
            sk = jnp.where(row_top, sinks_ref[4 * h + half_idx], sinks_ref[4 * h + 2 + half_idx])
            m = jnp.maximum(jnp.maximum(jnp.max(s_b, axis=2, keepdims=True),
                                        jnp.max(s_n, axis=2, keepdims=True)), sk)
            e_b = jnp.exp(s_b - m)
            e_n = jnp.exp(s_n - m)
            den = (jnp.sum(e_b, axis=2, keepdims=True) + jnp.sum(e_n, axis=2, keepdims=True)
                   + jnp.exp(sk - m))
            o = o + lax.dot_general((e_b / den).astype(BF16), vb, bkd, preferred_element_type=F32)
            o = o + lax.dot_general((e_n / den).astype(BF16), vn, bkd, preferred_element_type=F32)
        yb_ref[:, LANES * (2 * h):LANES * (2 * h + 1)] = (
            o[:, 0:s_len, :].reshape(sb * s_len, LANES).astype(yb_ref.dtype))
        yb_ref[:, LANES * (2 * h + 1):LANES * (2 * h + 2)] = (
            o[:, s_len:2 * s_len, :].reshape(sb * s_len, LANES).astype(yb_ref.dtype))


def _attn_sample(h, row0, nseq, s_len, sinks, k_buf, v_buf, cos_t, sin_t):
    sb = _tile(nseq, 8)
    rows = sb * s_len
    blk0 = row0 // rows
    w = k_buf.shape[1]
    ms = nseq * s_len
    cos_rows = jnp.tile(cos_t, (sb, 1))
    sin_rows = jnp.tile(sin_t, (sb, 1))
    return pl.pallas_call(
        _attn_sample_kernel,
        grid=(nseq // sb,),
        in_specs=[pl.BlockSpec(memory_space=pltpu.SMEM),
                  pl.BlockSpec((rows, Q_WIDTH), lambda i: (blk0 + i, OFF_Q // Q_WIDTH)),
                  pl.BlockSpec((rows, KV_WIDTH), lambda i: (blk0 + i, OFF_K // KV_WIDTH)),
                  pl.BlockSpec((rows, KV_WIDTH), lambda i: (blk0 + i, OFF_V // KV_WIDTH)),
                  pl.BlockSpec((sb, w, KV_WIDTH), lambda i: (i, 0, 0)),
                  pl.BlockSpec((sb, w, KV_WIDTH), lambda i: (i, 0, 0)),
                  pl.BlockSpec((rows, LANES), lambda i: (0, 0)),
                  pl.BlockSpec((rows, LANES), lambda i: (0, 0))],
        out_specs=[pl.BlockSpec((rows, Q_WIDTH), lambda i: (i, 0)),
                   pl.BlockSpec((sb, w, KV_WIDTH), lambda i: (i, 0, 0)),
                   pl.BlockSpec((sb, w, KV_WIDTH), lambda i: (i, 0, 0))],
        out_shape=[jax.ShapeDtypeStruct((ms, Q_WIDTH), BF16),
                   jax.ShapeDtypeStruct((nseq, w, KV_WIDTH), F32),
                   jax.ShapeDtypeStruct((nseq, w, KV_WIDTH), F32)],
        compiler_params=_params("parallel"),
        name="attn_sample",
    )(sinks, h, h, h, k_buf, v_buf, cos_rows, sin_rows)


def _d_prompt_kernel(du_ref, dv_ref, lg_ref, lb_ref, w_ref, bt_ref, yd_ref):
    c = CHUNK
    ri = lax.broadcasted_iota(jnp.int32, (c, c), 0)
    cj = lax.broadcasted_iota(jnp.int32, (c, c), 1)
    causal = cj <= ri
    w_causal = [jnp.where(causal, w_ref[g], 0.0).astype(BF16) for g in range(N_GROUPS_D)]
    for k in range(du_ref.shape[0] // c):
        rows = slice(k * c, (k + 1) * c)
        u = _gelu(du_ref[rows, :])
        vn = _layer_norm(_gelu(dv_ref[rows, :]), lg_ref[...], lb_ref[...]).astype(BF16)
        for g in range(N_GROUPS_D):
            sl = slice(GROUP_D * g, GROUP_D * (g + 1))
            s = jnp.dot(w_causal[g], vn[:, sl], preferred_element_type=F32) + bt_ref[:, g:g + 1]
            yd_ref[rows, sl] = (u[:, sl] * s).astype(yd_ref.dtype)


def _d_prompt(h, mp, ln_g, ln_b, sgu_w, sgu_bt):
    c = CHUNK
    rows = _tile(mp, 4 * c)
    assert rows % c == 0
    return pl.pallas_call(
        _d_prompt_kernel,
        grid=(mp // rows,),
        in_specs=[pl.BlockSpec((rows, D_D), lambda i: (i, OFF_D_U // D_D)),
                  pl.BlockSpec((rows, D_D), lambda i: (i, OFF_D_V // D_D)),
                  pl.BlockSpec((1, D_D), lambda i: (0, 0)),
                  pl.BlockSpec((1, D_D), lambda i: (0, 0)),
                  pl.BlockSpec((N_GROUPS_D, c, c), lambda i: (0, 0, 0)),
                  pl.BlockSpec((c, N_GROUPS_D), lambda i: (0, 0))],
        out_specs=pl.BlockSpec((rows, D_D), lambda i: (i, 0)),
        out_shape=jax.ShapeDtypeStruct((mp, D_D), BF16),
        compiler_params=_params("parallel"),
        name="d_prompt",
    )(h, h, ln_g, ln_b, sgu_w, sgu_bt)


def _d_sample_kernel(du_ref, dv_ref, lg_ref, lb_ref, wt_ref, bt_ref, yd_ref, vd_ref):
    sb, s_len = vd_ref.shape[0], vd_ref.shape[1]
    u = _gelu(du_ref[...])
    vn = _layer_norm(_gelu(dv_ref[...]), lg_ref[...], lb_ref[...])
    vn3 = vn.reshape(sb, s_len, D_D)
    vd_ref[...] = vn3
    ii = lax.broadcasted_iota(jnp.int32, (s_len, D_D), 0)
    s = jnp.zeros((sb, s_len, D_D), F32) + bt_ref[...][None]
    for j in range(s_len):
        wj = jnp.where(ii >= j, wt_ref[j], 0.0)
        s = s + wj[None] * vn3[:, j:j + 1, :]
    yd_ref[...] = (u * s.reshape(sb * s_len, D_D)).astype(yd_ref.dtype)


def _d_sample(h, row0, nseq, s_len, ln_g, ln_b, sgu_wt, sgu_bs):
    sb = _tile(nseq, 16)
    rows = sb * s_len
    blk0 = row0 // rows
    ms = nseq * s_len
    return pl.pallas_call(
        _d_sample_kernel,
        grid=(nseq // sb,),
        in_specs=[pl.BlockSpec((rows, D_D), lambda i: (blk0 + i, OFF_D_U // D_D)),
                  pl.BlockSpec((rows, D_D), lambda i: (blk0 + i, OFF_D_V // D_D)),
                  pl.BlockSpec((1, D_D), lambda i: (0, 0)),
                  pl.BlockSpec((1, D_D), lambda i: (0, 0)),
                  pl.BlockSpec((s_len, s_len, D_D), lambda i: (0, 0, 0)),
                  pl.BlockSpec((s_len, D_D), lambda i: (0, 0))],
        out_specs=[pl.BlockSpec((rows, D_D), lambda i: (i, 0)),
                   pl.BlockSpec((sb, s_len, D_D), lambda i: (i, 0, 0))],
        out_shape=[jax.ShapeDtypeStruct((ms, D_D), BF16),
                   jax.ShapeDtypeStruct((nseq, s_len, D_D), F32)],
        compiler_params=_params("parallel"),
        name="d_sample",
    )(h, h, ln_g, ln_b, sgu_wt, sgu_bs)


def _merge_kernel(x_ref, yap_ref, ybp_ref, ycp_ref, ydp_ref, yas_ref, ybs_ref, ycs_ref, yds_ref,
                  g0_ref, g1_ref, g2_ref, g3_ref, bg_ref, pa_ref, pb_ref, pc_ref, pd_ref, o_ref,
                  *, prompt_tiles):
    x = x_ref[...]
    is_prompt = pl.program_id(1) < prompt_tiles
    acc = None
    branches = ((yap_ref, yas_ref, g0_ref, pa_ref), (ybp_ref, ybs_ref, g1_ref, pb_ref),
                (ycp_ref, ycs_ref, g2_ref, pc_ref), (ydp_ref, yds_ref, g3_ref, pd_ref))
    for i, (yp_ref, ys_ref, g_ref, p_ref) in enumerate(branches):
        y = jnp.where(is_prompt, yp_ref[...], ys_ref[...])
        gate = jax.nn.sigmoid(jnp.dot(x, g_ref[...], preferred_element_type=F32) + bg_ref[i:i + 1, :])
        term = gate * jnp.dot(y, p_ref[...], preferred_element_type=F32)
        acc = term if acc is None else acc + term
    o_ref[...] = acc.astype(o_ref.dtype)


def _merge(xb, y_prompt, y_sample, w_in_b, b_gate, projs):
    m = xb.shape[0]
    mp, ms = y_prompt[0].shape[0], y_sample[0].shape[0]
    tm = _tile(np.gcd(mp, ms), 512)
    npt = mp // tm
    tn = 512
    nn = D_MODEL // tn
    widths = (D_A, Q_WIDTH, D_C, D_D)

    def act_p(width):
        return pl.BlockSpec((tm, width), lambda j, i: (jnp.minimum(i, npt - 1), 0))

    def act_s(width):
        return pl.BlockSpec((tm, width), lambda j, i: (jnp.maximum(i - npt, 0), 0))

    def gate_w(b):
        return pl.BlockSpec((D_MODEL, tn), lambda j, i: (0, (OFF_GATES + b * D_MODEL) // tn + j))

    def proj_w(k):
        return pl.BlockSpec((k, tn), lambda j, i: (0, j))

    return pl.pallas_call(
        functools.partial(_merge_kernel, prompt_tiles=npt),
        grid=(nn, m // tm),
        in_specs=[pl.BlockSpec((tm, D_MODEL), lambda j, i: (i, 0))]
                 + [act_p(wd) for wd in widths] + [act_s(wd) for wd in widths]
                 + [gate_w(0), gate_w(1), gate_w(2), gate_w(3),
                    pl.BlockSpec((N_BRANCH, tn), lambda j, i: (0, j))]
                 + [proj_w(wd) for wd in widths],
        out_specs=pl.BlockSpec((tm, tn), lambda j, i: (i, j)),
        out_shape=jax.ShapeDtypeStruct((m, D_MODEL), BF16),
        compiler_params=_params("parallel", "arbitrary"),
        name="merge",
    )(xb, *y_prompt, *y_sample, w_in_b, w_in_b, w_in_b, w_in_b, b_gate, *projs)


def _out_ln_kernel(mg_ref, *refs, prompt_tiles):
    if prompt_tiles is None:
        x_ref, w_ref, g_ref, b_ref, o_ref, ob_ref = refs
    else:
        xp_ref, xs_ref, w_ref, g_ref, b_ref, o_ref, ob_ref = refs
        is_prompt = pl.program_id(0) < prompt_tiles
    tm = mg_ref.shape[0]
    sub = tm // OUT_LN_SUBTILES
    for k in range(OUT_LN_SUBTILES):
        rows = slice(k * sub, (k + 1) * sub)
        if prompt_tiles is None:
            x = x_ref[rows, :]
        else:
            x = jnp.where(is_prompt, xp_ref[rows, :], xs_ref[rows, :])
        y = ALPHA * x + jnp.dot(mg_ref[rows, :], w_ref[...], preferred_element_type=F32)
        y = _layer_norm(y, g_ref[...], b_ref[...])
        o_ref[rows, :] = y
        ob_ref[rows, :] = y.astype(ob_ref.dtype)


def _out_ln(merged, x, w_out_b, g, b):
    m = merged.shape[0]
    if isinstance(x, tuple):
        mp, ms = x[0].shape[0], x[1].shape[0]
        tm = _tile(np.gcd(mp, ms), 512)
        npt = mp // tm
        x_specs = [pl.BlockSpec((tm, D_MODEL), lambda i: (jnp.minimum(i, npt - 1), 0)),
                   pl.BlockSpec((tm, D_MODEL), lambda i: (jnp.maximum(i - npt, 0), 0))]
    else:
        tm = _tile(m, 512)
        npt = None
        x, x_specs = (x,), [pl.BlockSpec((tm, D_MODEL), lambda i: (i, 0))]
    return pl.pallas_call(
        functools.partial(_out_ln_kernel, prompt_tiles=npt),
        grid=(m // tm,),
        in_specs=[pl.BlockSpec((tm, D_MODEL), lambda i: (i, 0))] + x_specs
                 + [pl.BlockSpec((D_MODEL, D_MODEL), lambda i: (0, 0), pipeline_mode=pl.Buffered(1)),
                    pl.BlockSpec((1, D_MODEL), lambda i: (0, 0)),
                    pl.BlockSpec((1, D_MODEL), lambda i: (0, 0))],
        out_specs=[pl.BlockSpec((tm, D_MODEL), lambda i: (i, 0)),
                   pl.BlockSpec((tm, D_MODEL), lambda i: (i, 0))],
        out_shape=[jax.ShapeDtypeStruct((m, D_MODEL), F32),
                   jax.ShapeDtypeStruct((m, D_MODEL), BF16)],
        compiler_params=_params("parallel"),
        name="out_ln",
    )(merged, *x, w_out_b, g, b)


def _ffn_kernel(x_ref, xb_ref, wg_ref, wu_ref, wd_ref, g_ref, b_ref, o_ref, ob_ref, acc_s):
    f = pl.program_id(1)

    @pl.when(f == 0)
    def _():
        acc_s[...] = jnp.zeros_like(acc_s)

    xb = xb_ref[...]
    hid = (_silu(jnp.dot(xb, wg_ref[...], preferred_element_type=F32))
           * jnp.dot(xb, wu_ref[...], preferred_element_type=F32))
    acc_s[...] += jnp.dot(hid.astype(BF16), wd_ref[...], preferred_element_type=F32)

    @pl.when(f == pl.num_programs(1) - 1)
    def _():
        y = _layer_norm(ALPHA * x_ref[...] + acc_s[...], g_ref[...], b_ref[...])
        o_ref[...] = y
        ob_ref[...] = y.astype(ob_ref.dtype)


def _ffn_ln(x, xb, wg, wu, wd, g, b):
    m = x.shape[0]
    tm = _tile(m, 512)
    tf = 512
    return pl.pallas_call(
        _ffn_kernel,
        grid=(m // tm, D_FF // tf),
        in_specs=[pl.BlockSpec((tm, D_MODEL), lambda i, f: (i, 0)),
                  pl.BlockSpec((tm, D_MODEL), lambda i, f: (i, 0)),
                  pl.BlockSpec((D_MODEL, tf), lambda i, f: (0, f)),
                  pl.BlockSpec((D_MODEL, tf), lambda i, f: (0, f)),
                  pl.BlockSpec((tf, D_MODEL), lambda i, f: (f, 0)),
                  pl.BlockSpec((1, D_MODEL), lambda i, f: (0, 0)),
                  pl.BlockSpec((1, D_MODEL), lambda i, f: (0, 0))],
        out_specs=[pl.BlockSpec((tm, D_MODEL), lambda i, f: (i, 0)),
                   pl.BlockSpec((tm, D_MODEL), lambda i, f: (i, 0))],
        out_shape=[jax.ShapeDtypeStruct((m, D_MODEL), F32),
                   jax.ShapeDtypeStruct((m, D_MODEL), BF16)],
        scratch_shapes=[pltpu.VMEM((tm, D_MODEL), F32)],
        compiler_params=_params("parallel", "arbitrary"),
        name="ffn_ln",
    )(x, xb, wg, wu, wd, g, b)


def _router_kernel(x_ref, w_ref, b_ref, idx_ref, prob_ref, rank_ref, cnt_ref, run_s):
    @pl.when(pl.program_id(0) == 0)
    def _():
        run_s[...] = jnp.zeros_like(run_s)

    logits = jnp.dot(x_ref[...], w_ref[...], preferred_element_type=F32,
                     precision=lax.Precision.HIGHEST) + b_ref[...]
    tm = logits.shape[0]
    idx = lax.broadcasted_iota(jnp.int32, logits.shape, 1)
    v1 = jnp.max(logits, axis=1, keepdims=True)
    i1 = jnp.min(jnp.where(logits == v1, idx, N_EXPERTS), axis=1, keepdims=True)
    rest = jnp.where(idx == i1, -jnp.inf, logits)
    v2 = jnp.max(rest, axis=1, keepdims=True)
    i2 = jnp.min(jnp.where(rest == v2, idx, N_EXPERTS), axis=1, keepdims=True)
    e2 = jnp.exp(v2 - v1)
    den = 1.0 + e2

    hit = ((idx == i1) | (idx == i2)).astype(F32)
    ri = lax.broadcasted_iota(jnp.int32, (tm, tm), 0)
    ci = lax.broadcasted_iota(jnp.int32, (tm, tm), 1)
    before = jnp.dot((ci < ri).astype(BF16), hit.astype(BF16), preferred_element_type=F32) + run_s[...]
    rank1 = jnp.sum(jnp.where(idx == i1, before, 0.0), axis=1, keepdims=True)
    rank2 = jnp.sum(jnp.where(idx == i2, before, 0.0), axis=1, keepdims=True)

    col = lax.broadcasted_iota(jnp.int32, (tm, 2), 1)
    idx_ref[...] = jnp.where(col == 0, i1, i2)
    prob_ref[...] = jnp.where(col == 0, 1.0 / den, e2 / den)
    rank_ref[...] = jnp.where(col == 0, rank1, rank2).astype(jnp.int32)
    total = run_s[...] + jnp.sum(hit, axis=0, keepdims=True)
    run_s[...] = total
    cnt_ref[...] = total.astype(jnp.int32)


def _router(x, router_w, router_b):
    m = x.shape[0]
    tm = _tile(m, 512)
    return pl.pallas_call(
        _router_kernel,
        grid=(m // tm,),
        in_specs=[pl.BlockSpec((tm, D_MODEL), lambda i: (i, 0)),
                  pl.BlockSpec((D_MODEL, N_EXPERTS), lambda i: (0, 0)),
                  pl.BlockSpec((1, N_EXPERTS), lambda i: (0, 0))],
        out_specs=[pl.BlockSpec((tm, 2), lambda i: (i, 0)),
                   pl.BlockSpec((tm, 2), lambda i: (i, 0)),
                   pl.BlockSpec((tm, 2), lambda i: (i, 0)),
                   pl.BlockSpec((1, N_EXPERTS), lambda i: (0, 0))],
        out_shape=[jax.ShapeDtypeStruct((m, 2), jnp.int32),
                   jax.ShapeDtypeStruct((m, 2), F32),
                   jax.ShapeDtypeStruct((m, 2), jnp.int32),
                   jax.ShapeDtypeStruct((1, N_EXPERTS), jnp.int32)],
        scratch_shapes=[pltpu.VMEM((1, N_EXPERTS), F32)],
        compiler_params=_params("arbitrary"),
        name="router",
    )(x, router_w, router_b)


def _route_tables(idx, rank, counts, tm, n_tiles):
    m = idx.shape[0]
    padded = ((counts + tm - 1) // tm) * tm
    ends = jnp.cumsum(padded)
    base = ends - padded
    pos = base[idx] + rank
    n_valid = ends[-1] // tm
    tiles = jnp.arange(n_tiles, dtype=jnp.int32)
    tile_valid = (tiles < n_valid).astype(jnp.int32)
    first_row = jnp.minimum(tiles, n_valid - 1) * tm
    owner = jnp.sum((ends[None, :] <= first_row[:, None]).astype(jnp.int32), axis=1)
    tile_expert = jnp.minimum(owner, N_EXPERTS - 1).astype(jnp.int32)
    flat = pos.reshape(-1)
    token = jnp.repeat(jnp.arange(m, dtype=jnp.int32), 2)
    src = jnp.zeros((n_tiles * tm,), jnp.int32).at[flat].set(token, unique_indices=True)
    return pos, src, tile_expert, tile_valid


def _row_copy(src_hbm, row, dst_vmem, dst_row, sem):
    return pltpu.make_async_copy(src_hbm.at[pl.ds(row, 1), :], dst_vmem.at[pl.ds(dst_row, 1), :], sem)


def _moe_group_kernel(te_ref, tv_ref, src_ref, src_next_ref, x_hbm, wg_ref, wu_ref, wd_ref,
                      o_ref, xg_s, xb_s, sem):
    i, f = pl.program_id(0), pl.program_id(1)
    nf = pl.num_programs(1)
    tm = xb_s.shape[0]
    chunk = tm // (MOE_F_STEPS - 1)
    slot = i % 2
    valid = tv_ref[i] == 1

    @pl.when((i == 0) & (f == 0))
    def _():
        def body(r, c):
            _row_copy(x_hbm, src_ref[0, r], xg_s.at[0], r, sem.at[0]).start()
            return c
        lax.fori_loop(0, tm, body, 0, unroll=8)

    @pl.when(f == 0)
    def _():
        o_ref[...] = jnp.zeros_like(o_ref)

    @pl.when((f == 0) & ((i == 0) | (tv_ref[jnp.maximum(i - 1, 0)] == 1)))
    def _():
        pltpu.make_async_copy(xg_s.at[slot], xg_s.at[slot], sem.at[slot]).wait()
        xb_s[...] = xg_s[slot].astype(BF16)

    def swiglu_step(issue_next):
        if issue_next:
            for r in range(chunk):
                row = f * chunk + r
                _row_copy(x_hbm, src_next_ref[0, row], xg_s.at[1 - slot], row,
                          sem.at[1 - slot]).start(priority=1)
        xb = xb_s[...]
        hid = (_silu(jnp.dot(xb, wg_ref[...], preferred_element_type=F32))
               * jnp.dot(xb, wu_ref[...], preferred_element_type=F32))
        o_ref[...] += jnp.dot(hid.astype(BF16), wd_ref[...], preferred_element_type=F32)

    @pl.when(valid & (f < nf - 1))
    def _():
        swiglu_step(True)

    @pl.when(valid & (f == nf - 1))
    def _():
        swiglu_step(False)


def _moe_group(x, src, tile_expert, tile_valid, wg, wu, wd, tm, n_tiles):
    tf = D_FF // MOE_F_STEPS
    nf = MOE_F_STEPS

    def w_col(i, f, te, tv):
        return (te[i], 0, jnp.where(tv[i] == 1, f, nf - 1))

    def w_row(i, f, te, tv):
        return (te[i], jnp.where(tv[i] == 1, f, nf - 1), 0)

    grid_spec = pltpu.PrefetchScalarGridSpec(
        num_scalar_prefetch=2,
        grid=(n_tiles, nf),
        in_specs=[pl.BlockSpec((None, 1, tm), lambda i, f, te, tv: (i, 0, 0), memory_space=pltpu.SMEM),
                  pl.BlockSpec((None, 1, tm), lambda i, f, te, tv: (jnp.minimum(i + 1, n_tiles - 1), 0, 0),
                               memory_space=pltpu.SMEM),
                  pl.BlockSpec(memory_space=pl.ANY),
                  pl.BlockSpec((None, D_MODEL, tf), w_col),
                  pl.BlockSpec((None, D_MODEL, tf), w_col),
                  pl.BlockSpec((None, tf, D_MODEL), w_row)],
        out_specs=pl.BlockSpec((tm, D_MODEL), lambda i, f, te, tv: (i, 0)),
        scratch_shapes=[pltpu.VMEM((2, tm, D_MODEL), F32),
                        pltpu.VMEM((tm, D_MODEL), BF16),
                        pltpu.SemaphoreType.DMA((2,))],
    )
    src3 = src.reshape(n_tiles, 1, tm)
    return pl.pallas_call(
        _moe_group_kernel,
        grid_spec=grid_spec,
        out_shape=jax.ShapeDtypeStruct((n_tiles * tm, D_MODEL), F32),
        compiler_params=_params("arbitrary", "arbitrary"),
        name="moe_group",
    )(tile_expert, tile_valid, src3, src3, x, wg, wu, wd)

def _combine_kernel(pos_ref, pos_next_ref, x_ref, p_ref, y_hbm, g_ref, b_ref, op_ref, os_ref, ybuf, sem,
                    *, prompt_tiles):
    i, nt = pl.program_id(0), pl.num_programs(0)
    tm = x_ref.shape[0]
    slot = i % 2

    def issue(idx_ref, s):
        def body(r, c):
            _row_copy(y_hbm, idx_ref[0, 2 * r], ybuf.at[s, 0], r, sem.at[s]).start()
            _row_copy(y_hbm, idx_ref[0, 2 * r + 1], ybuf.at[s, 1], r, sem.at[s]).start()
            return c
        lax.fori_loop(0, tm, body, 0, unroll=4)

    @pl.when(i == 0)
    def _():
        issue(pos_ref, 0)

    @pl.when(i + 1 < nt)
    def _():
        issue(pos_next_ref, 1 - slot)

    pltpu.make_async_copy(ybuf.at[slot], ybuf.at[slot], sem.at[slot]).wait()
    moe = p_ref[:, 0:1] * ybuf[slot, 0] + p_ref[:, 1:2] * ybuf[slot, 1]
    y = _layer_norm(ALPHA * x_ref[...] + moe, g_ref[...], b_ref[...])

    @pl.when(i < prompt_tiles)
    def _():
        op_ref[...] = y

    @pl.when(i >= prompt_tiles)
    def _():
        os_ref[...] = y


def _combine_ln(x, prob, y_sorted, pos, g, b, mp):
    m = x.shape[0]
    ms = m - mp
    tm = _tile(np.gcd(mp, ms), 256)
    nt, npt = m // tm, mp // tm
    pos3 = pos.reshape(nt, 1, 2 * tm)
    return pl.pallas_call(
        functools.partial(_combine_kernel, prompt_tiles=npt),
        grid=(nt,),
        in_specs=[pl.BlockSpec((None, 1, 2 * tm), lambda i: (i, 0, 0), memory_space=pltpu.SMEM),
                  pl.BlockSpec((None, 1, 2 * tm), lambda i: (jnp.minimum(i + 1, nt - 1), 0, 0),
                               memory_space=pltpu.SMEM),
                  pl.BlockSpec((tm, D_MODEL), lambda i: (i, 0)),
                  pl.BlockSpec((tm, 2), lambda i: (i, 0)),
                  pl.BlockSpec(memory_space=pl.ANY),
                  pl.BlockSpec((1, D_MODEL), lambda i: (0, 0)),
                  pl.BlockSpec((1, D_MODEL), lambda i: (0, 0))],
        out_specs=[pl.BlockSpec((tm, D_MODEL), lambda i: (jnp.minimum(i, npt - 1), 0)),
                   pl.BlockSpec((tm, D_MODEL), lambda i: (jnp.maximum(i - npt, 0), 0))],
        out_shape=[jax.ShapeDtypeStruct((mp, D_MODEL), F32),
                   jax.ShapeDtypeStruct((ms, D_MODEL), F32)],
        scratch_shapes=[pltpu.VMEM((2, 2, tm, D_MODEL), F32), pltpu.SemaphoreType.DMA((2,))],
        compiler_params=_params("arbitrary"),
        name="combine_ln",
    )(pos3, pos3, x, prob, y_sorted, g, b)


def _moe_ln(x, mp, router_w, router_b, wg, wu, wd, g, b):
    m = x.shape[0]
    tm = MOE_ROWS
    n_tiles = -(-2 * m // tm) + N_EXPERTS
    idx, prob, rank, counts = _router(x, router_w, router_b)
    pos, src, tile_expert, tile_valid = _route_tables(idx, rank, counts[0], tm, n_tiles)
    y_sorted = _moe_group(x, src, tile_expert, tile_valid, wg, wu, wd, tm, n_tiles)
    return _combine_ln(x, prob, y_sorted, pos, g, b, mp)


def kernel(x_prompt, x_sample, state_conv_a, cache_swa_k, cache_swa_v, state_conv_c, w_in, b_gate, conv_a_w, conv_a_b, ln_a_g, ln_a_b, w_branch_a, sinks, w_branch_b, conv_c_w, w_branch_c, ln_d_g, ln_d_b, sgu_w, sgu_b, w_branch_d, w_out, ln1_g, ln1_b, ffn_w_gate, ffn_w_up, ffn_w_down, router_w, router_b, exp_w_gate, exp_w_up, exp_w_down, ln2_g, ln2_b):
    nb, t_len, d = x_prompt.shape
    nseq, s_len, _ = x_sample.shape
    depth = w_in.shape[0]
    mp, ms = nb * t_len, nseq * s_len
    w = WINDOW

    x = (x_prompt.reshape(mp, d), x_sample.reshape(ms, d))
    xb = jnp.concatenate([x[0].astype(BF16), x[1].astype(BF16)], axis=0)

    cos_p, sin_p = _rope_tables(jnp.arange(t_len))
    cos_s, sin_s = _rope_tables(PAST_LEN + jnp.arange(s_len))

    def row(v):
        return v.reshape(1, -1)

    outs = {k: [] for k in ("pa", "pk", "pv", "pc", "sa", "sk", "sv", "sc", "sd")}
    for l in range(depth):
        w_in_b = w_in[l].astype(BF16)
        h = _in_proj(xb, w_in_b)

        wa, ba = conv_a_w[l], row(conv_a_b[l])
        lag, lab = row(ln_a_g[l]), row(ln_a_b[l])
        ya_p, yc_p, pa, pc = _ac_prompt(h, nb, t_len, wa, ba, lag, lab, conv_c_w[l])
        ya_s, yc_s, sa, sc = _ac_sample(h, mp, nseq, s_len, state_conv_a[l], state_conv_c[l],
                                        wa, ba, lag, lab, conv_c_w[l])

        yb_p, pk, pv = _attn_prompt(h, nb, t_len, sinks[l], cos_p, sin_p)
        yb_s, sk, sv = _attn_sample(h, mp, nseq, s_len, sinks[l],
                                    cache_swa_k[l].reshape(nseq, w, KV_WIDTH),
                                    cache_swa_v[l].reshape(nseq, w, KV_WIDTH), cos_s, sin_s)

        ldg, ldb = row(ln_d_g[l]), row(ln_d_b[l])
        yd_p = _d_prompt(h, mp, ldg, ldb, sgu_w[l], jnp.transpose(sgu_b[l]))
        sgu_wt = jnp.repeat(jnp.transpose(sgu_w[l][:, :s_len, :s_len], (2, 1, 0)), GROUP_D, axis=2)
        sgu_bs = jnp.repeat(jnp.transpose(sgu_b[l][:, :s_len]), GROUP_D, axis=1)
        yd_s, sd = _d_sample(h, mp, nseq, s_len, ldg, ldb, sgu_wt, sgu_bs)

        merged = _merge(xb, (ya_p, yb_p, yc_p, yd_p), (ya_s, yb_s, yc_s, yd_s), w_in_b, b_gate[l],
                        (w_branch_a[l].astype(BF16), w_branch_b[l].astype(BF16),
                         w_branch_c[l].astype(BF16), w_branch_d[l].astype(BF16)))
        x, xb = _out_ln(merged, x, w_out[l].astype(BF16), row(ln1_g[l]), row(ln1_b[l]))

        i = l // 2
        if l % 2 == 0:
            x, xb = _ffn_ln(x, xb, ffn_w_gate[i].astype(BF16), ffn_w_up[i].astype(BF16),
                            ffn_w_down[i].astype(BF16), row(ln2_g[l]), row(ln2_b[l]))
            x_p, x_s = x[:mp], x[mp:]
        else:
            x_p, x_s = _moe_ln(x, mp, router_w[i], row(router_b[i]), exp_w_gate[i].astype(BF16),
                               exp_w_up[i].astype(BF16), exp_w_down[i].astype(BF16),
                               row(ln2_g[l]), row(ln2_b[l]))
            if l + 1 < depth:
                x = jnp.concatenate([x_p, x_s], axis=0)
                xb = x.astype(BF16)

        outs["pa"].append(pa)
        outs["pk"].append(pk.reshape(nb, w, N_KV, HEAD_DIM))
        outs["pv"].append(pv.reshape(nb, w, N_KV, HEAD_DIM))
        outs["pc"].append(pc)
        outs["sa"].append(sa)
        outs["sk"].append(sk.reshape(nseq, w, N_KV, HEAD_DIM))
        outs["sv"].append(sv.reshape(nseq, w, N_KV, HEAD_DIM))
        outs["sc"].append(sc)
        outs["sd"].append(sd)

    y_prompt = x_p.reshape(nb, t_len, d)
    y_sample = x_s.reshape(nseq, s_len, d)
    st = {k: jnp.stack(v) for k, v in outs.items()}
    return (y_prompt, y_sample, st["pa"], st["pk"], st["pv"], st["pc"],
            st["sa"], st["sk"], st["sv"], st["sc"], st["sd"])
```

```python
import functools

import jax
import jax.numpy as jnp
import numpy as np
from jax import lax
from jax.experimental import pallas as pl
from jax.experimental.pallas import tpu as pltpu

D_MODEL = 2048
PAST_LEN = 8192
D_A = 512
CONV_A = 31
HEAD_DIM = 64
N_HEADS = 16
N_KV = 4
GQA = N_HEADS // N_KV
WINDOW = 128
ROPE_THETA = 10000.0
D_C = 512
CONV_C = 3
D_D = 512
CHUNK = 128
N_GROUPS_D = 4
GROUP_D = D_D // N_GROUPS_D
N_BRANCH = 4
Q_WIDTH = N_HEADS * HEAD_DIM
KV_WIDTH = N_KV * HEAD_DIM
D_FF = 5632
N_EXPERTS = 8
ALPHA = 4.0 ** 0.25
LN_EPS = 1e-5

OFF_A_VAL = 0
OFF_A_GATE = D_A
OFF_Q = 2 * D_A
OFF_K = OFF_Q + Q_WIDTH
OFF_V = OFF_K + KV_WIDTH
OFF_C_B = OFF_V + KV_WIDTH
OFF_C_C = OFF_C_B + D_C
OFF_C_X = OFF_C_C + D_C
OFF_D_U = OFF_C_X + D_C
OFF_D_V = OFF_D_U + D_D
OFF_GATES = OFF_D_V + D_D

LANES = 128
SUBLANES = 8
HALO = 32
VMEM_LIMIT = 48 * 1024 * 1024
OUT_LN_SUBTILES = 4
MOE_F_STEPS = 11
MOE_ROWS = 64 * (MOE_F_STEPS - 1)

BF16 = jnp.bfloat16
F32 = jnp.float32


def _tile(n, pref):
    n = int(n)
    if n <= pref:
        return n
    for t in range(pref, 7, -1):
        if n % t == 0 and t % 8 == 0:
            return t
    return n


def _params(*sem):
    return pltpu.CompilerParams(dimension_semantics=sem, vmem_limit_bytes=VMEM_LIMIT)


def _layer_norm(x, g, b):
    mu = jnp.mean(x, axis=-1, keepdims=True)
    xc = x - mu
    var = jnp.mean(xc * xc, axis=-1, keepdims=True)
    return xc * lax.rsqrt(var + LN_EPS) * g + b


def _silu(x):
    return x * jax.nn.sigmoid(x)


def _gelu(x):
    return jax.nn.gelu(x, approximate=True)


def _stack_cast_kernel(xp_ref, xs_ref, o_ref, *, prompt_tiles):
    is_prompt = pl.program_id(0) < prompt_tiles
    o_ref[...] = jnp.where(is_prompt, xp_ref[...], xs_ref[...]).astype(o_ref.dtype)


def _stack_cast(xp, xs):
    mp, ms = xp.shape[0], xs.shape[0]
    tm = _tile(np.gcd(mp, ms), 512)
    npt = mp // tm
    return pl.pallas_call(
        functools.partial(_stack_cast_kernel, prompt_tiles=npt),
        grid=((mp + ms) // tm,),
        in_specs=[pl.BlockSpec((tm, D_MODEL), lambda i: (jnp.minimum(i, npt - 1), 0)),
                  pl.BlockSpec((tm, D_MODEL), lambda i: (jnp.maximum(i - npt, 0), 0))],
        out_specs=pl.BlockSpec((tm, D_MODEL), lambda i: (i, 0)),
        out_shape=jax.ShapeDtypeStruct((mp + ms, D_MODEL), BF16),
        compiler_params=_params("parallel"),
        name="stack_cast",
    )(xp, xs)


def _mm_kernel(x_ref, w_ref, o_ref):
    o_ref[...] = jnp.dot(x_ref[...], w_ref[...], preferred_element_type=F32).astype(o_ref.dtype)


def _in_proj(xb, w_in_b):
    m, k = xb.shape
    n = OFF_GATES
    tm = _tile(m, 1024)
    tn = 1024
    return pl.pallas_call(
        _mm_kernel,
        grid=(m // tm, n // tn),
        in_specs=[pl.BlockSpec((tm, k), lambda i, j: (i, 0)),
                  pl.BlockSpec((k, tn), lambda i, j: (0, j))],
        out_specs=pl.BlockSpec((tm, tn), lambda i, j: (i, j)),
        out_shape=jax.ShapeDtypeStruct((m, n), F32),
        compiler_params=_params("parallel", "arbitrary"),
        name="in_proj",
    )(xb, w_in_b)


def _ac_prompt_kernel(av_ref, ag_ref, cb_ref, cc_ref, cx_ref,
                      avp_ref, agp_ref, ccp_ref, cxp_ref,
                      wa_ref, ba_ref, lg_ref, lb_ref, wc_ref,
                      ya_ref, yc_ref, ha_ref, hc_ref, ga_s, gc_s, sh_s):
    t = pl.program_id(1)
    tt = av_ref.shape[0]
    first = t == 0
    glu_prev = avp_ref[...] * jax.nn.sigmoid(agp_ref[...])
    ga_s[0:HALO, :] = jnp.where(first, 0.0, glu_prev)
    ga_s[HALO:HALO + tt, :] = av_ref[...] * jax.nn.sigmoid(ag_ref[...])
    acc = jnp.zeros((tt, D_A), F32) + ba_ref[...]
    base = HALO - (CONV_A - 1)
    for phase in range(SUBLANES):
        offs = [o for o in range(base, base + CONV_A) if o % SUBLANES == phase]
        span = offs[-1] - phase + tt
        sh_s[0:span, :] = ga_s[phase:phase + span, :]
        for o in offs:
            acc = acc + wa_ref[o - base:o - base + 1, :] * sh_s[o - phase:o - phase + tt, :]
    ya = _silu(_layer_norm(acc, lg_ref[...], lb_ref[...]))
    ya_ref[...] = ya.astype(ya_ref.dtype)
    ha_ref[0] = ga_s[HALO + tt - (CONV_A - 1):HALO + tt, :]

    gc_s[0:HALO, :] = jnp.where(first, 0.0, ccp_ref[...] * cxp_ref[...])
    gc_s[HALO:HALO + tt, :] = cc_ref[...] * cx_ref[...]
    base_c = HALO - (CONV_C - 1)
    yc = jnp.zeros((tt, D_C), F32)
    for j in range(CONV_C):
        yc = yc + wc_ref[j:j + 1, :] * gc_s[base_c + j:base_c + j + tt, :]
    yc_ref[...] = (cb_ref[...] * yc).astype(yc_ref.dtype)
    hc_ref[0] = gc_s[HALO + tt - (CONV_C - 1):HALO + tt, :]


def _ac_prompt(h, nb, t_len, wa, ba, lg, lb, wc):
    tt = _tile(t_len, 256)
    nt = t_len // tt
    cw = D_A

    def cur(col):
        return pl.BlockSpec((tt, cw), lambda b, t: (b * nt + t, col // cw))

    def prev(col):
        return pl.BlockSpec(
            (HALO, cw),
            lambda b, t: (jnp.maximum((b * t_len + t * tt) // HALO - 1, 0), col // cw))

    def full(shape):
        return pl.BlockSpec(shape, lambda b, t: (0,) * len(shape))

    mp = nb * t_len
    return pl.pallas_call(
        _ac_prompt_kernel,
        grid=(nb, nt),
        in_specs=[cur(OFF_A_VAL), cur(OFF_A_GATE), cur(OFF_C_B), cur(OFF_C_C), cur(OFF_C_X),
                  prev(OFF_A_VAL), prev(OFF_A_GATE), prev(OFF_C_C), prev(OFF_C_X),
                  full((CONV_A, D_A)), full((1, D_A)), full((1, D_A)), full((1, D_A)),
                  full((CONV_C, D_C))],
        out_specs=[pl.BlockSpec((tt, D_A), lambda b, t: (b * nt + t, 0)),
                   pl.BlockSpec((tt, D_C), lambda b, t: (b * nt + t, 0)),
                   pl.BlockSpec((1, CONV_A - 1, D_A), lambda b, t: (b, 0, 0)),
                   pl.BlockSpec((1, CONV_C - 1, D_C), lambda b, t: (b, 0, 0))],
        out_shape=[jax.ShapeDtypeStruct((mp, D_A), BF16),
                   jax.ShapeDtypeStruct((mp, D_C), BF16),
                   jax.ShapeDtypeStruct((nb, CONV_A - 1, D_A), F32),
                   jax.ShapeDtypeStruct((nb, CONV_C - 1, D_C), F32)],
        scratch_shapes=[pltpu.VMEM((HALO + tt, D_A), F32), pltpu.VMEM((HALO + tt, D_C), F32),
                        pltpu.VMEM((HALO + tt, D_A), F32)],
        compiler_params=_params("parallel", "arbitrary"),
        name="ac_prompt",
    )(h, h, h, h, h, h, h, h, h, wa, ba, lg, lb, wc)


def _ac_sample_kernel(av_ref, ag_ref, cb_ref, cc_ref, cx_ref, hista_ref, histc_ref,
                      wa_ref, ba_ref, lg_ref, lb_ref, wc_ref,
                      ya_ref, yc_ref, ha_ref, hc_ref, xa_s, xc_s):
    sb, s_len = hista_ref.shape[0], av_ref.shape[0] // hista_ref.shape[0]
    ka, kc = CONV_A - 1, CONV_C - 1
    glu = av_ref[...] * jax.nn.sigmoid(ag_ref[...])
    xa_s[:, 0:ka, :] = hista_ref[...]
    xa_s[:, ka:ka + s_len, :] = glu.reshape(sb, s_len, D_A)
    acc = jnp.zeros((sb, s_len, D_A), F32) + ba_ref[...][None]
    for j in range(CONV_A):
        acc = acc + wa_ref[j:j + 1, :][None] * xa_s[:, j:j + s_len, :]
    ya = _silu(_layer_norm(acc, lg_ref[...][None], lb_ref[...][None]))
    ya_ref[...] = ya.reshape(sb * s_len, D_A).astype(ya_ref.dtype)
    ha_ref[...] = xa_s[:, s_len:s_len + ka, :]

    xc_s[:, 0:kc, :] = histc_ref[...]
    xc_s[:, kc:kc + s_len, :] = (cc_ref[...] * cx_ref[...]).reshape(sb, s_len, D_C)
    yc = jnp.zeros((sb, s_len, D_C), F32)
    for j in range(CONV_C):
        yc = yc + wc_ref[j:j + 1, :][None] * xc_s[:, j:j + s_len, :]
    yc = cb_ref[...] * yc.reshape(sb * s_len, D_C)
    yc_ref[...] = yc.astype(yc_ref.dtype)
    hc_ref[...] = xc_s[:, s_len:s_len + kc, :]


def _ac_sample(h, row0, nseq, s_len, hist_a, hist_c, wa, ba, lg, lb, wc):
    sb = _tile(nseq, 16)
    rows = sb * s_len
    blk0 = row0 // rows
    cw = D_A

    def cur(col):
        return pl.BlockSpec((rows, cw), lambda i: (blk0 + i, col // cw))

    def full(shape):
        return pl.BlockSpec(shape, lambda i: (0,) * len(shape))

    ms = nseq * s_len
    ka, kc = CONV_A - 1, CONV_C - 1
    return pl.pallas_call(
        _ac_sample_kernel,
        grid=(nseq // sb,),
        in_specs=[cur(OFF_A_VAL), cur(OFF_A_GATE), cur(OFF_C_B), cur(OFF_C_C), cur(OFF_C_X),
                  pl.BlockSpec((sb, ka, D_A), lambda i: (i, 0, 0)),
                  pl.BlockSpec((sb, kc, D_C), lambda i: (i, 0, 0)),
                  full((CONV_A, D_A)), full((1, D_A)), full((1, D_A)), full((1, D_A)),
                  full((CONV_C, D_C))],
        out_specs=[pl.BlockSpec((rows, D_A), lambda i: (i, 0)),
                   pl.BlockSpec((rows, D_C), lambda i: (i, 0)),
                   pl.BlockSpec((sb, ka, D_A), lambda i: (i, 0, 0)),
                   pl.BlockSpec((sb, kc, D_C), lambda i: (i, 0, 0))],
        out_shape=[jax.ShapeDtypeStruct((ms, D_A), BF16),
                   jax.ShapeDtypeStruct((ms, D_C), BF16),
                   jax.ShapeDtypeStruct((nseq, ka, D_A), F32),
                   jax.ShapeDtypeStruct((nseq, kc, D_C), F32)],
        scratch_shapes=[pltpu.VMEM((sb, ka + s_len + 2, D_A), F32),
                        pltpu.VMEM((sb, kc + s_len + 6, D_C), F32)],
        compiler_params=_params("parallel"),
        name="ac_sample",
    )(h, h, h, h, h, hist_a, hist_c, wa, ba, lg, lb, wc)


def _rope_tables(pos):
    half = HEAD_DIM // 2
    inv_freq = jnp.power(ROPE_THETA, -jnp.arange(half, dtype=F32) * (2.0 / HEAD_DIM))
    ang = pos.astype(F32)[:, None] * inv_freq[None, :]
    cos, sin = jnp.cos(ang), jnp.sin(ang)
    cos_t = jnp.concatenate([cos, cos, cos, cos], axis=1)
    sin_t = jnp.concatenate([-sin, sin, -sin, sin], axis=1)
    return cos_t, sin_t


def _rope(x, cos_t, sin_t):
    half = HEAD_DIM // 2
    axis = x.ndim - 1
    shape = x.shape[:-1] + (LANES,)
    lane = lax.broadcasted_iota(jnp.int32, shape, axis)
    first = (lane % HEAD_DIM) < half
    out = []
    for i in range(x.shape[-1] // LANES):
        xi = x[..., LANES * i:LANES * (i + 1)]
        partner = jnp.where(first, pltpu.roll(xi, LANES - half, axis), pltpu.roll(xi, half, axis))
        out.append(xi * cos_t + partner * sin_t)
    return out


def _head_halves(t, axis, lo_valid):
    lane = lax.broadcasted_iota(jnp.int32, t.shape, axis)
    if lo_valid:
        lo = jnp.where(lane < HEAD_DIM, t, jnp.zeros_like(t))
        hi = pltpu.roll(lo, HEAD_DIM, axis)
    else:
        hi = jnp.where(lane >= HEAD_DIM, t, jnp.zeros_like(t))
        lo = pltpu.roll(hi, HEAD_DIM, axis)
    return lo, hi


def _swa_block(sinks_ref, q_tiles, kp_tiles, kc_tiles, vp, vc, has_prev, write):
    w = WINDOW
    ci = lax.broadcasted_iota(jnp.int32, (2 * w, 2 * w), 0)
    qi = lax.broadcasted_iota(jnp.int32, (2 * w, 2 * w), 1) % w
    mask = (ci > qi) & (ci <= qi + w) & (has_prev | (ci >= w))
    lane_first = lax.broadcasted_iota(jnp.int32, (1, 2 * w), 1) < w

    qt_tiles = [t.T.astype(BF16) for t in q_tiles]
    zeros_t = jnp.zeros((HEAD_DIM, 2 * w), BF16)

    for h in range(N_KV):
        tile, lo_valid = h // 2, (h % 2 == 0)
        lanes = slice(LANES * tile, LANES * (tile + 1))
        kcat = jnp.concatenate([kp_tiles[tile], kc_tiles[tile]], axis=0)
        k_lo, k_hi = _head_halves(kcat.astype(BF16), 1, lo_valid)
        vt = jnp.concatenate([vp[:, lanes], vc[:, lanes]], axis=0).T.astype(BF16)
        vt = vt[0:HEAD_DIM] if lo_valid else vt[HEAD_DIM:2 * HEAD_DIM]
        vt_halves = (jnp.concatenate([vt, zeros_t], axis=0), jnp.concatenate([zeros_t, vt], axis=0))
        qst = jnp.concatenate([qt_tiles[2 * h], qt_tiles[2 * h + 1]], axis=1)
        ot = None
        for half_idx, k_half in enumerate((k_lo, k_hi)):
            s = jnp.dot(k_half, qst, preferred_element_type=F32)
            s = jnp.where(mask, s, -jnp.inf)
            sk = jnp.where(lane_first, sinks_ref[4 * h + half_idx], sinks_ref[4 * h + 2 + half_idx])
            m = jnp.maximum(jnp.max(s, axis=0, keepdims=True), sk)
            e = jnp.exp(s - m)
            den = jnp.sum(e, axis=0, keepdims=True) + jnp.exp(sk - m)
            p = (e / den).astype(BF16)
            term = jnp.dot(vt_halves[half_idx], p, preferred_element_type=F32)
            ot = term if ot is None else ot + term
        write(2 * h, ot[:, 0:w].T)
        write(2 * h + 1, ot[:, w:2 * w].T)


def _attn_prompt_kernel(sinks_ref, q_ref, kc_ref, vc_ref, kp_ref, vp_ref,
                        cos_ref, sin_ref, cosp_ref, sinp_ref,
                        yb_ref, nk_ref, nv_ref):
    n = pl.program_id(1)
    w = WINDOW
    cos_c, sin_c = cos_ref[...], sin_ref[...]
    q_tiles = _rope(q_ref[...] * (HEAD_DIM ** -0.5), cos_c, sin_c)
    kc_tiles = _rope(kc_ref[...], cos_c, sin_c)
    kp_tiles = _rope(kp_ref[...], cosp_ref[...], sinp_ref[...])
    for i in range(KV_WIDTH // LANES):
        nk_ref[0, :, LANES * i:LANES * (i + 1)] = kc_tiles[i][w:2 * w]
    nv_ref[0] = vc_ref[w:2 * w, :]

    for j in range(2):
        rows = slice(j * w, (j + 1) * w)

        def write(tile, value, rows=rows):
            yb_ref[rows, LANES * tile:LANES * (tile + 1)] = value.astype(yb_ref.dtype)

        if j == 0:
            prev_k, prev_v, has_prev = kp_tiles, vp_ref[...], n > 0
        else:
            prev_k, prev_v, has_prev = [t[0:w] for t in kc_tiles], vc_ref[0:w, :], True
        _swa_block(sinks_ref, [t[rows] for t in q_tiles], prev_k, [t[rows] for t in kc_tiles],
                   prev_v, vc_ref[rows, :], has_prev, write)


def _attn_prompt(h, nb, t_len, sinks, cos_t, sin_t):
    w = WINDOW
    nblk = t_len // w
    assert nblk % 2 == 0
    npair = nblk // 2
    mp = nb * t_len

    def pair(b, n):
        return b * npair + n

    def halo(b, n):
        return b * nblk + jnp.maximum(2 * n - 1, 0)

    return pl.pallas_call(
        _attn_prompt_kernel,
        grid=(nb, npair),
        in_specs=[pl.BlockSpec(memory_space=pltpu.SMEM),
                  pl.BlockSpec((2 * w, Q_WIDTH), lambda b, n: (pair(b, n), OFF_Q // Q_WIDTH)),
                  pl.BlockSpec((2 * w, KV_WIDTH), lambda b, n: (pair(b, n), OFF_K // KV_WIDTH)),
                  pl.BlockSpec((2 * w, KV_WIDTH), lambda b, n: (pair(b, n), OFF_V // KV_WIDTH)),
                  pl.BlockSpec((w, KV_WIDTH), lambda b, n: (halo(b, n), OFF_K // KV_WIDTH)),
                  pl.BlockSpec((w, KV_WIDTH), lambda b, n: (halo(b, n), OFF_V // KV_WIDTH)),
                  pl.BlockSpec((2 * w, LANES), lambda b, n: (n, 0)),
                  pl.BlockSpec((2 * w, LANES), lambda b, n: (n, 0)),
                  pl.BlockSpec((w, LANES), lambda b, n: (jnp.maximum(2 * n - 1, 0), 0)),
                  pl.BlockSpec((w, LANES), lambda b, n: (jnp.maximum(2 * n - 1, 0), 0))],
        out_specs=[pl.BlockSpec((2 * w, Q_WIDTH), lambda b, n: (pair(b, n), 0)),
                   pl.BlockSpec((1, w, KV_WIDTH), lambda b, n: (b, 0, 0)),
                   pl.BlockSpec((1, w, KV_WIDTH), lambda b, n: (b, 0, 0))],
        out_shape=[jax.ShapeDtypeStruct((mp, Q_WIDTH), BF16),
                   jax.ShapeDtypeStruct((nb, w, KV_WIDTH), F32),
                   jax.ShapeDtypeStruct((nb, w, KV_WIDTH), F32)],
        compiler_params=_params("parallel", "arbitrary"),
        name="attn_prompt",
    )(sinks, h, h, h, h, h, cos_t, sin_t, cos_t, sin_t)


def _attn_sample_kernel(sinks_ref, q_ref, kn_ref, vn_ref, kbuf_ref, vbuf_ref, cos_ref, sin_ref,
                        yb_ref, nk_ref, nv_ref):
    sb, w = kbuf_ref.shape[0], kbuf_ref.shape[1]
    s_len = q_ref.shape[0] // sb
    cos_t, sin_t = cos_ref[...], sin_ref[...]
    q_tiles = _rope(q_ref[...] * (HEAD_DIM ** -0.5), cos_t, sin_t)
    kn_tiles = _rope(kn_ref[...], cos_t, sin_t)
    nk_ref[:, 0:w - s_len, :] = kbuf_ref[:, s_len:w, :]
    nv_ref[:, 0:w - s_len, :] = vbuf_ref[:, s_len:w, :]
    for i in range(KV_WIDTH // LANES):
        nk_ref[:, w - s_len:w, LANES * i:LANES * (i + 1)] = kn_tiles[i].reshape(sb, s_len, LANES)
    nv_ref[:, w - s_len:w, :] = vn_ref[...].reshape(sb, s_len, KV_WIDTH)

    qi = lax.broadcasted_iota(jnp.int32, (2 * s_len, w), 0) % s_len
    ci = lax.broadcasted_iota(jnp.int32, (2 * s_len, w), 1)
    mask_buf = (ci > qi)[None]
    qn = lax.broadcasted_iota(jnp.int32, (2 * s_len, s_len), 0) % s_len
    cn = lax.broadcasted_iota(jnp.int32, (2 * s_len, s_len), 1)
    mask_new = (cn <= qn)[None]
    row_top = (lax.broadcasted_iota(jnp.int32, (2 * s_len, 1), 0) < s_len)[None]
    bqk = (((2,), (2,)), ((0,), (0,)))
    bkd = (((2,), (1,)), ((0,), (0,)))

    for h in range(N_KV):
        tile, lo_valid = h // 2, (h % 2 == 0)
        sl = slice(LANES * tile, LANES * (tile + 1))
        kb_lo, kb_hi = _head_halves(kbuf_ref[:, :, sl].astype(BF16), 2, lo_valid)
        vb_lo, vb_hi = _head_halves(vbuf_ref[:, :, sl].astype(BF16), 2, lo_valid)
        kn3 = kn_tiles[tile].reshape(sb, s_len, LANES).astype(BF16)
        vn3 = vn_ref[:, sl].reshape(sb, s_len, LANES).astype(BF16)
        kn_lo, kn_hi = _head_halves(kn3, 2, lo_valid)
        vn_lo, vn_hi = _head_halves(vn3, 2, lo_valid)
        qs = jnp.concatenate([q_tiles[2 * h].reshape(sb, s_len, LANES),
                              q_tiles[2 * h + 1].reshape(sb, s_len, LANES)], axis=1).astype(BF16)
        o = jnp.zeros((sb, 2 * s_len, LANES), F32)
        for half_idx, (kb, kn, vb, vn) in enumerate(((kb_lo, kn_lo, vb_lo, vn_lo),
                                                     (kb_hi, kn_hi, vb_hi, vn_hi))):
            s_b = lax.dot_general(qs, kb, bqk, preferred_element_type=F32)
            s_n = lax.dot_general(qs, kn, bqk, preferred_element_type=F32)
            s_b = jnp.where(mask_buf, s_b, -jnp.inf)
            s_n = jnp.where(mask_new, s_n, -jnp.inf)
            sk = jnp.where(row_top, sinks_ref[4 * h + half_idx], sinks_ref[4 * h + 2 + half_idx])
            m = jnp.maximum(jnp.maximum(jnp.max(s_b, axis=2, keepdims=True),
                                        jnp.max(s_n, axis=2, keepdims=True)), sk)
            e_b = jnp.exp(s_b - m)
            e_n = jnp.exp(s_n - m)
            den = (jnp.sum(e_b, axis=2, keepdims=True) + jnp.sum(e_n, axis=2, keepdims=True)
                   + jnp.exp(sk - m))
            o = o + lax.dot_general((e_b / den).astype(BF16), vb, bkd, preferred_element_type=F32)
            o = o + lax.dot_general((e_n / den).astype(BF16), vn, bkd, preferred_element_type=F32)
        yb_ref[:, LANES * (2 * h):LANES * (2 * h + 1)] = (
            o[:, 0:s_len, :].reshape(sb * s_len, LANES).astype(yb_ref.dtype))
        yb_ref[:, LANES * (2 * h + 1):LANES * (2 * h + 2)] = (
            o[:, s_len:2 * s_len, :].reshape(sb * s_len, LANES).astype(yb_ref.dtype))


def _attn_sample(h, row0, nseq, s_len, sinks, k_buf, v_buf, cos_t, sin_t):
    sb = _tile(nseq, 8)
    rows = sb * s_len
    blk0 = row0 // rows
    w = k_buf.shape[1]
    ms = nseq * s_len
    cos_rows = jnp.tile(cos_t, (sb, 1))
    sin_rows = jnp.tile(sin_t, (sb, 1))
    return pl.pallas_call(
        _attn_sample_kernel,
        grid=(nseq // sb,),
        in_specs=[pl.BlockSpec(memory_space=pltpu.SMEM),
                  pl.BlockSpec((rows, Q_WIDTH), lambda i: (blk0 + i, OFF_Q // Q_WIDTH)),
                  pl.BlockSpec((rows, KV_WIDTH), lambda i: (blk0 + i, OFF_K // KV_WIDTH)),
                  pl.BlockSpec((rows, KV_WIDTH), lambda i: (blk0 + i, OFF_V // KV_WIDTH)),
                  pl.BlockSpec((sb, w, KV_WIDTH), lambda i: (i, 0, 0)),
                  pl.BlockSpec((sb, w, KV_WIDTH), lambda i: (i, 0, 0)),
                  pl.BlockSpec((rows, LANES), lambda i: (0, 0)),
                  pl.BlockSpec((rows, LANES), lambda i: (0, 0))],
        out_specs=[pl.BlockSpec((rows, Q_WIDTH), lambda i: (i, 0)),
                   pl.BlockSpec((sb, w, KV_WIDTH), lambda i: (i, 0, 0)),
                   pl.BlockSpec((sb, w, KV_WIDTH), lambda i: (i, 0, 0))],
        out_shape=[jax.ShapeDtypeStruct((ms, Q_WIDTH), BF16),
                   jax.ShapeDtypeStruct((nseq, w, KV_WIDTH), F32),
                   jax.ShapeDtypeStruct((nseq, w, KV_WIDTH), F32)],
        compiler_params=_params("parallel"),
        name="attn_sample",
    )(sinks, h, h, h, k_buf, v_buf, cos_rows, sin_rows)


def _d_prompt_kernel(du_ref, dv_ref, lg_ref, lb_ref, w_ref, bt_ref, yd_ref):
    c = CHUNK
    ri = lax.broadcasted_iota(jnp.int32, (c, c), 0)
    cj = lax.broadcasted_iota(jnp.int32, (c, c), 1)
    causal = cj <= ri
    w_causal = [jnp.where(causal, w_ref[g], 0.0).astype(BF16) for g in range(N_GROUPS_D)]
    for k in range(du_ref.shape[0] // c):
        rows = slice(k * c, (k + 1) * c)
        u = _gelu(du_ref[rows, :])
        vn = _layer_norm(_gelu(dv_ref[rows, :]), lg_ref[...], lb_ref[...]).astype(BF16)
        for g in range(N_GROUPS_D):
            sl = slice(GROUP_D * g, GROUP_D * (g + 1))
            s = jnp.dot(w_causal[g], vn[:, sl], preferred_element_type=F32) + bt_ref[:, g:g + 1]
            yd_ref[rows, sl] = (u[:, sl] * s).astype(yd_ref.dtype)


def _d_prompt(h, mp, ln_g, ln_b, sgu_w, sgu_bt):
    c = CHUNK
    rows = _tile(mp, 4 * c)
    assert rows % c == 0
    return pl.pallas_call(
        _d_prompt_kernel,
        grid=(mp // rows,),
        in_specs=[pl.BlockSpec((rows, D_D), lambda i: (i, OFF_D_U // D_D)),
                  pl.BlockSpec((rows, D_D), lambda i: (i, OFF_D_V // D_D)),
                  pl.BlockSpec((1, D_D), lambda i: (0, 0)),
                  pl.BlockSpec((1, D_D), lambda i: (0, 0)),
                  pl.BlockSpec((N_GROUPS_D, c, c), lambda i: (0, 0, 0)),
                  pl.BlockSpec((c, N_GROUPS_D), lambda i: (0, 0))],
        out_specs=pl.BlockSpec((rows, D_D), lambda i: (i, 0)),
        out_shape=jax.ShapeDtypeStruct((mp, D_D), BF16),
        compiler_params=_params("parallel"),
        name="d_prompt",
    )(h, h, ln_g, ln_b, sgu_w, sgu_bt)


def _d_sample_kernel(du_ref, dv_ref, lg_ref, lb_ref, wt_ref, bt_ref, yd_ref, vd_ref):
    sb, s_len = vd_ref.shape[0], vd_ref.shape[1]
    u = _gelu(du_ref[...])
    vn = _layer_norm(_gelu(dv_ref[...]), lg_ref[...], lb_ref[...])
    vn3 = vn.reshape(sb, s_len, D_D)
    vd_ref[...] = vn3
    ii = lax.broadcasted_iota(jnp.int32, (s_len, D_D), 0)
    s = jnp.zeros((sb, s_len, D_D), F32) + bt_ref[...][None]
    for j in range(s_len):
        wj = jnp.where(ii >= j, wt_ref[j], 0.0)
        s = s + wj[None] * vn3[:, j:j + 1, :]
    yd_ref[...] = (u * s.reshape(sb * s_len, D_D)).astype(yd_ref.dtype)


def _d_sample(h, row0, nseq, s_len, ln_g, ln_b, sgu_wt, sgu_bs):
    sb = _tile(nseq, 16)
    rows = sb * s_len
    blk0 = row0 // rows
    ms = nseq * s_len
    return pl.pallas_call(
        _d_sample_kernel,
        grid=(nseq // sb,),
        in_specs=[pl.BlockSpec((rows, D_D), lambda i: (blk0 + i, OFF_D_U // D_D)),
                  pl.BlockSpec((rows, D_D), lambda i: (blk0 + i, OFF_D_V // D_D)),
                  pl.BlockSpec((1, D_D), lambda i: (0, 0)),
                  pl.BlockSpec((1, D_D), lambda i: (0, 0)),
                  pl.BlockSpec((s_len, s_len, D_D), lambda i: (0, 0, 0)),
                  pl.BlockSpec((s_len, D_D), lambda i: (0, 0))],
        out_specs=[pl.BlockSpec((rows, D_D), lambda i: (i, 0)),
                   pl.BlockSpec((sb, s_len, D_D), lambda i: (i, 0, 0))],
        out_shape=[jax.ShapeDtypeStruct((ms, D_D), BF16),
                   jax.ShapeDtypeStruct((nseq, s_len, D_D), F32)],
        compiler_params=_params("parallel"),
        name="d_sample",
    )(h, h, ln_g, ln_b, sgu_wt, sgu_bs)


def _merge_kernel(x_ref, yap_ref, ybp_ref, ycp_ref, ydp_ref, yas_ref, ybs_ref, ycs_ref, yds_ref,
                  g0_ref, g1_ref, g2_ref, g3_ref, bg_ref, pa_ref, pb_ref, pc_ref, pd_ref, o_ref,
                  *, prompt_tiles):
    x = x_ref[...]
    is_prompt = pl.program_id(1) < prompt_tiles
    acc = None
    branches = ((yap_ref, yas_ref, g0_ref, pa_ref), (ybp_ref, ybs_ref, g1_ref, pb_ref),
                (ycp_ref, ycs_ref, g2_ref, pc_ref), (ydp_ref, yds_ref, g3_ref, pd_ref))
    for i, (yp_ref, ys_ref, g_ref, p_ref) in enumerate(branches):
        y = jnp.where(is_prompt, yp_ref[...], ys_ref[...])
        gate = jax.nn.sigmoid(jnp.dot(x, g_ref[...], preferred_element_type=F32) + bg_ref[i:i + 1, :])
        term = gate * jnp.dot(y, p_ref[...], preferred_element_type=F32)
        acc = term if acc is None else acc + term
    o_ref[...] = acc.astype(o_ref.dtype)


def _merge(xb, y_prompt, y_sample, w_in_b, b_gate, projs):
    m = xb.shape[0]
    mp, ms = y_prompt[0].shape[0], y_sample[0].shape[0]
    tm = _tile(np.gcd(mp, ms), 512)
    npt = mp // tm
    tn = 512
    nn = D_MODEL // tn
    widths = (D_A, Q_WIDTH, D_C, D_D)

    def act_p(width):
        return pl.BlockSpec((tm, width), lambda j, i: (jnp.minimum(i, npt - 1), 0))

    def act_s(width):
        return pl.BlockSpec((tm, width), lambda j, i: (jnp.maximum(i - npt, 0), 0))

    def gate_w(b):
        return pl.BlockSpec((D_MODEL, tn), lambda j, i: (0, (OFF_GATES + b * D_MODEL) // tn + j))

    def proj_w(k):
        return pl.BlockSpec((k, tn), lambda j, i: (0, j))

    return pl.pallas_call(
        functools.partial(_merge_kernel, prompt_tiles=npt),
        grid=(nn, m // tm),
        in_specs=[pl.BlockSpec((tm, D_MODEL), lambda j, i: (i, 0))]
                 + [act_p(wd) for wd in widths] + [act_s(wd) for wd in widths]
                 + [gate_w(0), gate_w(1), gate_w(2), gate_w(3),
                    pl.BlockSpec((N_BRANCH, tn), lambda j, i: (0, j))]
                 + [proj_w(wd) for wd in widths],
        out_specs=pl.BlockSpec((tm, tn), lambda j, i: (i, j)),
        out_shape=jax.ShapeDtypeStruct((m, D_MODEL), BF16),
        compiler_params=_params("parallel", "arbitrary"),
        name="merge",
    )(xb, *y_prompt, *y_sample, w_in_b, w_in_b, w_in_b, w_in_b, b_gate, *projs)


def _out_ln_kernel(mg_ref, *refs, prompt_tiles):
    if prompt_tiles is None:
        x_ref, w_ref, g_ref, b_ref, o_ref, ob_ref = refs
    else:
        xp_ref, xs_ref, w_ref, g_ref, b_ref, o_ref, ob_ref = refs
        is_prompt = pl.program_id(0) < prompt_tiles
    tm = mg_ref.shape[0]
    sub = tm // OUT_LN_SUBTILES
    for k in range(OUT_LN_SUBTILES):
        rows = slice(k * sub, (k + 1) * sub)
        if prompt_tiles is None:
            x = x_ref[rows, :]
        else:
            x = jnp.where(is_prompt, xp_ref[rows, :], xs_ref[rows, :])
        y = ALPHA * x + jnp.dot(mg_ref[rows, :], w_ref[...], preferred_element_type=F32)
        y = _layer_norm(y, g_ref[...], b_ref[...])
        o_ref[rows, :] = y
        ob_ref[rows, :] = y.astype(ob_ref.dtype)


def _out_ln(merged, x, w_out_b, g, b):
    m = merged.shape[0]
    if isinstance(x, tuple):
        mp, ms = x[0].shape[0], x[1].shape[0]
        tm = _tile(np.gcd(mp, ms), 512)
        npt = mp // tm
        x_specs = [pl.BlockSpec((tm, D_MODEL), lambda i: (jnp.minimum(i, npt - 1), 0)),
                   pl.BlockSpec((tm, D_MODEL), lambda i: (jnp.maximum(i - npt, 0), 0))]
    else:
        tm = _tile(m, 512)
        npt = None
        x, x_specs = (x,), [pl.BlockSpec((tm, D_MODEL), lambda i: (i, 0))]
    return pl.pallas_call(
        functools.partial(_out_ln_kernel, prompt_tiles=npt),
        grid=(m // tm,),
        in_specs=[pl.BlockSpec((tm, D_MODEL), lambda i: (i, 0))] + x_specs
                 + [pl.BlockSpec((D_MODEL, D_MODEL), lambda i: (0, 0), pipeline_mode=pl.Buffered(1)),
                    pl.BlockSpec((1, D_MODEL), lambda i: (0, 0)),
                    pl.BlockSpec((1, D_MODEL), lambda i: (0, 0))],
        out_specs=[pl.BlockSpec((tm, D_MODEL), lambda i: (i, 0)),
                   pl.BlockSpec((tm, D_MODEL), lambda i: (i, 0))],
        out_shape=[jax.ShapeDtypeStruct((m, D_MODEL), F32),
                   jax.ShapeDtypeStruct((m, D_MODEL), BF16)],
        compiler_params=_params("parallel"),
        name="out_ln",
    )(merged, *x, w_out_b, g, b)


def _ffn_kernel(x_ref, xb_ref, wg_ref, wu_ref, wd_ref, g_ref, b_ref, o_ref, ob_ref, acc_s):
    f = pl.program_id(1)

    @pl.when(f == 0)
    def _():
        acc_s[...] = jnp.zeros_like(acc_s)

    xb = xb_ref[...]
    hid = (_silu(jnp.dot(xb, wg_ref[...], preferred_element_type=F32))
           * jnp.dot(xb, wu_ref[...], preferred_element_type=F32))
    acc_s[...] += jnp.dot(hid.astype(BF16), wd_ref[...], preferred_element_type=F32)

    @pl.when(f == pl.num_programs(1) - 1)
    def _():
        y = _layer_norm(ALPHA * x_ref[...] + acc_s[...], g_ref[...], b_ref[...])
        o_ref[...] = y
        ob_ref[...] = y.astype(ob_ref.dtype)


def _ffn_ln(x, xb, wg, wu, wd, g, b):
    m = x.shape[0]
    tm = _tile(m, 512)
    tf = 512
    return pl.pallas_call(
        _ffn_kernel,
        grid=(m // tm, D_FF // tf),
        in_specs=[pl.BlockSpec((tm, D_MODEL), lambda i, f: (i, 0)),
                  pl.BlockSpec((tm, D_MODEL), lambda i, f: (i, 0)),
                  pl.BlockSpec((D_MODEL, tf), lambda i, f: (0, f)),
                  pl.BlockSpec((D_MODEL, tf), lambda i, f: (0, f)),
                  pl.BlockSpec((tf, D_MODEL), lambda i, f: (f, 0)),
                  pl.BlockSpec((1, D_MODEL), lambda i, f: (0, 0)),
                  pl.BlockSpec((1, D_MODEL), lambda i, f: (0, 0))],
        out_specs=[pl.BlockSpec((tm, D_MODEL), lambda i, f: (i, 0)),
                   pl.BlockSpec((tm, D_MODEL), lambda i, f: (i, 0))],
        out_shape=[jax.ShapeDtypeStruct((m, D_MODEL), F32),
                   jax.ShapeDtypeStruct((m, D_MODEL), BF16)],
        scratch_shapes=[pltpu.VMEM((tm, D_MODEL), F32)],
        compiler_params=_params("parallel", "arbitrary"),
        name="ffn_ln",
    )(x, xb, wg, wu, wd, g, b)


def _router_kernel(x_ref, w_ref, b_ref, idx_ref, prob_ref, rank_ref, cnt_ref, run_s):
    @pl.when(pl.program_id(0) == 0)
    def _():
        run_s[...] = jnp.zeros_like(run_s)

    x, w = x_ref[...], w_ref[...]
    x_hi, w_hi = x.astype(BF16), w.astype(BF16)
    x_lo = (x - x_hi.astype(F32)).astype(BF16)
    w_lo = (w - w_hi.astype(F32)).astype(BF16)
    logits = (jnp.dot(x_hi, w_hi, preferred_element_type=F32)
              + (jnp.dot(x_lo, w_hi, preferred_element_type=F32)
                 + jnp.dot(x_hi, w_lo, preferred_element_type=F32))) + b_ref[...]
    tm = logits.shape[0]
    idx = lax.broadcasted_iota(jnp.int32, logits.shape, 1)
    v1 = jnp.max(logits, axis=1, keepdims=True)
    i1 = jnp.min(jnp.where(logits == v1, idx, N_EXPERTS), axis=1, keepdims=True)
    rest = jnp.where(idx == i1, -jnp.inf, logits)
    v2 = jnp.max(rest, axis=1, keepdims=True)
    i2 = jnp.min(jnp.where(rest == v2, idx, N_EXPERTS), axis=1, keepdims=True)
    e2 = jnp.exp(v2 - v1)
    den = 1.0 + e2

    hit = ((idx == i1) | (idx == i2)).astype(F32)
    ri = lax.broadcasted_iota(jnp.int32, (tm, tm), 0)
    ci = lax.broadcasted_iota(jnp.int32, (tm, tm), 1)
    before = jnp.dot((ci < ri).astype(BF16), hit.astype(BF16), preferred_element_type=F32) + run_s[...]
    rank1 = jnp.sum(jnp.where(idx == i1, before, 0.0), axis=1, keepdims=True)
    rank2 = jnp.sum(jnp.where(idx == i2, before, 0.0), axis=1, keepdims=True)

    col = lax.broadcasted_iota(jnp.int32, (tm, 2), 1)
    idx_ref[...] = jnp.where(col == 0, i1, i2)
    prob_ref[...] = jnp.where(col == 0, 1.0 / den, e2 / den)
    rank_ref[...] = jnp.where(col == 0, rank1, rank2).astype(jnp.int32)
    total = run_s[...] + jnp.sum(hit, axis=0, keepdims=True)
    run_s[...] = total
    cnt_ref[...] = total.astype(jnp.int32)


def _router(x, router_w, router_b):
    m = x.shape[0]
    tm = _tile(m, 512)
    return pl.pallas_call(
        _router_kernel,
        grid=(m // tm,),
        in_specs=[pl.BlockSpec((tm, D_MODEL), lambda i: (i, 0)),
                  pl.BlockSpec((D_MODEL, N_EXPERTS), lambda i: (0, 0)),
                  pl.BlockSpec((1, N_EXPERTS), lambda i: (0, 0))],
        out_specs=[pl.BlockSpec((tm, 2), lambda i: (i, 0)),
                   pl.BlockSpec((tm, 2), lambda i: (i, 0)),
                   pl.BlockSpec((tm, 2), lambda i: (i, 0)),
                   pl.BlockSpec((1, N_EXPERTS), lambda i: (0, 0))],
        out_shape=[jax.ShapeDtypeStruct((m, 2), jnp.int32),
                   jax.ShapeDtypeStruct((m, 2), F32),
                   jax.ShapeDtypeStruct((m, 2), jnp.int32),
                   jax.ShapeDtypeStruct((1, N_EXPERTS), jnp.int32)],
        scratch_shapes=[pltpu.VMEM((1, N_EXPERTS), F32)],
        compiler_params=_params("arbitrary"),
        name="router",
    )(x, router_w, router_b)


def _route_tables(idx, rank, counts, tm, n_tiles):
    m = idx.shape[0]
    padded = ((counts + tm - 1) // tm) * tm
    ends = jnp.cumsum(padded)
    base = ends - padded
    pos = base[idx] + rank
    n_valid = ends[-1] // tm
    tiles = jnp.arange(n_tiles, dtype=jnp.int32)
    tile_valid = (tiles < n_valid).astype(jnp.int32)
    first_row = jnp.minimum(tiles, n_valid - 1) * tm
    owner = jnp.sum((ends[None, :] <= first_row[:, None]).astype(jnp.int32), axis=1)
    tile_expert = jnp.minimum(owner, N_EXPERTS - 1).astype(jnp.int32)
    flat = pos.reshape(-1)
    token = jnp.repeat(jnp.arange(m, dtype=jnp.int32), 2)
    src = jnp.zeros((n_tiles * tm,), jnp.int32).at[flat].set(token, unique_indices=True)
    return pos, src, tile_expert, tile_valid


def _row_copy(src_hbm, row, dst_vmem, dst_row, sem):
    return pltpu.make_async_copy(src_hbm.at[pl.ds(row, 1), :], dst_vmem.at[pl.ds(dst_row, 1), :], sem)


def _moe_group_kernel(te_ref, tv_ref, src_ref, src_next_ref, x_hbm, wg_ref, wu_ref, wd_ref,
                      o_ref, xg_s, xb_s, sem):
    i, f = pl.program_id(0), pl.program_id(1)
    nf = pl.num_programs(1)
    tm = xb_s.shape[0]
    chunk = tm // (MOE_F_STEPS - 1)
    slot = i % 2
    valid = tv_ref[i] == 1

    @pl.when((i == 0) & (f == 0))
    def _():
        def body(r, c):
            _row_copy(x_hbm, src_ref[0, r], xg_s.at[0], r, sem.at[0]).start()
            return c
        lax.fori_loop(0, tm, body, 0, unroll=8)

    @pl.when(f == 0)
    def _():
        o_ref[...] = jnp.zeros_like(o_ref)

    @pl.when((f == 0) & ((i == 0) | (tv_ref[jnp.maximum(i - 1, 0)] == 1)))
    def _():
        pltpu.make_async_copy(xg_s.at[slot], xg_s.at[slot], sem.at[slot]).wait()
        xb_s[...] = xg_s[slot].astype(BF16)

    def swiglu_step(issue_next):
        if issue_next:
            for r in range(chunk):
                row = f * chunk + r
                _row_copy(x_hbm, src_next_ref[0, row], xg_s.at[1 - slot], row, sem.at[1 - slot]).start()
        xb = xb_s[...]
        hid = (_silu(jnp.dot(xb, wg_ref[...], preferred_element_type=F32))
               * jnp.dot(xb, wu_ref[...], preferred_element_type=F32))
        o_ref[...] += jnp.dot(hid.astype(BF16), wd_ref[...], preferred_element_type=F32)

    @pl.when(valid & (f < nf - 1))
    def _():
        swiglu_step(True)

    @pl.when(valid & (f == nf - 1))
    def _():
        swiglu_step(False)


def _moe_group(x, src, tile_expert, tile_valid, wg, wu, wd, tm, n_tiles):
    tf = D_FF // MOE_F_STEPS
    nf = MOE_F_STEPS

    def w_col(i, f, te, tv):
        return (te[i], 0, jnp.where(tv[i] == 1, f, nf - 1))

    def w_row(i, f, te, tv):
        return (te[i], jnp.where(tv[i] == 1, f, nf - 1), 0)

    grid_spec = pltpu.PrefetchScalarGridSpec(
        num_scalar_prefetch=2,
        grid=(n_tiles, nf),
        in_specs=[pl.BlockSpec((None, 1, tm), lambda i, f, te, tv: (i, 0, 0), memory_space=pltpu.SMEM),
                  pl.BlockSpec((None, 1, tm), lambda i, f, te, tv: (jnp.minimum(i + 1, n_tiles - 1), 0, 0),
                               memory_space=pltpu.SMEM),
                  pl.BlockSpec(memory_space=pl.ANY),
                  pl.BlockSpec((None, D_MODEL, tf), w_col),
                  pl.BlockSpec((None, D_MODEL, tf), w_col),
                  pl.BlockSpec((None, tf, D_MODEL), w_row)],
        out_specs=pl.BlockSpec((tm, D_MODEL), lambda i, f, te, tv: (i, 0)),
        scratch_shapes=[pltpu.VMEM((2, tm, D_MODEL), F32),
                        pltpu.VMEM((tm, D_MODEL), BF16),
                        pltpu.SemaphoreType.DMA((2,))],
    )
    src3 = src.reshape(n_tiles, 1, tm)
    return pl.pallas_call(
        _moe_group_kernel,
        grid_spec=grid_spec,
        out_shape=jax.ShapeDtypeStruct((n_tiles * tm, D_MODEL), F32),
        compiler_params=_params("arbitrary", "arbitrary"),
        name="moe_group",
    )(tile_expert, tile_valid, src3, src3, x, wg, wu, wd)

def _combine_kernel(pos_ref, pos_next_ref, x_ref, p_ref, y_hbm, g_ref, b_ref, op_ref, os_ref, ybuf, sem,
                    *, prompt_tiles):
    i, nt = pl.program_id(0), pl.num_programs(0)
    tm = x_ref.shape[0]
    slot = i % 2

    def issue(idx_ref, s):
        def body(r, c):
            _row_copy(y_hbm, idx_ref[0, 2 * r], ybuf.at[s, 0], r, sem.at[s]).start()
            _row_copy(y_hbm, idx_ref[0, 2 * r + 1], ybuf.at[s, 1], r, sem.at[s]).start()
            return c
        lax.fori_loop(0, tm, body, 0, unroll=4)

    @pl.when(i == 0)
    def _():
        issue(pos_ref, 0)

    @pl.when(i + 1 < nt)
    def _():
        issue(pos_next_ref, 1 - slot)

    pltpu.make_async_copy(ybuf.at[slot], ybuf.at[slot], sem.at[slot]).wait()
    moe = p_ref[:, 0:1] * ybuf[slot, 0] + p_ref[:, 1:2] * ybuf[slot, 1]
    y = _layer_norm(ALPHA * x_ref[...] + moe, g_ref[...], b_ref[...])

    @pl.when(i < prompt_tiles)
    def _():
        op_ref[...] = y

    @pl.when(i >= prompt_tiles)
    def _():
        os_ref[...] = y


def _combine_ln(x, prob, y_sorted, pos, g, b, mp):
    m = x.shape[0]
    ms = m - mp
    tm = _tile(np.gcd(mp, ms), 256)
    nt, npt = m // tm, mp // tm
    pos3 = pos.reshape(nt, 1, 2 * tm)
    return pl.pallas_call(
        functools.partial(_combine_kernel, prompt_tiles=npt),
        grid=(nt,),
        in_specs=[pl.BlockSpec((None, 1, 2 * tm), lambda i: (i, 0, 0), memory_space=pltpu.SMEM),
                  pl.BlockSpec((None, 1, 2 * tm), lambda i: (jnp.minimum(i + 1, nt - 1), 0, 0),
                               memory_space=pltpu.SMEM),
                  pl.BlockSpec((tm, D_MODEL), lambda i: (i, 0)),
                  pl.BlockSpec((tm, 2), lambda i: (i, 0)),
                  pl.BlockSpec(memory_space=pl.ANY),
                  pl.BlockSpec((1, D_MODEL), lambda i: (0, 0)),
                  pl.BlockSpec((1, D_MODEL), lambda i: (0, 0))],
        out_specs=[pl.BlockSpec((tm, D_MODEL), lambda i: (jnp.minimum(i, npt - 1), 0)),
                   pl.BlockSpec((tm, D_MODEL), lambda i: (jnp.maximum(i - npt, 0), 0))],
        out_shape=[jax.ShapeDtypeStruct((mp, D_MODEL), F32),
                   jax.ShapeDtypeStruct((ms, D_MODEL), F32)],
        scratch_shapes=[pltpu.VMEM((2, 2, tm, D_MODEL), F32), pltpu.SemaphoreType.DMA((2,))],
        compiler_params=_params("arbitrary"),
        name="combine_ln",
    )(pos3, pos3, x, prob, y_sorted, g, b)


def _moe_ln(x, mp, router_w, router_b, wg, wu, wd, g, b):
    m = x.shape[0]
    tm = MOE_ROWS
    n_tiles = -(-2 * m // tm) + N_EXPERTS
    idx, prob, rank, counts = _router(x, router_w, router_b)
    pos, src, tile_expert, tile_valid = _route_tables(idx, rank, counts[0], tm, n_tiles)
    y_sorted = _moe_group(x, src, tile_expert, tile_valid, wg, wu, wd, tm, n_tiles)
    return _combine_ln(x, prob, y_sorted, pos, g, b, mp)


def kernel(x_prompt, x_sample, state_conv_a, cache_swa_k, cache_swa_v, state_conv_c, w_in, b_gate, conv_a_w, conv_a_b, ln_a_g, ln_a_b, w_branch_a, sinks, w_branch_b, conv_c_w, w_branch_c, ln_d_g, ln_d_b, sgu_w, sgu_b, w_branch_d, w_out, ln1_g, ln1_b, ffn_w_gate, ffn_w_up, ffn_w_down, router_w, router_b, exp_w_gate, exp_w_up, exp_w_down, ln2_g, ln2_b):
    nb, t_len, d = x_prompt.shape
    nseq, s_len, _ = x_sample.shape
    depth = w_in.shape[0]
    mp, ms = nb * t_len, nseq * s_len
    w = WINDOW

    x = (x_prompt.reshape(mp, d), x_sample.reshape(ms, d))
    xb = _stack_cast(*x)

    cos_p, sin_p = _rope_tables(jnp.arange(t_len))
    cos_s, sin_s = _rope_tables(PAST_LEN + jnp.arange(s_len))

    def row(v):
        return v.reshape(1, -1)

    outs = {k: [] for k in ("pa", "pk", "pv", "pc", "sa", "sk", "sv", "sc", "sd")}
    for l in range(depth):
        w_in_b = w_in[l].astype(BF16)
        h = _in_proj(xb, w_in_b)

        wa, ba = conv_a_w[l], row(conv_a_b[l])
        lag, lab = row(ln_a_g[l]), row(ln_a_b[l])
        ya_p, yc_p, pa, pc = _ac_prompt(h, nb, t_len, wa, ba, lag, lab, conv_c_w[l])
        ya_s, yc_s, sa, sc = _ac_sample(h, mp, nseq, s_len, state_conv_a[l], state_conv_c[l],
                                        wa, ba, lag, lab, conv_c_w[l])

        yb_p, pk, pv = _attn_prompt(h, nb, t_len, sinks[l], cos_p, sin_p)
        yb_s, sk, sv = _attn_sample(h, mp, nseq, s_len, sinks[l],
                                    cache_swa_k[l].reshape(nseq, w, KV_WIDTH),
                                    cache_swa_v[l].reshape(nseq, w, KV_WIDTH), cos_s, sin_s)

        ldg, ldb = row(ln_d_g[l]), row(ln_d_b[l])
        yd_p = _d_prompt(h, mp, ldg, ldb, sgu_w[l], jnp.transpose(sgu_b[l]))
        sgu_wt = jnp.repeat(jnp.transpose(sgu_w[l][:, :s_len, :s_len], (2, 1, 0)), GROUP_D, axis=2)
        sgu_bs = jnp.repeat(jnp.transpose(sgu_b[l][:, :s_len]), GROUP_D, axis=1)
        yd_s, sd = _d_sample(h, mp, nseq, s_len, ldg, ldb, sgu_wt, sgu_bs)

        merged = _merge(xb, (ya_p, yb_p, yc_p, yd_p), (ya_s, yb_s, yc_s, yd_s), w_in_b, b_gate[l],
                        (w_branch_a[l].astype(BF16), w_branch_b[l].astype(BF16),
                         w_branch_c[l].astype(BF16), w_branch_d[l].astype(BF16)))
        x, xb = _out_ln(merged, x, w_out[l].astype(BF16), row(ln1_g[l]), row(ln1_b[l]))

        i = l // 2
        if l % 2 == 0:
            x, xb = _ffn_ln(x, xb, ffn_w_gate[i].astype(BF16), ffn_w_up[i].astype(BF16),
                            ffn_w_down[i].astype(BF16), row(ln2_g[l]), row(ln2_b[l]))
            x_p, x_s = x[:mp], x[mp:]
        else:
            x_p, x_s = _moe_ln(x, mp, router_w[i], row(router_b[i]), exp_w_gate[i].astype(BF16),
                               exp_w_up[i].astype(BF16), exp_w_down[i].astype(BF16),
                               row(ln2_g[l]), row(ln2_b[l]))
            if l + 1 < depth:
                x = jnp.concatenate([x_p, x_s], axis=0)
                xb = x.astype(BF16)

        outs["pa"].append(pa)
        outs["pk"].append(pk.reshape(nb, w, N_KV, HEAD_DIM))
        outs["pv"].append(pv.reshape(nb, w, N_KV, HEAD_DIM))
        outs["pc"].append(pc)
        outs["sa"].append(sa)
        outs["sk"].append(sk.reshape(nseq, w, N_KV, HEAD_DIM))
        outs["sv"].append(sv.reshape(nseq, w, N_KV, HEAD_DIM))
        outs["sc"].append(sc)
        outs["sd"].append(sd)

    y_prompt = x_p.reshape(nb, t_len, d)
    y_sample = x_s.reshape(nseq, s_len, d)
    st = {k: jnp.stack(v) for k, v in outs.items()}
    return (y_prompt, y_sample, st["pa"], st["pk"], st["pv"], st["pc"],
            st["sa"], st["sk"], st["sv"], st["sc"], st["sd"])
```

```python
import functools

import jax
import jax.numpy as jnp
import numpy as np
from jax import lax
from jax.experimental import pallas as pl
from jax.experimental.pallas import tpu as pltpu

D_MODEL = 2048
PAST_LEN = 8192
D_A = 512
CONV_A = 31
HEAD_DIM = 64
N_HEADS = 16
N_KV = 4
GQA = N_HEADS // N_KV
WINDOW = 128
ROPE_THETA = 10000.0
D_C = 512
CONV_C = 3
D_D = 512
CHUNK = 128
N_GROUPS_D = 4
GROUP_D = D_D // N_GROUPS_D
N_BRANCH = 4
Q_WIDTH = N_HEADS * HEAD_DIM
KV_WIDTH = N_KV * HEAD_DIM
D_FF = 5632
N_EXPERTS = 8
ALPHA = 4.0 ** 0.25
LN_EPS = 1e-5

OFF_A_VAL = 0
OFF_A_GATE = D_A
OFF_Q = 2 * D_A
OFF_K = OFF_Q + Q_WIDTH
OFF_V = OFF_K + KV_WIDTH
OFF_C_B = OFF_V + KV_WIDTH
OFF_C_C = OFF_C_B + D_C
OFF_C_X = OFF_C_C + D_C
OFF_D_U = OFF_C_X + D_C
OFF_D_V = OFF_D_U + D_D
OFF_GATES = OFF_D_V + D_D

LANES = 128
SUBLANES = 8
HALO = 32
VMEM_LIMIT = 48 * 1024 * 1024
OUT_LN_SUBTILES = 4
MOE_F_STEPS = 11
MOE_ROWS = 64 * (MOE_F_STEPS - 1)

BF16 = jnp.bfloat16
F32 = jnp.float32


def _tile(n, pref):
    n = int(n)
    if n <= pref:
        return n
    for t in range(pref, 7, -1):
        if n % t == 0 and t % 8 == 0:
            return t
    return n


def _params(*sem):
    return pltpu.CompilerParams(dimension_semantics=sem, vmem_limit_bytes=VMEM_LIMIT)


def _layer_norm(x, g, b):
    mu = jnp.mean(x, axis=-1, keepdims=True)
    xc = x - mu
    var = jnp.mean(xc * xc, axis=-1, keepdims=True)
    return xc * lax.rsqrt(var + LN_EPS) * g + b


def _silu(x):
    return x * jax.nn.sigmoid(x)


def _gelu(x):
    return jax.nn.gelu(x, approximate=True)


def _stack_cast_kernel(xp_ref, xs_ref, o_ref, *, prompt_tiles):
    is_prompt = pl.program_id(0) < prompt_tiles
    o_ref[...] = jnp.where(is_prompt, xp_ref[...], xs_ref[...]).astype(o_ref.dtype)


def _stack_cast(xp, xs):
    mp, ms = xp.shape[0], xs.shape[0]
    tm = _tile(np.gcd(mp, ms), 512)
    npt = mp // tm
    return pl.pallas_call(
        functools.partial(_stack_cast_kernel, prompt_tiles=npt),
        grid=((mp + ms) // tm,),
        in_specs=[pl.BlockSpec((tm, D_MODEL), lambda i: (jnp.minimum(i, npt - 1), 0)),
                  pl.BlockSpec((tm, D_MODEL), lambda i: (jnp.maximum(i - npt, 0), 0))],
        out_specs=pl.BlockSpec((tm, D_MODEL), lambda i: (i, 0)),
        out_shape=jax.ShapeDtypeStruct((mp + ms, D_MODEL), BF16),
        compiler_params=_params("parallel"),
        name="stack_cast",
    )(xp, xs)


def _mm_kernel(x_ref, w_ref, o_ref):
    o_ref[...] = jnp.dot(x_ref[...], w_ref[...], preferred_element_type=F32).astype(o_ref.dtype)


def _in_proj(xb, w_in_b):
    m, k = xb.shape
    n = OFF_GATES
    tm = _tile(m, 1024)
    tn = 1024
    return pl.pallas_call(
        _mm_kernel,
        grid=(m // tm, n // tn),
        in_specs=[pl.BlockSpec((tm, k), lambda i, j: (i, 0)),
                  pl.BlockSpec((k, tn), lambda i, j: (0, j))],
        out_specs=pl.BlockSpec((tm, tn), lambda i, j: (i, j)),
        out_shape=jax.ShapeDtypeStruct((m, n), F32),
        compiler_params=_params("parallel", "arbitrary"),
        name="in_proj",
    )(xb, w_in_b)


def _ac_prompt_kernel(av_ref, ag_ref, cb_ref, cc_ref, cx_ref,
                      avp_ref, agp_ref, ccp_ref, cxp_ref,
                      wa_ref, ba_ref, lg_ref, lb_ref, wc_ref,
                      ya_ref, yc_ref, ha_ref, hc_ref, ga_s, gc_s, sh_s):
    t = pl.program_id(1)
    tt = av_ref.shape[0]
    first = t == 0
    glu_prev = avp_ref[...] * jax.nn.sigmoid(agp_ref[...])
    ga_s[0:HALO, :] = jnp.where(first, 0.0, glu_prev)
    ga_s[HALO:HALO + tt, :] = av_ref[...] * jax.nn.sigmoid(ag_ref[...])
    acc = jnp.zeros((tt, D_A), F32) + ba_ref[...]
    base = HALO - (CONV_A - 1)
    for phase in range(SUBLANES):
        offs = [o for o in range(base, base + CONV_A) if o % SUBLANES == phase]
        span = offs[-1] - phase + tt
        sh_s[0:span, :] = ga_s[phase:phase + span, :]
        for o in offs:
            acc = acc + wa_ref[o - base:o - base + 1, :] * sh_s[o - phase:o - phase + tt, :]
    ya = _silu(_layer_norm(acc, lg_ref[...], lb_ref[...]))
    ya_ref[...] = ya.astype(ya_ref.dtype)
    ha_ref[0] = ga_s[HALO + tt - (CONV_A - 1):HALO + tt, :]

    gc_s[0:HALO, :] = jnp.where(first, 0.0, ccp_ref[...] * cxp_ref[...])
    gc_s[HALO:HALO + tt, :] = cc_ref[...] * cx_ref[...]
    base_c = HALO - (CONV_C - 1)
    yc = jnp.zeros((tt, D_C), F32)
    for j in range(CONV_C):
        yc = yc + wc_ref[j:j + 1, :] * gc_s[base_c + j:base_c + j + tt, :]
    yc_ref[...] = (cb_ref[...] * yc).astype(yc_ref.dtype)
    hc_ref[0] = gc_s[HALO + tt - (CONV_C - 1):HALO + tt, :]


def _ac_prompt(h, nb, t_len, wa, ba, lg, lb, wc):
    tt = _tile(t_len, 512)
    nt = t_len // tt
    cw = D_A

    def cur(col):
        return pl.BlockSpec((tt, cw), lambda b, t: (b * nt + t, col // cw))

    def prev(col):
        return pl.BlockSpec(
            (HALO, cw),
            lambda b, t: (jnp.maximum((b * t_len + t * tt) // HALO - 1, 0), col // cw))

    def full(shape):
        return pl.BlockSpec(shape, lambda b, t: (0,) * len(shape))

    mp = nb * t_len
    return pl.pallas_call(
        _ac_prompt_kernel,
        grid=(nb, nt),
        in_specs=[cur(OFF_A_VAL), cur(OFF_A_GATE), cur(OFF_C_B), cur(OFF_C_C), cur(OFF_C_X),
                  prev(OFF_A_VAL), prev(OFF_A_GATE), prev(OFF_C_C), prev(OFF_C_X),
                  full((CONV_A, D_A)), full((1, D_A)), full((1, D_A)), full((1, D_A)),
                  full((CONV_C, D_C))],
        out_specs=[pl.BlockSpec((tt, D_A), lambda b, t: (b * nt + t, 0)),
                   pl.BlockSpec((tt, D_C), lambda b, t: (b * nt + t, 0)),
                   pl.BlockSpec((1, CONV_A - 1, D_A), lambda b, t: (b, 0, 0)),
                   pl.BlockSpec((1, CONV_C - 1, D_C), lambda b, t: (b, 0, 0))],
        out_shape=[jax.ShapeDtypeStruct((mp, D_A), BF16),
                   jax.ShapeDtypeStruct((mp, D_C), BF16),
                   jax.ShapeDtypeStruct((nb, CONV_A - 1, D_A), F32),
                   jax.ShapeDtypeStruct((nb, CONV_C - 1, D_C), F32)],
        scratch_shapes=[pltpu.VMEM((HALO + tt, D_A), F32), pltpu.VMEM((HALO + tt, D_C), F32),
                        pltpu.VMEM((HALO + tt, D_A), F32)],
        compiler_params=_params("parallel", "arbitrary"),
        name="ac_prompt",
    )(h, h, h, h, h, h, h, h, h, wa, ba, lg, lb, wc)


def _ac_sample_kernel(av_ref, ag_ref, cb_ref, cc_ref, cx_ref, hista_ref, histc_ref,
                      wa_ref, ba_ref, lg_ref, lb_ref, wc_ref,
                      ya_ref, yc_ref, ha_ref, hc_ref, xa_s, xc_s):
    sb, s_len = hista_ref.shape[0], av_ref.shape[0] // hista_ref.shape[0]
    ka, kc = CONV_A - 1, CONV_C - 1
    glu = av_ref[...] * jax.nn.sigmoid(ag_ref[...])
    xa_s[:, 0:ka, :] = hista_ref[...]
    xa_s[:, ka:ka + s_len, :] = glu.reshape(sb, s_len, D_A)
    acc = jnp.zeros((sb, s_len, D_A), F32) + ba_ref[...][None]
    for j in range(CONV_A):
        acc = acc + wa_ref[j:j + 1, :][None] * xa_s[:, j:j + s_len, :]
    ya = _silu(_layer_norm(acc, lg_ref[...][None], lb_ref[...][None]))
    ya_ref[...] = ya.reshape(sb * s_len, D_A).astype(ya_ref.dtype)
    ha_ref[...] = xa_s[:, s_len:s_len + ka, :]

    xc_s[:, 0:kc, :] = histc_ref[...]
    xc_s[:, kc:kc + s_len, :] = (cc_ref[...] * cx_ref[...]).reshape(sb, s_len, D_C)
    yc = jnp.zeros((sb, s_len, D_C), F32)
    for j in range(CONV_C):
        yc = yc + wc_ref[j:j + 1, :][None] * xc_s[:, j:j + s_len, :]
    yc = cb_ref[...] * yc.reshape(sb * s_len, D_C)
    yc_ref[...] = yc.astype(yc_ref.dtype)
    hc_ref[...] = xc_s[:, s_len:s_len + kc, :]


def _ac_sample(h, row0, nseq, s_len, hist_a, hist_c, wa, ba, lg, lb, wc):
    sb = _tile(nseq, 16)
    rows = sb * s_len
    blk0 = row0 // rows
    cw = D_A

    def cur(col):
        return pl.BlockSpec((rows, cw), lambda i: (blk0 + i, col // cw))

    def full(shape):
        return pl.BlockSpec(shape, lambda i: (0,) * len(shape))

    ms = nseq * s_len
    ka, kc = CONV_A - 1, CONV_C - 1
    return pl.pallas_call(
        _ac_sample_kernel,
        grid=(nseq // sb,),
        in_specs=[cur(OFF_A_VAL), cur(OFF_A_GATE), cur(OFF_C_B), cur(OFF_C_C), cur(OFF_C_X),
                  pl.BlockSpec((sb, ka, D_A), lambda i: (i, 0, 0)),
                  pl.BlockSpec((sb, kc, D_C), lambda i: (i, 0, 0)),
                  full((CONV_A, D_A)), full((1, D_A)), full((1, D_A)), full((1, D_A)),
                  full((CONV_C, D_C))],
        out_specs=[pl.BlockSpec((rows, D_A), lambda i: (i, 0)),
                   pl.BlockSpec((rows, D_C), lambda i: (i, 0)),
                   pl.BlockSpec((sb, ka, D_A), lambda i: (i, 0, 0)),
                   pl.BlockSpec((sb, kc, D_C), lambda i: (i, 0, 0))],
        out_shape=[jax.ShapeDtypeStruct((ms, D_A), BF16),
                   jax.ShapeDtypeStruct((ms, D_C), BF16),
                   jax.ShapeDtypeStruct((nseq, ka, D_A), F32),
                   jax.ShapeDtypeStruct((nseq, kc, D_C), F32)],
        scratch_shapes=[pltpu.VMEM((sb, ka + s_len + 2, D_A), F32),
                        pltpu.VMEM((sb, kc + s_len + 6, D_C), F32)],
        compiler_params=_params("parallel"),
        name="ac_sample",
    )(h, h, h, h, h, hist_a, hist_c, wa, ba, lg, lb, wc)


def _rope_tables(pos):
    half = HEAD_DIM // 2
    inv_freq = jnp.power(ROPE_THETA, -jnp.arange(half, dtype=F32) * (2.0 / HEAD_DIM))
    ang = pos.astype(F32)[:, None] * inv_freq[None, :]
    cos, sin = jnp.cos(ang), jnp.sin(ang)
    cos_t = jnp.concatenate([cos, cos, cos, cos], axis=1)
    sin_t = jnp.concatenate([-sin, sin, -sin, sin], axis=1)
    return cos_t, sin_t


def _rope(x, cos_t, sin_t):
    half = HEAD_DIM // 2
    axis = x.ndim - 1
    shape = x.shape[:-1] + (LANES,)
    lane = lax.broadcasted_iota(jnp.int32, shape, axis)
    first = (lane % HEAD_DIM) < half
    out = []
    for i in range(x.shape[-1] // LANES):
        xi = x[..., LANES * i:LANES * (i + 1)]
        partner = jnp.where(first, pltpu.roll(xi, LANES - half, axis), pltpu.roll(xi, half, axis))
        out.append(xi * cos_t + partner * sin_t)
    return out


def _head_halves(t, axis, lo_valid):
    lane = lax.broadcasted_iota(jnp.int32, t.shape, axis)
    if lo_valid:
        lo = jnp.where(lane < HEAD_DIM, t, jnp.zeros_like(t))
        hi = pltpu.roll(lo, HEAD_DIM, axis)
    else:
        hi = jnp.where(lane >= HEAD_DIM, t, jnp.zeros_like(t))
        lo = pltpu.roll(hi, HEAD_DIM, axis)
    return lo, hi


def _swa_block(sinks_ref, q_tiles, kp_tiles, kc_tiles, vp, vc, has_prev, write):
    w = WINDOW
    ci = lax.broadcasted_iota(jnp.int32, (2 * w, 2 * w), 0)
    qi = lax.broadcasted_iota(jnp.int32, (2 * w, 2 * w), 1) % w
    mask = (ci > qi) & (ci <= qi + w) & (has_prev | (ci >= w))
    lane_first = lax.broadcasted_iota(jnp.int32, (1, 2 * w), 1) < w

    qt_tiles = [t.T.astype(BF16) for t in q_tiles]
    zeros_t = jnp.zeros((HEAD_DIM, 2 * w), BF16)

    for h in range(N_KV):
        tile, lo_valid = h // 2, (h % 2 == 0)
        lanes = slice(LANES * tile, LANES * (tile + 1))
        kcat = jnp.concatenate([kp_tiles[tile], kc_tiles[tile]], axis=0)
        k_lo, k_hi = _head_halves(kcat.astype(BF16), 1, lo_valid)
        vt = jnp.concatenate([vp[:, lanes], vc[:, lanes]], axis=0).T.astype(BF16)
        vt = vt[0:HEAD_DIM] if lo_valid else vt[HEAD_DIM:2 * HEAD_DIM]
        vt_halves = (jnp.concatenate([vt, zeros_t], axis=0), jnp.concatenate([zeros_t, vt], axis=0))
        qst = jnp.concatenate([qt_tiles[2 * h], qt_tiles[2 * h + 1]], axis=1)
        ot = None
        for half_idx, k_half in enumerate((k_lo, k_hi)):
            s = jnp.dot(k_half, qst, preferred_element_type=F32)
            s = jnp.where(mask, s, -jnp.inf)
            sk = jnp.where(lane_first, sinks_ref[4 * h + half_idx], sinks_ref[4 * h + 2 + half_idx])
            m = jnp.maximum(jnp.max(s, axis=0, keepdims=True), sk)
            e = jnp.exp(s - m)
            den = jnp.sum(e, axis=0, keepdims=True) + jnp.exp(sk - m)
            p = (e / den).astype(BF16)
            term = jnp.dot(vt_halves[half_idx], p, preferred_element_type=F32)
            ot = term if ot is None else ot + term
        write(2 * h, ot[:, 0:w].T)
        write(2 * h + 1, ot[:, w:2 * w].T)


def _attn_prompt_kernel(sinks_ref, q_ref, kc_ref, vc_ref, kp_ref, vp_ref,
                        cos_ref, sin_ref, cosp_ref, sinp_ref,
                        yb_ref, nk_ref, nv_ref):
    n = pl.program_id(1)
    w = WINDOW
    cos_c, sin_c = cos_ref[...], sin_ref[...]
    q_tiles = _rope(q_ref[...] * (HEAD_DIM ** -0.5), cos_c, sin_c)
    kc_tiles = _rope(kc_ref[...], cos_c, sin_c)
    kp_tiles = _rope(kp_ref[...], cosp_ref[...], sinp_ref[...])
    for i in range(KV_WIDTH // LANES):
        nk_ref[0, :, LANES * i:LANES * (i + 1)] = kc_tiles[i][w:2 * w]
    nv_ref[0] = vc_ref[w:2 * w, :]

    for j in range(2):
        rows = slice(j * w, (j + 1) * w)

        def write(tile, value, rows=rows):
            yb_ref[rows, LANES * tile:LANES * (tile + 1)] = value.astype(yb_ref.dtype)

        if j == 0:
            prev_k, prev_v, has_prev = kp_tiles, vp_ref[...], n > 0
        else:
            prev_k, prev_v, has_prev = [t[0:w] for t in kc_tiles], vc_ref[0:w, :], True
        _swa_block(sinks_ref, [t[rows] for t in q_tiles], prev_k, [t[rows] for t in kc_tiles],
                   prev_v, vc_ref[rows, :], has_prev, write)


def _attn_prompt(h, nb, t_len, sinks, cos_t, sin_t):
    w = WINDOW
    nblk = t_len // w
    assert nblk % 2 == 0
    npair = nblk // 2
    mp = nb * t_len

    def pair(b, n):
        return b * npair + n

    def halo(b, n):
        return b * nblk + jnp.maximum(2 * n - 1, 0)

    return pl.pallas_call(
        _attn_prompt_kernel,
        grid=(nb, npair),
        in_specs=[pl.BlockSpec(memory_space=pltpu.SMEM),
                  pl.BlockSpec((2 * w, Q_WIDTH), lambda b, n: (pair(b, n), OFF_Q // Q_WIDTH)),
                  pl.BlockSpec((2 * w, KV_WIDTH), lambda b, n: (pair(b, n), OFF_K // KV_WIDTH)),
                  pl.BlockSpec((2 * w, KV_WIDTH), lambda b, n: (pair(b, n), OFF_V // KV_WIDTH)),
                  pl.BlockSpec((w, KV_WIDTH), lambda b, n: (halo(b, n), OFF_K // KV_WIDTH)),
                  pl.BlockSpec((w, KV_WIDTH), lambda b, n: (halo(b, n), OFF_V // KV_WIDTH)),
                  pl.BlockSpec((2 * w, LANES), lambda b, n: (n, 0)),
                  pl.BlockSpec((2 * w, LANES), lambda b, n: (n, 0)),
                  pl.BlockSpec((w, LANES), lambda b, n: (jnp.maximum(2 * n - 1, 0), 0)),
                  pl.BlockSpec((w, LANES), lambda b, n: (jnp.maximum(2 * n - 1, 0), 0))],
        out_specs=[pl.BlockSpec((2 * w, Q_WIDTH), lambda b, n: (pair(b, n), 0)),
                   pl.BlockSpec((1, w, KV_WIDTH), lambda b, n: (b, 0, 0)),
                   pl.BlockSpec((1, w, KV_WIDTH), lambda b, n: (b, 0, 0))],
        out_shape=[jax.ShapeDtypeStruct((mp, Q_WIDTH), BF16),
                   jax.ShapeDtypeStruct((nb, w, KV_WIDTH), F32),
                   jax.ShapeDtypeStruct((nb, w, KV_WIDTH), F32)],
        compiler_params=_params("parallel", "arbitrary"),
        name="attn_prompt",
    )(sinks, h, h, h, h, h, cos_t, sin_t, cos_t, sin_t)


def _attn_sample_kernel(sinks_ref, q_ref, kn_ref, vn_ref, kbuf_ref, vbuf_ref, cos_ref, sin_ref,
                        yb_ref, nk_ref, nv_ref):
    sb, w = kbuf_ref.shape[0], kbuf_ref.shape[1]
    s_len = q_ref.shape[0] // sb
    cos_t, sin_t = cos_ref[...], sin_ref[...]
    q_tiles = _rope(q_ref[...] * (HEAD_DIM ** -0.5), cos_t, sin_t)
    kn_tiles = _rope(kn_ref[...], cos_t, sin_t)
    nk_ref[:, 0:w - s_len, :] = kbuf_ref[:, s_len:w, :]
    nv_ref[:, 0:w - s_len, :] = vbuf_ref[:, s_len:w, :]
    for i in range(KV_WIDTH // LANES):
        nk_ref[:, w - s_len:w, LANES * i:LANES * (i + 1)] = kn_tiles[i].reshape(sb, s_len, LANES)
    nv_ref[:, w - s_len:w, :] = vn_ref[...].reshape(sb, s_len, KV_WIDTH)

    qi = lax.broadcasted_iota(jnp.int32, (2 * s_len, w), 0) % s_len
    ci = lax.broadcasted_iota(jnp.int32, (2 * s_len, w), 1)
    mask_buf = (ci > qi)[None]
    qn = lax.broadcasted_iota(jnp.int32, (2 * s_len, s_len), 0) % s_len
    cn = lax.broadcasted_iota(jnp.int32, (2 * s_len, s_len), 1)
    mask_new = (cn <= qn)[None]
    row_top = (lax.broadcasted_iota(jnp.int32, (2 * s_len, 1), 0) < s_len)[None]
    bqk = (((2,), (2,)), ((0,), (0,)))
    bkd = (((2,), (1,)), ((0,), (0,)))

    for h in range(N_KV):
        tile, lo_valid = h // 2, (h % 2 == 0)
        sl = slice(LANES * tile, LANES * (tile + 1))
        kb_lo, kb_hi = _head_halves(kbuf_ref[:, :, sl].astype(BF16), 2, lo_valid)
        vb_lo, vb_hi = _head_halves(vbuf_ref[:, :, sl].astype(BF16), 2, lo_valid)
        kn3 = kn_tiles[tile].reshape(sb, s_len, LANES).astype(BF16)
        vn3 = vn_ref[:, sl].reshape(sb, s_len, LANES).astype(BF16)
        kn_lo, kn_hi = _head_halves(kn3, 2, lo_valid)
        vn_lo, vn_hi = _head_halves(vn3, 2, lo_valid)
        qs = jnp.concatenate([q_tiles[2 * h].reshape(sb, s_len, LANES),
                              q_tiles[2 * h + 1].reshape(sb, s_len, LANES)], axis=1).astype(BF16)
        o = jnp.zeros((sb, 2 * s_len, LANES), F32)
        for half_idx, (kb, kn, vb, vn) in enumerate(((kb_lo, kn_lo, vb_lo, vn_lo),
                                                     (kb_hi, kn_hi, vb_hi, vn_hi))):
            s_b = lax.dot_general(qs, kb, bqk, preferred_element_type=F32)
            s_n = lax.dot_general(qs, kn, bqk, preferred_element_type=F32)
            s_b = jnp.where(mask_buf, s_b, -jnp.inf)
            s_n = jnp.where(mask_new, s_n, -jnp.inf)
            sk = jnp.where(row_top, sinks_ref[4 * h + half_idx], sinks_ref[4 * h + 2 + half_idx])
            m = jnp.maximum(jnp.maximum(jnp.max(s_b, axis=2, keepdims=True),
                                        jnp.max(s_n, axis=2, keepdims=True)), sk)
            e_b = jnp.exp(s_b - m)
            e_n = jnp.exp(s_n - m)
            den = (jnp.sum(e_b, axis=2, keepdims=True) + jnp.sum(e_n, axis=2, keepdims=True)
                   + jnp.exp(sk - m))
            o = o + lax.dot_general((e_b / den).astype(BF16), vb, bkd, preferred_element_type=F32)
            o = o + lax.dot_general((e_n / den).astype(BF16), vn, bkd, preferred_element_type=F32)
        yb_ref[:, LANES * (2 * h):LANES * (2 * h + 1)] = (
            o[:, 0:s_len, :].reshape(sb * s_len, LANES).astype(yb_ref.dtype))
        yb_ref[:, LANES * (2 * h + 1):LANES * (2 * h + 2)] = (
            o[:, s_len:2 * s_len, :].reshape(sb * s_len, LANES).astype(yb_ref.dtype))


def _attn_sample(h, row0, nseq, s_len, sinks, k_buf, v_buf, cos_t, sin_t):
    sb = _tile(nseq, 8)
    rows = sb * s_len
    blk0 = row0 // rows
    w = k_buf.shape[1]
    ms = nseq * s_len
    cos_rows = jnp.tile(cos_t, (sb, 1))
    sin_rows = jnp.tile(sin_t, (sb, 1))
    return pl.pallas_call(
        _attn_sample_kernel,
        grid=(nseq // sb,),
        in_specs=[pl.BlockSpec(memory_space=pltpu.SMEM),
                  pl.BlockSpec((rows, Q_WIDTH), lambda i: (blk0 + i, OFF_Q // Q_WIDTH)),
                  pl.BlockSpec((rows, KV_WIDTH), lambda i: (blk0 + i, OFF_K // KV_WIDTH)),
                  pl.BlockSpec((rows, KV_WIDTH), lambda i: (blk0 + i, OFF_V // KV_WIDTH)),
                  pl.BlockSpec((sb, w, KV_WIDTH), lambda i: (i, 0, 0)),
                  pl.BlockSpec((sb, w, KV_WIDTH), lambda i: (i, 0, 0)),
                  pl.BlockSpec((rows, LANES), lambda i: (0, 0)),
                  pl.BlockSpec((rows, LANES), lambda i: (0, 0))],
        out_specs=[pl.BlockSpec((rows, Q_WIDTH), lambda i: (i, 0)),
                   pl.BlockSpec((sb, w, KV_WIDTH), lambda i: (i, 0, 0)),
                   pl.BlockSpec((sb, w, KV_WIDTH), lambda i: (i, 0, 0))],
        out_shape=[jax.ShapeDtypeStruct((ms, Q_WIDTH), BF16),
                   jax.ShapeDtypeStruct((nseq, w, KV_WIDTH), F32),
                   jax.ShapeDtypeStruct((nseq, w, KV_WIDTH), F32)],
        compiler_params=_params("parallel"),
        name="attn_sample",
    )(sinks, h, h, h, k_buf, v_buf, cos_rows, sin_rows)


def _d_prompt_kernel(du_ref, dv_ref, lg_ref, lb_ref, w_ref, bt_ref, yd_ref):
    c = CHUNK
    ri = lax.broadcasted_iota(jnp.int32, (c, c), 0)
    cj = lax.broadcasted_iota(jnp.int32, (c, c), 1)
    causal = cj <= ri
    w_causal = [jnp.where(causal, w_ref[g], 0.0).astype(BF16) for g in range(N_GROUPS_D)]
    for k in range(du_ref.shape[0] // c):
        rows = slice(k * c, (k + 1) * c)
        u = _gelu(du_ref[rows, :])
        vn = _layer_norm(_gelu(dv_ref[rows, :]), lg_ref[...], lb_ref[...]).astype(BF16)
        for g in range(N_GROUPS_D):
            sl = slice(GROUP_D * g, GROUP_D * (g + 1))
            s = jnp.dot(w_causal[g], vn[:, sl], preferred_element_type=F32) + bt_ref[:, g:g + 1]
            yd_ref[rows, sl] = (u[:, sl] * s).astype(yd_ref.dtype)


def _d_prompt(h, mp, ln_g, ln_b, sgu_w, sgu_bt):
    c = CHUNK
    rows = _tile(mp, 4 * c)
    assert rows % c == 0
    return pl.pallas_call(
        _d_prompt_kernel,
        grid=(mp // rows,),
        in_specs=[pl.BlockSpec((rows, D_D), lambda i: (i, OFF_D_U // D_D)),
                  pl.BlockSpec((rows, D_D), lambda i: (i, OFF_D_V // D_D)),
                  pl.BlockSpec((1, D_D), lambda i: (0, 0)),
                  pl.BlockSpec((1, D_D), lambda i: (0, 0)),
                  pl.BlockSpec((N_GROUPS_D, c, c), lambda i: (0, 0, 0)),
                  pl.BlockSpec((c, N_GROUPS_D), lambda i: (0, 0))],
        out_specs=pl.BlockSpec((rows, D_D), lambda i: (i, 0)),
        out_shape=jax.ShapeDtypeStruct((mp, D_D), BF16),
        compiler_params=_params("parallel"),
        name="d_prompt",
    )(h, h, ln_g, ln_b, sgu_w, sgu_bt)


def _d_sample_kernel(du_ref, dv_ref, lg_ref, lb_ref, wt_ref, bt_ref, yd_ref, vd_ref):
    sb, s_len = vd_ref.shape[0], vd_ref.shape[1]
    u = _gelu(du_ref[...])
    vn = _layer_norm(_gelu(dv_ref[...]), lg_ref[...], lb_ref[...])
    vn3 = vn.reshape(sb, s_len, D_D)
    vd_ref[...] = vn3
    ii = lax.broadcasted_iota(jnp.int32, (s_len, D_D), 0)
    s = jnp.zeros((sb, s_len, D_D), F32) + bt_ref[...][None]
    for j in range(s_len):
        wj = jnp.where(ii >= j, wt_ref[j], 0.0)
        s = s + wj[None] * vn3[:, j:j + 1, :]
    yd_ref[...] = (u * s.reshape(sb * s_len, D_D)).astype(yd_ref.dtype)


def _d_sample(h, row0, nseq, s_len, ln_g, ln_b, sgu_wt, sgu_bs):
    sb = _tile(nseq, 16)
    rows = sb * s_len
    blk0 = row0 // rows
    ms = nseq * s_len
    return pl.pallas_call(
        _d_sample_kernel,
        grid=(nseq // sb,),
        in_specs=[pl.BlockSpec((rows, D_D), lambda i: (blk0 + i, OFF_D_U // D_D)),
                  pl.BlockSpec((rows, D_D), lambda i: (blk0 + i, OFF_D_V // D_D)),
                  pl.BlockSpec((1, D_D), lambda i: (0, 0)),
                  pl.BlockSpec((1, D_D), lambda i: (0, 0)),
                  pl.BlockSpec((s_len, s_len, D_D), lambda i: (0, 0, 0)),
                  pl.BlockSpec((s_len, D_D), lambda i: (0, 0))],
        out_specs=[pl.BlockSpec((rows, D_D), lambda i: (i, 0)),
                   pl.BlockSpec((sb, s_len, D_D), lambda i: (i, 0, 0))],
        out_shape=[jax.ShapeDtypeStruct((ms, D_D), BF16),
                   jax.ShapeDtypeStruct((nseq, s_len, D_D), F32)],
        compiler_params=_params("parallel"),
        name="d_sample",
    )(h, h, ln_g, ln_b, sgu_wt, sgu_bs)


def _merge_kernel(x_ref, yap_ref, ybp_ref, ycp_ref, ydp_ref, yas_ref, ybs_ref, ycs_ref, yds_ref,
                  g0_ref, g1_ref, g2_ref, g3_ref, bg_ref, pa_ref, pb_ref, pc_ref, pd_ref, o_ref,
                  *, prompt_tiles):
    x = x_ref[...]
    is_prompt = pl.program_id(1) < prompt_tiles
    acc = None
    branches = ((yap_ref, yas_ref, g0_ref, pa_ref), (ybp_ref, ybs_ref, g1_ref, pb_ref),
                (ycp_ref, ycs_ref, g2_ref, pc_ref), (ydp_ref, yds_ref, g3_ref, pd_ref))
    for i, (yp_ref, ys_ref, g_ref, p_ref) in enumerate(branches):
        y = jnp.where(is_prompt, yp_ref[...], ys_ref[...])
        gate = jax.nn.sigmoid(jnp.dot(x, g_ref[...], preferred_element_type=F32) + bg_ref[i:i + 1, :])
        term = gate * jnp.dot(y, p_ref[...], preferred_element_type=F32)
        acc = term if acc is None else acc + term
    o_ref[...] = acc.astype(o_ref.dtype)


def _merge(xb, y_prompt, y_sample, w_in_b, b_gate, projs):
    m = xb.shape[0]
    mp, ms = y_prompt[0].shape[0], y_sample[0].shape[0]
    tm = _tile(np.gcd(mp, ms), 512)
    npt = mp // tm
    tn = 512
    nn = D_MODEL // tn
    widths = (D_A, Q_WIDTH, D_C, D_D)

    def act_p(width):
        return pl.BlockSpec((tm, width), lambda j, i: (jnp.minimum(i, npt - 1), 0))

    def act_s(width):
        return pl.BlockSpec((tm, width), lambda j, i: (jnp.maximum(i - npt, 0), 0))

    def gate_w(b):
        return pl.BlockSpec((D_MODEL, tn), lambda j, i: (0, (OFF_GATES + b * D_MODEL) // tn + j))

    def proj_w(k):
        return pl.BlockSpec((k, tn), lambda j, i: (0, j))

    return pl.pallas_call(
        functools.partial(_merge_kernel, prompt_tiles=npt),
        grid=(nn, m // tm),
        in_specs=[pl.BlockSpec((tm, D_MODEL), lambda j, i: (i, 0))]
                 + [act_p(wd) for wd in widths] + [act_s(wd) for wd in widths]
                 + [gate_w(0), gate_w(1), gate_w(2), gate_w(3),
                    pl.BlockSpec((N_BRANCH, tn), lambda j, i: (0, j))]
                 + [proj_w(wd) for wd in widths],
        out_specs=pl.BlockSpec((tm, tn), lambda j, i: (i, j)),
        out_shape=jax.ShapeDtypeStruct((m, D_MODEL), BF16),
        compiler_params=_params("parallel", "arbitrary"),
        name="merge",
    )(xb, *y_prompt, *y_sample, w_in_b, w_in_b, w_in_b, w_in_b, b_gate, *projs)


def _out_ln_kernel(mg_ref, *refs, prompt_tiles):
    if prompt_tiles is None:
        x_ref, w_ref, g_ref, b_ref, o_ref, ob_ref = refs
    else:
        xp_ref, xs_ref, w_ref, g_ref, b_ref, o_ref, ob_ref = refs
        is_prompt = pl.program_id(0) < prompt_tiles
    tm = mg_ref.shape[0]
    sub = tm // OUT_LN_SUBTILES
    for k in range(OUT_LN_SUBTILES):
        rows = slice(k * sub, (k + 1) * sub)
        if prompt_tiles is None:
            x = x_ref[rows, :]
        else:
            x = jnp.where(is_prompt, xp_ref[rows, :], xs_ref[rows, :])
        y = ALPHA * x + jnp.dot(mg_ref[rows, :], w_ref[...], preferred_element_type=F32)
        y = _layer_norm(y, g_ref[...], b_ref[...])
        o_ref[rows, :] = y
        ob_ref[rows, :] = y.astype(ob_ref.dtype)


def _out_ln(merged, x, w_out_b, g, b):
    m = merged.shape[0]
    if isinstance(x, tuple):
        mp, ms = x[0].shape[0], x[1].shape[0]
        tm = _tile(np.gcd(mp, ms), 512)
        npt = mp // tm
        x_specs = [pl.BlockSpec((tm, D_MODEL), lambda i: (jnp.minimum(i, npt - 1), 0)),
                   pl.BlockSpec((tm, D_MODEL), lambda i: (jnp.maximum(i - npt, 0), 0))]
    else:
        tm = _tile(m, 512)
        npt = None
        x, x_specs = (x,), [pl.BlockSpec((tm, D_MODEL), lambda i: (i, 0))]
    return pl.pallas_call(
        functools.partial(_out_ln_kernel, prompt_tiles=npt),
        grid=(m // tm,),
        in_specs=[pl.BlockSpec((tm, D_MODEL), lambda i: (i, 0))] + x_specs
                 + [pl.BlockSpec((D_MODEL, D_MODEL), lambda i: (0, 0), pipeline_mode=pl.Buffered(1)),
                    pl.BlockSpec((1, D_MODEL), lambda i: (0, 0)),
                    pl.BlockSpec((1, D_MODEL), lambda i: (0, 0))],
        out_specs=[pl.BlockSpec((tm, D_MODEL), lambda i: (i, 0)),
                   pl.BlockSpec((tm, D_MODEL), lambda i: (i, 0))],
        out_shape=[jax.ShapeDtypeStruct((m, D_MODEL), F32),
                   jax.ShapeDtypeStruct((m, D_MODEL), BF16)],
        compiler_params=_params("parallel"),
        name="out_ln",
    )(merged, *x, w_out_b, g, b)


def _ffn_kernel(x_ref, xb_ref, wg_ref, wu_ref, wd_ref, g_ref, b_ref, o_ref, ob_ref, acc_s):
    f = pl.program_id(1)
    last = pl.num_programs(1) - 1

    def partial_down():
        xb = xb_ref[...]
        hid = (_silu(jnp.dot(xb, wg_ref[...], preferred_element_type=F32))
               * jnp.dot(xb, wu_ref[...], preferred_element_type=F32))
        return jnp.dot(hid.astype(BF16), wd_ref[...], preferred_element_type=F32)

    @pl.when(f == 0)
    def _():
        acc_s[...] = partial_down()

    @pl.when((f > 0) & (f < last))
    def _():
        acc_s[...] += partial_down()

    @pl.when(f == last)
    def _():
        acc_s[...] += partial_down()
        sub = acc_s.shape[0] // OUT_LN_SUBTILES
        for k in range(OUT_LN_SUBTILES):
            rows = slice(k * sub, (k + 1) * sub)
            y = _layer_norm(ALPHA * x_ref[rows, :] + acc_s[rows, :], g_ref[...], b_ref[...])
            o_ref[rows, :] = y
            ob_ref[rows, :] = y.astype(ob_ref.dtype)


def _ffn_ln(x, xb, wg, wu, wd, g, b):
    m = x.shape[0]
    tm = _tile(m, 512)
    tf = 512
    return pl.pallas_call(
        _ffn_kernel,
        grid=(m // tm, D_FF // tf),
        in_specs=[pl.BlockSpec((tm, D_MODEL), lambda i, f: (i, 0)),
                  pl.BlockSpec((tm, D_MODEL), lambda i, f: (i, 0)),
                  pl.BlockSpec((D_MODEL, tf), lambda i, f: (0, f)),
                  pl.BlockSpec((D_MODEL, tf), lambda i, f: (0, f)),
                  pl.BlockSpec((tf, D_MODEL), lambda i, f: (f, 0)),
                  pl.BlockSpec((1, D_MODEL), lambda i, f: (0, 0)),
                  pl.BlockSpec((1, D_MODEL), lambda i, f: (0, 0))],
        out_specs=[pl.BlockSpec((tm, D_MODEL), lambda i, f: (i, 0)),
                   pl.BlockSpec((tm, D_MODEL), lambda i, f: (i, 0))],
        out_shape=[jax.ShapeDtypeStruct((m, D_MODEL), F32),
                   jax.ShapeDtypeStruct((m, D_MODEL), BF16)],
        scratch_shapes=[pltpu.VMEM((tm, D_MODEL), F32)],
        compiler_params=_params("parallel", "arbitrary"),
        name="ffn_ln",
    )(x, xb, wg, wu, wd, g, b)


def _router_kernel(x_ref, w_ref, b_ref, idx_ref, prob_ref, rank_ref, cnt_ref, run_s):
    @pl.when(pl.program_id(0) == 0)
    def _():
        run_s[...] = jnp.zeros_like(run_s)

    x, w = x_ref[...], w_ref[...]
    x_hi, w_hi = x.astype(BF16), w.astype(BF16)
    x_lo = (x - x_hi.astype(F32)).astype(BF16)
    w_lo = (w - w_hi.astype(F32)).astype(BF16)
    logits = (jnp.dot(x_hi, w_hi, preferred_element_type=F32)
              + (jnp.dot(x_lo, w_hi, preferred_element_type=F32)
                 + jnp.dot(x_hi, w_lo, preferred_element_type=F32))) + b_ref[...]
    tm = logits.shape[0]
    idx = lax.broadcasted_iota(jnp.int32, logits.shape, 1)
    v1 = jnp.max(logits, axis=1, keepdims=True)
    i1 = jnp.min(jnp.where(logits == v1, idx, N_EXPERTS), axis=1, keepdims=True)
    rest = jnp.where(idx == i1, -jnp.inf, logits)
    v2 = jnp.max(rest, axis=1, keepdims=True)
    i2 = jnp.min(jnp.where(rest == v2, idx, N_EXPERTS), axis=1, keepdims=True)
    e2 = jnp.exp(v2 - v1)
    den = 1.0 + e2

    hit = ((idx == i1) | (idx == i2)).astype(F32)
    ri = lax.broadcasted_iota(jnp.int32, (tm, tm), 0)
    ci = lax.broadcasted_iota(jnp.int32, (tm, tm), 1)
    before = jnp.dot((ci < ri).astype(BF16), hit.astype(BF16), preferred_element_type=F32) + run_s[...]
    rank1 = jnp.sum(jnp.where(idx == i1, before, 0.0), axis=1, keepdims=True)
    rank2 = jnp.sum(jnp.where(idx == i2, before, 0.0), axis=1, keepdims=True)

    col = lax.broadcasted_iota(jnp.int32, (tm, 2), 1)
    idx_ref[...] = jnp.where(col == 0, i1, i2)
    prob_ref[...] = jnp.where(col == 0, 1.0 / den, e2 / den)
    rank_ref[...] = jnp.where(col == 0, rank1, rank2).astype(jnp.int32)
    total = run_s[...] + jnp.sum(hit, axis=0, keepdims=True)
    run_s[...] = total
    cnt_ref[...] = total.astype(jnp.int32)


def _router(x, router_w, router_b):
    m = x.shape[0]
    tm = _tile(m, 512)
    return pl.pallas_call(
        _router_kernel,
        grid=(m // tm,),
        in_specs=[pl.BlockSpec((tm, D_MODEL), lambda i: (i, 0)),
                  pl.BlockSpec((D_MODEL, N_EXPERTS), lambda i: (0, 0)),
                  pl.BlockSpec((1, N_EXPERTS), lambda i: (0, 0))],
        out_specs=[pl.BlockSpec((tm, 2), lambda i: (i, 0)),
                   pl.BlockSpec((tm, 2), lambda i: (i, 0)),
                   pl.BlockSpec((tm, 2), lambda i: (i, 0)),
                   pl.BlockSpec((1, N_EXPERTS), lambda i: (0, 0))],
        out_shape=[jax.ShapeDtypeStruct((m, 2), jnp.int32),
                   jax.ShapeDtypeStruct((m, 2), F32),
                   jax.ShapeDtypeStruct((m, 2), jnp.int32),
                   jax.ShapeDtypeStruct((1, N_EXPERTS), jnp.int32)],
        scratch_shapes=[pltpu.VMEM((1, N_EXPERTS), F32)],
        compiler_params=_params("arbitrary"),
        name="router",
    )(x, router_w, router_b)


def _route_tables(idx, rank, counts, tm, n_tiles):
    m = idx.shape[0]
    padded = ((counts + tm - 1) // tm) * tm
    ends = jnp.cumsum(padded)
    base = ends - padded
    pos = base[idx] + rank
    n_valid = ends[-1] // tm
    tiles = jnp.arange(n_tiles, dtype=jnp.int32)
    tile_valid = (tiles < n_valid).astype(jnp.int32)
    first_row = jnp.minimum(tiles, n_valid - 1) * tm
    owner = jnp.sum((ends[None, :] <= first_row[:, None]).astype(jnp.int32), axis=1)
    tile_expert = jnp.minimum(owner, N_EXPERTS - 1).astype(jnp.int32)
    flat = pos.reshape(-1)
    token = jnp.repeat(jnp.arange(m, dtype=jnp.int32), 2)
    src = jnp.zeros((n_tiles * tm,), jnp.int32).at[flat].set(token, unique_indices=True)
    return pos, src, tile_expert, tile_valid


def _row_copy(src_hbm, row, dst_vmem, dst_row, sem):
    return pltpu.make_async_copy(src_hbm.at[pl.ds(row, 1), :], dst_vmem.at[pl.ds(dst_row, 1), :], sem)


def _moe_group_kernel(te_ref, tv_ref, src_ref, src_next_ref, x_hbm, wg_ref, wu_ref, wd_ref,
                      o_ref, xg_s, xb_s, sem):
    i, f = pl.program_id(0), pl.program_id(1)
    nf = pl.num_programs(1)
    tm = xb_s.shape[0]
    chunk = tm // (MOE_F_STEPS - 1)
    slot = i % 2
    valid = tv_ref[i] == 1

    @pl.when((i == 0) & (f == 0))
    def _():
        def body(r, c):
            _row_copy(x_hbm, src_ref[0, r], xg_s.at[0], r, sem.at[0]).start()
            return c
        lax.fori_loop(0, tm, body, 0, unroll=8)

    def wait_rows():
        pltpu.make_async_copy(xg_s.at[slot], xg_s.at[slot], sem.at[slot]).wait()

    @pl.when((f == 0) & jnp.logical_not(valid))
    def _():
        o_ref[...] = jnp.zeros_like(o_ref)

        @pl.when((i == 0) | (tv_ref[jnp.maximum(i - 1, 0)] == 1))
        def _():
            wait_rows()

    def swiglu_step(first, issue_next):
        if first:
            wait_rows()
            xb = xg_s[slot].astype(BF16)
            xb_s[...] = xb
        else:
            xb = xb_s[...]
        if issue_next:
            for r in range(chunk):
                row = f * chunk + r
                _row_copy(x_hbm, src_next_ref[0, row], xg_s.at[1 - slot], row, sem.at[1 - slot]).start()
        hid = (_silu(jnp.dot(xb, wg_ref[...], preferred_element_type=F32))
               * jnp.dot(xb, wu_ref[...], preferred_element_type=F32))
        down = jnp.dot(hid.astype(BF16), wd_ref[...], preferred_element_type=F32)
        if first:
            o_ref[...] = down
        else:
            o_ref[...] += down

    @pl.when(valid & (f == 0))
    def _():
        swiglu_step(True, True)

    @pl.when(valid & (f > 0) & (f < nf - 1))
    def _():
        swiglu_step(False, True)

    @pl.when(valid & (f == nf - 1))
    def _():
        swiglu_step(False, False)


def _moe_group(x, src, tile_expert, tile_valid, wg, wu, wd, tm, n_tiles):
    tf = D_FF // MOE_F_STEPS
    nf = MOE_F_STEPS

    def w_col(i, f, te, tv):
        return (te[i], 0, jnp.where(tv[i] == 1, f, nf - 1))

    def w_row(i, f, te, tv):
        return (te[i], jnp.where(tv[i] == 1, f, nf - 1), 0)

    grid_spec = pltpu.PrefetchScalarGridSpec(
        num_scalar_prefetch=2,
        grid=(n_tiles, nf),
        in_specs=[pl.BlockSpec((None, 1, tm), lambda i, f, te, tv: (i, 0, 0), memory_space=pltpu.SMEM),
                  pl.BlockSpec((None, 1, tm), lambda i, f, te, tv: (jnp.minimum(i + 1, n_tiles - 1), 0, 0),
                               memory_space=pltpu.SMEM),
                  pl.BlockSpec(memory_space=pl.ANY),
                  pl.BlockSpec((None, D_MODEL, tf), w_col),
                  pl.BlockSpec((None, D_MODEL, tf), w_col),
                  pl.BlockSpec((None, tf, D_MODEL), w_row)],
        out_specs=pl.BlockSpec((tm, D_MODEL), lambda i, f, te, tv: (i, 0)),
        scratch_shapes=[pltpu.VMEM((2, tm, D_MODEL), F32),
                        pltpu.VMEM((tm, D_MODEL), BF16),
                        pltpu.SemaphoreType.DMA((2,))],
    )
    src3 = src.reshape(n_tiles, 1, tm)
    return pl.pallas_call(
        _moe_group_kernel,
        grid_spec=grid_spec,
        out_shape=jax.ShapeDtypeStruct((n_tiles * tm, D_MODEL), F32),
        compiler_params=_params("arbitrary", "arbitrary"),
        name="moe_group",
    )(tile_expert, tile_valid, src3, src3, x, wg, wu, wd)

def _combine_kernel(pos_ref, pos_next_ref, x_ref, p_ref, y_hbm, g_ref, b_ref, op_ref, os_ref, ybuf, sem,
                    *, prompt_tiles):
    i, nt = pl.program_id(0), pl.num_programs(0)
    tm = x_ref.shape[0]
    slot = i % 2

    def start_pair(idx_ref, s, r):
        _row_copy(y_hbm, idx_ref[0, 2 * r], ybuf.at[s, 0], r, sem.at[s]).start()
        _row_copy(y_hbm, idx_ref[0, 2 * r + 1], ybuf.at[s, 1], r, sem.at[s]).start()

    @pl.when(i == 0)
    def _():
        def body(r, c):
            start_pair(pos_ref, 0, r)
            return c
        lax.fori_loop(0, tm, body, 0, unroll=4)

    pltpu.make_async_copy(ybuf.at[slot], ybuf.at[slot], sem.at[slot]).wait()

    def norm_tile(issue_next):
        if issue_next:
            for r in range(tm):
                start_pair(pos_next_ref, 1 - slot, r)
        moe = p_ref[:, 0:1] * ybuf[slot, 0] + p_ref[:, 1:2] * ybuf[slot, 1]
        y = _layer_norm(ALPHA * x_ref[...] + moe, g_ref[...], b_ref[...])

        @pl.when(i < prompt_tiles)
        def _():
            op_ref[...] = y

        @pl.when(i >= prompt_tiles)
        def _():
            os_ref[...] = y

    @pl.when(i + 1 < nt)
    def _():
        norm_tile(True)

    @pl.when(i + 1 >= nt)
    def _():
        norm_tile(False)


def _combine_ln(x, prob, y_sorted, pos, g, b, mp):
    m = x.shape[0]
    ms = m - mp
    tm = _tile(np.gcd(mp, ms), 256)
    nt, npt = m // tm, mp // tm
    pos3 = pos.reshape(nt, 1, 2 * tm)
    return pl.pallas_call(
        functools.partial(_combine_kernel, prompt_tiles=npt),
        grid=(nt,),
        in_specs=[pl.BlockSpec((None, 1, 2 * tm), lambda i: (i, 0, 0), memory_space=pltpu.SMEM),
                  pl.BlockSpec((None, 1, 2 * tm), lambda i: (jnp.minimum(i + 1, nt - 1), 0, 0),
                               memory_space=pltpu.SMEM),
                  pl.BlockSpec((tm, D_MODEL), lambda i: (i, 0)),
                  pl.BlockSpec((tm, 2), lambda i: (i, 0)),
                  pl.BlockSpec(memory_space=pl.ANY),
                  pl.BlockSpec((1, D_MODEL), lambda i: (0, 0)),
                  pl.BlockSpec((1, D_MODEL), lambda i: (0, 0))],
        out_specs=[pl.BlockSpec((tm, D_MODEL), lambda i: (jnp.minimum(i, npt - 1), 0)),
                   pl.BlockSpec((tm, D_MODEL), lambda i: (jnp.maximum(i - npt, 0), 0))],
        out_shape=[jax.ShapeDtypeStruct((mp, D_MODEL), F32),
                   jax.ShapeDtypeStruct((ms, D_MODEL), F32)],
        scratch_shapes=[pltpu.VMEM((2, 2, tm, D_MODEL), F32), pltpu.SemaphoreType.DMA((2,))],
        compiler_params=_params("arbitrary"),
        name="combine_ln",
    )(pos3, pos3, x, prob, y_sorted, g, b)


def _moe_ln(x, mp, router_w, router_b, wg, wu, wd, g, b):
    m = x.shape[0]
    tm = MOE_ROWS
    n_tiles = -(-2 * m // tm) + N_EXPERTS
    idx, prob, rank, counts = _router(x, router_w, router_b)
    pos, src, tile_expert, tile_valid = _route_tables(idx, rank, counts[0], tm, n_tiles)
    y_sorted = _moe_group(x, src, tile_expert, tile_valid, wg, wu, wd, tm, n_tiles)
    return _combine_ln(x, prob, y_sorted, pos, g, b, mp)


def kernel(x_prompt, x_sample, state_conv_a, cache_swa_k, cache_swa_v, state_conv_c, w_in, b_gate, conv_a_w, conv_a_b, ln_a_g, ln_a_b, w_branch_a, sinks, w_branch_b, conv_c_w, w_branch_c, ln_d_g, ln_d_b, sgu_w, sgu_b, w_branch_d, w_out, ln1_g, ln1_b, ffn_w_gate, ffn_w_up, ffn_w_down, router_w, router_b, exp_w_gate, exp_w_up, exp_w_down, ln2_g, ln2_b):
    nb, t_len, d = x_prompt.shape
    nseq, s_len, _ = x_sample.shape
    depth = w_in.shape[0]
    mp, ms = nb * t_len, nseq * s_len
    w = WINDOW

    x = (x_prompt.reshape(mp, d), x_sample.reshape(ms, d))
    xb = _stack_cast(*x)

    cos_p, sin_p = _rope_tables(jnp.arange(t_len))
    cos_s, sin_s = _rope_tables(PAST_LEN + jnp.arange(s_len))

    def row(v):
        return v.reshape(1, -1)

    outs = {k: [] for k in ("pa", "pk", "pv", "pc", "sa", "sk", "sv", "sc", "sd")}
    for l in range(depth):
        w_in_b = w_in[l].astype(BF16)
        h = _in_proj(xb, w_in_b)

        wa, ba = conv_a_w[l], row(conv_a_b[l])
        lag, lab = row(ln_a_g[l]), row(ln_a_b[l])
        ya_p, yc_p, pa, pc = _ac_prompt(h, nb, t_len, wa, ba, lag, lab, conv_c_w[l])
        ya_s, yc_s, sa, sc = _ac_sample(h, mp, nseq, s_len, state_conv_a[l], state_conv_c[l],
                                        wa, ba, lag, lab, conv_c_w[l])

        yb_p, pk, pv = _attn_prompt(h, nb, t_len, sinks[l], cos_p, sin_p)
        yb_s, sk, sv = _attn_sample(h, mp, nseq, s_len, sinks[l],
                                    cache_swa_k[l].reshape(nseq, w, KV_WIDTH),
                                    cache_swa_v[l].reshape(nseq, w, KV_WIDTH), cos_s, sin_s)

        ldg, ldb = row(ln_d_g[l]), row(ln_d_b[l])
        yd_p = _d_prompt(h, mp, ldg, ldb, sgu_w[l], jnp.transpose(sgu_b[l]))
        sgu_wt = jnp.repeat(jnp.transpose(sgu_w[l][:, :s_len, :s_len], (2, 1, 0)), GROUP_D, axis=2)
        sgu_bs = jnp.repeat(jnp.transpose(sgu_b[l][:, :s_len]), GROUP_D, axis=1)
        yd_s, sd = _d_sample(h, mp, nseq, s_len, ldg, ldb, sgu_wt, sgu_bs)

        merged = _merge(xb, (ya_p, yb_p, yc_p, yd_p), (ya_s, yb_s, yc_s, yd_s), w_in_b, b_gate[l],
                        (w_branch_a[l].astype(BF16), w_branch_b[l].astype(BF16),
                         w_branch_c[l].astype(BF16), w_branch_d[l].astype(BF16)))
        x, xb = _out_ln(merged, x, w_out[l].astype(BF16), row(ln1_g[l]), row(ln1_b[l]))

        i = l // 2
        if l % 2 == 0:
            x, xb = _ffn_ln(x, xb, ffn_w_gate[i].astype(BF16), ffn_w_up[i].astype(BF16),
                            ffn_w_down[i].astype(BF16), row(ln2_g[l]), row(ln2_b[l]))
            x_p, x_s = x[:mp], x[mp:]
        else:
            x_p, x_s = _moe_ln(x, mp, router_w[i], row(router_b[i]), exp_w_gate[i].astype(BF16),
                               exp_w_up[i].astype(BF16), exp_w_down[i].astype(BF16),
                               row(ln2_g[l]), row(ln2_b[l]))
            if l + 1 < depth:
                x = jnp.concatenate([x_p, x_s], axis=0)
                xb = x.astype(BF16)

        outs["pa"].append(pa)
        outs["pk"].append(pk.reshape(nb, w, N_KV, HEAD_DIM))
        outs["pv"].append(pv.reshape(nb, w, N_KV, HEAD_DIM))
        outs["pc"].append(pc)
        outs["sa"].append(sa)
        outs["sk"].append(sk.reshape(nseq, w, N_KV, HEAD_DIM))
        outs["sv"].append(sv.reshape(nseq, w, N_KV, HEAD_DIM))
        outs["sc"].append(sc)
        outs["sd"].append(sd)

    y_prompt = x_p.reshape(nb, t_len, d)
    y_sample = x_s.reshape(nseq, s_len, d)
    st = {k: jnp.stack(v) for k, v in outs.items()}
    return (y_prompt, y_sample, st["pa"], st["pk"], st["pv"], st["pc"],
            st["sa"], st["sk"], st["sv"], st["sc"], st["sd"])
```

```python
import functools

import jax
import jax.numpy as jnp
import numpy as np
from jax import lax
from jax.experimental import pallas as pl
from jax.experimental.pallas import tpu as pltpu

D_MODEL = 2048
PAST_LEN = 8192
D_A = 512
CONV_A = 31
HEAD_DIM = 64
N_HEADS = 16
N_KV = 4
GQA = N_HEADS // N_KV
WINDOW = 128
ROPE_THETA = 10000.0
D_C = 512
CONV_C = 3
D_D = 512
CHUNK = 128
N_GROUPS_D = 4
GROUP_D = D_D // N_GROUPS_D
N_BRANCH = 4
Q_WIDTH = N_HEADS * HEAD_DIM
KV_WIDTH = N_KV * HEAD_DIM
D_FF = 5632
N_EXPERTS = 8
ALPHA = 4.0 ** 0.25
LN_EPS = 1e-5

OFF_A_VAL = 0
OFF_A_GATE = D_A
OFF_Q = 2 * D_A
OFF_K = OFF_Q + Q_WIDTH
OFF_V = OFF_K + KV_WIDTH
OFF_C_B = OFF_V + KV_WIDTH
OFF_C_C = OFF_C_B + D_C
OFF_C_X = OFF_C_C + D_C
OFF_D_U = OFF_C_X + D_C
OFF_D_V = OFF_D_U + D_D
OFF_GATES = OFF_D_V + D_D

LANES = 128
SUBLANES = 8
HALO = 32
VMEM_LIMIT = 48 * 1024 * 1024
OUT_LN_SUBTILES = 4
MOE_F_STEPS = 11
MOE_ROWS = 80 * (MOE_F_STEPS - 1)
W_BANDS = 2

BF16 = jnp.bfloat16
F32 = jnp.float32


def _tile(n, pref):
    n = int(n)
    if n <= pref:
        return n
    for t in range(pref, 7, -1):
        if n % t == 0 and t % 8 == 0:
            return t
    return n


def _params(*sem):
    return pltpu.CompilerParams(dimension_semantics=sem, vmem_limit_bytes=VMEM_LIMIT)


def _layer_norm(x, g, b):
    mu = jnp.mean(x, axis=-1, keepdims=True)
    xc = x - mu
    var = jnp.mean(xc * xc, axis=-1, keepdims=True)
    return xc * lax.rsqrt(var + LN_EPS) * g + b


def _silu(x):
    return x * jax.nn.sigmoid(x)


def _gelu(x):
    return jax.nn.gelu(x, approximate=True)


def _stack_cast_kernel(xp_ref, xs_ref, o_ref, *, prompt_tiles):
    is_prompt = pl.program_id(0) < prompt_tiles
    o_ref[...] = jnp.where(is_prompt, xp_ref[...], xs_ref[...]).astype(o_ref.dtype)


def _stack_cast(xp, xs):
    mp, ms = xp.shape[0], xs.shape[0]
    tm = _tile(np.gcd(mp, ms), 512)
    npt = mp // tm
    return pl.pallas_call(
        functools.partial(_stack_cast_kernel, prompt_tiles=npt),
        grid=((mp + ms) // tm,),
        in_specs=[pl.BlockSpec((tm, D_MODEL), lambda i: (jnp.minimum(i, npt - 1), 0)),
                  pl.BlockSpec((tm, D_MODEL), lambda i: (jnp.maximum(i - npt, 0), 0))],
        out_specs=pl.BlockSpec((tm, D_MODEL), lambda i: (i, 0)),
        out_shape=jax.ShapeDtypeStruct((mp + ms, D_MODEL), BF16),
        compiler_params=_params("parallel"),
        name="stack_cast",
    )(xp, xs)


def _mm_kernel(x_ref, w_ref, o_ref):
    o_ref[...] = jnp.dot(x_ref[...], w_ref[...], preferred_element_type=F32).astype(o_ref.dtype)


def _in_proj(xb, w_in_b):
    m, k = xb.shape
    n = OFF_GATES
    tm = _tile(m, 1024)
    tn = 1024
    return pl.pallas_call(
        _mm_kernel,
        grid=(m // tm, n // tn),
        in_specs=[pl.BlockSpec((tm, k), lambda i, j: (i, 0)),
                  pl.BlockSpec((k, tn), lambda i, j: (0, j))],
        out_specs=pl.BlockSpec((tm, tn), lambda i, j: (i, j)),
        out_shape=jax.ShapeDtypeStruct((m, n), F32),
        compiler_params=_params("parallel", "arbitrary"),
        name="in_proj",
    )(xb, w_in_b)


def _ac_prompt_kernel(av_ref, ag_ref, cb_ref, cc_ref, cx_ref,
                      avp_ref, agp_ref, ccp_ref, cxp_ref,
                      wa_ref, ba_ref, lg_ref, lb_ref, wc_ref,
                      ya_ref, yc_ref, ha_ref, hc_ref, ga_s, gc_s, sh_s):
    t = pl.program_id(1)
    tt = av_ref.shape[0]
    first = t == 0
    glu_prev = avp_ref[...] * jax.nn.sigmoid(agp_ref[...])
    ga_s[0:HALO, :] = jnp.where(first, 0.0, glu_prev)
    ga_s[HALO:HALO + tt, :] = av_ref[...] * jax.nn.sigmoid(ag_ref[...])
    acc = jnp.zeros((tt, D_A), F32) + ba_ref[...]
    base = HALO - (CONV_A - 1)
    for phase in range(SUBLANES):
        offs = [o for o in range(base, base + CONV_A) if o % SUBLANES == phase]
        span = offs[-1] - phase + tt
        sh_s[0:span, :] = ga_s[phase:phase + span, :]
        for o in offs:
            acc = acc + wa_ref[o - base:o - base + 1, :] * sh_s[o - phase:o - phase + tt, :]
    ya = _silu(_layer_norm(acc, lg_ref[...], lb_ref[...]))
    ya_ref[...] = ya.astype(ya_ref.dtype)
    ha_ref[0] = ga_s[HALO + tt - (CONV_A - 1):HALO + tt, :]

    gc_s[0:HALO, :] = jnp.where(first, 0.0, ccp_ref[...] * cxp_ref[...])
    gc_s[HALO:HALO + tt, :] = cc_ref[...] * cx_ref[...]
    base_c = HALO - (CONV_C - 1)
    yc = jnp.zeros((tt, D_C), F32)
    for j in range(CONV_C):
        yc = yc + wc_ref[j:j + 1, :] * gc_s[base_c + j:base_c + j + tt, :]
    yc_ref[...] = (cb_ref[...] * yc).astype(yc_ref.dtype)
    hc_ref[0] = gc_s[HALO + tt - (CONV_C - 1):HALO + tt, :]


def _ac_prompt(h, nb, t_len, wa, ba, lg, lb, wc):
    tt = _tile(t_len, 512)
    nt = t_len // tt
    cw = D_A

    def cur(col):
        return pl.BlockSpec((tt, cw), lambda b, t: (b * nt + t, col // cw))

    def prev(col):
        return pl.BlockSpec(
            (HALO, cw),
            lambda b, t: (jnp.maximum((b * t_len + t * tt) // HALO - 1, 0), col // cw))

    def full(shape):
        return pl.BlockSpec(shape, lambda b, t: (0,) * len(shape))

    mp = nb * t_len
    return pl.pallas_call(
        _ac_prompt_kernel,
        grid=(nb, nt),
        in_specs=[cur(OFF_A_VAL), cur(OFF_A_GATE), cur(OFF_C_B), cur(OFF_C_C), cur(OFF_C_X),
                  prev(OFF_A_VAL), prev(OFF_A_GATE), prev(OFF_C_C), prev(OFF_C_X),
                  full((CONV_A, D_A)), full((1, D_A)), full((1, D_A)), full((1, D_A)),
                  full((CONV_C, D_C))],
        out_specs=[pl.BlockSpec((tt, D_A), lambda b, t: (b * nt + t, 0)),
                   pl.BlockSpec((tt, D_C), lambda b, t: (b * nt + t, 0)),
                   pl.BlockSpec((1, CONV_A - 1, D_A), lambda b, t: (b, 0, 0)),
                   pl.BlockSpec((1, CONV_C - 1, D_C), lambda b, t: (b, 0, 0))],
        out_shape=[jax.ShapeDtypeStruct((mp, D_A), BF16),
                   jax.ShapeDtypeStruct((mp, D_C), BF16),
                   jax.ShapeDtypeStruct((nb, CONV_A - 1, D_A), F32),
                   jax.ShapeDtypeStruct((nb, CONV_C - 1, D_C), F32)],
        scratch_shapes=[pltpu.VMEM((HALO + tt, D_A), F32), pltpu.VMEM((HALO + tt, D_C), F32),
                        pltpu.VMEM((HALO + tt, D_A), F32)],
        compiler_params=_params("parallel", "arbitrary"),
        name="ac_prompt",
    )(h, h, h, h, h, h, h, h, h, wa, ba, lg, lb, wc)


def _ac_sample_kernel(av_ref, ag_ref, cb_ref, cc_ref, cx_ref, hista_ref, histc_ref,
                      wa_ref, ba_ref, lg_ref, lb_ref, wc_ref,
                      ya_ref, yc_ref, ha_ref, hc_ref, xa_s, xc_s):
    sb, s_len = hista_ref.shape[0], av_ref.shape[0] // hista_ref.shape[0]
    ka, kc = CONV_A - 1, CONV_C - 1
    glu = av_ref[...] * jax.nn.sigmoid(ag_ref[...])
    xa_s[:, 0:ka, :] = hista_ref[...]
    xa_s[:, ka:ka + s_len, :] = glu.reshape(sb, s_len, D_A)
    acc = jnp.zeros((sb, s_len, D_A), F32) + ba_ref[...][None]
    for j in range(CONV_A):
        acc = acc + wa_ref[j:j + 1, :][None] * xa_s[:, j:j + s_len, :]
    ya = _silu(_layer_norm(acc, lg_ref[...][None], lb_ref[...][None]))
    ya_ref[...] = ya.reshape(sb * s_len, D_A).astype(ya_ref.dtype)
    ha_ref[...] = xa_s[:, s_len:s_len + ka, :]

    xc_s[:, 0:kc, :] = histc_ref[...]
    xc_s[:, kc:kc + s_len, :] = (cc_ref[...] * cx_ref[...]).reshape(sb, s_len, D_C)
    yc = jnp.zeros((sb, s_len, D_C), F32)
    for j in range(CONV_C):
        yc = yc + wc_ref[j:j + 1, :][None] * xc_s[:, j:j + s_len, :]
    yc = cb_ref[...] * yc.reshape(sb * s_len, D_C)
    yc_ref[...] = yc.astype(yc_ref.dtype)
    hc_ref[...] = xc_s[:, s_len:s_len + kc, :]


def _ac_sample(h, row0, nseq, s_len, hist_a, hist_c, wa, ba, lg, lb, wc):
    sb = _tile(nseq, 16)
    rows = sb * s_len
    blk0 = row0 // rows
    cw = D_A

    def cur(col):
        return pl.BlockSpec((rows, cw), lambda i: (blk0 + i, col // cw))

    def full(shape):
        return pl.BlockSpec(shape, lambda i: (0,) * len(shape))

    ms = nseq * s_len
    ka, kc = CONV_A - 1, CONV_C - 1
    return pl.pallas_call(
        _ac_sample_kernel,
        grid=(nseq // sb,),
        in_specs=[cur(OFF_A_VAL), cur(OFF_A_GATE), cur(OFF_C_B), cur(OFF_C_C), cur(OFF_C_X),
                  pl.BlockSpec((sb, ka, D_A), lambda i: (i, 0, 0)),
                  pl.BlockSpec((sb, kc, D_C), lambda i: (i, 0, 0)),
                  full((CONV_A, D_A)), full((1, D_A)), full((1, D_A)), full((1, D_A)),
                  full((CONV_C, D_C))],
        out_specs=[pl.BlockSpec((rows, D_A), lambda i: (i, 0)),
                   pl.BlockSpec((rows, D_C), lambda i: (i, 0)),
                   pl.BlockSpec((sb, ka, D_A), lambda i: (i, 0, 0)),
                   pl.BlockSpec((sb, kc, D_C), lambda i: (i, 0, 0))],
        out_shape=[jax.ShapeDtypeStruct((ms, D_A), BF16),
                   jax.ShapeDtypeStruct((ms, D_C), BF16),
                   jax.ShapeDtypeStruct((nseq, ka, D_A), F32),
                   jax.ShapeDtypeStruct((nseq, kc, D_C), F32)],
        scratch_shapes=[pltpu.VMEM((sb, ka + s_len + 2, D_A), F32),
                        pltpu.VMEM((sb, kc + s_len + 6, D_C), F32)],
        compiler_params=_params("parallel"),
        name="ac_sample",
    )(h, h, h, h, h, hist_a, hist_c, wa, ba, lg, lb, wc)


def _rope_tables(pos):
    half = HEAD_DIM // 2
    inv_freq = jnp.power(ROPE_THETA, -jnp.arange(half, dtype=F32) * (2.0 / HEAD_DIM))
    ang = pos.astype(F32)[:, None] * inv_freq[None, :]
    cos, sin = jnp.cos(ang), jnp.sin(ang)
    cos_t = jnp.concatenate([cos, cos, cos, cos], axis=1)
    sin_t = jnp.concatenate([-sin, sin, -sin, sin], axis=1)
    return cos_t, sin_t


def _rope(x, cos_t, sin_t):
    half = HEAD_DIM // 2
    axis = x.ndim - 1
    shape = x.shape[:-1] + (LANES,)
    lane = lax.broadcasted_iota(jnp.int32, shape, axis)
    first = (lane % HEAD_DIM) < half
    out = []
    for i in range(x.shape[-1] // LANES):
        xi = x[..., LANES * i:LANES * (i + 1)]
        partner = jnp.where(first, pltpu.roll(xi, LANES - half, axis), pltpu.roll(xi, half, axis))
        out.append(xi * cos_t + partner * sin_t)
    return out


def _head_halves(t, axis, lo_valid):
    lane = lax.broadcasted_iota(jnp.int32, t.shape, axis)
    if lo_valid:
        lo = jnp.where(lane < HEAD_DIM, t, jnp.zeros_like(t))
        hi = pltpu.roll(lo, HEAD_DIM, axis)
    else:
        hi = jnp.where(lane >= HEAD_DIM, t, jnp.zeros_like(t))
        lo = pltpu.roll(hi, HEAD_DIM, axis)
    return lo, hi


def _swa_block(sinks_ref, q_tiles, kp_tiles, kc_tiles, vp, vc, has_prev, write):
    w = WINDOW
    ci = lax.broadcasted_iota(jnp.int32, (2 * w, 2 * w), 0)
    qi = lax.broadcasted_iota(jnp.int32, (2 * w, 2 * w), 1) % w
    mask = (ci > qi) & (ci <= qi + w) & (has_prev | (ci >= w))
    lane_first = lax.broadcasted_iota(jnp.int32, (1, 2 * w), 1) < w

    qt_tiles = [t.T.astype(BF16) for t in q_tiles]
    zeros_t = jnp.zeros((HEAD_DIM, 2 * w), BF16)

    for h in range(N_KV):
        tile, lo_valid = h // 2, (h % 2 == 0)
        lanes = slice(LANES * tile, LANES * (tile + 1))
        kcat = jnp.concatenate([kp_tiles[tile], kc_tiles[tile]], axis=0)
        k_lo, k_hi = _head_halves(kcat.astype(BF16), 1, lo_valid)
        vt = jnp.concatenate([vp[:, lanes], vc[:, lanes]], axis=0).T.astype(BF16)
        vt = vt[0:HEAD_DIM] if lo_valid else vt[HEAD_DIM:2 * HEAD_DIM]
        vt_halves = (jnp.concatenate([vt, zeros_t], axis=0), jnp.concatenate([zeros_t, vt], axis=0))
        qst = jnp.concatenate([qt_tiles[2 * h], qt_tiles[2 * h + 1]], axis=1)
        ot = None
        for half_idx, k_half in enumerate((k_lo, k_hi)):
            s = jnp.dot(k_half, qst, preferred_element_type=F32)
            s = jnp.where(mask, s, -jnp.inf)
            sk = jnp.where(lane_first, sinks_ref[4 * h + half_idx], sinks_ref[4 * h + 2 + half_idx])
            m = jnp.maximum(jnp.max(s, axis=0, keepdims=True), sk)
            e = jnp.exp(s - m)
            den = jnp.sum(e, axis=0, keepdims=True) + jnp.exp(sk - m)
            p = (e / den).astype(BF16)
            term = jnp.dot(vt_halves[half_idx], p, preferred_element_type=F32)
            ot = term if ot is None else ot + term
        write(2 * h, ot[:, 0:w].T)
        write(2 * h + 1, ot[:, w:2 * w].T)


def _attn_prompt_kernel(sinks_ref, q_ref, kc_ref, vc_ref, kp_ref, vp_ref,
                        cos_ref, sin_ref, cosp_ref, sinp_ref,
                        yb_ref, nk_ref, nv_ref):
    n = pl.program_id(1)
    w = WINDOW
    cos_c, sin_c = cos_ref[...], sin_ref[...]
    q_tiles = _rope(q_ref[...] * (HEAD_DIM ** -0.5), cos_c, sin_c)
    kc_tiles = _rope(kc_ref[...], cos_c, sin_c)
    kp_tiles = _rope(kp_ref[...], cosp_ref[...], sinp_ref[...])
    for i in range(KV_WIDTH // LANES):
        nk_ref[0, :, LANES * i:LANES * (i + 1)] = kc_tiles[i][w:2 * w]
    nv_ref[0] = vc_ref[w:2 * w, :]

    for j in range(2):
        rows = slice(j * w, (j + 1) * w)

        def write(tile, value, rows=rows):
            yb_ref[rows, LANES * tile:LANES * (tile + 1)] = value.astype(yb_ref.dtype)

        if j == 0:
            prev_k, prev_v, has_prev = kp_tiles, vp_ref[...], n > 0
        else:
            prev_k, prev_v, has_prev = [t[0:w] for t in kc_tiles], vc_ref[0:w, :], True
        _swa_block(sinks_ref, [t[rows] for t in q_tiles], prev_k, [t[rows] for t in kc_tiles],
                   prev_v, vc_ref[rows, :], has_prev, write)


def _attn_prompt(h, nb, t_len, sinks, cos_t, sin_t):
    w = WINDOW
    nblk = t_len // w
    assert nblk % 2 == 0
    npair = nblk // 2
    mp = nb * t_len

    def pair(b, n):
        return b * npair + n

    def halo(b, n):
        return b * nblk + jnp.maximum(2 * n - 1, 0)

    return pl.pallas_call(
        _attn_prompt_kernel,
        grid=(nb, npair),
        in_specs=[pl.BlockSpec(memory_space=pltpu.SMEM),
                  pl.BlockSpec((2 * w, Q_WIDTH), lambda b, n: (pair(b, n), OFF_Q // Q_WIDTH)),
                  pl.BlockSpec((2 * w, KV_WIDTH), lambda b, n: (pair(b, n), OFF_K // KV_WIDTH)),
                  pl.BlockSpec((2 * w, KV_WIDTH), lambda b, n: (pair(b, n), OFF_V // KV_WIDTH)),
                  pl.BlockSpec((w, KV_WIDTH), lambda b, n: (halo(b, n), OFF_K // KV_WIDTH)),
                  pl.BlockSpec((w, KV_WIDTH), lambda b, n: (halo(b, n), OFF_V // KV_WIDTH)),
                  pl.BlockSpec((2 * w, LANES), lambda b, n: (n, 0)),
                  pl.BlockSpec((2 * w, LANES), lambda b, n: (n, 0)),
                  pl.BlockSpec((w, LANES), lambda b, n: (jnp.maximum(2 * n - 1, 0), 0)),
                  pl.BlockSpec((w, LANES), lambda b, n: (jnp.maximum(2 * n - 1, 0), 0))],
        out_specs=[pl.BlockSpec((2 * w, Q_WIDTH), lambda b, n: (pair(b, n), 0)),
                   pl.BlockSpec((1, w, KV_WIDTH), lambda b, n: (b, 0, 0)),
                   pl.BlockSpec((1, w, KV_WIDTH), lambda b, n: (b, 0, 0))],
        out_shape=[jax.ShapeDtypeStruct((mp, Q_WIDTH), BF16),
                   jax.ShapeDtypeStruct((nb, w, KV_WIDTH), F32),
                   jax.ShapeDtypeStruct((nb, w, KV_WIDTH), F32)],
        compiler_params=_params("parallel", "arbitrary"),
        name="attn_prompt",
    )(sinks, h, h, h, h, h, cos_t, sin_t, cos_t, sin_t)


def _attn_sample_kernel(sinks_ref, q_ref, kn_ref, vn_ref, kbuf_ref, vbuf_ref, cos_ref, sin_ref,
                        yb_ref, nk_ref, nv_ref):
    sb, w = kbuf_ref.shape[0], kbuf_ref.shape[1]
    s_len = q_ref.shape[0] // sb
    cos_t, sin_t = cos_ref[...], sin_ref[...]
    q_tiles = _rope(q_ref[...] * (HEAD_DIM ** -0.5), cos_t, sin_t)
    kn_tiles = _rope(kn_ref[...], cos_t, sin_t)
    nk_ref[:, 0:w - s_len, :] = kbuf_ref[:, s_len:w, :]
    nv_ref[:, 0:w - s_len, :] = vbuf_ref[:, s_len:w, :]
    for i in range(KV_WIDTH // LANES):
        nk_ref[:, w - s_len:w, LANES * i:LANES * (i + 1)] = kn_tiles[i].reshape(sb, s_len, LANES)
    nv_ref[:, w - s_len:w, :] = vn_ref[...].reshape(sb, s_len, KV_WIDTH)

    qi = lax.broadcasted_iota(jnp.int32, (2 * s_len, w), 0) % s_len
    ci = lax.broadcasted_iota(jnp.int32, (2 * s_len, w), 1)
    mask_buf = (ci > qi)[None]
    qn = lax.broadcasted_iota(jnp.int32, (2 * s_len, s_len), 0) % s_len
    cn = lax.broadcasted_iota(jnp.int32, (2 * s_len, s_len), 1)
    mask_new = (cn <= qn)[None]
    row_top = (lax.broadcasted_iota(jnp.int32, (2 * s_len, 1), 0) < s_len)[None]
    bqk = (((2,), (2,)), ((0,), (0,)))
    bkd = (((2,), (1,)), ((0,), (0,)))

    for h in range(N_KV):
        tile, lo_valid = h // 2, (h % 2 == 0)
        sl = slice(LANES * tile, LANES * (tile + 1))
        kb_lo, kb_hi = _head_halves(kbuf_ref[:, :, sl].astype(BF16), 2, lo_valid)
        vb_lo, vb_hi = _head_halves(vbuf_ref[:, :, sl].astype(BF16), 2, lo_valid)
        kn3 = kn_tiles[tile].reshape(sb, s_len, LANES).astype(BF16)
        vn3 = vn_ref[:, sl].reshape(sb, s_len, LANES).astype(BF16)
        kn_lo, kn_hi = _head_halves(kn3, 2, lo_valid)
        vn_lo, vn_hi = _head_halves(vn3, 2, lo_valid)
        qs = jnp.concatenate([q_tiles[2 * h].reshape(sb, s_len, LANES),
                              q_tiles[2 * h + 1].reshape(sb, s_len, LANES)], axis=1).astype(BF16)
        o = jnp.zeros((sb, 2 * s_len, LANES), F32)
        for half_idx, (kb, kn, vb, vn) in enumerate(((kb_lo, kn_lo, vb_lo, vn_lo),
                                                     (kb_hi, kn_hi, vb_hi, vn_hi))):
            s_b = lax.dot_general(qs, kb, bqk, preferred_element_type=F32)
            s_n = lax.dot_general(qs, kn, bqk, preferred_element_type=F32)
            s_b = jnp.where(mask_buf, s_b, -jnp.inf)
            s_n = jnp.where(mask_new, s_n, -jnp.inf)
            sk = jnp.where(row_top, sinks_ref[4 * h + half_idx], sinks_ref[4 * h + 2 + half_idx])
            m = jnp.maximum(jnp.maximum(jnp.max(s_b, axis=2, keepdims=True),
                                        jnp.max(s_n, axis=2, keepdims=True)), sk)
            e_b = jnp.exp(s_b - m)
            e_n = jnp.exp(s_n - m)
            den = (jnp.sum(e_b, axis=2, keepdims=True) + jnp.sum(e_n, axis=2, keepdims=True)
                   + jnp.exp(sk - m))
            o = o + lax.dot_general((e_b / den).astype(BF16), vb, bkd, preferred_element_type=F32)
            o = o + lax.dot_general((e_n / den).astype(BF16), vn, bkd, preferred_element_type=F32)
        yb_ref[:, LANES * (2 * h):LANES * (2 * h + 1)] = (
            o[:, 0:s_len, :].reshape(sb * s_len, LANES).astype(yb_ref.dtype))
        yb_ref[:, LANES * (2 * h + 1):LANES * (2 * h + 2)] = (
            o[:, s_len:2 * s_len, :].reshape(sb * s_len, LANES).astype(yb_ref.dtype))


def _attn_sample(h, row0, nseq, s_len, sinks, k_buf, v_buf, cos_t, sin_t):
    sb = _tile(nseq, 8)
    rows = sb * s_len
    blk0 = row0 // rows
    w = k_buf.shape[1]
    ms = nseq * s_len
    cos_rows = jnp.tile(cos_t, (sb, 1))
    sin_rows = jnp.tile(sin_t, (sb, 1))
    return pl.pallas_call(
        _attn_sample_kernel,
        grid=(nseq // sb,),
        in_specs=[pl.BlockSpec(memory_space=pltpu.SMEM),
                  pl.BlockSpec((rows, Q_WIDTH), lambda i: (blk0 + i, OFF_Q // Q_WIDTH)),
                  pl.BlockSpec((rows, KV_WIDTH), lambda i: (blk0 + i, OFF_K // KV_WIDTH)),
                  pl.BlockSpec((rows, KV_WIDTH), lambda i: (blk0 + i, OFF_V // KV_WIDTH)),
                  pl.BlockSpec((sb, w, KV_WIDTH), lambda i: (i, 0, 0)),
                  pl.BlockSpec((sb, w, KV_WIDTH), lambda i: (i, 0, 0)),
                  pl.BlockSpec((rows, LANES), lambda i: (0, 0)),
                  pl.BlockSpec((rows, LANES), lambda i: (0, 0))],
        out_specs=[pl.BlockSpec((rows, Q_WIDTH), lambda i: (i, 0)),
                   pl.BlockSpec((sb, w, KV_WIDTH), lambda i: (i, 0, 0)),
                   pl.BlockSpec((sb, w, KV_WIDTH), lambda i: (i, 0, 0))],
        out_shape=[jax.ShapeDtypeStruct((ms, Q_WIDTH), BF16),
                   jax.ShapeDtypeStruct((nseq, w, KV_WIDTH), F32),
                   jax.ShapeDtypeStruct((nseq, w, KV_WIDTH), F32)],
        compiler_params=_params("parallel"),
        name="attn_sample",
    )(sinks, h, h, h, k_buf, v_buf, cos_rows, sin_rows)


def _d_prompt_kernel(du_ref, dv_ref, lg_ref, lb_ref, w_ref, bt_ref, yd_ref):
    c = CHUNK
    ri = lax.broadcasted_iota(jnp.int32, (c, c), 0)
    cj = lax.broadcasted_iota(jnp.int32, (c, c), 1)
    causal = cj <= ri
    w_causal = [jnp.where(causal, w_ref[g], 0.0).astype(BF16) for g in range(N_GROUPS_D)]
    for k in range(du_ref.shape[0] // c):
        rows = slice(k * c, (k + 1) * c)
        u = _gelu(du_ref[rows, :])
        vn = _layer_norm(_gelu(dv_ref[rows, :]), lg_ref[...], lb_ref[...]).astype(BF16)
        for g in range(N_GROUPS_D):
            sl = slice(GROUP_D * g, GROUP_D * (g + 1))
            s = jnp.dot(w_causal[g], vn[:, sl], preferred_element_type=F32) + bt_ref[:, g:g + 1]
            yd_ref[rows, sl] = (u[:, sl] * s).astype(yd_ref.dtype)


def _d_prompt(h, mp, ln_g, ln_b, sgu_w, sgu_bt):
    c = CHUNK
    rows = _tile(mp, 4 * c)
    assert rows % c == 0
    return pl.pallas_call(
        _d_prompt_kernel,
        grid=(mp // rows,),
        in_specs=[pl.BlockSpec((rows, D_D), lambda i: (i, OFF_D_U // D_D)),
                  pl.BlockSpec((rows, D_D), lambda i: (i, OFF_D_V // D_D)),
                  pl.BlockSpec((1, D_D), lambda i: (0, 0)),
                  pl.BlockSpec((1, D_D), lambda i: (0, 0)),
                  pl.BlockSpec((N_GROUPS_D, c, c), lambda i: (0, 0, 0)),
                  pl.BlockSpec((c, N_GROUPS_D), lambda i: (0, 0))],
        out_specs=pl.BlockSpec((rows, D_D), lambda i: (i, 0)),
        out_shape=jax.ShapeDtypeStruct((mp, D_D), BF16),
        compiler_params=_params("parallel"),
        name="d_prompt",
    )(h, h, ln_g, ln_b, sgu_w, sgu_bt)


def _d_sample_kernel(du_ref, dv_ref, lg_ref, lb_ref, wt_ref, bt_ref, yd_ref, vd_ref):
    sb, s_len = vd_ref.shape[0], vd_ref.shape[1]
    u = _gelu(du_ref[...])
    vn = _layer_norm(_gelu(dv_ref[...]), lg_ref[...], lb_ref[...])
    vn3 = vn.reshape(sb, s_len, D_D)
    vd_ref[...] = vn3
    ii = lax.broadcasted_iota(jnp.int32, (s_len, D_D), 0)
    s = jnp.zeros((sb, s_len, D_D), F32) + bt_ref[...][None]
    for j in range(s_len):
        wj = jnp.where(ii >= j, wt_ref[j], 0.0)
        s = s + wj[None] * vn3[:, j:j + 1, :]
    yd_ref[...] = (u * s.reshape(sb * s_len, D_D)).astype(yd_ref.dtype)


def _d_sample(h, row0, nseq, s_len, ln_g, ln_b, sgu_wt, sgu_bs):
    sb = _tile(nseq, 16)
    rows = sb * s_len
    blk0 = row0 // rows
    ms = nseq * s_len
    return pl.pallas_call(
        _d_sample_kernel,
        grid=(nseq // sb,),
        in_specs=[pl.BlockSpec((rows, D_D), lambda i: (blk0 + i, OFF_D_U // D_D)),
                  pl.BlockSpec((rows, D_D), lambda i: (blk0 + i, OFF_D_V // D_D)),
                  pl.BlockSpec((1, D_D), lambda i: (0, 0)),
                  pl.BlockSpec((1, D_D), lambda i: (0, 0)),
                  pl.BlockSpec((s_len, s_len, D_D), lambda i: (0, 0, 0)),
                  pl.BlockSpec((s_len, D_D), lambda i: (0, 0))],
        out_specs=[pl.BlockSpec((rows, D_D), lambda i: (i, 0)),
                   pl.BlockSpec((sb, s_len, D_D), lambda i: (i, 0, 0))],
        out_shape=[jax.ShapeDtypeStruct((ms, D_D), BF16),
                   jax.ShapeDtypeStruct((nseq, s_len, D_D), F32)],
        compiler_params=_params("parallel"),
        name="d_sample",
    )(h, h, ln_g, ln_b, sgu_wt, sgu_bs)


def _merge_kernel(x_ref, yap_ref, ybp_ref, ycp_ref, ydp_ref, yas_ref, ybs_ref, ycs_ref, yds_ref,
                  g0_ref, g1_ref, g2_ref, g3_ref, bg_ref, pa_ref, pb_ref, pc_ref, pd_ref, o_ref,
                  *, prompt_tiles):
    x = x_ref[...]
    is_prompt = pl.program_id(1) < prompt_tiles
    acc = None
    branches = ((yap_ref, yas_ref, g0_ref, pa_ref), (ybp_ref, ybs_ref, g1_ref, pb_ref),
                (ycp_ref, ycs_ref, g2_ref, pc_ref), (ydp_ref, yds_ref, g3_ref, pd_ref))
    for i, (yp_ref, ys_ref, g_ref, p_ref) in enumerate(branches):
        y = jnp.where(is_prompt, yp_ref[...], ys_ref[...])
        gate = jax.nn.sigmoid(jnp.dot(x, g_ref[...], preferred_element_type=F32) + bg_ref[i:i + 1, :])
        term = gate * jnp.dot(y, p_ref[...], preferred_element_type=F32)
        acc = term if acc is None else acc + term
    o_ref[...] = acc.astype(o_ref.dtype)


def _merge(xb, y_prompt, y_sample, w_in_b, b_gate, projs):
    m = xb.shape[0]
    mp, ms = y_prompt[0].shape[0], y_sample[0].shape[0]
    tm = _tile(np.gcd(mp, ms), 512)
    npt = mp // tm
    tn = 512
    nn = D_MODEL // tn
    widths = (D_A, Q_WIDTH, D_C, D_D)

    def act_p(width):
        return pl.BlockSpec((tm, width), lambda j, i: (jnp.minimum(i, npt - 1), 0))

    def act_s(width):
        return pl.BlockSpec((tm, width), lambda j, i: (jnp.maximum(i - npt, 0), 0))

    def gate_w(b):
        return pl.BlockSpec((D_MODEL, tn), lambda j, i: (0, (OFF_GATES + b * D_MODEL) // tn + j))

    def proj_w(k):
        return pl.BlockSpec((k, tn), lambda j, i: (0, j))

    return pl.pallas_call(
        functools.partial(_merge_kernel, prompt_tiles=npt),
        grid=(nn, m // tm),
        in_specs=[pl.BlockSpec((tm, D_MODEL), lambda j, i: (i, 0))]
                 + [act_p(wd) for wd in widths] + [act_s(wd) for wd in widths]
                 + [gate_w(0), gate_w(1), gate_w(2), gate_w(3),
                    pl.BlockSpec((N_BRANCH, tn), lambda j, i: (0, j))]
                 + [proj_w(wd) for wd in widths],
        out_specs=pl.BlockSpec((tm, tn), lambda j, i: (i, j)),
        out_shape=jax.ShapeDtypeStruct((m, D_MODEL), BF16),
        compiler_params=_params("parallel", "arbitrary"),
        name="merge",
    )(xb, *y_prompt, *y_sample, w_in_b, w_in_b, w_in_b, w_in_b, b_gate, *projs)


def _out_ln_kernel(mg_ref, *refs, prompt_tiles):
    if prompt_tiles is None:
        x_ref, w_ref, g_ref, b_ref, o_ref, ob_ref = refs
    else:
        xp_ref, xs_ref, w_ref, g_ref, b_ref, o_ref, ob_ref = refs
        is_prompt = pl.program_id(0) < prompt_tiles
    tm = mg_ref.shape[0]
    sub = tm // OUT_LN_SUBTILES
    for k in range(OUT_LN_SUBTILES):
        rows = slice(k * sub, (k + 1) * sub)
        if prompt_tiles is None:
            x = x_ref[rows, :]
        else:
            x = jnp.where(is_prompt, xp_ref[rows, :], xs_ref[rows, :])
        y = ALPHA * x + jnp.dot(mg_ref[rows, :], w_ref[...], preferred_element_type=F32)
        y = _layer_norm(y, g_ref[...], b_ref[...])
        o_ref[rows, :] = y
        ob_ref[rows, :] = y.astype(ob_ref.dtype)


def _out_ln(merged, x, w_out_b, g, b):
    m = merged.shape[0]
    if isinstance(x, tuple):
        mp, ms = x[0].shape[0], x[1].shape[0]
        tm = _tile(np.gcd(mp, ms), 512)
        npt = mp // tm
        x_specs = [pl.BlockSpec((tm, D_MODEL), lambda i: (jnp.minimum(i, npt - 1), 0)),
                   pl.BlockSpec((tm, D_MODEL), lambda i: (jnp.maximum(i - npt, 0), 0))]
    else:
        tm = _tile(m, 512)
        npt = None
        x, x_specs = (x,), [pl.BlockSpec((tm, D_MODEL), lambda i: (i, 0))]
    return pl.pallas_call(
        functools.partial(_out_ln_kernel, prompt_tiles=npt),
        grid=(m // tm,),
        in_specs=[pl.BlockSpec((tm, D_MODEL), lambda i: (i, 0))] + x_specs
                 + [pl.BlockSpec((D_MODEL, D_MODEL), lambda i: (0, 0), pipeline_mode=pl.Buffered(1)),
                    pl.BlockSpec((1, D_MODEL), lambda i: (0, 0)),
                    pl.BlockSpec((1, D_MODEL), lambda i: (0, 0))],
        out_specs=[pl.BlockSpec((tm, D_MODEL), lambda i: (i, 0)),
                   pl.BlockSpec((tm, D_MODEL), lambda i: (i, 0))],
        out_shape=[jax.ShapeDtypeStruct((m, D_MODEL), F32),
                   jax.ShapeDtypeStruct((m, D_MODEL), BF16)],
        compiler_params=_params("parallel"),
        name="out_ln",
    )(merged, *x, w_out_b, g, b)


def _swiglu_partial(xb, wg_refs, wu_refs, wd_refs):
    kb = xb.shape[1] // W_BANDS
    gate = up = None
    for j in range(W_BANDS):
        xs = xb[:, j * kb:(j + 1) * kb]
        gj = jnp.dot(xs, wg_refs[j][...], preferred_element_type=F32)
        uj = jnp.dot(xs, wu_refs[j][...], preferred_element_type=F32)
        gate, up = (gj, uj) if gate is None else (gate + gj, up + uj)
    hid = (_silu(gate) * up).astype(BF16)
    fb = hid.shape[1] // W_BANDS
    down = None
    for j in range(W_BANDS):
        dj = jnp.dot(hid[:, j * fb:(j + 1) * fb], wd_refs[j][...], preferred_element_type=F32)
        down = dj if down is None else down + dj
    return down


def _band_specs(lead, tf, col_index, row_index):
    kb, fb = D_MODEL // W_BANDS, tf // W_BANDS

    def col(j):
        def index(*args):
            *head, f = col_index(*args)
            return (*head, j, f)
        return pl.BlockSpec((*lead, kb, tf), index)

    def row(j):
        def index(*args):
            *head, f = row_index(*args)
            return (*head, W_BANDS * f + j, 0)
        return pl.BlockSpec((*lead, fb, D_MODEL), index)

    return ([col(j) for j in range(W_BANDS)] + [col(j) for j in range(W_BANDS)]
            + [row(j) for j in range(W_BANDS)])


def _ffn_kernel(x_ref, xb_ref, *refs):
    wg_refs, wu_refs, wd_refs = (refs[0:W_BANDS], refs[W_BANDS:2 * W_BANDS],
                                 refs[2 * W_BANDS:3 * W_BANDS])
    g_ref, b_ref, o_ref, ob_ref, acc_s = refs[3 * W_BANDS:]
    f = pl.program_id(1)
    last = pl.num_programs(1) - 1

    def partial_down():
        return _swiglu_partial(xb_ref[...], wg_refs, wu_refs, wd_refs)

    @pl.when(f == 0)
    def _():
        acc_s[...] = partial_down()

    @pl.when((f > 0) & (f < last))
    def _():
        acc_s[...] += partial_down()

    @pl.when(f == last)
    def _():
        acc_s[...] += partial_down()
        sub = acc_s.shape[0] // OUT_LN_SUBTILES
        for k in range(OUT_LN_SUBTILES):
            rows = slice(k * sub, (k + 1) * sub)
            y = _layer_norm(ALPHA * x_ref[rows, :] + acc_s[rows, :], g_ref[...], b_ref[...])
            o_ref[rows, :] = y
            ob_ref[rows, :] = y.astype(ob_ref.dtype)


def _ffn_ln(x, xb, wg, wu, wd, g, b):
    m = x.shape[0]
    tm = _tile(m, 512)
    tf = 512
    return pl.pallas_call(
        _ffn_kernel,
        grid=(m // tm, D_FF // tf),
        in_specs=[pl.BlockSpec((tm, D_MODEL), lambda i, f: (i, 0)),
                  pl.BlockSpec((tm, D_MODEL), lambda i, f: (i, 0))]
                 + _band_specs((), tf, lambda i, f: (f,), lambda i, f: (f,))
                 + [pl.BlockSpec((1, D_MODEL), lambda i, f: (0, 0)),
                    pl.BlockSpec((1, D_MODEL), lambda i, f: (0, 0))],
        out_specs=[pl.BlockSpec((tm, D_MODEL), lambda i, f: (i, 0)),
                   pl.BlockSpec((tm, D_MODEL), lambda i, f: (i, 0))],
        out_shape=[jax.ShapeDtypeStruct((m, D_MODEL), F32),
                   jax.ShapeDtypeStruct((m, D_MODEL), BF16)],
        scratch_shapes=[pltpu.VMEM((tm, D_MODEL), F32)],
        compiler_params=_params("parallel", "arbitrary"),
        name="ffn_ln",
    )(x, xb, *([wg] * W_BANDS), *([wu] * W_BANDS), *([wd] * W_BANDS), g, b)


def _router_kernel(x_ref, w_ref, b_ref, idx_ref, prob_ref, rank_ref, cnt_ref, run_s):
    @pl.when(pl.program_id(0) == 0)
    def _():
        run_s[...] = jnp.zeros_like(run_s)

    x, w = x_ref[...], w_ref[...]
    x_hi, w_hi = x.astype(BF16), w.astype(BF16)
    x_lo = (x - x_hi.astype(F32)).astype(BF16)
    w_lo = (w - w_hi.astype(F32)).astype(BF16)
    logits = (jnp.dot(x_hi, w_hi, preferred_element_type=F32)
              + (jnp.dot(x_lo, w_hi, preferred_element_type=F32)
                 + jnp.dot(x_hi, w_lo, preferred_element_type=F32))) + b_ref[...]
    tm = logits.shape[0]
    idx = lax.broadcasted_iota(jnp.int32, logits.shape, 1)
    v1 = jnp.max(logits, axis=1, keepdims=True)
    i1 = jnp.min(jnp.where(logits == v1, idx, N_EXPERTS), axis=1, keepdims=True)
    rest = jnp.where(idx == i1, -jnp.inf, logits)
    v2 = jnp.max(rest, axis=1, keepdims=True)
    i2 = jnp.min(jnp.where(rest == v2, idx, N_EXPERTS), axis=1, keepdims=True)
    e2 = jnp.exp(v2 - v1)
    den = 1.0 + e2

    hit = ((idx == i1) | (idx == i2)).astype(F32)
    ri = lax.broadcasted_iota(jnp.int32, (tm, tm), 0)
    ci = lax.broadcasted_iota(jnp.int32, (tm, tm), 1)
    before = jnp.dot((ci < ri).astype(BF16), hit.astype(BF16), preferred_element_type=F32) + run_s[...]
    rank1 = jnp.sum(jnp.where(idx == i1, before, 0.0), axis=1, keepdims=True)
    rank2 = jnp.sum(jnp.where(idx == i2, before, 0.0), axis=1, keepdims=True)

    col = lax.broadcasted_iota(jnp.int32, (tm, 2), 1)
    idx_ref[...] = jnp.where(col == 0, i1, i2)
    prob_ref[...] = jnp.where(col == 0, 1.0 / den, e2 / den)
    rank_ref[...] = jnp.where(col == 0, rank1, rank2).astype(jnp.int32)
    total = run_s[...] + jnp.sum(hit, axis=0, keepdims=True)
    run_s[...] = total
    cnt_ref[...] = total.astype(jnp.int32)


def _router(x, router_w, router_b):
    m = x.shape[0]
    tm = _tile(m, 512)
    return pl.pallas_call(
        _router_kernel,
        grid=(m // tm,),
        in_specs=[pl.BlockSpec((tm, D_MODEL), lambda i: (i, 0)),
                  pl.BlockSpec((D_MODEL, N_EXPERTS), lambda i: (0, 0)),
                  pl.BlockSpec((1, N_EXPERTS), lambda i: (0, 0))],
        out_specs=[pl.BlockSpec((tm, 2), lambda i: (i, 0)),
                   pl.BlockSpec((tm, 2), lambda i: (i, 0)),
                   pl.BlockSpec((tm, 2), lambda i: (i, 0)),
                   pl.BlockSpec((1, N_EXPERTS), lambda i: (0, 0))],
        out_shape=[jax.ShapeDtypeStruct((m, 2), jnp.int32),
                   jax.ShapeDtypeStruct((m, 2), F32),
                   jax.ShapeDtypeStruct((m, 2), jnp.int32),
                   jax.ShapeDtypeStruct((1, N_EXPERTS), jnp.int32)],
        scratch_shapes=[pltpu.VMEM((1, N_EXPERTS), F32)],
        compiler_params=_params("arbitrary"),
        name="router",
    )(x, router_w, router_b)


def _route_tables(idx, rank, counts, tm, n_tiles):
    m = idx.shape[0]
    padded = ((counts + tm - 1) // tm) * tm
    ends = jnp.cumsum(padded)
    base = ends - padded
    pos = base[idx] + rank
    n_valid = ends[-1] // tm
    tiles = jnp.arange(n_tiles, dtype=jnp.int32)
    tile_valid = (tiles < n_valid).astype(jnp.int32)
    first_row = jnp.minimum(tiles, n_valid - 1) * tm
    owner = jnp.sum((ends[None, :] <= first_row[:, None]).astype(jnp.int32), axis=1)
    tile_expert = jnp.minimum(owner, N_EXPERTS - 1).astype(jnp.int32)
    flat = pos.reshape(-1)
    token = jnp.repeat(jnp.arange(m, dtype=jnp.int32), 2)
    src = jnp.zeros((n_tiles * tm,), jnp.int32).at[flat].set(token, unique_indices=True)
    return pos, src, tile_expert, tile_valid


def _row_copy(src_hbm, row, dst_vmem, dst_row, sem):
    return pltpu.make_async_copy(src_hbm.at[pl.ds(row, 1), :], dst_vmem.at[pl.ds(dst_row, 1), :], sem)


def _moe_group_kernel(te_ref, tv_ref, src_ref, src_next_ref, x_hbm, *refs):
    wg_refs, wu_refs, wd_refs = (refs[0:W_BANDS], refs[W_BANDS:2 * W_BANDS],
                                 refs[2 * W_BANDS:3 * W_BANDS])
    o_ref, xg_s, xb_s, sem = refs[3 * W_BANDS:]
    i, f = pl.program_id(0), pl.program_id(1)
    nf = pl.num_programs(1)
    tm = xb_s.shape[0]
    chunk = tm // (MOE_F_STEPS - 1)
    slot = i % 2
    valid = tv_ref[i] == 1

    @pl.when((i == 0) & (f == 0))
    def _():
        def body(r, c):
            _row_copy(x_hbm, src_ref[0, r], xg_s.at[0], r, sem.at[0]).start()
            return c
        lax.fori_loop(0, tm, body, 0, unroll=8)

    def wait_rows():
        pltpu.make_async_copy(xg_s.at[slot], xg_s.at[slot], sem.at[slot]).wait()

    @pl.when((f == 0) & jnp.logical_not(valid))
    def _():
        o_ref[...] = jnp.zeros_like(o_ref)

        @pl.when((i == 0) | (tv_ref[jnp.maximum(i - 1, 0)] == 1))
        def _():
            wait_rows()

    def swiglu_step(first, issue_next):
        if first:
            wait_rows()
            xb = xg_s[slot].astype(BF16)
            xb_s[...] = xb
        else:
            xb = xb_s[...]
        if issue_next:
            for r in range(chunk):
                row = f * chunk + r
                _row_copy(x_hbm, src_next_ref[0, row], xg_s.at[1 - slot], row, sem.at[1 - slot]).start()
        down = _swiglu_partial(xb, wg_refs, wu_refs, wd_refs)
        if first:
            o_ref[...] = down
        else:
            o_ref[...] += down

    @pl.when(valid & (f == 0))
    def _():
        swiglu_step(True, True)

    @pl.when(valid & (f > 0) & (f < nf - 1))
    def _():
        swiglu_step(False, True)

    @pl.when(valid & (f == nf - 1))
    def _():
        swiglu_step(False, False)


def _moe_group(x, src, tile_expert, tile_valid, wg, wu, wd, tm, n_tiles):
    tf = D_FF // MOE_F_STEPS
    nf = MOE_F_STEPS

    def w_tile(i, f, te, tv):
        return (te[i], jnp.where(tv[i] == 1, f, nf - 1))

    grid_spec = pltpu.PrefetchScalarGridSpec(
        num_scalar_prefetch=2,
        grid=(n_tiles, nf),
        in_specs=[pl.BlockSpec((None, 1, tm), lambda i, f, te, tv: (i, 0, 0), memory_space=pltpu.SMEM),
                  pl.BlockSpec((None, 1, tm), lambda i, f, te, tv: (jnp.minimum(i + 1, n_tiles - 1), 0, 0),
                               memory_space=pltpu.SMEM),
                  pl.BlockSpec(memory_space=pl.ANY)]
                 + _band_specs((None,), tf, w_tile, w_tile),
        out_specs=pl.BlockSpec((tm, D_MODEL), lambda i, f, te, tv: (i, 0)),
        scratch_shapes=[pltpu.VMEM((2, tm, D_MODEL), F32),
                        pltpu.VMEM((tm, D_MODEL), BF16),
                        pltpu.SemaphoreType.DMA((2,))],
    )
    src3 = src.reshape(n_tiles, 1, tm)
    return pl.pallas_call(
        _moe_group_kernel,
        grid_spec=grid_spec,
        out_shape=jax.ShapeDtypeStruct((n_tiles * tm, D_MODEL), F32),
        compiler_params=_params("arbitrary", "arbitrary"),
        name="moe_group",
    )(tile_expert, tile_valid, src3, src3, x, *([wg] * W_BANDS), *([wu] * W_BANDS), *([wd] * W_BANDS))

def _combine_kernel(pos_ref, pos_next_ref, x_ref, p_ref, y_hbm, g_ref, b_ref, op_ref, os_ref, ybuf, sem,
                    *, prompt_tiles):
    i, nt = pl.program_id(0), pl.num_programs(0)
    tm = x_ref.shape[0]
    slot = i % 2

    def start_pair(idx_ref, s, r):
        _row_copy(y_hbm, idx_ref[0, 2 * r], ybuf.at[s, 0], r, sem.at[s]).start()
        _row_copy(y_hbm, idx_ref[0, 2 * r + 1], ybuf.at[s, 1], r, sem.at[s]).start()

    @pl.when(i == 0)
    def _():
        def body(r, c):
            start_pair(pos_ref, 0, r)
            return c
        lax.fori_loop(0, tm, body, 0, unroll=4)

    pltpu.make_async_copy(ybuf.at[slot], ybuf.at[slot], sem.at[slot]).wait()

    def norm_tile(issue_next):
        if issue_next:
            for r in range(tm):
                start_pair(pos_next_ref, 1 - slot, r)
        moe = p_ref[:, 0:1] * ybuf[slot, 0] + p_ref[:, 1:2] * ybuf[slot, 1]
        y = _layer_norm(ALPHA * x_ref[...] + moe, g_ref[...], b_ref[...])

        @pl.when(i < prompt_tiles)
        def _():
            op_ref[...] = y

        @pl.when(i >= prompt_tiles)
        def _():
            os_ref[...] = y

    @pl.when(i + 1 < nt)
    def _():
        norm_tile(True)

    @pl.when(i + 1 >= nt)
    def _():
        norm_tile(False)


def _combine_ln(x, prob, y_sorted, pos, g, b, mp):
    m = x.shape[0]
    ms = m - mp
    tm = _tile(np.gcd(mp, ms), 256)
    nt, npt = m // tm, mp // tm
    pos3 = pos.reshape(nt, 1, 2 * tm)
    return pl.pallas_call(
        functools.partial(_combine_kernel, prompt_tiles=npt),
        grid=(nt,),
        in_specs=[pl.BlockSpec((None, 1, 2 * tm), lambda i: (i, 0, 0), memory_space=pltpu.SMEM),
                  pl.BlockSpec((None, 1, 2 * tm), lambda i: (jnp.minimum(i + 1, nt - 1), 0, 0),
                               memory_space=pltpu.SMEM),
                  pl.BlockSpec((tm, D_MODEL), lambda i: (i, 0)),
                  pl.BlockSpec((tm, 2), lambda i: (i, 0)),
                  pl.BlockSpec(memory_space=pl.ANY),
                  pl.BlockSpec((1, D_MODEL), lambda i: (0, 0)),
                  pl.BlockSpec((1, D_MODEL), lambda i: (0, 0))],
        out_specs=[pl.BlockSpec((tm, D_MODEL), lambda i: (jnp.minimum(i, npt - 1), 0)),
                   pl.BlockSpec((tm, D_MODEL), lambda i: (jnp.maximum(i - npt, 0), 0))],
        out_shape=[jax.ShapeDtypeStruct((mp, D_MODEL), F32),
                   jax.ShapeDtypeStruct((ms, D_MODEL), F32)],
        scratch_shapes=[pltpu.VMEM((2, 2, tm, D_MODEL), F32), pltpu.SemaphoreType.DMA((2,))],
        compiler_params=_params("arbitrary"),
        name="combine_ln",
    )(pos3, pos3, x, prob, y_sorted, g, b)


def _moe_ln(x, mp, router_w, router_b, wg, wu, wd, g, b):
    m = x.shape[0]
    tm = MOE_ROWS
    n_tiles = -(-2 * m // tm) + N_EXPERTS
    idx, prob, rank, counts = _router(x, router_w, router_b)
    pos, src, tile_expert, tile_valid = _route_tables(idx, rank, counts[0], tm, n_tiles)
    y_sorted = _moe_group(x, src, tile_expert, tile_valid, wg, wu, wd, tm, n_tiles)
    return _combine_ln(x, prob, y_sorted, pos, g, b, mp)


def kernel(x_prompt, x_sample, state_conv_a, cache_swa_k, cache_swa_v, state_conv_c, w_in, b_gate, conv_a_w, conv_a_b, ln_a_g, ln_a_b, w_branch_a, sinks, w_branch_b, conv_c_w, w_branch_c, ln_d_g, ln_d_b, sgu_w, sgu_b, w_branch_d, w_out, ln1_g, ln1_b, ffn_w_gate, ffn_w_up, ffn_w_down, router_w, router_b, exp_w_gate, exp_w_up, exp_w_down, ln2_g, ln2_b):
    nb, t_len, d = x_prompt.shape
    nseq, s_len, _ = x_sample.shape
    depth = w_in.shape[0]
    mp, ms = nb * t_len, nseq * s_len
    w = WINDOW

    x = (x_prompt.reshape(mp, d), x_sample.reshape(ms, d))
    xb = _stack_cast(*x)

    cos_p, sin_p = _rope_tables(jnp.arange(t_len))
    cos_s, sin_s = _rope_tables(PAST_LEN + jnp.arange(s_len))

    def row(v):
        return v.reshape(1, -1)

    outs = {k: [] for k in ("pa", "pk", "pv", "pc", "sa", "sk", "sv", "sc", "sd")}
    for l in range(depth):
        w_in_b = w_in[l].astype(BF16)
        h = _in_proj(xb, w_in_b)

        wa, ba = conv_a_w[l], row(conv_a_b[l])
        lag, lab = row(ln_a_g[l]), row(ln_a_b[l])
        ya_p, yc_p, pa, pc = _ac_prompt(h, nb, t_len, wa, ba, lag, lab, conv_c_w[l])
        ya_s, yc_s, sa, sc = _ac_sample(h, mp, nseq, s_len, state_conv_a[l], state_conv_c[l],
                                        wa, ba, lag, lab, conv_c_w[l])

        yb_p, pk, pv = _attn_prompt(h, nb, t_len, sinks[l], cos_p, sin_p)
        yb_s, sk, sv = _attn_sample(h, mp, nseq, s_len, sinks[l],
                                    cache_swa_k[l].reshape(nseq, w, KV_WIDTH),
                                    cache_swa_v[l].reshape(nseq, w, KV_WIDTH), cos_s, sin_s)

        ldg, ldb = row(ln_d_g[l]), row(ln_d_b[l])
        yd_p = _d_prompt(h, mp, ldg, ldb, sgu_w[l], jnp.transpose(sgu_b[l]))
        sgu_wt = jnp.repeat(jnp.transpose(sgu_w[l][:, :s_len, :s_len], (2, 1, 0)), GROUP_D, axis=2)
        sgu_bs = jnp.repeat(jnp.transpose(sgu_b[l][:, :s_len]), GROUP_D, axis=1)
        yd_s, sd = _d_sample(h, mp, nseq, s_len, ldg, ldb, sgu_wt, sgu_bs)

        merged = _merge(xb, (ya_p, yb_p, yc_p, yd_p), (ya_s, yb_s, yc_s, yd_s), w_in_b, b_gate[l],
                        (w_branch_a[l].astype(BF16), w_branch_b[l].astype(BF16),
                         w_branch_c[l].astype(BF16), w_branch_d[l].astype(BF16)))
        x, xb = _out_ln(merged, x, w_out[l].astype(BF16), row(ln1_g[l]), row(ln1_b[l]))

        i = l // 2
        if l % 2 == 0:
            x, xb = _ffn_ln(x, xb, ffn_w_gate[i].astype(BF16), ffn_w_up[i].astype(BF16),
                            ffn_w_down[i].astype(BF16), row(ln2_g[l]), row(ln2_b[l]))
            x_p, x_s = x[:mp], x[mp:]
        else:
            x_p, x_s = _moe_ln(x, mp, router_w[i], row(router_b[i]), exp_w_gate[i].astype(BF16),
                               exp_w_up[i].astype(BF16), exp_w_down[i].astype(BF16),
                               row(ln2_g[l]), row(ln2_b[l]))
            if l + 1 < depth:
                x = jnp.concatenate([x_p, x_s], axis=0)
                xb = x.astype(BF16)

        outs["pa"].append(pa)
        outs["pk"].append(pk.reshape(nb, w, N_KV, HEAD_DIM))
        outs["pv"].append(pv.reshape(nb, w, N_KV, HEAD_DIM))
        outs["pc"].append(pc)
        outs["sa"].append(sa)
        outs["sk"].append(sk.reshape(nseq, w, N_KV, HEAD_DIM))
        outs["sv"].append(sv.reshape(nseq, w, N_KV, HEAD_DIM))
        outs["sc"].append(sc)
        outs["sd"].append(sd)

    y_prompt = x_p.reshape(nb, t_len, d)
    y_sample = x_s.reshape(nseq, s_len, d)
    st = {k: jnp.stack(v) for k, v in outs.items()}
    return (y_prompt, y_sample, st["pa"], st["pk"], st["pv"], st["pc"],
            st["sa"], st["sk"], st["sv"], st["sc"], st["sd"])
```

```python
import functools

import jax
import jax.numpy as jnp
import numpy as np
from jax import lax
from jax.experimental import pallas as pl
from jax.experimental.pallas import tpu as pltpu

D_MODEL = 2048
PAST_LEN = 8192
D_A = 512
CONV_A = 31
HEAD_DIM = 64
N_HEADS = 16
N_KV = 4
GQA = N_HEADS // N_KV
WINDOW = 128
ROPE_THETA = 10000.0
D_C = 512
CONV_C = 3
D_D = 512
CHUNK = 128
N_GROUPS_D = 4
GROUP_D = D_D // N_GROUPS_D
N_BRANCH = 4
Q_WIDTH = N_HEADS * HEAD_DIM
KV_WIDTH = N_KV * HEAD_DIM
D_FF = 5632
N_EXPERTS = 8
ALPHA = 4.0 ** 0.25
LN_EPS = 1e-5

OFF_A_VAL = 0
OFF_A_GATE = D_A
OFF_Q = 2 * D_A
OFF_K = OFF_Q + Q_WIDTH
OFF_V = OFF_K + KV_WIDTH
OFF_C_B = OFF_V + KV_WIDTH
OFF_C_C = OFF_C_B + D_C
OFF_C_X = OFF_C_C + D_C
OFF_D_U = OFF_C_X + D_C
OFF_D_V = OFF_D_U + D_D
OFF_GATES = OFF_D_V + D_D

LANES = 128
SUBLANES = 8
HALO = 32
VMEM_LIMIT = 48 * 1024 * 1024
OUT_LN_SUBTILES = 4
MOE_F_STEPS = 11
MOE_ROWS = 88 * (MOE_F_STEPS - 1)
MOE_VMEM_LIMIT = (2 * 4 + 2 + 2 * 4) * MOE_ROWS * D_MODEL + 2 * 3 * 2 * D_MODEL * 512 + 8 * 1024 * 1024
W_BANDS = 1

BF16 = jnp.bfloat16
F32 = jnp.float32


def _tile(n, pref):
    n = int(n)
    if n <= pref:
        return n
    for t in range(pref, 7, -1):
        if n % t == 0 and t % 8 == 0:
            return t
    return n


def _params(*sem):
    return pltpu.CompilerParams(dimension_semantics=sem, vmem_limit_bytes=VMEM_LIMIT)


def _layer_norm(x, g, b):
    mu = jnp.mean(x, axis=-1, keepdims=True)
    xc = x - mu
    var = jnp.mean(xc * xc, axis=-1, keepdims=True)
    return xc * lax.rsqrt(var + LN_EPS) * g + b


def _silu(x):
    return x * jax.nn.sigmoid(x)


def _gelu(x):
    return jax.nn.gelu(x, approximate=True)


def _stack_cast_kernel(xp_ref, xs_ref, o_ref, *, prompt_tiles):
    is_prompt = pl.program_id(0) < prompt_tiles
    o_ref[...] = jnp.where(is_prompt, xp_ref[...], xs_ref[...]).astype(o_ref.dtype)


def _stack_cast(xp, xs):
    mp, ms = xp.shape[0], xs.shape[0]
    tm = _tile(np.gcd(mp, ms), 512)
    npt = mp // tm
    return pl.pallas_call(
        functools.partial(_stack_cast_kernel, prompt_tiles=npt),
        grid=((mp + ms) // tm,),
        in_specs=[pl.BlockSpec((tm, D_MODEL), lambda i: (jnp.minimum(i, npt - 1), 0)),
                  pl.BlockSpec((tm, D_MODEL), lambda i: (jnp.maximum(i - npt, 0), 0))],
        out_specs=pl.BlockSpec((tm, D_MODEL), lambda i: (i, 0)),
        out_shape=jax.ShapeDtypeStruct((mp + ms, D_MODEL), BF16),
        compiler_params=_params("parallel"),
        name="stack_cast",
    )(xp, xs)


def _mm_kernel(x_ref, w_ref, o_ref):
    o_ref[...] = jnp.dot(x_ref[...], w_ref[...], preferred_element_type=F32).astype(o_ref.dtype)


def _in_proj(xb, w_in_b):
    m, k = xb.shape
    n = OFF_GATES
    tm = _tile(m, 1024)
    tn = 1024
    return pl.pallas_call(
        _mm_kernel,
        grid=(m // tm, n // tn),
        in_specs=[pl.BlockSpec((tm, k), lambda i, j: (i, 0)),
                  pl.BlockSpec((k, tn), lambda i, j: (0, j))],
        out_specs=pl.BlockSpec((tm, tn), lambda i, j: (i, j)),
        out_shape=jax.ShapeDtypeStruct((m, n), F32),
        compiler_params=_params("parallel", "arbitrary"),
        name="in_proj",
    )(xb, w_in_b)


def _ac_prompt_kernel(av_ref, ag_ref, cb_ref, cc_ref, cx_ref,
                      avp_ref, agp_ref, ccp_ref, cxp_ref,
                      wa_ref, ba_ref, lg_ref, lb_ref, wc_ref,
                      ya_ref, yc_ref, ha_ref, hc_ref, ga_s, gc_s, sh_s):
    t = pl.program_id(1)
    tt = av_ref.shape[0]
    first = t == 0
    glu_prev = avp_ref[...] * jax.nn.sigmoid(agp_ref[...])
    ga_s[0:HALO, :] = jnp.where(first, 0.0, glu_prev)
    ga_s[HALO:HALO + tt, :] = av_ref[...] * jax.nn.sigmoid(ag_ref[...])
    acc = jnp.zeros((tt, D_A), F32) + ba_ref[...]
    base = HALO - (CONV_A - 1)
    for phase in range(SUBLANES):
        offs = [o for o in range(base, base + CONV_A) if o % SUBLANES == phase]
        span = offs[-1] - phase + tt
        sh_s[0:span, :] = ga_s[phase:phase + span, :]
        for o in offs:
            acc = acc + wa_ref[o - base:o - base + 1, :] * sh_s[o - phase:o - phase + tt, :]
    ya = _silu(_layer_norm(acc, lg_ref[...], lb_ref[...]))
    ya_ref[...] = ya.astype(ya_ref.dtype)
    ha_ref[0] = ga_s[HALO + tt - (CONV_A - 1):HALO + tt, :]

    gc_s[0:HALO, :] = jnp.where(first, 0.0, ccp_ref[...] * cxp_ref[...])
    gc_s[HALO:HALO + tt, :] = cc_ref[...] * cx_ref[...]
    base_c = HALO - (CONV_C - 1)
    yc = jnp.zeros((tt, D_C), F32)
    for j in range(CONV_C):
        yc = yc + wc_ref[j:j + 1, :] * gc_s[base_c + j:base_c + j + tt, :]
    yc_ref[...] = (cb_ref[...] * yc).astype(yc_ref.dtype)
    hc_ref[0] = gc_s[HALO + tt - (CONV_C - 1):HALO + tt, :]


def _ac_prompt(h, nb, t_len, wa, ba, lg, lb, wc):
    tt = _tile(t_len, 512)
    nt = t_len // tt
    cw = D_A

    def cur(col):
        return pl.BlockSpec((tt, cw), lambda b, t: (b * nt + t, col // cw))

    def prev(col):
        return pl.BlockSpec(
            (HALO, cw),
            lambda b, t: (jnp.maximum((b * t_len + t * tt) // HALO - 1, 0), col // cw))

    def full(shape):
        return pl.BlockSpec(shape, lambda b, t: (0,) * len(shape))

    mp = nb * t_len
    return pl.pallas_call(
        _ac_prompt_kernel,
        grid=(nb, nt),
        in_specs=[cur(OFF_A_VAL), cur(OFF_A_GATE), cur(OFF_C_B), cur(OFF_C_C), cur(OFF_C_X),
                  prev(OFF_A_VAL), prev(OFF_A_GATE), prev(OFF_C_C), prev(OFF_C_X),
                  full((CONV_A, D_A)), full((1, D_A)), full((1, D_A)), full((1, D_A)),
                  full((CONV_C, D_C))],
        out_specs=[pl.BlockSpec((tt, D_A), lambda b, t: (b * nt + t, 0)),
                   pl.BlockSpec((tt, D_C), lambda b, t: (b * nt + t, 0)),
                   pl.BlockSpec((1, CONV_A - 1, D_A), lambda b, t: (b, 0, 0)),
                   pl.BlockSpec((1, CONV_C - 1, D_C), lambda b, t: (b, 0, 0))],
        out_shape=[jax.ShapeDtypeStruct((mp, D_A), BF16),
                   jax.ShapeDtypeStruct((mp, D_C), BF16),
                   jax.ShapeDtypeStruct((nb, CONV_A - 1, D_A), F32),
                   jax.ShapeDtypeStruct((nb, CONV_C - 1, D_C), F32)],
        scratch_shapes=[pltpu.VMEM((HALO + tt, D_A), F32), pltpu.VMEM((HALO + tt, D_C), F32),
                        pltpu.VMEM((HALO + tt, D_A), F32)],
        compiler_params=_params("parallel", "arbitrary"),
        name="ac_prompt",
    )(h, h, h, h, h, h, h, h, h, wa, ba, lg, lb, wc)


def _ac_sample_kernel(av_ref, ag_ref, cb_ref, cc_ref, cx_ref, hista_ref, histc_ref,
                      wa_ref, ba_ref, lg_ref, lb_ref, wc_ref,
                      ya_ref, yc_ref, ha_ref, hc_ref, xa_s, xc_s):
    sb, s_len = hista_ref.shape[0], av_ref.shape[0] // hista_ref.shape[0]
    ka, kc = CONV_A - 1, CONV_C - 1
    glu = av_ref[...] * jax.nn.sigmoid(ag_ref[...])
    xa_s[:, 0:ka, :] = hista_ref[...]
    xa_s[:, ka:ka + s_len, :] = glu.reshape(sb, s_len, D_A)
    acc = jnp.zeros((sb, s_len, D_A), F32) + ba_ref[...][None]
    for j in range(CONV_A):
        acc = acc + wa_ref[j:j + 1, :][None] * xa_s[:, j:j + s_len, :]
    ya = _silu(_layer_norm(acc, lg_ref[...][None], lb_ref[...][None]))
    ya_ref[...] = ya.reshape(sb * s_len, D_A).astype(ya_ref.dtype)
    ha_ref[...] = xa_s[:, s_len:s_len + ka, :]

    xc_s[:, 0:kc, :] = histc_ref[...]
    xc_s[:, kc:kc + s_len, :] = (cc_ref[...] * cx_ref[...]).reshape(sb, s_len, D_C)
    yc = jnp.zeros((sb, s_len, D_C), F32)
    for j in range(CONV_C):
        yc = yc + wc_ref[j:j + 1, :][None] * xc_s[:, j:j + s_len, :]
    yc = cb_ref[...] * yc.reshape(sb * s_len, D_C)
    yc_ref[...] = yc.astype(yc_ref.dtype)
    hc_ref[...] = xc_s[:, s_len:s_len + kc, :]


def _ac_sample(h, row0, nseq, s_len, hist_a, hist_c, wa, ba, lg, lb, wc):
    sb = _tile(nseq, 16)
    rows = sb * s_len
    blk0 = row0 // rows
    cw = D_A

    def cur(col):
        return pl.BlockSpec((rows, cw), lambda i: (blk0 + i, col // cw))

    def full(shape):
        return pl.BlockSpec(shape, lambda i: (0,) * len(shape))

    ms = nseq * s_len
    ka, kc = CONV_A - 1, CONV_C - 1
    return pl.pallas_call(
        _ac_sample_kernel,
        grid=(nseq // sb,),
        in_specs=[cur(OFF_A_VAL), cur(OFF_A_GATE), cur(OFF_C_B), cur(OFF_C_C), cur(OFF_C_X),
                  pl.BlockSpec((sb, ka, D_A), lambda i: (i, 0, 0)),
                  pl.BlockSpec((sb, kc, D_C), lambda i: (i, 0, 0)),
                  full((CONV_A, D_A)), full((1, D_A)), full((1, D_A)), full((1, D_A)),
                  full((CONV_C, D_C))],
        out_specs=[pl.BlockSpec((rows, D_A), lambda i: (i, 0)),
                   pl.BlockSpec((rows, D_C), lambda i: (i, 0)),
                   pl.BlockSpec((sb, ka, D_A), lambda i: (i, 0, 0)),
                   pl.BlockSpec((sb, kc, D_C), lambda i: (i, 0, 0))],
        out_shape=[jax.ShapeDtypeStruct((ms, D_A), BF16),
                   jax.ShapeDtypeStruct((ms, D_C), BF16),
                   jax.ShapeDtypeStruct((nseq, ka, D_A), F32),
                   jax.ShapeDtypeStruct((nseq, kc, D_C), F32)],
        scratch_shapes=[pltpu.VMEM((sb, ka + s_len + 2, D_A), F32),
                        pltpu.VMEM((sb, kc + s_len + 6, D_C), F32)],
        compiler_params=_params("parallel"),
        name="ac_sample",
    )(h, h, h, h, h, hist_a, hist_c, wa, ba, lg, lb, wc)


def _rope_tables(pos):
    half = HEAD_DIM // 2
    inv_freq = jnp.power(ROPE_THETA, -jnp.arange(half, dtype=F32) * (2.0 / HEAD_DIM))
    ang = pos.astype(F32)[:, None] * inv_freq[None, :]
    cos, sin = jnp.cos(ang), jnp.sin(ang)
    cos_t = jnp.concatenate([cos, cos, cos, cos], axis=1)
    sin_t = jnp.concatenate([-sin, sin, -sin, sin], axis=1)
    return cos_t, sin_t


def _rope(x, cos_t, sin_t):
    half = HEAD_DIM // 2
    axis = x.ndim - 1
    shape = x.shape[:-1] + (LANES,)
    lane = lax.broadcasted_iota(jnp.int32, shape, axis)
    first = (lane % HEAD_DIM) < half
    out = []
    for i in range(x.shape[-1] // LANES):
        xi = x[..., LANES * i:LANES * (i + 1)]
        partner = jnp.where(first, pltpu.roll(xi, LANES - half, axis), pltpu.roll(xi, half, axis))
        out.append(xi * cos_t + partner * sin_t)
    return out


def _head_halves(t, axis, lo_valid):
    lane = lax.broadcasted_iota(jnp.int32, t.shape, axis)
    if lo_valid:
        lo = jnp.where(lane < HEAD_DIM, t, jnp.zeros_like(t))
        hi = pltpu.roll(lo, HEAD_DIM, axis)
    else:
        hi = jnp.where(lane >= HEAD_DIM, t, jnp.zeros_like(t))
        lo = pltpu.roll(hi, HEAD_DIM, axis)
    return lo, hi


def _swa_block(sinks_ref, q_tiles, kp_tiles, kc_tiles, vp, vc, has_prev, write):
    w = WINDOW
    ci = lax.broadcasted_iota(jnp.int32, (2 * w, 2 * w), 0)
    qi = lax.broadcasted_iota(jnp.int32, (2 * w, 2 * w), 1) % w
    mask = (ci > qi) & (ci <= qi + w) & (has_prev | (ci >= w))
    lane_first = lax.broadcasted_iota(jnp.int32, (1, 2 * w), 1) < w

    qt_tiles = [t.T.astype(BF16) for t in q_tiles]
    zeros_t = jnp.zeros((HEAD_DIM, 2 * w), BF16)

    for h in range(N_KV):
        tile, lo_valid = h // 2, (h % 2 == 0)
        lanes = slice(LANES * tile, LANES * (tile + 1))
        kcat = jnp.concatenate([kp_tiles[tile], kc_tiles[tile]], axis=0)
        k_lo, k_hi = _head_halves(kcat.astype(BF16), 1, lo_valid)
        vt = jnp.concatenate([vp[:, lanes], vc[:, lanes]], axis=0).T.astype(BF16)
        vt = vt[0:HEAD_DIM] if lo_valid else vt[HEAD_DIM:2 * HEAD_DIM]
        vt_halves = (jnp.concatenate([vt, zeros_t], axis=0), jnp.concatenate([zeros_t, vt], axis=0))
        qst = jnp.concatenate([qt_tiles[2 * h], qt_tiles[2 * h + 1]], axis=1)
        ot = None
        for half_idx, k_half in enumerate((k_lo, k_hi)):
            s = jnp.dot(k_half, qst, preferred_element_type=F32)
            s = jnp.where(mask, s, -jnp.inf)
            sk = jnp.where(lane_first, sinks_ref[4 * h + half_idx], sinks_ref[4 * h + 2 + half_idx])
            m = jnp.maximum(jnp.max(s, axis=0, keepdims=True), sk)
            e = jnp.exp(s - m)
            den = jnp.sum(e, axis=0, keepdims=True) + jnp.exp(sk - m)
            p = (e / den).astype(BF16)
            term = jnp.dot(vt_halves[half_idx], p, preferred_element_type=F32)
            ot = term if ot is None else ot + term
        write(2 * h, ot[:, 0:w].T)
        write(2 * h + 1, ot[:, w:2 * w].T)


def _attn_prompt_kernel(sinks_ref, q_ref, kc_ref, vc_ref, kp_ref, vp_ref,
                        cos_ref, sin_ref, cosp_ref, sinp_ref,
                        yb_ref, nk_ref, nv_ref):
    n = pl.program_id(1)
    w = WINDOW
    cos_c, sin_c = cos_ref[...], sin_ref[...]
    q_tiles = _rope(q_ref[...] * (HEAD_DIM ** -0.5), cos_c, sin_c)
    kc_tiles = _rope(kc_ref[...], cos_c, sin_c)
    kp_tiles = _rope(kp_ref[...], cosp_ref[...], sinp_ref[...])
    for i in range(KV_WIDTH // LANES):
        nk_ref[0, :, LANES * i:LANES * (i + 1)] = kc_tiles[i][w:2 * w]
    nv_ref[0] = vc_ref[w:2 * w, :]

    for j in range(2):
        rows = slice(j * w, (j + 1) * w)

        def write(tile, value, rows=rows):
            yb_ref[rows, LANES * tile:LANES * (tile + 1)] = value.astype(yb_ref.dtype)

        if j == 0:
            prev_k, prev_v, has_prev = kp_tiles, vp_ref[...], n > 0
        else:
            prev_k, prev_v, has_prev = [t[0:w] for t in kc_tiles], vc_ref[0:w, :], True
        _swa_block(sinks_ref, [t[rows] for t in q_tiles], prev_k, [t[rows] for t in kc_tiles],
                   prev_v, vc_ref[rows, :], has_prev, write)


def _attn_prompt(h, nb, t_len, sinks, cos_t, sin_t):
    w = WINDOW
    nblk = t_len // w
    assert nblk % 2 == 0
    npair = nblk // 2
    mp = nb * t_len

    def pair(b, n):
        return b * npair + n

    def halo(b, n):
        return b * nblk + jnp.maximum(2 * n - 1, 0)

    return pl.pallas_call(
        _attn_prompt_kernel,
        grid=(nb, npair),
        in_specs=[pl.BlockSpec(memory_space=pltpu.SMEM),
                  pl.BlockSpec((2 * w, Q_WIDTH), lambda b, n: (pair(b, n), OFF_Q // Q_WIDTH)),
                  pl.BlockSpec((2 * w, KV_WIDTH), lambda b, n: (pair(b, n), OFF_K // KV_WIDTH)),
                  pl.BlockSpec((2 * w, KV_WIDTH), lambda b, n: (pair(b, n), OFF_V // KV_WIDTH)),
                  pl.BlockSpec((w, KV_WIDTH), lambda b, n: (halo(b, n), OFF_K // KV_WIDTH)),
                  pl.BlockSpec((w, KV_WIDTH), lambda b, n: (halo(b, n), OFF_V // KV_WIDTH)),
                  pl.BlockSpec((2 * w, LANES), lambda b, n: (n, 0)),
                  pl.BlockSpec((2 * w, LANES), lambda b, n: (n, 0)),
                  pl.BlockSpec((w, LANES), lambda b, n: (jnp.maximum(2 * n - 1, 0), 0)),
                  pl.BlockSpec((w, LANES), lambda b, n: (jnp.maximum(2 * n - 1, 0), 0))],
        out_specs=[pl.BlockSpec((2 * w, Q_WIDTH), lambda b, n: (pair(b, n), 0)),
                   pl.BlockSpec((1, w, KV_WIDTH), lambda b, n: (b, 0, 0)),
                   pl.BlockSpec((1, w, KV_WIDTH), lambda b, n: (b, 0, 0))],
        out_shape=[jax.ShapeDtypeStruct((mp, Q_WIDTH), BF16),
                   jax.ShapeDtypeStruct((nb, w, KV_WIDTH), F32),
                   jax.ShapeDtypeStruct((nb, w, KV_WIDTH), F32)],
        compiler_params=_params("parallel", "arbitrary"),
        name="attn_prompt",
    )(sinks, h, h, h, h, h, cos_t, sin_t, cos_t, sin_t)


def _attn_sample_kernel(sinks_ref, q_ref, kn_ref, vn_ref, kbuf_ref, vbuf_ref, cos_ref, sin_ref,
                        yb_ref, nk_ref, nv_ref):
    sb, w = kbuf_ref.shape[0], kbuf_ref.shape[1]
    s_len = q_ref.shape[0] // sb
    cos_t, sin_t = cos_ref[...], sin_ref[...]
    q_tiles = _rope(q_ref[...] * (HEAD_DIM ** -0.5), cos_t, sin_t)
    kn_tiles = _rope(kn_ref[...], cos_t, sin_t)
    nk_ref[:, 0:w - s_len, :] = kbuf_ref[:, s_len:w, :]
    nv_ref[:, 0:w - s_len, :] = vbuf_ref[:, s_len:w, :]
    for i in range(KV_WIDTH // LANES):
        nk_ref[:, w - s_len:w, LANES * i:LANES * (i + 1)] = kn_tiles[i].reshape(sb, s_len, LANES)
    nv_ref[:, w - s_len:w, :] = vn_ref[...].reshape(sb, s_len, KV_WIDTH)

    qi = lax.broadcasted_iota(jnp.int32, (2 * s_len, w), 0) % s_len
    ci = lax.broadcasted_iota(jnp.int32, (2 * s_len, w), 1)
    mask_buf = (ci > qi)[None]
    qn = lax.broadcasted_iota(jnp.int32, (2 * s_len, s_len), 0) % s_len
    cn = lax.broadcasted_iota(jnp.int32, (2 * s_len, s_len), 1)
    mask_new = (cn <= qn)[None]
    row_top = (lax.broadcasted_iota(jnp.int32, (2 * s_len, 1), 0) < s_len)[None]
    bqk = (((2,), (2,)), ((0,), (0,)))
    bkd = (((2,), (1,)), ((0,), (0,)))

    for h in range(N_KV):
        tile, lo_valid = h // 2, (h % 2 == 0)
        sl = slice(LANES * tile, LANES * (tile + 1))
        kb_lo, kb_hi = _head_halves(kbuf_ref[:, :, sl].astype(BF16), 2, lo_valid)
        vb_lo, vb_hi = _head_halves(vbuf_ref[:, :, sl].astype(BF16), 2, lo_valid)
        kn3 = kn_tiles[tile].reshape(sb, s_len, LANES).astype(BF16)
        vn3 = vn_ref[:, sl].reshape(sb, s_len, LANES).astype(BF16)
        kn_lo, kn_hi = _head_halves(kn3, 2, lo_valid)
        vn_lo, vn_hi = _head_halves(vn3, 2, lo_valid)
        qs = jnp.concatenate([q_tiles[2 * h].reshape(sb, s_len, LANES),
                              q_tiles[2 * h + 1].reshape(sb, s_len, LANES)], axis=1).astype(BF16)
        o = jnp.zeros((sb, 2 * s_len, LANES), F32)
        for half_idx, (kb, kn, vb, vn) in enumerate(((kb_lo, kn_lo, vb_lo, vn_lo),
                                                     (kb_hi, kn_hi, vb_hi, vn_hi))):
            s_b = lax.dot_general(qs, kb, bqk, preferred_element_type=F32)
            s_n = lax.dot_general(qs, kn, bqk, preferred_element_type=F32)
            s_b = jnp.where(mask_buf, s_b, -jnp.inf)
            s_n = jnp.where(mask_new, s_n, -jnp.inf)
            sk = jnp.where(row_top, sinks_ref[4 * h + half_idx], sinks_ref[4 * h + 2 + half_idx])
            m = jnp.maximum(jnp.maximum(jnp.max(s_b, axis=2, keepdims=True),
                                        jnp.max(s_n, axis=2, keepdims=True)), sk)
            e_b = jnp.exp(s_b - m)
            e_n = jnp.exp(s_n - m)
            den = (jnp.sum(e_b, axis=2, keepdims=True) + jnp.sum(e_n, axis=2, keepdims=True)
                   + jnp.exp(sk - m))
            o = o + lax.dot_general((e_b / den).astype(BF16), vb, bkd, preferred_element_type=F32)
            o = o + lax.dot_general((e_n / den).astype(BF16), vn, bkd, preferred_element_type=F32)
        yb_ref[:, LANES * (2 * h):LANES * (2 * h + 1)] = (
            o[:, 0:s_len, :].reshape(sb * s_len, LANES).astype(yb_ref.dtype))
        yb_ref[:, LANES * (2 * h + 1):LANES * (2 * h + 2)] = (
            o[:, s_len:2 * s_len, :].reshape(sb * s_len, LANES).astype(yb_ref.dtype))


def _attn_sample(h, row0, nseq, s_len, sinks, k_buf, v_buf, cos_t, sin_t):
    sb = _tile(nseq, 8)
    rows = sb * s_len
    blk0 = row0 // rows
    w = k_buf.shape[1]
    ms = nseq * s_len
    cos_rows = jnp.tile(cos_t, (sb, 1))
    sin_rows = jnp.tile(sin_t, (sb, 1))
    return pl.pallas_call(
        _attn_sample_kernel,
        grid=(nseq // sb,),
        in_specs=[pl.BlockSpec(memory_space=pltpu.SMEM),
                  pl.BlockSpec((rows, Q_WIDTH), lambda i: (blk0 + i, OFF_Q // Q_WIDTH)),
                  pl.BlockSpec((rows, KV_WIDTH), lambda i: (blk0 + i, OFF_K // KV_WIDTH)),
                  pl.BlockSpec((rows, KV_WIDTH), lambda i: (blk0 + i, OFF_V // KV_WIDTH)),
                  pl.BlockSpec((sb, w, KV_WIDTH), lambda i: (i, 0, 0)),
                  pl.BlockSpec((sb, w, KV_WIDTH), lambda i: (i, 0, 0)),
                  pl.BlockSpec((rows, LANES), lambda i: (0, 0)),
                  pl.BlockSpec((rows, LANES), lambda i: (0, 0))],
        out_specs=[pl.BlockSpec((rows, Q_WIDTH), lambda i: (i, 0)),
                   pl.BlockSpec((sb, w, KV_WIDTH), lambda i: (i, 0, 0)),
                   pl.BlockSpec((sb, w, KV_WIDTH), lambda i: (i, 0, 0))],
        out_shape=[jax.ShapeDtypeStruct((ms, Q_WIDTH), BF16),
                   jax.ShapeDtypeStruct((nseq, w, KV_WIDTH), F32),
                   jax.ShapeDtypeStruct((nseq, w, KV_WIDTH), F32)],
        compiler_params=_params("parallel"),
        name="attn_sample",
    )(sinks, h, h, h, k_buf, v_buf, cos_rows, sin_rows)


def _d_prompt_kernel(du_ref, dv_ref, lg_ref, lb_ref, w_ref, bt_ref, yd_ref):
    c = CHUNK
    ri = lax.broadcasted_iota(jnp.int32, (c, c), 0)
    cj = lax.broadcasted_iota(jnp.int32, (c, c), 1)
    causal = cj <= ri
    w_causal = [jnp.where(causal, w_ref[g], 0.0).astype(BF16) for g in range(N_GROUPS_D)]
    for k in range(du_ref.shape[0] // c):
        rows = slice(k * c, (k + 1) * c)
        u = _gelu(du_ref[rows, :])
        vn = _layer_norm(_gelu(dv_ref[rows, :]), lg_ref[...], lb_ref[...]).astype(BF16)
        for g in range(N_GROUPS_D):
            sl = slice(GROUP_D * g, GROUP_D * (g + 1))
            s = jnp.dot(w_causal[g], vn[:, sl], preferred_element_type=F32) + bt_ref[:, g:g + 1]
            yd_ref[rows, sl] = (u[:, sl] * s).astype(yd_ref.dtype)


def _d_prompt(h, mp, ln_g, ln_b, sgu_w, sgu_bt):
    c = CHUNK
    rows = _tile(mp, 4 * c)
    assert rows % c == 0
    return pl.pallas_call(
        _d_prompt_kernel,
        grid=(mp // rows,),
        in_specs=[pl.BlockSpec((rows, D_D), lambda i: (i, OFF_D_U // D_D)),
                  pl.BlockSpec((rows, D_D), lambda i: (i, OFF_D_V // D_D)),
                  pl.BlockSpec((1, D_D), lambda i: (0, 0)),
                  pl.BlockSpec((1, D_D), lambda i: (0, 0)),
                  pl.BlockSpec((N_GROUPS_D, c, c), lambda i: (0, 0, 0)),
                  pl.BlockSpec((c, N_GROUPS_D), lambda i: (0, 0))],
        out_specs=pl.BlockSpec((rows, D_D), lambda i: (i, 0)),
        out_shape=jax.ShapeDtypeStruct((mp, D_D), BF16),
        compiler_params=_params("parallel"),
        name="d_prompt",
    )(h, h, ln_g, ln_b, sgu_w, sgu_bt)


def _d_sample_kernel(du_ref, dv_ref, lg_ref, lb_ref, wt_ref, bt_ref, yd_ref, vd_ref):
    sb, s_len = vd_ref.shape[0], vd_ref.shape[1]
    u = _gelu(du_ref[...])
    vn = _layer_norm(_gelu(dv_ref[...]), lg_ref[...], lb_ref[...])
    vn3 = vn.reshape(sb, s_len, D_D)
    vd_ref[...] = vn3
    ii = lax.broadcasted_iota(jnp.int32, (s_len, D_D), 0)
    s = jnp.zeros((sb, s_len, D_D), F32) + bt_ref[...][None]
    for j in range(s_len):
        wj = jnp.where(ii >= j, wt_ref[j], 0.0)
        s = s + wj[None] * vn3[:, j:j + 1, :]
    yd_ref[...] = (u * s.reshape(sb * s_len, D_D)).astype(yd_ref.dtype)


def _d_sample(h, row0, nseq, s_len, ln_g, ln_b, sgu_wt, sgu_bs):
    sb = _tile(nseq, 16)
    rows = sb * s_len
    blk0 = row0 // rows
    ms = nseq * s_len
    return pl.pallas_call(
        _d_sample_kernel,
        grid=(nseq // sb,),
        in_specs=[pl.BlockSpec((rows, D_D), lambda i: (blk0 + i, OFF_D_U // D_D)),
                  pl.BlockSpec((rows, D_D), lambda i: (blk0 + i, OFF_D_V // D_D)),
                  pl.BlockSpec((1, D_D), lambda i: (0, 0)),
                  pl.BlockSpec((1, D_D), lambda i: (0, 0)),
                  pl.BlockSpec((s_len, s_len, D_D), lambda i: (0, 0, 0)),
                  pl.BlockSpec((s_len, D_D), lambda i: (0, 0))],
        out_specs=[pl.BlockSpec((rows, D_D), lambda i: (i, 0)),
                   pl.BlockSpec((sb, s_len, D_D), lambda i: (i, 0, 0))],
        out_shape=[jax.ShapeDtypeStruct((ms, D_D), BF16),
                   jax.ShapeDtypeStruct((nseq, s_len, D_D), F32)],
        compiler_params=_params("parallel"),
        name="d_sample",
    )(h, h, ln_g, ln_b, sgu_wt, sgu_bs)


def _merge_kernel(x_ref, yap_ref, ybp_ref, ycp_ref, ydp_ref, yas_ref, ybs_ref, ycs_ref, yds_ref,
                  g0_ref, g1_ref, g2_ref, g3_ref, bg_ref, pa_ref, pb_ref, pc_ref, pd_ref, o_ref,
                  *, prompt_tiles):
    x = x_ref[...]
    is_prompt = pl.program_id(1) < prompt_tiles
    acc = None
    branches = ((yap_ref, yas_ref, g0_ref, pa_ref), (ybp_ref, ybs_ref, g1_ref, pb_ref),
                (ycp_ref, ycs_ref, g2_ref, pc_ref), (ydp_ref, yds_ref, g3_ref, pd_ref))
    for i, (yp_ref, ys_ref, g_ref, p_ref) in enumerate(branches):
        y = jnp.where(is_prompt, yp_ref[...], ys_ref[...])
        gate = jax.nn.sigmoid(jnp.dot(x, g_ref[...], preferred_element_type=F32) + bg_ref[i:i + 1, :])
        term = gate * jnp.dot(y, p_ref[...], preferred_element_type=F32)
        acc = term if acc is None else acc + term
    o_ref[...] = acc.astype(o_ref.dtype)


def _merge(xb, y_prompt, y_sample, w_in_b, b_gate, projs):
    m = xb.shape[0]
    mp, ms = y_prompt[0].shape[0], y_sample[0].shape[0]
    tm = _tile(np.gcd(mp, ms), 512)
    npt = mp // tm
    tn = 512
    nn = D_MODEL // tn
    widths = (D_A, Q_WIDTH, D_C, D_D)

    def act_p(width):
        return pl.BlockSpec((tm, width), lambda j, i: (jnp.minimum(i, npt - 1), 0))

    def act_s(width):
        return pl.BlockSpec((tm, width), lambda j, i: (jnp.maximum(i - npt, 0), 0))

    def gate_w(b):
        return pl.BlockSpec((D_MODEL, tn), lambda j, i: (0, (OFF_GATES + b * D_MODEL) // tn + j))

    def proj_w(k):
        return pl.BlockSpec((k, tn), lambda j, i: (0, j))

    return pl.pallas_call(
        functools.partial(_merge_kernel, prompt_tiles=npt),
        grid=(nn, m // tm),
        in_specs=[pl.BlockSpec((tm, D_MODEL), lambda j, i: (i, 0))]
                 + [act_p(wd) for wd in widths] + [act_s(wd) for wd in widths]
                 + [gate_w(0), gate_w(1), gate_w(2), gate_w(3),
                    pl.BlockSpec((N_BRANCH, tn), lambda j, i: (0, j))]
                 + [proj_w(wd) for wd in widths],
        out_specs=pl.BlockSpec((tm, tn), lambda j, i: (i, j)),
        out_shape=jax.ShapeDtypeStruct((m, D_MODEL), BF16),
        compiler_params=_params("parallel", "arbitrary"),
        name="merge",
    )(xb, *y_prompt, *y_sample, w_in_b, w_in_b, w_in_b, w_in_b, b_gate, *projs)


def _out_ln_kernel(mg_ref, *refs, prompt_tiles):
    if prompt_tiles is None:
        x_ref, w_ref, g_ref, b_ref, o_ref, ob_ref = refs
    else:
        xp_ref, xs_ref, w_ref, g_ref, b_ref, o_ref, ob_ref = refs
        is_prompt = pl.program_id(0) < prompt_tiles
    tm = mg_ref.shape[0]
    sub = tm // OUT_LN_SUBTILES
    for k in range(OUT_LN_SUBTILES):
        rows = slice(k * sub, (k + 1) * sub)
        if prompt_tiles is None:
            x = x_ref[rows, :]
        else:
            x = jnp.where(is_prompt, xp_ref[rows, :], xs_ref[rows, :])
        y = ALPHA * x + jnp.dot(mg_ref[rows, :], w_ref[...], preferred_element_type=F32)
        y = _layer_norm(y, g_ref[...], b_ref[...])
        o_ref[rows, :] = y
        ob_ref[rows, :] = y.astype(ob_ref.dtype)


def _out_ln(merged, x, w_out_b, g, b):
    m = merged.shape[0]
    if isinstance(x, tuple):
        mp, ms = x[0].shape[0], x[1].shape[0]
        tm = _tile(np.gcd(mp, ms), 512)
        npt = mp // tm
        x_specs = [pl.BlockSpec((tm, D_MODEL), lambda i: (jnp.minimum(i, npt - 1), 0)),
                   pl.BlockSpec((tm, D_MODEL), lambda i: (jnp.maximum(i - npt, 0), 0))]
    else:
        tm = _tile(m, 512)
        npt = None
        x, x_specs = (x,), [pl.BlockSpec((tm, D_MODEL), lambda i: (i, 0))]
    return pl.pallas_call(
        functools.partial(_out_ln_kernel, prompt_tiles=npt),
        grid=(m // tm,),
        in_specs=[pl.BlockSpec((tm, D_MODEL), lambda i: (i, 0))] + x_specs
                 + [pl.BlockSpec((D_MODEL, D_MODEL), lambda i: (0, 0), pipeline_mode=pl.Buffered(1)),
                    pl.BlockSpec((1, D_MODEL), lambda i: (0, 0)),
                    pl.BlockSpec((1, D_MODEL), lambda i: (0, 0))],
        out_specs=[pl.BlockSpec((tm, D_MODEL), lambda i: (i, 0)),
                   pl.BlockSpec((tm, D_MODEL), lambda i: (i, 0))],
        out_shape=[jax.ShapeDtypeStruct((m, D_MODEL), F32),
                   jax.ShapeDtypeStruct((m, D_MODEL), BF16)],
        compiler_params=_params("parallel"),
        name="out_ln",
    )(merged, *x, w_out_b, g, b)


def _swiglu_partial(xb, wg_refs, wu_refs, wd_refs):
    kb = xb.shape[1] // W_BANDS
    gate = up = None
    for j in range(W_BANDS):
        xs = xb[:, j * kb:(j + 1) * kb]
        gj = jnp.dot(xs, wg_refs[j][...], preferred_element_type=F32)
        uj = jnp.dot(xs, wu_refs[j][...], preferred_element_type=F32)
        gate, up = (gj, uj) if gate is None else (gate + gj, up + uj)
    hid = (_silu(gate) * up).astype(BF16)
    fb = hid.shape[1] // W_BANDS
    down = None
    for j in range(W_BANDS):
        dj = jnp.dot(hid[:, j * fb:(j + 1) * fb], wd_refs[j][...], preferred_element_type=F32)
        down = dj if down is None else down + dj
    return down


def _band_specs(lead, tf, col_index, row_index):
    kb, fb = D_MODEL // W_BANDS, tf // W_BANDS

    def col(j):
        def index(*args):
            *head, f = col_index(*args)
            return (*head, j, f)
        return pl.BlockSpec((*lead, kb, tf), index)

    def row(j):
        def index(*args):
            *head, f = row_index(*args)
            return (*head, W_BANDS * f + j, 0)
        return pl.BlockSpec((*lead, fb, D_MODEL), index)

    return ([col(j) for j in range(W_BANDS)] + [col(j) for j in range(W_BANDS)]
            + [row(j) for j in range(W_BANDS)])


def _ffn_kernel(x_ref, xb_ref, *refs):
    wg_refs, wu_refs, wd_refs = (refs[0:W_BANDS], refs[W_BANDS:2 * W_BANDS],
                                 refs[2 * W_BANDS:3 * W_BANDS])
    g_ref, b_ref, o_ref, ob_ref, acc_s = refs[3 * W_BANDS:]
    f = pl.program_id(1)
    last = pl.num_programs(1) - 1

    def partial_down():
        return _swiglu_partial(xb_ref[...], wg_refs, wu_refs, wd_refs)

    @pl.when(f == 0)
    def _():
        acc_s[...] = partial_down()

    @pl.when((f > 0) & (f < last))
    def _():
        acc_s[...] += partial_down()

    @pl.when(f == last)
    def _():
        acc_s[...] += partial_down()
        sub = acc_s.shape[0] // OUT_LN_SUBTILES
        for k in range(OUT_LN_SUBTILES):
            rows = slice(k * sub, (k + 1) * sub)
            y = _layer_norm(ALPHA * x_ref[rows, :] + acc_s[rows, :], g_ref[...], b_ref[...])
            o_ref[rows, :] = y
            ob_ref[rows, :] = y.astype(ob_ref.dtype)


def _ffn_ln(x, xb, wg, wu, wd, g, b):
    m = x.shape[0]
    tm = _tile(m, 512)
    tf = 512
    return pl.pallas_call(
        _ffn_kernel,
        grid=(m // tm, D_FF // tf),
        in_specs=[pl.BlockSpec((tm, D_MODEL), lambda i, f: (i, 0)),
                  pl.BlockSpec((tm, D_MODEL), lambda i, f: (i, 0))]
                 + _band_specs((), tf, lambda i, f: (f,), lambda i, f: (f,))
                 + [pl.BlockSpec((1, D_MODEL), lambda i, f: (0, 0)),
                    pl.BlockSpec((1, D_MODEL), lambda i, f: (0, 0))],
        out_specs=[pl.BlockSpec((tm, D_MODEL), lambda i, f: (i, 0)),
                   pl.BlockSpec((tm, D_MODEL), lambda i, f: (i, 0))],
        out_shape=[jax.ShapeDtypeStruct((m, D_MODEL), F32),
                   jax.ShapeDtypeStruct((m, D_MODEL), BF16)],
        scratch_shapes=[pltpu.VMEM((tm, D_MODEL), F32)],
        compiler_params=_params("parallel", "arbitrary"),
        name="ffn_ln",
    )(x, xb, *([wg] * W_BANDS), *([wu] * W_BANDS), *([wd] * W_BANDS), g, b)


def _router_kernel(x_ref, w_ref, b_ref, idx_ref, prob_ref, rank_ref, cnt_ref, run_s):
    @pl.when(pl.program_id(0) == 0)
    def _():
        run_s[...] = jnp.zeros_like(run_s)

    x, w = x_ref[...], w_ref[...]
    x_hi, w_hi = x.astype(BF16), w.astype(BF16)
    x_lo = (x - x_hi.astype(F32)).astype(BF16)
    w_lo = (w - w_hi.astype(F32)).astype(BF16)
    logits = (jnp.dot(x_hi, w_hi, preferred_element_type=F32)
              + (jnp.dot(x_lo, w_hi, preferred_element_type=F32)
                 + jnp.dot(x_hi, w_lo, preferred_element_type=F32))) + b_ref[...]
    tm = logits.shape[0]
    idx = lax.broadcasted_iota(jnp.int32, logits.shape, 1)
    v1 = jnp.max(logits, axis=1, keepdims=True)
    i1 = jnp.min(jnp.where(logits == v1, idx, N_EXPERTS), axis=1, keepdims=True)
    rest = jnp.where(idx == i1, -jnp.inf, logits)
    v2 = jnp.max(rest, axis=1, keepdims=True)
    i2 = jnp.min(jnp.where(rest == v2, idx, N_EXPERTS), axis=1, keepdims=True)
    e2 = jnp.exp(v2 - v1)
    den = 1.0 + e2

    hit = ((idx == i1) | (idx == i2)).astype(F32)
    ri = lax.broadcasted_iota(jnp.int32, (tm, tm), 0)
    ci = lax.broadcasted_iota(jnp.int32, (tm, tm), 1)
    before = jnp.dot((ci < ri).astype(BF16), hit.astype(BF16), preferred_element_type=F32) + run_s[...]
    rank1 = jnp.sum(jnp.where(idx == i1, before, 0.0), axis=1, keepdims=True)
    rank2 = jnp.sum(jnp.where(idx == i2, before, 0.0), axis=1, keepdims=True)

    col = lax.broadcasted_iota(jnp.int32, (tm, 2), 1)
    idx_ref[...] = jnp.where(col == 0, i1, i2)
    prob_ref[...] = jnp.where(col == 0, 1.0 / den, e2 / den)
    rank_ref[...] = jnp.where(col == 0, rank1, rank2).astype(jnp.int32)
    total = run_s[...] + jnp.sum(hit, axis=0, keepdims=True)
    run_s[...] = total
    cnt_ref[...] = total.astype(jnp.int32)


def _router(x, router_w, router_b):
    m = x.shape[0]
    tm = _tile(m, 512)
    return pl.pallas_call(
        _router_kernel,
        grid=(m // tm,),
        in_specs=[pl.BlockSpec((tm, D_MODEL), lambda i: (i, 0)),
                  pl.BlockSpec((D_MODEL, N_EXPERTS), lambda i: (0, 0)),
                  pl.BlockSpec((1, N_EXPERTS), lambda i: (0, 0))],
        out_specs=[pl.BlockSpec((tm, 2), lambda i: (i, 0)),
                   pl.BlockSpec((tm, 2), lambda i: (i, 0)),
                   pl.BlockSpec((tm, 2), lambda i: (i, 0)),
                   pl.BlockSpec((1, N_EXPERTS), lambda i: (0, 0))],
        out_shape=[jax.ShapeDtypeStruct((m, 2), jnp.int32),
                   jax.ShapeDtypeStruct((m, 2), F32),
                   jax.ShapeDtypeStruct((m, 2), jnp.int32),
                   jax.ShapeDtypeStruct((1, N_EXPERTS), jnp.int32)],
        scratch_shapes=[pltpu.VMEM((1, N_EXPERTS), F32)],
        compiler_params=_params("arbitrary"),
        name="router",
    )(x, router_w, router_b)


def _route_tables(idx, rank, counts, tm, n_tiles):
    m = idx.shape[0]
    padded = ((counts + tm - 1) // tm) * tm
    ends = jnp.cumsum(padded)
    base = ends - padded
    pos = base[idx] + rank
    n_valid = ends[-1] // tm
    tiles = jnp.arange(n_tiles, dtype=jnp.int32)
    tile_valid = (tiles < n_valid).astype(jnp.int32)
    first_row = jnp.minimum(tiles, n_valid - 1) * tm
    owner = jnp.sum((ends[None, :] <= first_row[:, None]).astype(jnp.int32), axis=1)
    tile_expert = jnp.minimum(owner, N_EXPERTS - 1).astype(jnp.int32)
    flat = pos.reshape(-1)
    token = jnp.repeat(jnp.arange(m, dtype=jnp.int32), 2)
    src = jnp.zeros((n_tiles * tm,), jnp.int32).at[flat].set(token, unique_indices=True)
    return pos, src, tile_expert, tile_valid


def _row_copy(src_hbm, row, dst_vmem, dst_row, sem):
    return pltpu.make_async_copy(src_hbm.at[pl.ds(row, 1), :], dst_vmem.at[pl.ds(dst_row, 1), :], sem)


def _moe_group_kernel(te_ref, tv_ref, src_ref, src_next_ref, x_hbm, *refs):
    wg_refs, wu_refs, wd_refs = (refs[0:W_BANDS], refs[W_BANDS:2 * W_BANDS],
                                 refs[2 * W_BANDS:3 * W_BANDS])
    o_ref, xg_s, xb_s, sem = refs[3 * W_BANDS:]
    i, f = pl.program_id(0), pl.program_id(1)
    nf = pl.num_programs(1)
    tm = xb_s.shape[0]
    chunk = tm // (MOE_F_STEPS - 1)
    slot = i % 2
    valid = tv_ref[i] == 1

    @pl.when((i == 0) & (f == 0))
    def _():
        def body(r, c):
            _row_copy(x_hbm, src_ref[0, r], xg_s.at[0], r, sem.at[0]).start()
            return c
        lax.fori_loop(0, tm, body, 0, unroll=8)

    def wait_rows():
        pltpu.make_async_copy(xg_s.at[slot], xg_s.at[slot], sem.at[slot]).wait()

    @pl.when((f == 0) & jnp.logical_not(valid))
    def _():
        o_ref[...] = jnp.zeros_like(o_ref)

        @pl.when((i == 0) | (tv_ref[jnp.maximum(i - 1, 0)] == 1))
        def _():
            wait_rows()

    def swiglu_step(first, issue_next):
        if first:
            wait_rows()
            xb = xg_s[slot].astype(BF16)
            xb_s[...] = xb
        else:
            xb = xb_s[...]
        if issue_next:
            for r in range(chunk):
                row = f * chunk + r
                _row_copy(x_hbm, src_next_ref[0, row], xg_s.at[1 - slot], row, sem.at[1 - slot]).start()
        down = _swiglu_partial(xb, wg_refs, wu_refs, wd_refs)
        if first:
            o_ref[...] = down
        else:
            o_ref[...] += down

    @pl.when(valid & (f == 0))
    def _():
        swiglu_step(True, True)

    @pl.when(valid & (f > 0) & (f < nf - 1))
    def _():
        swiglu_step(False, True)

    @pl.when(valid & (f == nf - 1))
    def _():
        swiglu_step(False, False)


def _moe_group(x, src, tile_expert, tile_valid, wg, wu, wd, tm, n_tiles):
    tf = D_FF // MOE_F_STEPS
    nf = MOE_F_STEPS

    def w_tile(i, f, te, tv):
        return (te[i], jnp.where(tv[i] == 1, f, nf - 1))

    grid_spec = pltpu.PrefetchScalarGridSpec(
        num_scalar_prefetch=2,
        grid=(n_tiles, nf),
        in_specs=[pl.BlockSpec((None, 1, tm), lambda i, f, te, tv: (i, 0, 0), memory_space=pltpu.SMEM),
                  pl.BlockSpec((None, 1, tm), lambda i, f, te, tv: (jnp.minimum(i + 1, n_tiles - 1), 0, 0),
                               memory_space=pltpu.SMEM),
                  pl.BlockSpec(memory_space=pl.ANY)]
                 + _band_specs((None,), tf, w_tile, w_tile),
        out_specs=pl.BlockSpec((tm, D_MODEL), lambda i, f, te, tv: (i, 0)),
        scratch_shapes=[pltpu.VMEM((2, tm, D_MODEL), F32),
                        pltpu.VMEM((tm, D_MODEL), BF16),
                        pltpu.SemaphoreType.DMA((2,))],
    )
    src3 = src.reshape(n_tiles, 1, tm)
    return pl.pallas_call(
        _moe_group_kernel,
        grid_spec=grid_spec,
        out_shape=jax.ShapeDtypeStruct((n_tiles * tm, D_MODEL), F32),
        compiler_params=pltpu.CompilerParams(dimension_semantics=("arbitrary", "arbitrary"),
                                             vmem_limit_bytes=MOE_VMEM_LIMIT),
        name="moe_group",
    )(tile_expert, tile_valid, src3, src3, x, *([wg] * W_BANDS), *([wu] * W_BANDS), *([wd] * W_BANDS))

def _combine_kernel(pos_ref, pos_next_ref, x_ref, p_ref, y_hbm, g_ref, b_ref, op_ref, os_ref, ybuf, sem,
                    *, prompt_tiles):
    i, nt = pl.program_id(0), pl.num_programs(0)
    tm = x_ref.shape[0]
    slot = i % 2

    def start_pair(idx_ref, s, r):
        _row_copy(y_hbm, idx_ref[0, 2 * r], ybuf.at[s, 0], r, sem.at[s]).start()
        _row_copy(y_hbm, idx_ref[0, 2 * r + 1], ybuf.at[s, 1], r, sem.at[s]).start()

    @pl.when(i == 0)
    def _():
        def body(r, c):
            start_pair(pos_ref, 0, r)
            return c
        lax.fori_loop(0, tm, body, 0, unroll=4)

    pltpu.make_async_copy(ybuf.at[slot], ybuf.at[slot], sem.at[slot]).wait()

    def norm_tile(issue_next):
        if issue_next:
            for r in range(tm):
                start_pair(pos_next_ref, 1 - slot, r)
        moe = p_ref[:, 0:1] * ybuf[slot, 0] + p_ref[:, 1:2] * ybuf[slot, 1]
        y = _layer_norm(ALPHA * x_ref[...] + moe, g_ref[...], b_ref[...])

        @pl.when(i < prompt_tiles)
        def _():
            op_ref[...] = y

        @pl.when(i >= prompt_tiles)
        def _():
            os_ref[...] = y

    @pl.when(i + 1 < nt)
    def _():
        norm_tile(True)

    @pl.when(i + 1 >= nt)
    def _():
        norm_tile(False)


def _combine_ln(x, prob, y_sorted, pos, g, b, mp):
    m = x.shape[0]
    ms = m - mp
    tm = _tile(np.gcd(mp, ms), 256)
    nt, npt = m // tm, mp // tm
    pos3 = pos.reshape(nt, 1, 2 * tm)
    return pl.pallas_call(
        functools.partial(_combine_kernel, prompt_tiles=npt),
        grid=(nt,),
        in_specs=[pl.BlockSpec((None, 1, 2 * tm), lambda i: (i, 0, 0), memory_space=pltpu.SMEM),
                  pl.BlockSpec((None, 1, 2 * tm), lambda i: (jnp.minimum(i + 1, nt - 1), 0, 0),
                               memory_space=pltpu.SMEM),
                  pl.BlockSpec((tm, D_MODEL), lambda i: (i, 0)),
                  pl.BlockSpec((tm, 2), lambda i: (i, 0)),
                  pl.BlockSpec(memory_space=pl.ANY),
                  pl.BlockSpec((1, D_MODEL), lambda i: (0, 0)),
                  pl.BlockSpec((1, D_MODEL), lambda i: (0, 0))],
        out_specs=[pl.BlockSpec((tm, D_MODEL), lambda i: (jnp.minimum(i, npt - 1), 0)),
                   pl.BlockSpec((tm, D_MODEL), lambda i: (jnp.maximum(i - npt, 0), 0))],
        out_shape=[jax.ShapeDtypeStruct((mp, D_MODEL), F32),
                   jax.ShapeDtypeStruct((ms, D_MODEL), F32)],
        scratch_shapes=[pltpu.VMEM((2, 2, tm, D_MODEL), F32), pltpu.SemaphoreType.DMA((2,))],
        compiler_params=_params("arbitrary"),
        name="combine_ln",
    )(pos3, pos3, x, prob, y_sorted, g, b)


def _moe_ln(x, mp, router_w, router_b, wg, wu, wd, g, b):
    m = x.shape[0]
    tm = MOE_ROWS
    n_tiles = -(-2 * m // tm) + N_EXPERTS
    idx, prob, rank, counts = _router(x, router_w, router_b)
    pos, src, tile_expert, tile_valid = _route_tables(idx, rank, counts[0], tm, n_tiles)
    y_sorted = _moe_group(x, src, tile_expert, tile_valid, wg, wu, wd, tm, n_tiles)
    return _combine_ln(x, prob, y_sorted, pos, g, b, mp)


def kernel(x_prompt, x_sample, state_conv_a, cache_swa_k, cache_swa_v, state_conv_c, w_in, b_gate, conv_a_w, conv_a_b, ln_a_g, ln_a_b, w_branch_a, sinks, w_branch_b, conv_c_w, w_branch_c, ln_d_g, ln_d_b, sgu_w, sgu_b, w_branch_d, w_out, ln1_g, ln1_b, ffn_w_gate, ffn_w_up, ffn_w_down, router_w, router_b, exp_w_gate, exp_w_up, exp_w_down, ln2_g, ln2_b):
    nb, t_len, d = x_prompt.shape
    nseq, s_len, _ = x_sample.shape
    depth = w_in.shape[0]
    mp, ms = nb * t_len, nseq * s_len
    w = WINDOW

    x = (x_prompt.reshape(mp, d), x_sample.reshape(ms, d))
    xb = _stack_cast(*x)

    cos_p, sin_p = _rope_tables(jnp.arange(t_len))
    cos_s, sin_s = _rope_tables(PAST_LEN + jnp.arange(s_len))

    def row(v):
        return v.reshape(1, -1)

    outs = {k: [] for k in ("pa", "pk", "pv", "pc", "sa", "sk", "sv", "sc", "sd")}
    for l in range(depth):
        w_in_b = w_in[l].astype(BF16)
        h = _in_proj(xb, w_in_b)

        wa, ba = conv_a_w[l], row(conv_a_b[l])
        lag, lab = row(ln_a_g[l]), row(ln_a_b[l])
        ya_p, yc_p, pa, pc = _ac_prompt(h, nb, t_len, wa, ba, lag, lab, conv_c_w[l])
        ya_s, yc_s, sa, sc = _ac_sample(h, mp, nseq, s_len, state_conv_a[l], state_conv_c[l],
                                        wa, ba, lag, lab, conv_c_w[l])

        yb_p, pk, pv = _attn_prompt(h, nb, t_len, sinks[l], cos_p, sin_p)
        yb_s, sk, sv = _attn_sample(h, mp, nseq, s_len, sinks[l],
                                    cache_swa_k[l].reshape(nseq, w, KV_WIDTH),
                                    cache_swa_v[l].reshape(nseq, w, KV_WIDTH), cos_s, sin_s)

        ldg, ldb = row(ln_d_g[l]), row(ln_d_b[l])
        yd_p = _d_prompt(h, mp, ldg, ldb, sgu_w[l], jnp.transpose(sgu_b[l]))
        sgu_wt = jnp.repeat(jnp.transpose(sgu_w[l][:, :s_len, :s_len], (2, 1, 0)), GROUP_D, axis=2)
        sgu_bs = jnp.repeat(jnp.transpose(sgu_b[l][:, :s_len]), GROUP_D, axis=1)
        yd_s, sd = _d_sample(h, mp, nseq, s_len, ldg, ldb, sgu_wt, sgu_bs)

        merged = _merge(xb, (ya_p, yb_p, yc_p, yd_p), (ya_s, yb_s, yc_s, yd_s), w_in_b, b_gate[l],
                        (w_branch_a[l].astype(BF16), w_branch_b[l].astype(BF16),
                         w_branch_c[l].astype(BF16), w_branch_d[l].astype(BF16)))
        x, xb = _out_ln(merged, x, w_out[l].astype(BF16), row(ln1_g[l]), row(ln1_b[l]))

        i = l // 2
        if l % 2 == 0:
            x, xb = _ffn_ln(x, xb, ffn_w_gate[i].astype(BF16), ffn_w_up[i].astype(BF16),
                            ffn_w_down[i].astype(BF16), row(ln2_g[l]), row(ln2_b[l]))
            x_p, x_s = x[:mp], x[mp:]
        else:
            x_p, x_s = _moe_ln(x, mp, router_w[i], row(router_b[i]), exp_w_gate[i].astype(BF16),
                               exp_w_up[i].astype(BF16), exp_w_down[i].astype(BF16),
                               row(ln2_g[l]), row(ln2_b[l]))
            if l + 1 < depth:
                x = jnp.concatenate([x_p, x_s], axis=0)
                xb = x.astype(BF16)

        outs["pa"].append(pa)
        outs["pk"].append(pk.reshape(nb, w, N_KV, HEAD_DIM))
        outs["pv"].append(pv.reshape(nb, w, N_KV, HEAD_DIM))
        outs["pc"].append(pc)
        outs["sa"].append(sa)
        outs["sk"].append(sk.reshape(nseq, w, N_KV, HEAD_DIM))
        outs["sv"].append(sv.reshape(nseq, w, N_KV, HEAD_DIM))
        outs["sc"].append(sc)
        outs["sd"].append(sd)

    y_prompt = x_p.reshape(nb, t_len, d)
    y_sample = x_s.reshape(nseq, s_len, d)
    st = {k: jnp.stack(v) for k, v in outs.items()}
    return (y_prompt, y_sample, st["pa"], st["pk"], st["pv"], st["pc"],
            st["sa"], st["sk"], st["sv"], st["sc"], st["sd"])
```

```python
import functools

import jax
import jax.numpy as jnp
import numpy as np
from jax import lax
from jax.experimental import pallas as pl
from jax.experimental.pallas import tpu as pltpu

D_MODEL = 2048
PAST_LEN = 8192
D_A = 512
CONV_A = 31
HEAD_DIM = 64
N_HEADS = 16
N_KV = 4
GQA = N_HEADS // N_KV
WINDOW = 128
ROPE_THETA = 10000.0
D_C = 512
CONV_C = 3
D_D = 512
CHUNK = 128
N_GROUPS_D = 4
GROUP_D = D_D // N_GROUPS_D
N_BRANCH = 4
Q_WIDTH = N_HEADS * HEAD_DIM
KV_WIDTH = N_KV * HEAD_DIM
D_FF = 5632
N_EXPERTS = 8
ALPHA = 4.0 ** 0.25
LN_EPS = 1e-5

OFF_A_VAL = 0
OFF_A_GATE = D_A
OFF_Q = 2 * D_A
OFF_K = OFF_Q + Q_WIDTH
OFF_V = OFF_K + KV_WIDTH
OFF_C_B = OFF_V + KV_WIDTH
OFF_C_C = OFF_C_B + D_C
OFF_C_X = OFF_C_C + D_C
OFF_D_U = OFF_C_X + D_C
OFF_D_V = OFF_D_U + D_D
OFF_GATES = OFF_D_V + D_D

LANES = 128
SUBLANES = 8
HALO = 32
VMEM_LIMIT = 48 * 1024 * 1024
OUT_LN_SUBTILES = 4
FFN_ROWS = 544
MOE_F_STEPS = 11
MOE_ROWS = 80 * (MOE_F_STEPS - 1)
MOE_VMEM_LIMIT = (2 * 4 + 2 + 2 * 4) * MOE_ROWS * D_MODEL + 2 * 3 * 2 * D_MODEL * 512 + 8 * 1024 * 1024
W_BANDS = 1

BF16 = jnp.bfloat16
F32 = jnp.float32


def _tile(n, pref):
    n = int(n)
    if n <= pref:
        return n
    for t in range(pref, 7, -1):
        if n % t == 0 and t % 8 == 0:
            return t
    return n


def _params(*sem):
    return pltpu.CompilerParams(dimension_semantics=sem, vmem_limit_bytes=VMEM_LIMIT)


def _layer_norm(x, g, b):
    mu = jnp.mean(x, axis=-1, keepdims=True)
    xc = x - mu
    var = jnp.mean(xc * xc, axis=-1, keepdims=True)
    return xc * lax.rsqrt(var + LN_EPS) * g + b


def _silu(x):
    return x * jax.nn.sigmoid(x)


def _gelu(x):
    return jax.nn.gelu(x, approximate=True)


def _stack_cast_kernel(xp_ref, xs_ref, o_ref, *, prompt_tiles):
    is_prompt = pl.program_id(0) < prompt_tiles
    o_ref[...] = jnp.where(is_prompt, xp_ref[...], xs_ref[...]).astype(o_ref.dtype)


def _stack_cast(xp, xs):
    mp, ms = xp.shape[0], xs.shape[0]
    tm = _tile(np.gcd(mp, ms), 512)
    npt = mp // tm
    return pl.pallas_call(
        functools.partial(_stack_cast_kernel, prompt_tiles=npt),
        grid=((mp + ms) // tm,),
        in_specs=[pl.BlockSpec((tm, D_MODEL), lambda i: (jnp.minimum(i, npt - 1), 0)),
                  pl.BlockSpec((tm, D_MODEL), lambda i: (jnp.maximum(i - npt, 0), 0))],
        out_specs=pl.BlockSpec((tm, D_MODEL), lambda i: (i, 0)),
        out_shape=jax.ShapeDtypeStruct((mp + ms, D_MODEL), BF16),
        compiler_params=_params("parallel"),
        name="stack_cast",
    )(xp, xs)


def _mm_kernel(x_ref, w_ref, o_ref):
    o_ref[...] = jnp.dot(x_ref[...], w_ref[...], preferred_element_type=F32).astype(o_ref.dtype)


def _in_proj(xb, w_in_b):
    m, k = xb.shape
    n = OFF_GATES
    tm = _tile(m, 1024)
    tn = 1024
    return pl.pallas_call(
        _mm_kernel,
        grid=(m // tm, n // tn),
        in_specs=[pl.BlockSpec((tm, k), lambda i, j: (i, 0)),
                  pl.BlockSpec((k, tn), lambda i, j: (0, j))],
        out_specs=pl.BlockSpec((tm, tn), lambda i, j: (i, j)),
        out_shape=jax.ShapeDtypeStruct((m, n), F32),
        compiler_params=_params("parallel", "arbitrary"),
        name="in_proj",
    )(xb, w_in_b)


def _ac_prompt_kernel(av_ref, ag_ref, cb_ref, cc_ref, cx_ref,
                      avp_ref, agp_ref, ccp_ref, cxp_ref,
                      wa_ref, ba_ref, lg_ref, lb_ref, wc_ref,
                      ya_ref, yc_ref, ha_ref, hc_ref, ga_s, gc_s, sh_s):
    t = pl.program_id(1)
    tt = av_ref.shape[0]
    first = t == 0
    glu_prev = avp_ref[...] * jax.nn.sigmoid(agp_ref[...])
    ga_s[0:HALO, :] = jnp.where(first, 0.0, glu_prev)
    ga_s[HALO:HALO + tt, :] = av_ref[...] * jax.nn.sigmoid(ag_ref[...])
    acc = jnp.zeros((tt, D_A), F32) + ba_ref[...]
    base = HALO - (CONV_A - 1)
    for phase in range(SUBLANES):
        offs = [o for o in range(base, base + CONV_A) if o % SUBLANES == phase]
        span = offs[-1] - phase + tt
        sh_s[0:span, :] = ga_s[phase:phase + span, :]
        for o in offs:
            acc = acc + wa_ref[o - base:o - base + 1, :] * sh_s[o - phase:o - phase + tt, :]
    ya = _silu(_layer_norm(acc, lg_ref[...], lb_ref[...]))
    ya_ref[...] = ya.astype(ya_ref.dtype)
    ha_ref[0] = ga_s[HALO + tt - (CONV_A - 1):HALO + tt, :]

    gc_s[0:HALO, :] = jnp.where(first, 0.0, ccp_ref[...] * cxp_ref[...])
    gc_s[HALO:HALO + tt, :] = cc_ref[...] * cx_ref[...]
    base_c = HALO - (CONV_C - 1)
    yc = jnp.zeros((tt, D_C), F32)
    for j in range(CONV_C):
        yc = yc + wc_ref[j:j + 1, :] * gc_s[base_c + j:base_c + j + tt, :]
    yc_ref[...] = (cb_ref[...] * yc).astype(yc_ref.dtype)
    hc_ref[0] = gc_s[HALO + tt - (CONV_C - 1):HALO + tt, :]


def _ac_prompt(h, nb, t_len, wa, ba, lg, lb, wc):
    tt = _tile(t_len, 512)
    nt = t_len // tt
    cw = D_A

    def cur(col):
        return pl.BlockSpec((tt, cw), lambda b, t: (b * nt + t, col // cw))

    def prev(col):
        return pl.BlockSpec(
            (HALO, cw),
            lambda b, t: (jnp.maximum((b * t_len + t * tt) // HALO - 1, 0), col // cw))

    def full(shape):
        return pl.BlockSpec(shape, lambda b, t: (0,) * len(shape))

    mp = nb * t_len
    return pl.pallas_call(
        _ac_prompt_kernel,
        grid=(nb, nt),
        in_specs=[cur(OFF_A_VAL), cur(OFF_A_GATE), cur(OFF_C_B), cur(OFF_C_C), cur(OFF_C_X),
                  prev(OFF_A_VAL), prev(OFF_A_GATE), prev(OFF_C_C), prev(OFF_C_X),
                  full((CONV_A, D_A)), full((1, D_A)), full((1, D_A)), full((1, D_A)),
                  full((CONV_C, D_C))],
        out_specs=[pl.BlockSpec((tt, D_A), lambda b, t: (b * nt + t, 0)),
                   pl.BlockSpec((tt, D_C), lambda b, t: (b * nt + t, 0)),
                   pl.BlockSpec((1, CONV_A - 1, D_A), lambda b, t: (b, 0, 0)),
                   pl.BlockSpec((1, CONV_C - 1, D_C), lambda b, t: (b, 0, 0))],
        out_shape=[jax.ShapeDtypeStruct((mp, D_A), BF16),
                   jax.ShapeDtypeStruct((mp, D_C), BF16),
                   jax.ShapeDtypeStruct((nb, CONV_A - 1, D_A), F32),
                   jax.ShapeDtypeStruct((nb, CONV_C - 1, D_C), F32)],
        scratch_shapes=[pltpu.VMEM((HALO + tt, D_A), F32), pltpu.VMEM((HALO + tt, D_C), F32),
                        pltpu.VMEM((HALO + tt, D_A), F32)],
        compiler_params=_params("parallel", "arbitrary"),
        name="ac_prompt",
    )(h, h, h, h, h, h, h, h, h, wa, ba, lg, lb, wc)


def _ac_sample_kernel(av_ref, ag_ref, cb_ref, cc_ref, cx_ref, hista_ref, histc_ref,
                      wa_ref, ba_ref, lg_ref, lb_ref, wc_ref,
                      ya_ref, yc_ref, ha_ref, hc_ref, xa_s, xc_s):
    sb, s_len = hista_ref.shape[0], av_ref.shape[0] // hista_ref.shape[0]
    ka, kc = CONV_A - 1, CONV_C - 1
    glu = av_ref[...] * jax.nn.sigmoid(ag_ref[...])
    xa_s[:, 0:ka, :] = hista_ref[...]
    xa_s[:, ka:ka + s_len, :] = glu.reshape(sb, s_len, D_A)
    acc = jnp.zeros((sb, s_len, D_A), F32) + ba_ref[...][None]
    for j in range(CONV_A):
        acc = acc + wa_ref[j:j + 1, :][None] * xa_s[:, j:j + s_len, :]
    ya = _silu(_layer_norm(acc, lg_ref[...][None], lb_ref[...][None]))
    ya_ref[...] = ya.reshape(sb * s_len, D_A).astype(ya_ref.dtype)
    ha_ref[...] = xa_s[:, s_len:s_len + ka, :]

    xc_s[:, 0:kc, :] = histc_ref[...]
    xc_s[:, kc:kc + s_len, :] = (cc_ref[...] * cx_ref[...]).reshape(sb, s_len, D_C)
    yc = jnp.zeros((sb, s_len, D_C), F32)
    for j in range(CONV_C):
        yc = yc + wc_ref[j:j + 1, :][None] * xc_s[:, j:j + s_len, :]
    yc = cb_ref[...] * yc.reshape(sb * s_len, D_C)
    yc_ref[...] = yc.astype(yc_ref.dtype)
    hc_ref[...] = xc_s[:, s_len:s_len + kc, :]


def _ac_sample(h, row0, nseq, s_len, hist_a, hist_c, wa, ba, lg, lb, wc):
    sb = _tile(nseq, 16)
    rows = sb * s_len
    blk0 = row0 // rows
    cw = D_A

    def cur(col):
        return pl.BlockSpec((rows, cw), lambda i: (blk0 + i, col // cw))

    def full(shape):
        return pl.BlockSpec(shape, lambda i: (0,) * len(shape))

    ms = nseq * s_len
    ka, kc = CONV_A - 1, CONV_C - 1
    return pl.pallas_call(
        _ac_sample_kernel,
        grid=(nseq // sb,),
        in_specs=[cur(OFF_A_VAL), cur(OFF_A_GATE), cur(OFF_C_B), cur(OFF_C_C), cur(OFF_C_X),
                  pl.BlockSpec((sb, ka, D_A), lambda i: (i, 0, 0)),
                  pl.BlockSpec((sb, kc, D_C), lambda i: (i, 0, 0)),
                  full((CONV_A, D_A)), full((1, D_A)), full((1, D_A)), full((1, D_A)),
                  full((CONV_C, D_C))],
        out_specs=[pl.BlockSpec((rows, D_A), lambda i: (i, 0)),
                   pl.BlockSpec((rows, D_C), lambda i: (i, 0)),
                   pl.BlockSpec((sb, ka, D_A), lambda i: (i, 0, 0)),
                   pl.BlockSpec((sb, kc, D_C), lambda i: (i, 0, 0))],
        out_shape=[jax.ShapeDtypeStruct((ms, D_A), BF16),
                   jax.ShapeDtypeStruct((ms, D_C), BF16),
                   jax.ShapeDtypeStruct((nseq, ka, D_A), F32),
                   jax.ShapeDtypeStruct((nseq, kc, D_C), F32)],
        scratch_shapes=[pltpu.VMEM((sb, ka + s_len + 2, D_A), F32),
                        pltpu.VMEM((sb, kc + s_len + 6, D_C), F32)],
        compiler_params=_params("parallel"),
        name="ac_sample",
    )(h, h, h, h, h, hist_a, hist_c, wa, ba, lg, lb, wc)


def _rope_tables(pos):
    half = HEAD_DIM // 2
    inv_freq = jnp.power(ROPE_THETA, -jnp.arange(half, dtype=F32) * (2.0 / HEAD_DIM))
    ang = pos.astype(F32)[:, None] * inv_freq[None, :]
    cos, sin = jnp.cos(ang), jnp.sin(ang)
    cos_t = jnp.concatenate([cos, cos, cos, cos], axis=1)
    sin_t = jnp.concatenate([-sin, sin, -sin, sin], axis=1)
    return cos_t, sin_t


def _rope(x, cos_t, sin_t):
    half = HEAD_DIM // 2
    axis = x.ndim - 1
    shape = x.shape[:-1] + (LANES,)
    lane = lax.broadcasted_iota(jnp.int32, shape, axis)
    first = (lane % HEAD_DIM) < half
    out = []
    for i in range(x.shape[-1] // LANES):
        xi = x[..., LANES * i:LANES * (i + 1)]
        partner = jnp.where(first, pltpu.roll(xi, LANES - half, axis), pltpu.roll(xi, half, axis))
        out.append(xi * cos_t + partner * sin_t)
    return out


def _head_halves(t, axis, lo_valid):
    lane = lax.broadcasted_iota(jnp.int32, t.shape, axis)
    if lo_valid:
        lo = jnp.where(lane < HEAD_DIM, t, jnp.zeros_like(t))
        hi = pltpu.roll(lo, HEAD_DIM, axis)
    else:
        hi = jnp.where(lane >= HEAD_DIM, t, jnp.zeros_like(t))
        lo = pltpu.roll(hi, HEAD_DIM, axis)
    return lo, hi


def _swa_block(sinks_ref, q_tiles, kp_tiles, kc_tiles, vp, vc, has_prev, write):
    w = WINDOW
    ci = lax.broadcasted_iota(jnp.int32, (2 * w, 2 * w), 0)
    qi = lax.broadcasted_iota(jnp.int32, (2 * w, 2 * w), 1) % w
    mask = (ci > qi) & (ci <= qi + w) & (has_prev | (ci >= w))
    lane_first = lax.broadcasted_iota(jnp.int32, (1, 2 * w), 1) < w

    qt_tiles = [t.T.astype(BF16) for t in q_tiles]
    zeros_t = jnp.zeros((HEAD_DIM, 2 * w), BF16)

    for h in range(N_KV):
        tile, lo_valid = h // 2, (h % 2 == 0)
        lanes = slice(LANES * tile, LANES * (tile + 1))
        kcat = jnp.concatenate([kp_tiles[tile], kc_tiles[tile]], axis=0)
        k_lo, k_hi = _head_halves(kcat.astype(BF16), 1, lo_valid)
        vt = jnp.concatenate([vp[:, lanes], vc[:, lanes]], axis=0).T.astype(BF16)
        vt = vt[0:HEAD_DIM] if lo_valid else vt[HEAD_DIM:2 * HEAD_DIM]
        vt_halves = (jnp.concatenate([vt, zeros_t], axis=0), jnp.concatenate([zeros_t, vt], axis=0))
        qst = jnp.concatenate([qt_tiles[2 * h], qt_tiles[2 * h + 1]], axis=1)
        ot = None
        for half_idx, k_half in enumerate((k_lo, k_hi)):
            s = jnp.dot(k_half, qst, preferred_element_type=F32)
            s = jnp.where(mask, s, -jnp.inf)
            sk = jnp.where(lane_first, sinks_ref[4 * h + half_idx], sinks_ref[4 * h + 2 + half_idx])
            m = jnp.maximum(jnp.max(s, axis=0, keepdims=True), sk)
            e = jnp.exp(s - m)
            den = jnp.sum(e, axis=0, keepdims=True) + jnp.exp(sk - m)
            p = (e / den).astype(BF16)
            term = jnp.dot(vt_halves[half_idx], p, preferred_element_type=F32)
            ot = term if ot is None else ot + term
        write(2 * h, ot[:, 0:w].T)
        write(2 * h + 1, ot[:, w:2 * w].T)


def _attn_prompt_kernel(sinks_ref, q_ref, kc_ref, vc_ref, kp_ref, vp_ref,
                        cos_ref, sin_ref, cosp_ref, sinp_ref,
                        yb_ref, nk_ref, nv_ref):
    n = pl.program_id(1)
    w = WINDOW
    cos_c, sin_c = cos_ref[...], sin_ref[...]
    q_tiles = _rope(q_ref[...] * (HEAD_DIM ** -0.5), cos_c, sin_c)
    kc_tiles = _rope(kc_ref[...], cos_c, sin_c)
    kp_tiles = _rope(kp_ref[...], cosp_ref[...], sinp_ref[...])
    for i in range(KV_WIDTH // LANES):
        nk_ref[0, :, LANES * i:LANES * (i + 1)] = kc_tiles[i][w:2 * w]
    nv_ref[0] = vc_ref[w:2 * w, :]

    for j in range(2):
        rows = slice(j * w, (j + 1) * w)

        def write(tile, value, rows=rows):
            yb_ref[rows, LANES * tile:LANES * (tile + 1)] = value.astype(yb_ref.dtype)

        if j == 0:
            prev_k, prev_v, has_prev = kp_tiles, vp_ref[...], n > 0
        else:
            prev_k, prev_v, has_prev = [t[0:w] for t in kc_tiles], vc_ref[0:w, :], True
        _swa_block(sinks_ref, [t[rows] for t in q_tiles], prev_k, [t[rows] for t in kc_tiles],
                   prev_v, vc_ref[rows, :], has_prev, write)


def _attn_prompt(h, nb, t_len, sinks, cos_t, sin_t):
    w = WINDOW
    nblk = t_len // w
    assert nblk % 2 == 0
    npair = nblk // 2
    mp = nb * t_len

    def pair(b, n):
        return b * npair + n

    def halo(b, n):
        return b * nblk + jnp.maximum(2 * n - 1, 0)

    return pl.pallas_call(
        _attn_prompt_kernel,
        grid=(nb, npair),
        in_specs=[pl.BlockSpec(memory_space=pltpu.SMEM),
                  pl.BlockSpec((2 * w, Q_WIDTH), lambda b, n: (pair(b, n), OFF_Q // Q_WIDTH)),
                  pl.BlockSpec((2 * w, KV_WIDTH), lambda b, n: (pair(b, n), OFF_K // KV_WIDTH)),
                  pl.BlockSpec((2 * w, KV_WIDTH), lambda b, n: (pair(b, n), OFF_V // KV_WIDTH)),
                  pl.BlockSpec((w, KV_WIDTH), lambda b, n: (halo(b, n), OFF_K // KV_WIDTH)),
                  pl.BlockSpec((w, KV_WIDTH), lambda b, n: (halo(b, n), OFF_V // KV_WIDTH)),
                  pl.BlockSpec((2 * w, LANES), lambda b, n: (n, 0)),
                  pl.BlockSpec((2 * w, LANES), lambda b, n: (n, 0)),
                  pl.BlockSpec((w, LANES), lambda b, n: (jnp.maximum(2 * n - 1, 0), 0)),
                  pl.BlockSpec((w, LANES), lambda b, n: (jnp.maximum(2 * n - 1, 0), 0))],
        out_specs=[pl.BlockSpec((2 * w, Q_WIDTH), lambda b, n: (pair(b, n), 0)),
                   pl.BlockSpec((1, w, KV_WIDTH), lambda b, n: (b, 0, 0)),
                   pl.BlockSpec((1, w, KV_WIDTH), lambda b, n: (b, 0, 0))],
        out_shape=[jax.ShapeDtypeStruct((mp, Q_WIDTH), BF16),
                   jax.ShapeDtypeStruct((nb, w, KV_WIDTH), F32),
                   jax.ShapeDtypeStruct((nb, w, KV_WIDTH), F32)],
        compiler_params=_params("parallel", "arbitrary"),
        name="attn_prompt",
    )(sinks, h, h, h, h, h, cos_t, sin_t, cos_t, sin_t)


def _attn_sample_kernel(sinks_ref, q_ref, kn_ref, vn_ref, kbuf_ref, vbuf_ref, cos_ref, sin_ref,
                        yb_ref, nk_ref, nv_ref):
    sb, w = kbuf_ref.shape[0], kbuf_ref.shape[1]
    s_len = q_ref.shape[0] // sb
    cos_t, sin_t = cos_ref[...], sin_ref[...]
    q_tiles = _rope(q_ref[...] * (HEAD_DIM ** -0.5), cos_t, sin_t)
    kn_tiles = _rope(kn_ref[...], cos_t, sin_t)
    nk_ref[:, 0:w - s_len, :] = kbuf_ref[:, s_len:w, :]
    nv_ref[:, 0:w - s_len, :] = vbuf_ref[:, s_len:w, :]
    for i in range(KV_WIDTH // LANES):
        nk_ref[:, w - s_len:w, LANES * i:LANES * (i + 1)] = kn_tiles[i].reshape(sb, s_len, LANES)
    nv_ref[:, w - s_len:w, :] = vn_ref[...].reshape(sb, s_len, KV_WIDTH)

    qi = lax.broadcasted_iota(jnp.int32, (2 * s_len, w), 0) % s_len
    ci = lax.broadcasted_iota(jnp.int32, (2 * s_len, w), 1)
    mask_buf = (ci > qi)[None]
    qn = lax.broadcasted_iota(jnp.int32, (2 * s_len, s_len), 0) % s_len
    cn = lax.broadcasted_iota(jnp.int32, (2 * s_len, s_len), 1)
    mask_new = (cn <= qn)[None]
    row_top = (lax.broadcasted_iota(jnp.int32, (2 * s_len, 1), 0) < s_len)[None]
    bqk = (((2,), (2,)), ((0,), (0,)))
    bkd = (((2,), (1,)), ((0,), (0,)))

    for h in range(N_KV):
        tile, lo_valid = h // 2, (h % 2 == 0)
        sl = slice(LANES * tile, LANES * (tile + 1))
        kb_lo, kb_hi = _head_halves(kbuf_ref[:, :, sl].astype(BF16), 2, lo_valid)
        vb_lo, vb_hi = _head_halves(vbuf_ref[:, :, sl].astype(BF16), 2, lo_valid)
        kn3 = kn_tiles[tile].reshape(sb, s_len, LANES).astype(BF16)
        vn3 = vn_ref[:, sl].reshape(sb, s_len, LANES).astype(BF16)
        kn_lo, kn_hi = _head_halves(kn3, 2, lo_valid)
        vn_lo, vn_hi = _head_halves(vn3, 2, lo_valid)
        qs = jnp.concatenate([q_tiles[2 * h].reshape(sb, s_len, LANES),
                              q_tiles[2 * h + 1].reshape(sb, s_len, LANES)], axis=1).astype(BF16)
        o = jnp.zeros((sb, 2 * s_len, LANES), F32)
        for half_idx, (kb, kn, vb, vn) in enumerate(((kb_lo, kn_lo, vb_lo, vn_lo),
                                                     (kb_hi, kn_hi, vb_hi, vn_hi))):
            s_b = lax.dot_general(qs, kb, bqk, preferred_element_type=F32)
            s_n = lax.dot_general(qs, kn, bqk, preferred_element_type=F32)
            s_b = jnp.where(mask_buf, s_b, -jnp.inf)
            s_n = jnp.where(mask_new, s_n, -jnp.inf)
            sk = jnp.where(row_top, sinks_ref[4 * h + half_idx], sinks_ref[4 * h + 2 + half_idx])
            m = jnp.maximum(jnp.maximum(jnp.max(s_b, axis=2, keepdims=True),
                                        jnp.max(s_n, axis=2, keepdims=True)), sk)
            e_b = jnp.exp(s_b - m)
            e_n = jnp.exp(s_n - m)
            den = (jnp.sum(e_b, axis=2, keepdims=True) + jnp.sum(e_n, axis=2, keepdims=True)
                   + jnp.exp(sk - m))
            o = o + lax.dot_general((e_b / den).astype(BF16), vb, bkd, preferred_element_type=F32)
            o = o + lax.dot_general((e_n / den).astype(BF16), vn, bkd, preferred_element_type=F32)
        yb_ref[:, LANES * (2 * h):LANES * (2 * h + 1)] = (
            o[:, 0:s_len, :].reshape(sb * s_len, LANES).astype(yb_ref.dtype))
        yb_ref[:, LANES * (2 * h + 1):LANES * (2 * h + 2)] = (
            o[:, s_len:2 * s_len, :].reshape(sb * s_len, LANES).astype(yb_ref.dtype))


def _attn_sample(h, row0, nseq, s_len, sinks, k_buf, v_buf, cos_t, sin_t):
    sb = _tile(nseq, 8)
    rows = sb * s_len
    blk0 = row0 // rows
    w = k_buf.shape[1]
    ms = nseq * s_len
    cos_rows = jnp.tile(cos_t, (sb, 1))
    sin_rows = jnp.tile(sin_t, (sb, 1))
    return pl.pallas_call(
        _attn_sample_kernel,
        grid=(nseq // sb,),
        in_specs=[pl.BlockSpec(memory_space=pltpu.SMEM),
                  pl.BlockSpec((rows, Q_WIDTH), lambda i: (blk0 + i, OFF_Q // Q_WIDTH)),
                  pl.BlockSpec((rows, KV_WIDTH), lambda i: (blk0 + i, OFF_K // KV_WIDTH)),
                  pl.BlockSpec((rows, KV_WIDTH), lambda i: (blk0 + i, OFF_V // KV_WIDTH)),
                  pl.BlockSpec((sb, w, KV_WIDTH), lambda i: (i, 0, 0)),
                  pl.BlockSpec((sb, w, KV_WIDTH), lambda i: (i, 0, 0)),
                  pl.BlockSpec((rows, LANES), lambda i: (0, 0)),
                  pl.BlockSpec((rows, LANES), lambda i: (0, 0))],
        out_specs=[pl.BlockSpec((rows, Q_WIDTH), lambda i: (i, 0)),
                   pl.BlockSpec((sb, w, KV_WIDTH), lambda i: (i, 0, 0)),
                   pl.BlockSpec((sb, w, KV_WIDTH), lambda i: (i, 0, 0))],
        out_shape=[jax.ShapeDtypeStruct((ms, Q_WIDTH), BF16),
                   jax.ShapeDtypeStruct((nseq, w, KV_WIDTH), F32),
                   jax.ShapeDtypeStruct((nseq, w, KV_WIDTH), F32)],
        compiler_params=_params("parallel"),
        name="attn_sample",
    )(sinks, h, h, h, k_buf, v_buf, cos_rows, sin_rows)


def _d_prompt_kernel(du_ref, dv_ref, lg_ref, lb_ref, w_ref, bt_ref, yd_ref):
    c = CHUNK
    ri = lax.broadcasted_iota(jnp.int32, (c, c), 0)
    cj = lax.broadcasted_iota(jnp.int32, (c, c), 1)
    causal = cj <= ri
    w_causal = [jnp.where(causal, w_ref[g], 0.0).astype(BF16) for g in range(N_GROUPS_D)]
    for k in range(du_ref.shape[0] // c):
        rows = slice(k * c, (k + 1) * c)
        u = _gelu(du_ref[rows, :])
        vn = _layer_norm(_gelu(dv_ref[rows, :]), lg_ref[...], lb_ref[...]).astype(BF16)
        for g in range(N_GROUPS_D):
            sl = slice(GROUP_D * g, GROUP_D * (g + 1))
            s = jnp.dot(w_causal[g], vn[:, sl], preferred_element_type=F32) + bt_ref[:, g:g + 1]
            yd_ref[rows, sl] = (u[:, sl] * s).astype(yd_ref.dtype)


def _d_prompt(h, mp, ln_g, ln_b, sgu_w, sgu_bt):
    c = CHUNK
    rows = _tile(mp, 4 * c)
    assert rows % c == 0
    return pl.pallas_call(
        _d_prompt_kernel,
        grid=(mp // rows,),
        in_specs=[pl.BlockSpec((rows, D_D), lambda i: (i, OFF_D_U // D_D)),
                  pl.BlockSpec((rows, D_D), lambda i: (i, OFF_D_V // D_D)),
                  pl.BlockSpec((1, D_D), lambda i: (0, 0)),
                  pl.BlockSpec((1, D_D), lambda i: (0, 0)),
                  pl.BlockSpec((N_GROUPS_D, c, c), lambda i: (0, 0, 0)),
                  pl.BlockSpec((c, N_GROUPS_D), lambda i: (0, 0))],
        out_specs=pl.BlockSpec((rows, D_D), lambda i: (i, 0)),
        out_shape=jax.ShapeDtypeStruct((mp, D_D), BF16),
        compiler_params=_params("parallel"),
        name="d_prompt",
    )(h, h, ln_g, ln_b, sgu_w, sgu_bt)


def _d_sample_kernel(du_ref, dv_ref, lg_ref, lb_ref, wt_ref, bt_ref, yd_ref, vd_ref):
    sb, s_len = vd_ref.shape[0], vd_ref.shape[1]
    u = _gelu(du_ref[...])
    vn = _layer_norm(_gelu(dv_ref[...]), lg_ref[...], lb_ref[...])
    vn3 = vn.reshape(sb, s_len, D_D)
    vd_ref[...] = vn3
    ii = lax.broadcasted_iota(jnp.int32, (s_len, D_D), 0)
    s = jnp.zeros((sb, s_len, D_D), F32) + bt_ref[...][None]
    for j in range(s_len):
        wj = jnp.where(ii >= j, wt_ref[j], 0.0)
        s = s + wj[None] * vn3[:, j:j + 1, :]
    yd_ref[...] = (u * s.reshape(sb * s_len, D_D)).astype(yd_ref.dtype)


def _d_sample(h, row0, nseq, s_len, ln_g, ln_b, sgu_wt, sgu_bs):
    sb = _tile(nseq, 16)
    rows = sb * s_len
    blk0 = row0 // rows
    ms = nseq * s_len
    return pl.pallas_call(
        _d_sample_kernel,
        grid=(nseq // sb,),
        in_specs=[pl.BlockSpec((rows, D_D), lambda i: (blk0 + i, OFF_D_U // D_D)),
                  pl.BlockSpec((rows, D_D), lambda i: (blk0 + i, OFF_D_V // D_D)),
                  pl.BlockSpec((1, D_D), lambda i: (0, 0)),
                  pl.BlockSpec((1, D_D), lambda i: (0, 0)),
                  pl.BlockSpec((s_len, s_len, D_D), lambda i: (0, 0, 0)),
                  pl.BlockSpec((s_len, D_D), lambda i: (0, 0))],
        out_specs=[pl.BlockSpec((rows, D_D), lambda i: (i, 0)),
                   pl.BlockSpec((sb, s_len, D_D), lambda i: (i, 0, 0))],
        out_shape=[jax.ShapeDtypeStruct((ms, D_D), BF16),
                   jax.ShapeDtypeStruct((nseq, s_len, D_D), F32)],
        compiler_params=_params("parallel"),
        name="d_sample",
    )(h, h, ln_g, ln_b, sgu_wt, sgu_bs)


def _merge_kernel(x_ref, yap_ref, ybp_ref, ycp_ref, ydp_ref, yas_ref, ybs_ref, ycs_ref, yds_ref,
                  g0_ref, g1_ref, g2_ref, g3_ref, bg_ref, pa_ref, pb_ref, pc_ref, pd_ref, o_ref,
                  *, prompt_tiles):
    x = x_ref[...]
    is_prompt = pl.program_id(1) < prompt_tiles
    acc = None
    branches = ((yap_ref, yas_ref, g0_ref, pa_ref), (ybp_ref, ybs_ref, g1_ref, pb_ref),
                (ycp_ref, ycs_ref, g2_ref, pc_ref), (ydp_ref, yds_ref, g3_ref, pd_ref))
    for i, (yp_ref, ys_ref, g_ref, p_ref) in enumerate(branches):
        y = jnp.where(is_prompt, yp_ref[...], ys_ref[...])
        gate = jax.nn.sigmoid(jnp.dot(x, g_ref[...], preferred_element_type=F32) + bg_ref[i:i + 1, :])
        term = gate * jnp.dot(y, p_ref[...], preferred_element_type=F32)
        acc = term if acc is None else acc + term
    o_ref[...] = acc.astype(o_ref.dtype)


def _merge(xb, y_prompt, y_sample, w_in_b, b_gate, projs):
    m = xb.shape[0]
    mp, ms = y_prompt[0].shape[0], y_sample[0].shape[0]
    tm = _tile(np.gcd(mp, ms), 512)
    npt = mp // tm
    tn = 512
    nn = D_MODEL // tn
    widths = (D_A, Q_WIDTH, D_C, D_D)

    def act_p(width):
        return pl.BlockSpec((tm, width), lambda j, i: (jnp.minimum(i, npt - 1), 0))

    def act_s(width):
        return pl.BlockSpec((tm, width), lambda j, i: (jnp.maximum(i - npt, 0), 0))

    def gate_w(b):
        return pl.BlockSpec((D_MODEL, tn), lambda j, i: (0, (OFF_GATES + b * D_MODEL) // tn + j))

    def proj_w(k):
        return pl.BlockSpec((k, tn), lambda j, i: (0, j))

    return pl.pallas_call(
        functools.partial(_merge_kernel, prompt_tiles=npt),
        grid=(nn, m // tm),
        in_specs=[pl.BlockSpec((tm, D_MODEL), lambda j, i: (i, 0))]
                 + [act_p(wd) for wd in widths] + [act_s(wd) for wd in widths]
                 + [gate_w(0), gate_w(1), gate_w(2), gate_w(3),
                    pl.BlockSpec((N_BRANCH, tn), lambda j, i: (0, j))]
                 + [proj_w(wd) for wd in widths],
        out_specs=pl.BlockSpec((tm, tn), lambda j, i: (i, j)),
        out_shape=jax.ShapeDtypeStruct((m, D_MODEL), BF16),
        compiler_params=_params("parallel", "arbitrary"),
        name="merge",
    )(xb, *y_prompt, *y_sample, w_in_b, w_in_b, w_in_b, w_in_b, b_gate, *projs)


def _out_ln_kernel(mg_ref, *refs, prompt_tiles):
    if prompt_tiles is None:
        x_ref, w_ref, g_ref, b_ref, o_ref, ob_ref = refs
    else:
        xp_ref, xs_ref, w_ref, g_ref, b_ref, o_ref, ob_ref = refs
        is_prompt = pl.program_id(0) < prompt_tiles
    tm = mg_ref.shape[0]
    sub = tm // OUT_LN_SUBTILES
    for k in range(OUT_LN_SUBTILES):
        rows = slice(k * sub, (k + 1) * sub)
        if prompt_tiles is None:
            x = x_ref[rows, :]
        else:
            x = jnp.where(is_prompt, xp_ref[rows, :], xs_ref[rows, :])
        y = ALPHA * x + jnp.dot(mg_ref[rows, :], w_ref[...], preferred_element_type=F32)
        y = _layer_norm(y, g_ref[...], b_ref[...])
        o_ref[rows, :] = y
        ob_ref[rows, :] = y.astype(ob_ref.dtype)


def _out_ln(merged, x, w_out_b, g, b):
    m = merged.shape[0]
    if isinstance(x, tuple):
        mp, ms = x[0].shape[0], x[1].shape[0]
        tm = _tile(np.gcd(mp, ms), 512)
        npt = mp // tm
        x_specs = [pl.BlockSpec((tm, D_MODEL), lambda i: (jnp.minimum(i, npt - 1), 0)),
                   pl.BlockSpec((tm, D_MODEL), lambda i: (jnp.maximum(i - npt, 0), 0))]
    else:
        tm = _tile(m, 512)
        npt = None
        x, x_specs = (x,), [pl.BlockSpec((tm, D_MODEL), lambda i: (i, 0))]
    return pl.pallas_call(
        functools.partial(_out_ln_kernel, prompt_tiles=npt),
        grid=(m // tm,),
        in_specs=[pl.BlockSpec((tm, D_MODEL), lambda i: (i, 0))] + x_specs
                 + [pl.BlockSpec((D_MODEL, D_MODEL), lambda i: (0, 0), pipeline_mode=pl.Buffered(1)),
                    pl.BlockSpec((1, D_MODEL), lambda i: (0, 0)),
                    pl.BlockSpec((1, D_MODEL), lambda i: (0, 0))],
        out_specs=[pl.BlockSpec((tm, D_MODEL), lambda i: (i, 0)),
                   pl.BlockSpec((tm, D_MODEL), lambda i: (i, 0))],
        out_shape=[jax.ShapeDtypeStruct((m, D_MODEL), F32),
                   jax.ShapeDtypeStruct((m, D_MODEL), BF16)],
        compiler_params=_params("parallel"),
        name="out_ln",
    )(merged, *x, w_out_b, g, b)


def _swiglu_partial(xb, wg_refs, wu_refs, wd_refs):
    kb = xb.shape[1] // W_BANDS
    gate = up = None
    for j in range(W_BANDS):
        xs = xb[:, j * kb:(j + 1) * kb]
        gj = jnp.dot(xs, wg_refs[j][...], preferred_element_type=F32)
        uj = jnp.dot(xs, wu_refs[j][...], preferred_element_type=F32)
        gate, up = (gj, uj) if gate is None else (gate + gj, up + uj)
    hid = (_silu(gate) * up).astype(BF16)
    fb = hid.shape[1] // W_BANDS
    down = None
    for j in range(W_BANDS):
        dj = jnp.dot(hid[:, j * fb:(j + 1) * fb], wd_refs[j][...], preferred_element_type=F32)
        down = dj if down is None else down + dj
    return down


def _band_specs(lead, tf, col_index, row_index):
    kb, fb = D_MODEL // W_BANDS, tf // W_BANDS

    def col(j):
        def index(*args):
            *head, f = col_index(*args)
            return (*head, j, f)
        return pl.BlockSpec((*lead, kb, tf), index)

    def row(j):
        def index(*args):
            *head, f = row_index(*args)
            return (*head, W_BANDS * f + j, 0)
        return pl.BlockSpec((*lead, fb, D_MODEL), index)

    return ([col(j) for j in range(W_BANDS)] + [col(j) for j in range(W_BANDS)]
            + [row(j) for j in range(W_BANDS)])


def _ffn_kernel(x_ref, xb_ref, *refs):
    wg_refs, wu_refs, wd_refs = (refs[0:W_BANDS], refs[W_BANDS:2 * W_BANDS],
                                 refs[2 * W_BANDS:3 * W_BANDS])
    g_ref, b_ref, o_ref, ob_ref, acc_s = refs[3 * W_BANDS:]
    f = pl.program_id(1)
    last = pl.num_programs(1) - 1

    def partial_down():
        return _swiglu_partial(xb_ref[...], wg_refs, wu_refs, wd_refs)

    @pl.when(f == 0)
    def _():
        acc_s[...] = partial_down()

    @pl.when((f > 0) & (f < last))
    def _():
        acc_s[...] += partial_down()

    @pl.when(f == last)
    def _():
        acc_s[...] += partial_down()
        n_sub = max(k for k in (1, 2, OUT_LN_SUBTILES) if acc_s.shape[0] % (16 * k) == 0)
        sub = acc_s.shape[0] // n_sub
        for k in range(n_sub):
            rows = slice(k * sub, (k + 1) * sub)
            y = _layer_norm(ALPHA * x_ref[rows, :] + acc_s[rows, :], g_ref[...], b_ref[...])
            o_ref[rows, :] = y
            ob_ref[rows, :] = y.astype(ob_ref.dtype)


def _ffn_ln(x, xb, wg, wu, wd, g, b):
    m = x.shape[0]
    tm = _tile(m, FFN_ROWS)
    tf = 512
    return pl.pallas_call(
        _ffn_kernel,
        grid=(m // tm, D_FF // tf),
        in_specs=[pl.BlockSpec((tm, D_MODEL), lambda i, f: (i, 0)),
                  pl.BlockSpec((tm, D_MODEL), lambda i, f: (i, 0))]
                 + _band_specs((), tf, lambda i, f: (f,), lambda i, f: (f,))
                 + [pl.BlockSpec((1, D_MODEL), lambda i, f: (0, 0)),
                    pl.BlockSpec((1, D_MODEL), lambda i, f: (0, 0))],
        out_specs=[pl.BlockSpec((tm, D_MODEL), lambda i, f: (i, 0)),
                   pl.BlockSpec((tm, D_MODEL), lambda i, f: (i, 0))],
        out_shape=[jax.ShapeDtypeStruct((m, D_MODEL), F32),
                   jax.ShapeDtypeStruct((m, D_MODEL), BF16)],
        scratch_shapes=[pltpu.VMEM((tm, D_MODEL), F32)],
        compiler_params=_params("parallel", "arbitrary"),
        name="ffn_ln",
    )(x, xb, *([wg] * W_BANDS), *([wu] * W_BANDS), *([wd] * W_BANDS), g, b)


def _router_kernel(x_ref, w_ref, b_ref, idx_ref, prob_ref, rank_ref, cnt_ref, run_s):
    @pl.when(pl.program_id(0) == 0)
    def _():
        run_s[...] = jnp.zeros_like(run_s)

    x, w = x_ref[...], w_ref[...]
    x_hi, w_hi = x.astype(BF16), w.astype(BF16)
    x_lo = (x - x_hi.astype(F32)).astype(BF16)
    w_lo = (w - w_hi.astype(F32)).astype(BF16)
    logits = (jnp.dot(x_hi, w_hi, preferred_element_type=F32)
              + (jnp.dot(x_lo, w_hi, preferred_element_type=F32)
                 + jnp.dot(x_hi, w_lo, preferred_element_type=F32))) + b_ref[...]
    tm = logits.shape[0]
    idx = lax.broadcasted_iota(jnp.int32, logits.shape, 1)
    v1 = jnp.max(logits, axis=1, keepdims=True)
    i1 = jnp.min(jnp.where(logits == v1, idx, N_EXPERTS), axis=1, keepdims=True)
    rest = jnp.where(idx == i1, -jnp.inf, logits)
    v2 = jnp.max(rest, axis=1, keepdims=True)
    i2 = jnp.min(jnp.where(rest == v2, idx, N_EXPERTS), axis=1, keepdims=True)
    e2 = jnp.exp(v2 - v1)
    den = 1.0 + e2

    hit = ((idx == i1) | (idx == i2)).astype(F32)
    ri = lax.broadcasted_iota(jnp.int32, (tm, tm), 0)
    ci = lax.broadcasted_iota(jnp.int32, (tm, tm), 1)
    before = jnp.dot((ci < ri).astype(BF16), hit.astype(BF16), preferred_element_type=F32) + run_s[...]
    rank1 = jnp.sum(jnp.where(idx == i1, before, 0.0), axis=1, keepdims=True)
    rank2 = jnp.sum(jnp.where(idx == i2, before, 0.0), axis=1, keepdims=True)

    col = lax.broadcasted_iota(jnp.int32, (tm, 2), 1)
    idx_ref[...] = jnp.where(col == 0, i1, i2)
    prob_ref[...] = jnp.where(col == 0, 1.0 / den, e2 / den)
    rank_ref[...] = jnp.where(col == 0, rank1, rank2).astype(jnp.int32)
    total = run_s[...] + jnp.sum(hit, axis=0, keepdims=True)
    run_s[...] = total
    cnt_ref[...] = total.astype(jnp.int32)


def _router(x, router_w, router_b):
    m = x.shape[0]
    tm = _tile(m, 512)
    return pl.pallas_call(
        _router_kernel,
        grid=(m // tm,),
        in_specs=[pl.BlockSpec((tm, D_MODEL), lambda i: (i, 0)),
                  pl.BlockSpec((D_MODEL, N_EXPERTS), lambda i: (0, 0)),
                  pl.BlockSpec((1, N_EXPERTS), lambda i: (0, 0))],
        out_specs=[pl.BlockSpec((tm, 2), lambda i: (i, 0)),
                   pl.BlockSpec((tm, 2), lambda i: (i, 0)),
                   pl.BlockSpec((tm, 2), lambda i: (i, 0)),
                   pl.BlockSpec((1, N_EXPERTS), lambda i: (0, 0))],
        out_shape=[jax.ShapeDtypeStruct((m, 2), jnp.int32),
                   jax.ShapeDtypeStruct((m, 2), F32),
                   jax.ShapeDtypeStruct((m, 2), jnp.int32),
                   jax.ShapeDtypeStruct((1, N_EXPERTS), jnp.int32)],
        scratch_shapes=[pltpu.VMEM((1, N_EXPERTS), F32)],
        compiler_params=_params("arbitrary"),
        name="router",
    )(x, router_w, router_b)


def _route_tables(idx, rank, counts, tm, n_tiles):
    m = idx.shape[0]
    padded = ((counts + tm - 1) // tm) * tm
    ends = jnp.cumsum(padded)
    base = ends - padded
    pos = base[idx] + rank
    n_valid = ends[-1] // tm
    tiles = jnp.arange(n_tiles, dtype=jnp.int32)
    tile_valid = (tiles < n_valid).astype(jnp.int32)
    first_row = jnp.minimum(tiles, n_valid - 1) * tm
    owner = jnp.sum((ends[None, :] <= first_row[:, None]).astype(jnp.int32), axis=1)
    tile_expert = jnp.minimum(owner, N_EXPERTS - 1).astype(jnp.int32)
    flat = pos.reshape(-1)
    token = jnp.repeat(jnp.arange(m, dtype=jnp.int32), 2)
    src = jnp.zeros((n_tiles * tm,), jnp.int32).at[flat].set(token, unique_indices=True)
    return pos, src, tile_expert, tile_valid


def _row_copy(src_hbm, row, dst_vmem, dst_row, sem):
    return pltpu.make_async_copy(src_hbm.at[pl.ds(row, 1), :], dst_vmem.at[pl.ds(dst_row, 1), :], sem)


def _moe_group_kernel(te_ref, tv_ref, src_ref, src_next_ref, x_hbm, *refs):
    wg_refs, wu_refs, wd_refs = (refs[0:W_BANDS], refs[W_BANDS:2 * W_BANDS],
                                 refs[2 * W_BANDS:3 * W_BANDS])
    o_ref, xg_s, xb_s, sem = refs[3 * W_BANDS:]
    i, f = pl.program_id(0), pl.program_id(1)
    nf = pl.num_programs(1)
    tm = xb_s.shape[0]
    chunk = tm // (MOE_F_STEPS - 1)
    slot = i % 2
    valid = tv_ref[i] == 1

    @pl.when((i == 0) & (f == 0))
    def _():
        def body(r, c):
            _row_copy(x_hbm, src_ref[0, r], xg_s.at[0], r, sem.at[0]).start()
            return c
        lax.fori_loop(0, tm, body, 0, unroll=8)

    def wait_rows():
        pltpu.make_async_copy(xg_s.at[slot], xg_s.at[slot], sem.at[slot]).wait()

    @pl.when((f == 0) & jnp.logical_not(valid))
    def _():
        o_ref[...] = jnp.zeros_like(o_ref)

        @pl.when((i == 0) | (tv_ref[jnp.maximum(i - 1, 0)] == 1))
        def _():
            wait_rows()

    def swiglu_step(first, issue_next):
        if first:
            wait_rows()
            xb = xg_s[slot].astype(BF16)
            xb_s[...] = xb
        else:
            xb = xb_s[...]
        if issue_next:
            for r in range(chunk):
                row = f * chunk + r
                _row_copy(x_hbm, src_next_ref[0, row], xg_s.at[1 - slot], row, sem.at[1 - slot]).start()
        down = _swiglu_partial(xb, wg_refs, wu_refs, wd_refs)
        if first:
            o_ref[...] = down
        else:
            o_ref[...] += down

    @pl.when(valid & (f == 0))
    def _():
        swiglu_step(True, True)

    @pl.when(valid & (f > 0) & (f < nf - 1))
    def _():
        swiglu_step(False, True)

    @pl.when(valid & (f == nf - 1))
    def _():
        swiglu_step(False, False)


def _moe_group(x, src, tile_expert, tile_valid, wg, wu, wd, tm, n_tiles):
    tf = D_FF // MOE_F_STEPS
    nf = MOE_F_STEPS

    def w_tile(i, f, te, tv):
        return (te[i], jnp.where(tv[i] == 1, f, nf - 1))

    grid_spec = pltpu.PrefetchScalarGridSpec(
        num_scalar_prefetch=2,
        grid=(n_tiles, nf),
        in_specs=[pl.BlockSpec((None, 1, tm), lambda i, f, te, tv: (i, 0, 0), memory_space=pltpu.SMEM),
                  pl.BlockSpec((None, 1, tm), lambda i, f, te, tv: (jnp.minimum(i + 1, n_tiles - 1), 0, 0),
                               memory_space=pltpu.SMEM),
                  pl.BlockSpec(memory_space=pl.ANY)]
                 + _band_specs((None,), tf, w_tile, w_tile),
        out_specs=pl.BlockSpec((tm, D_MODEL), lambda i, f, te, tv: (i, 0)),
        scratch_shapes=[pltpu.VMEM((2, tm, D_MODEL), F32),
                        pltpu.VMEM((tm, D_MODEL), BF16),
                        pltpu.SemaphoreType.DMA((2,))],
    )
    src3 = src.reshape(n_tiles, 1, tm)
    return pl.pallas_call(
        _moe_group_kernel,
        grid_spec=grid_spec,
        out_shape=jax.ShapeDtypeStruct((n_tiles * tm, D_MODEL), F32),
        compiler_params=pltpu.CompilerParams(dimension_semantics=("arbitrary", "arbitrary"),
                                             vmem_limit_bytes=MOE_VMEM_LIMIT),
        name="moe_group",
    )(tile_expert, tile_valid, src3, src3, x, *([wg] * W_BANDS), *([wu] * W_BANDS), *([wd] * W_BANDS))

def _combine_kernel(pos_ref, pos_next_ref, x_ref, p_ref, y_hbm, g_ref, b_ref, op_ref, os_ref, ybuf, sem,
                    *, prompt_tiles):
    i, nt = pl.program_id(0), pl.num_programs(0)
    tm = x_ref.shape[0]
    slot = i % 2

    def start_pair(idx_ref, s, r):
        _row_copy(y_hbm, idx_ref[0, 2 * r], ybuf.at[s, 0], r, sem.at[s]).start()
        _row_copy(y_hbm, idx_ref[0, 2 * r + 1], ybuf.at[s, 1], r, sem.at[s]).start()

    @pl.when(i == 0)
    def _():
        def body(r, c):
            start_pair(pos_ref, 0, r)
            return c
        lax.fori_loop(0, tm, body, 0, unroll=4)

    pltpu.make_async_copy(ybuf.at[slot], ybuf.at[slot], sem.at[slot]).wait()

    def norm_tile(issue_next):
        if issue_next:
            for r in range(tm):
                start_pair(pos_next_ref, 1 - slot, r)
        moe = p_ref[:, 0:1] * ybuf[slot, 0] + p_ref[:, 1:2] * ybuf[slot, 1]
        y = _layer_norm(ALPHA * x_ref[...] + moe, g_ref[...], b_ref[...])

        @pl.when(i < prompt_tiles)
        def _():
            op_ref[...] = y

        @pl.when(i >= prompt_tiles)
        def _():
            os_ref[...] = y

    @pl.when(i + 1 < nt)
    def _():
        norm_tile(True)

    @pl.when(i + 1 >= nt)
    def _():
        norm_tile(False)


def _combine_ln(x, prob, y_sorted, pos, g, b, mp):
    m = x.shape[0]
    ms = m - mp
    tm = _tile(np.gcd(mp, ms), 256)
    nt, npt = m // tm, mp // tm
    pos3 = pos.reshape(nt, 1, 2 * tm)
    return pl.pallas_call(
        functools.partial(_combine_kernel, prompt_tiles=npt),
        grid=(nt,),
        in_specs=[pl.BlockSpec((None, 1, 2 * tm), lambda i: (i, 0, 0), memory_space=pltpu.SMEM),
                  pl.BlockSpec((None, 1, 2 * tm), lambda i: (jnp.minimum(i + 1, nt - 1), 0, 0),
                               memory_space=pltpu.SMEM),
                  pl.BlockSpec((tm, D_MODEL), lambda i: (i, 0)),
                  pl.BlockSpec((tm, 2), lambda i: (i, 0)),
                  pl.BlockSpec(memory_space=pl.ANY),
                  pl.BlockSpec((1, D_MODEL), lambda i: (0, 0)),
                  pl.BlockSpec((1, D_MODEL), lambda i: (0, 0))],
        out_specs=[pl.BlockSpec((tm, D_MODEL), lambda i: (jnp.minimum(i, npt - 1), 0)),
                   pl.BlockSpec((tm, D_MODEL), lambda i: (jnp.maximum(i - npt, 0), 0))],
        out_shape=[jax.ShapeDtypeStruct((mp, D_MODEL), F32),
                   jax.ShapeDtypeStruct((ms, D_MODEL), F32)],
        scratch_shapes=[pltpu.VMEM((2, 2, tm, D_MODEL), F32), pltpu.SemaphoreType.DMA((2,))],
        compiler_params=_params("arbitrary"),
        name="combine_ln",
    )(pos3, pos3, x, prob, y_sorted, g, b)


def _moe_ln(x, mp, router_w, router_b, wg, wu, wd, g, b):
    m = x.shape[0]
    tm = MOE_ROWS
    n_tiles = -(-2 * m // tm) + N_EXPERTS
    idx, prob, rank, counts = _router(x, router_w, router_b)
    pos, src, tile_expert, tile_valid = _route_tables(idx, rank, counts[0], tm, n_tiles)
    y_sorted = _moe_group(x, src, tile_expert, tile_valid, wg, wu, wd, tm, n_tiles)
    return _combine_ln(x, prob, y_sorted, pos, g, b, mp)


def kernel(x_prompt, x_sample, state_conv_a, cache_swa_k, cache_swa_v, state_conv_c, w_in, b_gate, conv_a_w, conv_a_b, ln_a_g, ln_a_b, w_branch_a, sinks, w_branch_b, conv_c_w, w_branch_c, ln_d_g, ln_d_b, sgu_w, sgu_b, w_branch_d, w_out, ln1_g, ln1_b, ffn_w_gate, ffn_w_up, ffn_w_down, router_w, router_b, exp_w_gate, exp_w_up, exp_w_down, ln2_g, ln2_b):
    nb, t_len, d = x_prompt.shape
    nseq, s_len, _ = x_sample.shape
    depth = w_in.shape[0]
    mp, ms = nb * t_len, nseq * s_len
    w = WINDOW

    x = (x_prompt.reshape(mp, d), x_sample.reshape(ms, d))
    xb = _stack_cast(*x)

    cos_p, sin_p = _rope_tables(jnp.arange(t_len))
    cos_s, sin_s = _rope_tables(PAST_LEN + jnp.arange(s_len))

    def row(v):
        return v.reshape(1, -1)

    outs = {k: [] for k in ("pa", "pk", "pv", "pc", "sa", "sk", "sv", "sc", "sd")}
    for l in range(depth):
        w_in_b = w_in[l].astype(BF16)
        h = _in_proj(xb, w_in_b)

        wa, ba = conv_a_w[l], row(conv_a_b[l])
        lag, lab = row(ln_a_g[l]), row(ln_a_b[l])
        ya_p, yc_p, pa, pc = _ac_prompt(h, nb, t_len, wa, ba, lag, lab, conv_c_w[l])
        ya_s, yc_s, sa, sc = _ac_sample(h, mp, nseq, s_len, state_conv_a[l], state_conv_c[l],
                                        wa, ba, lag, lab, conv_c_w[l])

        yb_p, pk, pv = _attn_prompt(h, nb, t_len, sinks[l], cos_p, sin_p)
        yb_s, sk, sv = _attn_sample(h, mp, nseq, s_len, sinks[l],
                                    cache_swa_k[l].reshape(nseq, w, KV_WIDTH),
                                    cache_swa_v[l].reshape(nseq, w, KV_WIDTH), cos_s, sin_s)

        ldg, ldb = row(ln_d_g[l]), row(ln_d_b[l])
        yd_p = _d_prompt(h, mp, ldg, ldb, sgu_w[l], jnp.transpose(sgu_b[l]))
        sgu_wt = jnp.repeat(jnp.transpose(sgu_w[l][:, :s_len, :s_len], (2, 1, 0)), GROUP_D, axis=2)
        sgu_bs = jnp.repeat(jnp.transpose(sgu_b[l][:, :s_len]), GROUP_D, axis=1)
        yd_s, sd = _d_sample(h, mp, nseq, s_len, ldg, ldb, sgu_wt, sgu_bs)

        merged = _merge(xb, (ya_p, yb_p, yc_p, yd_p), (ya_s, yb_s, yc_s, yd_s), w_in_b, b_gate[l],
                        (w_branch_a[l].astype(BF16), w_branch_b[l].astype(BF16),
                         w_branch_c[l].astype(BF16), w_branch_d[l].astype(BF16)))
        x, xb = _out_ln(merged, x, w_out[l].astype(BF16), row(ln1_g[l]), row(ln1_b[l]))

        i = l // 2
        if l % 2 == 0:
            x, xb = _ffn_ln(x, xb, ffn_w_gate[i].astype(BF16), ffn_w_up[i].astype(BF16),
                            ffn_w_down[i].astype(BF16), row(ln2_g[l]), row(ln2_b[l]))
            x_p, x_s = x[:mp], x[mp:]
        else:
            x_p, x_s = _moe_ln(x, mp, router_w[i], row(router_b[i]), exp_w_gate[i].astype(BF16),
                               exp_w_up[i].astype(BF16), exp_w_down[i].astype(BF16),
                               row(ln2_g[l]), row(ln2_b[l]))
            if l + 1 < depth:
                x = jnp.concatenate([x_p, x_s], axis=0)
                xb = x.astype(BF16)

        outs["pa"].append(pa)
        outs["pk"].append(pk.reshape(nb, w, N_KV, HEAD_DIM))
        outs["pv"].append(pv.reshape(nb, w, N_KV, HEAD_DIM))
        outs["pc"].append(pc)
        outs["sa"].append(sa)
        outs["sk"].append(sk.reshape(nseq, w, N_KV, HEAD_DIM))
        outs["sv"].append(sv.reshape(nseq, w, N_KV, HEAD_DIM))
        outs["sc"].append(sc)
        outs["sd"].append(sd)

    y_prompt = x_p.reshape(nb, t_len, d)
    y_sample = x_s.reshape(nseq, s_len, d)
    st = {k: jnp.stack(v) for k, v in outs.items()}
    return (y_prompt, y_sample, st["pa"], st["pk"], st["pv"], st["pc"],
            st["sa"], st["sk"], st["sv"], st["sc"], st["sd"])
```

```python
import functools

import jax
import jax.numpy as jnp
import numpy as np
from jax import lax
from jax.experimental import pallas as pl
from jax.experimental.pallas import tpu as pltpu

D_MODEL = 2048
PAST_LEN = 8192
D_A = 512
CONV_A = 31
HEAD_DIM = 64
N_HEADS = 16
N_KV = 4
GQA = N_HEADS // N_KV
WINDOW = 128
ROPE_THETA = 10000.0
D_C = 512
CONV_C = 3
D_D = 512
CHUNK = 128
N_GROUPS_D = 4
GROUP_D = D_D // N_GROUPS_D
N_BRANCH = 4
Q_WIDTH = N_HEADS * HEAD_DIM
KV_WIDTH = N_KV * HEAD_DIM
D_FF = 5632
N_EXPERTS = 8
ALPHA = 4.0 ** 0.25
LN_EPS = 1e-5

OFF_A_VAL = 0
OFF_A_GATE = D_A
OFF_Q = 2 * D_A
OFF_K = OFF_Q + Q_WIDTH
OFF_V = OFF_K + KV_WIDTH
OFF_C_B = OFF_V + KV_WIDTH
OFF_C_C = OFF_C_B + D_C
OFF_C_X = OFF_C_C + D_C
OFF_D_U = OFF_C_X + D_C
OFF_D_V = OFF_D_U + D_D
OFF_GATES = OFF_D_V + D_D

LANES = 128
SUBLANES = 8
HALO = 32
VMEM_LIMIT = 48 * 1024 * 1024
OUT_LN_SUBTILES = 4
IN_PROJ_ROWS = 2176
FFN_ROWS = 544
MOE_F_STEPS = 11
MOE_ROWS = 80 * (MOE_F_STEPS - 1)
MOE_VMEM_LIMIT = (2 * 4 + 2 + 2 * 4) * MOE_ROWS * D_MODEL + 2 * 3 * 2 * D_MODEL * 512 + 8 * 1024 * 1024
W_BANDS = 1

BF16 = jnp.bfloat16
F32 = jnp.float32


def _tile(n, pref):
    n = int(n)
    if n <= pref:
        return n
    for t in range(pref, 7, -1):
        if n % t == 0 and t % 8 == 0:
            return t
    return n


def _params(*sem):
    return pltpu.CompilerParams(dimension_semantics=sem, vmem_limit_bytes=VMEM_LIMIT)


def _layer_norm(x, g, b):
    mu = jnp.mean(x, axis=-1, keepdims=True)
    xc = x - mu
    var = jnp.mean(xc * xc, axis=-1, keepdims=True)
    return xc * lax.rsqrt(var + LN_EPS) * g + b


def _silu(x):
    return x * jax.nn.sigmoid(x)


def _gelu(x):
    return jax.nn.gelu(x, approximate=True)


def _stack_cast_kernel(xp_ref, xs_ref, o_ref, *, prompt_tiles):
    is_prompt = pl.program_id(0) < prompt_tiles
    o_ref[...] = jnp.where(is_prompt, xp_ref[...], xs_ref[...]).astype(o_ref.dtype)


def _stack_cast(xp, xs):
    mp, ms = xp.shape[0], xs.shape[0]
    tm = _tile(np.gcd(mp, ms), 512)
    npt = mp // tm
    return pl.pallas_call(
        functools.partial(_stack_cast_kernel, prompt_tiles=npt),
        grid=((mp + ms) // tm,),
        in_specs=[pl.BlockSpec((tm, D_MODEL), lambda i: (jnp.minimum(i, npt - 1), 0)),
                  pl.BlockSpec((tm, D_MODEL), lambda i: (jnp.maximum(i - npt, 0), 0))],
        out_specs=pl.BlockSpec((tm, D_MODEL), lambda i: (i, 0)),
        out_shape=jax.ShapeDtypeStruct((mp + ms, D_MODEL), BF16),
        compiler_params=_params("parallel"),
        name="stack_cast",
    )(xp, xs)


def _mm_kernel(x_ref, w_ref, o_ref):
    o_ref[...] = jnp.dot(x_ref[...], w_ref[...], preferred_element_type=F32).astype(o_ref.dtype)


def _in_proj(xb, w_in_b):
    m, k = xb.shape
    n = OFF_GATES
    tm = _tile(m, IN_PROJ_ROWS)
    tn = 512
    return pl.pallas_call(
        _mm_kernel,
        grid=(m // tm, n // tn),
        in_specs=[pl.BlockSpec((tm, k), lambda i, j: (i, 0)),
                  pl.BlockSpec((k, tn), lambda i, j: (0, j))],
        out_specs=pl.BlockSpec((tm, tn), lambda i, j: (i, j)),
        out_shape=jax.ShapeDtypeStruct((m, n), F32),
        compiler_params=_params("parallel", "arbitrary"),
        name="in_proj",
    )(xb, w_in_b)


def _ac_prompt_kernel(av_ref, ag_ref, cb_ref, cc_ref, cx_ref,
                      avp_ref, agp_ref, ccp_ref, cxp_ref,
                      wa_ref, ba_ref, lg_ref, lb_ref, wc_ref,
                      ya_ref, yc_ref, ha_ref, hc_ref, ga_s, gc_s, sh_s):
    t = pl.program_id(1)
    tt = av_ref.shape[0]
    first = t == 0
    glu_prev = avp_ref[...] * jax.nn.sigmoid(agp_ref[...])
    ga_s[0:HALO, :] = jnp.where(first, 0.0, glu_prev)
    ga_s[HALO:HALO + tt, :] = av_ref[...] * jax.nn.sigmoid(ag_ref[...])
    acc = jnp.zeros((tt, D_A), F32) + ba_ref[...]
    base = HALO - (CONV_A - 1)
    for phase in range(SUBLANES):
        offs = [o for o in range(base, base + CONV_A) if o % SUBLANES == phase]
        span = offs[-1] - phase + tt
        sh_s[0:span, :] = ga_s[phase:phase + span, :]
        for o in offs:
            acc = acc + wa_ref[o - base:o - base + 1, :] * sh_s[o - phase:o - phase + tt, :]
    ya = _silu(_layer_norm(acc, lg_ref[...], lb_ref[...]))
    ya_ref[...] = ya.astype(ya_ref.dtype)
    ha_ref[0] = ga_s[HALO + tt - (CONV_A - 1):HALO + tt, :]

    gc_s[0:HALO, :] = jnp.where(first, 0.0, ccp_ref[...] * cxp_ref[...])
    gc_s[HALO:HALO + tt, :] = cc_ref[...] * cx_ref[...]
    base_c = HALO - (CONV_C - 1)
    yc = jnp.zeros((tt, D_C), F32)
    for j in range(CONV_C):
        yc = yc + wc_ref[j:j + 1, :] * gc_s[base_c + j:base_c + j + tt, :]
    yc_ref[...] = (cb_ref[...] * yc).astype(yc_ref.dtype)
    hc_ref[0] = gc_s[HALO + tt - (CONV_C - 1):HALO + tt, :]


def _ac_prompt(h, nb, t_len, wa, ba, lg, lb, wc):
    tt = _tile(t_len, 512)
    nt = t_len // tt
    cw = D_A

    def cur(col):
        return pl.BlockSpec((tt, cw), lambda b, t: (b * nt + t, col // cw))

    def prev(col):
        return pl.BlockSpec(
            (HALO, cw),
            lambda b, t: (jnp.maximum((b * t_len + t * tt) // HALO - 1, 0), col // cw))

    def full(shape):
        return pl.BlockSpec(shape, lambda b, t: (0,) * len(shape))

    mp = nb * t_len
    return pl.pallas_call(
        _ac_prompt_kernel,
        grid=(nb, nt),
        in_specs=[cur(OFF_A_VAL), cur(OFF_A_GATE), cur(OFF_C_B), cur(OFF_C_C), cur(OFF_C_X),
                  prev(OFF_A_VAL), prev(OFF_A_GATE), prev(OFF_C_C), prev(OFF_C_X),
                  full((CONV_A, D_A)), full((1, D_A)), full((1, D_A)), full((1, D_A)),
                  full((CONV_C, D_C))],
        out_specs=[pl.BlockSpec((tt, D_A), lambda b, t: (b * nt + t, 0)),
                   pl.BlockSpec((tt, D_C), lambda b, t: (b * nt + t, 0)),
                   pl.BlockSpec((1, CONV_A - 1, D_A), lambda b, t: (b, 0, 0)),
                   pl.BlockSpec((1, CONV_C - 1, D_C), lambda b, t: (b, 0, 0))],
        out_shape=[jax.ShapeDtypeStruct((mp, D_A), BF16),
                   jax.ShapeDtypeStruct((mp, D_C), BF16),
                   jax.ShapeDtypeStruct((nb, CONV_A - 1, D_A), F32),
                   jax.ShapeDtypeStruct((nb, CONV_C - 1, D_C), F32)],
        scratch_shapes=[pltpu.VMEM((HALO + tt, D_A), F32), pltpu.VMEM((HALO + tt, D_C), F32),
                        pltpu.VMEM((HALO + tt, D_A), F32)],
        compiler_params=_params("parallel", "arbitrary"),
        name="ac_prompt",
    )(h, h, h, h, h, h, h, h, h, wa, ba, lg, lb, wc)


def _ac_sample_kernel(av_ref, ag_ref, cb_ref, cc_ref, cx_ref, hista_ref, histc_ref,
                      wa_ref, ba_ref, lg_ref, lb_ref, wc_ref,
                      ya_ref, yc_ref, ha_ref, hc_ref, xa_s, xc_s):
    sb, s_len = hista_ref.shape[0], av_ref.shape[0] // hista_ref.shape[0]
    ka, kc = CONV_A - 1, CONV_C - 1
    glu = av_ref[...] * jax.nn.sigmoid(ag_ref[...])
    xa_s[:, 0:ka, :] = hista_ref[...]
    xa_s[:, ka:ka + s_len, :] = glu.reshape(sb, s_len, D_A)
    acc = jnp.zeros((sb, s_len, D_A), F32) + ba_ref[...][None]
    for j in range(CONV_A):
        acc = acc + wa_ref[j:j + 1, :][None] * xa_s[:, j:j + s_len, :]
    ya = _silu(_layer_norm(acc, lg_ref[...][None], lb_ref[...][None]))
    ya_ref[...] = ya.reshape(sb * s_len, D_A).astype(ya_ref.dtype)
    ha_ref[...] = xa_s[:, s_len:s_len + ka, :]

    xc_s[:, 0:kc, :] = histc_ref[...]
    xc_s[:, kc:kc + s_len, :] = (cc_ref[...] * cx_ref[...]).reshape(sb, s_len, D_C)
    yc = jnp.zeros((sb, s_len, D_C), F32)
    for j in range(CONV_C):
        yc = yc + wc_ref[j:j + 1, :][None] * xc_s[:, j:j + s_len, :]
    yc = cb_ref[...] * yc.reshape(sb * s_len, D_C)
    yc_ref[...] = yc.astype(yc_ref.dtype)
    hc_ref[...] = xc_s[:, s_len:s_len + kc, :]


def _ac_sample(h, row0, nseq, s_len, hist_a, hist_c, wa, ba, lg, lb, wc):
    sb = _tile(nseq, 16)
    rows = sb * s_len
    blk0 = row0 // rows
    cw = D_A

    def cur(col):
        return pl.BlockSpec((rows, cw), lambda i: (blk0 + i, col // cw))

    def full(shape):
        return pl.BlockSpec(shape, lambda i: (0,) * len(shape))

    ms = nseq * s_len
    ka, kc = CONV_A - 1, CONV_C - 1
    return pl.pallas_call(
        _ac_sample_kernel,
        grid=(nseq // sb,),
        in_specs=[cur(OFF_A_VAL), cur(OFF_A_GATE), cur(OFF_C_B), cur(OFF_C_C), cur(OFF_C_X),
                  pl.BlockSpec((sb, ka, D_A), lambda i: (i, 0, 0)),
                  pl.BlockSpec((sb, kc, D_C), lambda i: (i, 0, 0)),
                  full((CONV_A, D_A)), full((1, D_A)), full((1, D_A)), full((1, D_A)),
                  full((CONV_C, D_C))],
        out_specs=[pl.BlockSpec((rows, D_A), lambda i: (i, 0)),
                   pl.BlockSpec((rows, D_C), lambda i: (i, 0)),
                   pl.BlockSpec((sb, ka, D_A), lambda i: (i, 0, 0)),
                   pl.BlockSpec((sb, kc, D_C), lambda i: (i, 0, 0))],
        out_shape=[jax.ShapeDtypeStruct((ms, D_A), BF16),
                   jax.ShapeDtypeStruct((ms, D_C), BF16),
                   jax.ShapeDtypeStruct((nseq, ka, D_A), F32),
                   jax.ShapeDtypeStruct((nseq, kc, D_C), F32)],
        scratch_shapes=[pltpu.VMEM((sb, ka + s_len + 2, D_A), F32),
                        pltpu.VMEM((sb, kc + s_len + 6, D_C), F32)],
        compiler_params=_params("parallel"),
        name="ac_sample",
    )(h, h, h, h, h, hist_a, hist_c, wa, ba, lg, lb, wc)


def _rope_tables(pos):
    half = HEAD_DIM // 2
    inv_freq = jnp.power(ROPE_THETA, -jnp.arange(half, dtype=F32) * (2.0 / HEAD_DIM))
    ang = pos.astype(F32)[:, None] * inv_freq[None, :]
    cos, sin = jnp.cos(ang), jnp.sin(ang)
    cos_t = jnp.concatenate([cos, cos, cos, cos], axis=1)
    sin_t = jnp.concatenate([-sin, sin, -sin, sin], axis=1)
    return cos_t, sin_t


def _rope(x, cos_t, sin_t):
    half = HEAD_DIM // 2
    axis = x.ndim - 1
    shape = x.shape[:-1] + (LANES,)
    lane = lax.broadcasted_iota(jnp.int32, shape, axis)
    first = (lane % HEAD_DIM) < half
    out = []
    for i in range(x.shape[-1] // LANES):
        xi = x[..., LANES * i:LANES * (i + 1)]
        partner = jnp.where(first, pltpu.roll(xi, LANES - half, axis), pltpu.roll(xi, half, axis))
        out.append(xi * cos_t + partner * sin_t)
    return out


def _head_halves(t, axis, lo_valid):
    lane = lax.broadcasted_iota(jnp.int32, t.shape, axis)
    if lo_valid:
        lo = jnp.where(lane < HEAD_DIM, t, jnp.zeros_like(t))
        hi = pltpu.roll(lo, HEAD_DIM, axis)
    else:
        hi = jnp.where(lane >= HEAD_DIM, t, jnp.zeros_like(t))
        lo = pltpu.roll(hi, HEAD_DIM, axis)
    return lo, hi


def _swa_block(sinks_ref, q_tiles, kp_tiles, kc_tiles, vp, vc, has_prev, write):
    w = WINDOW
    ci = lax.broadcasted_iota(jnp.int32, (2 * w, 2 * w), 0)
    qi = lax.broadcasted_iota(jnp.int32, (2 * w, 2 * w), 1) % w
    mask = (ci > qi) & (ci <= qi + w) & (has_prev | (ci >= w))
    lane_first = lax.broadcasted_iota(jnp.int32, (1, 2 * w), 1) < w

    qt_tiles = [t.T.astype(BF16) for t in q_tiles]
    zeros_t = jnp.zeros((HEAD_DIM, 2 * w), BF16)

    for h in range(N_KV):
        tile, lo_valid = h // 2, (h % 2 == 0)
        lanes = slice(LANES * tile, LANES * (tile + 1))
        kcat = jnp.concatenate([kp_tiles[tile], kc_tiles[tile]], axis=0)
        k_lo, k_hi = _head_halves(kcat.astype(BF16), 1, lo_valid)
        vt = jnp.concatenate([vp[:, lanes], vc[:, lanes]], axis=0).T.astype(BF16)
        vt = vt[0:HEAD_DIM] if lo_valid else vt[HEAD_DIM:2 * HEAD_DIM]
        vt_halves = (jnp.concatenate([vt, zeros_t], axis=0), jnp.concatenate([zeros_t, vt], axis=0))
        qst = jnp.concatenate([qt_tiles[2 * h], qt_tiles[2 * h + 1]], axis=1)
        ot = None
        for half_idx, k_half in enumerate((k_lo, k_hi)):
            s = jnp.dot(k_half, qst, preferred_element_type=F32)
            s = jnp.where(mask, s, -jnp.inf)
            sk = jnp.where(lane_first, sinks_ref[4 * h + half_idx], sinks_ref[4 * h + 2 + half_idx])
            m = jnp.maximum(jnp.max(s, axis=0, keepdims=True), sk)
            e = jnp.exp(s - m)
            den = jnp.sum(e, axis=0, keepdims=True) + jnp.exp(sk - m)
            p = (e / den).astype(BF16)
            term = jnp.dot(vt_halves[half_idx], p, preferred_element_type=F32)
            ot = term if ot is None else ot + term
        write(2 * h, ot[:, 0:w].T)
        write(2 * h + 1, ot[:, w:2 * w].T)


def _attn_prompt_kernel(sinks_ref, q_ref, kc_ref, vc_ref, kp_ref, vp_ref,
                        cos_ref, sin_ref, cosp_ref, sinp_ref,
                        yb_ref, nk_ref, nv_ref):
    n = pl.program_id(1)
    w = WINDOW
    cos_c, sin_c = cos_ref[...], sin_ref[...]
    q_tiles = _rope(q_ref[...] * (HEAD_DIM ** -0.5), cos_c, sin_c)
    kc_tiles = _rope(kc_ref[...], cos_c, sin_c)
    kp_tiles = _rope(kp_ref[...], cosp_ref[...], sinp_ref[...])
    for i in range(KV_WIDTH // LANES):
        nk_ref[0, :, LANES * i:LANES * (i + 1)] = kc_tiles[i][w:2 * w]
    nv_ref[0] = vc_ref[w:2 * w, :]

    for j in range(2):
        rows = slice(j * w, (j + 1) * w)

        def write(tile, value, rows=rows):
            yb_ref[rows, LANES * tile:LANES * (tile + 1)] = value.astype(yb_ref.dtype)

        if j == 0:
            prev_k, prev_v, has_prev = kp_tiles, vp_ref[...], n > 0
        else:
            prev_k, prev_v, has_prev = [t[0:w] for t in kc_tiles], vc_ref[0:w, :], True
        _swa_block(sinks_ref, [t[rows] for t in q_tiles], prev_k, [t[rows] for t in kc_tiles],
                   prev_v, vc_ref[rows, :], has_prev, write)


def _attn_prompt(h, nb, t_len, sinks, cos_t, sin_t):
    w = WINDOW
    nblk = t_len // w
    assert nblk % 2 == 0
    npair = nblk // 2
    mp = nb * t_len

    def pair(b, n):
        return b * npair + n

    def halo(b, n):
        return b * nblk + jnp.maximum(2 * n - 1, 0)

    return pl.pallas_call(
        _attn_prompt_kernel,
        grid=(nb, npair),
        in_specs=[pl.BlockSpec(memory_space=pltpu.SMEM),
                  pl.BlockSpec((2 * w, Q_WIDTH), lambda b, n: (pair(b, n), OFF_Q // Q_WIDTH)),
                  pl.BlockSpec((2 * w, KV_WIDTH), lambda b, n: (pair(b, n), OFF_K // KV_WIDTH)),
                  pl.BlockSpec((2 * w, KV_WIDTH), lambda b, n: (pair(b, n), OFF_V // KV_WIDTH)),
                  pl.BlockSpec((w, KV_WIDTH), lambda b, n: (halo(b, n), OFF_K // KV_WIDTH)),
                  pl.BlockSpec((w, KV_WIDTH), lambda b, n: (halo(b, n), OFF_V // KV_WIDTH)),
                  pl.BlockSpec((2 * w, LANES), lambda b, n: (n, 0)),
                  pl.BlockSpec((2 * w, LANES), lambda b, n: (n, 0)),
                  pl.BlockSpec((w, LANES), lambda b, n: (jnp.maximum(2 * n - 1, 0), 0)),
                  pl.BlockSpec((w, LANES), lambda b, n: (jnp.maximum(2 * n - 1, 0), 0))],
        out_specs=[pl.BlockSpec((2 * w, Q_WIDTH), lambda b, n: (pair(b, n), 0)),
                   pl.BlockSpec((1, w, KV_WIDTH), lambda b, n: (b, 0, 0)),
                   pl.BlockSpec((1, w, KV_WIDTH), lambda b, n: (b, 0, 0))],
        out_shape=[jax.ShapeDtypeStruct((mp, Q_WIDTH), BF16),
                   jax.ShapeDtypeStruct((nb, w, KV_WIDTH), F32),
                   jax.ShapeDtypeStruct((nb, w, KV_WIDTH), F32)],
        compiler_params=_params("parallel", "arbitrary"),
        name="attn_prompt",
    )(sinks, h, h, h, h, h, cos_t, sin_t, cos_t, sin_t)


def _attn_sample_kernel(sinks_ref, q_ref, kn_ref, vn_ref, kbuf_ref, vbuf_ref, cos_ref, sin_ref,
                        yb_ref, nk_ref, nv_ref):
    sb, w = kbuf_ref.shape[0], kbuf_ref.shape[1]
    s_len = q_ref.shape[0] // sb
    cos_t, sin_t = cos_ref[...], sin_ref[...]
    q_tiles = _rope(q_ref[...] * (HEAD_DIM ** -0.5), cos_t, sin_t)
    kn_tiles = _rope(kn_ref[...], cos_t, sin_t)
    nk_ref[:, 0:w - s_len, :] = kbuf_ref[:, s_len:w, :]
    nv_ref[:, 0:w - s_len, :] = vbuf_ref[:, s_len:w, :]
    for i in range(KV_WIDTH // LANES):
        nk_ref[:, w - s_len:w, LANES * i:LANES * (i + 1)] = kn_tiles[i].reshape(sb, s_len, LANES)
    nv_ref[:, w - s_len:w, :] = vn_ref[...].reshape(sb, s_len, KV_WIDTH)

    qi = lax.broadcasted_iota(jnp.int32, (2 * s_len, w), 0) % s_len
    ci = lax.broadcasted_iota(jnp.int32, (2 * s_len, w), 1)
    mask_buf = (ci > qi)[None]
    qn = lax.broadcasted_iota(jnp.int32, (2 * s_len, s_len), 0) % s_len
    cn = lax.broadcasted_iota(jnp.int32, (2 * s_len, s_len), 1)
    mask_new = (cn <= qn)[None]
    row_top = (lax.broadcasted_iota(jnp.int32, (2 * s_len, 1), 0) < s_len)[None]
    bqk = (((2,), (2,)), ((0,), (0,)))
    bkd = (((2,), (1,)), ((0,), (0,)))

    for h in range(N_KV):
        tile, lo_valid = h // 2, (h % 2 == 0)
        sl = slice(LANES * tile, LANES * (tile + 1))
        kb_lo, kb_hi = _head_halves(kbuf_ref[:, :, sl].astype(BF16), 2, lo_valid)
        vb_lo, vb_hi = _head_halves(vbuf_ref[:, :, sl].astype(BF16), 2, lo_valid)
        kn3 = kn_tiles[tile].reshape(sb, s_len, LANES).astype(BF16)
        vn3 = vn_ref[:, sl].reshape(sb, s_len, LANES).astype(BF16)
        kn_lo, kn_hi = _head_halves(kn3, 2, lo_valid)
        vn_lo, vn_hi = _head_halves(vn3, 2, lo_valid)
        qs = jnp.concatenate([q_tiles[2 * h].reshape(sb, s_len, LANES),
                              q_tiles[2 * h + 1].reshape(sb, s_len, LANES)], axis=1).astype(BF16)
        o = jnp.zeros((sb, 2 * s_len, LANES), F32)
        for half_idx, (kb, kn, vb, vn) in enumerate(((kb_lo, kn_lo, vb_lo, vn_lo),
                                                     (kb_hi, kn_hi, vb_hi, vn_hi))):
            s_b = lax.dot_general(qs, kb, bqk, preferred_element_type=F32)
            s_n = lax.dot_general(qs, kn, bqk, preferred_element_type=F32)
            s_b = jnp.where(mask_buf, s_b, -jnp.inf)
            s_n = jnp.where(mask_new, s_n, -jnp.inf)
            sk = jnp.where(row_top, sinks_ref[4 * h + half_idx], sinks_ref[4 * h + 2 + half_idx])
            m = jnp.maximum(jnp.maximum(jnp.max(s_b, axis=2, keepdims=True),
                                        jnp.max(s_n, axis=2, keepdims=True)), sk)
            e_b = jnp.exp(s_b - m)
            e_n = jnp.exp(s_n - m)
            den = (jnp.sum(e_b, axis=2, keepdims=True) + jnp.sum(e_n, axis=2, keepdims=True)
                   + jnp.exp(sk - m))
            o = o + lax.dot_general((e_b / den).astype(BF16), vb, bkd, preferred_element_type=F32)
            o = o + lax.dot_general((e_n / den).astype(BF16), vn, bkd, preferred_element_type=F32)
        yb_ref[:, LANES * (2 * h):LANES * (2 * h + 1)] = (
            o[:, 0:s_len, :].reshape(sb * s_len, LANES).astype(yb_ref.dtype))
        yb_ref[:, LANES * (2 * h + 1):LANES * (2 * h + 2)] = (
            o[:, s_len:2 * s_len, :].reshape(sb * s_len, LANES).astype(yb_ref.dtype))


def _attn_sample(h, row0, nseq, s_len, sinks, k_buf, v_buf, cos_t, sin_t):
    sb = _tile(nseq, 8)
    rows = sb * s_len
    blk0 = row0 // rows
    w = k_buf.shape[1]
    ms = nseq * s_len
    cos_rows = jnp.tile(cos_t, (sb, 1))
    sin_rows = jnp.tile(sin_t, (sb, 1))
    return pl.pallas_call(
        _attn_sample_kernel,
        grid=(nseq // sb,),
        in_specs=[pl.BlockSpec(memory_space=pltpu.SMEM),
                  pl.BlockSpec((rows, Q_WIDTH), lambda i: (blk0 + i, OFF_Q // Q_WIDTH)),
                  pl.BlockSpec((rows, KV_WIDTH), lambda i: (blk0 + i, OFF_K // KV_WIDTH)),
                  pl.BlockSpec((rows, KV_WIDTH), lambda i: (blk0 + i, OFF_V // KV_WIDTH)),
                  pl.BlockSpec((sb, w, KV_WIDTH), lambda i: (i, 0, 0)),
                  pl.BlockSpec((sb, w, KV_WIDTH), lambda i: (i, 0, 0)),
                  pl.BlockSpec((rows, LANES), lambda i: (0, 0)),
                  pl.BlockSpec((rows, LANES), lambda i: (0, 0))],
        out_specs=[pl.BlockSpec((rows, Q_WIDTH), lambda i: (i, 0)),
                   pl.BlockSpec((sb, w, KV_WIDTH), lambda i: (i, 0, 0)),
                   pl.BlockSpec((sb, w, KV_WIDTH), lambda i: (i, 0, 0))],
        out_shape=[jax.ShapeDtypeStruct((ms, Q_WIDTH), BF16),
                   jax.ShapeDtypeStruct((nseq, w, KV_WIDTH), F32),
                   jax.ShapeDtypeStruct((nseq, w, KV_WIDTH), F32)],
        compiler_params=_params("parallel"),
        name="attn_sample",
    )(sinks, h, h, h, k_buf, v_buf, cos_rows, sin_rows)


def _d_prompt_kernel(du_ref, dv_ref, lg_ref, lb_ref, w_ref, bt_ref, yd_ref):
    c = CHUNK
    ri = lax.broadcasted_iota(jnp.int32, (c, c), 0)
    cj = lax.broadcasted_iota(jnp.int32, (c, c), 1)
    causal = cj <= ri
    w_causal = [jnp.where(causal, w_ref[g], 0.0).astype(BF16) for g in range(N_GROUPS_D)]
    for k in range(du_ref.shape[0] // c):
        rows = slice(k * c, (k + 1) * c)
        u = _gelu(du_ref[rows, :])
        vn = _layer_norm(_gelu(dv_ref[rows, :]), lg_ref[...], lb_ref[...]).astype(BF16)
        for g in range(N_GROUPS_D):
            sl = slice(GROUP_D * g, GROUP_D * (g + 1))
            s = jnp.dot(w_causal[g], vn[:, sl], preferred_element_type=F32) + bt_ref[:, g:g + 1]
            yd_ref[rows, sl] = (u[:, sl] * s).astype(yd_ref.dtype)


def _d_prompt(h, mp, ln_g, ln_b, sgu_w, sgu_bt):
    c = CHUNK
    rows = _tile(mp, 4 * c)
    assert rows % c == 0
    return pl.pallas_call(
        _d_prompt_kernel,
        grid=(mp // rows,),
        in_specs=[pl.BlockSpec((rows, D_D), lambda i: (i, OFF_D_U // D_D)),
                  pl.BlockSpec((rows, D_D), lambda i: (i, OFF_D_V // D_D)),
                  pl.BlockSpec((1, D_D), lambda i: (0, 0)),
                  pl.BlockSpec((1, D_D), lambda i: (0, 0)),
                  pl.BlockSpec((N_GROUPS_D, c, c), lambda i: (0, 0, 0)),
                  pl.BlockSpec((c, N_GROUPS_D), lambda i: (0, 0))],
        out_specs=pl.BlockSpec((rows, D_D), lambda i: (i, 0)),
        out_shape=jax.ShapeDtypeStruct((mp, D_D), BF16),
        compiler_params=_params("parallel"),
        name="d_prompt",
    )(h, h, ln_g, ln_b, sgu_w, sgu_bt)


def _d_sample_kernel(du_ref, dv_ref, lg_ref, lb_ref, wt_ref, bt_ref, yd_ref, vd_ref):
    sb, s_len = vd_ref.shape[0], vd_ref.shape[1]
    u = _gelu(du_ref[...])
    vn = _layer_norm(_gelu(dv_ref[...]), lg_ref[...], lb_ref[...])
    vn3 = vn.reshape(sb, s_len, D_D)
    vd_ref[...] = vn3
    ii = lax.broadcasted_iota(jnp.int32, (s_len, D_D), 0)
    s = jnp.zeros((sb, s_len, D_D), F32) + bt_ref[...][None]
    for j in range(s_len):
        wj = jnp.where(ii >= j, wt_ref[j], 0.0)
        s = s + wj[None] * vn3[:, j:j + 1, :]
    yd_ref[...] = (u * s.reshape(sb * s_len, D_D)).astype(yd_ref.dtype)


def _d_sample(h, row0, nseq, s_len, ln_g, ln_b, sgu_wt, sgu_bs):
    sb = _tile(nseq, 16)
    rows = sb * s_len
    blk0 = row0 // rows
    ms = nseq * s_len
    return pl.pallas_call(
        _d_sample_kernel,
        grid=(nseq // sb,),
        in_specs=[pl.BlockSpec((rows, D_D), lambda i: (blk0 + i, OFF_D_U // D_D)),
                  pl.BlockSpec((rows, D_D), lambda i: (blk0 + i, OFF_D_V // D_D)),
                  pl.BlockSpec((1, D_D), lambda i: (0, 0)),
                  pl.BlockSpec((1, D_D), lambda i: (0, 0)),
                  pl.BlockSpec((s_len, s_len, D_D), lambda i: (0, 0, 0)),
                  pl.BlockSpec((s_len, D_D), lambda i: (0, 0))],
        out_specs=[pl.BlockSpec((rows, D_D), lambda i: (i, 0)),
                   pl.BlockSpec((sb, s_len, D_D), lambda i: (i, 0, 0))],
        out_shape=[jax.ShapeDtypeStruct((ms, D_D), BF16),
                   jax.ShapeDtypeStruct((nseq, s_len, D_D), F32)],
        compiler_params=_params("parallel"),
        name="d_sample",
    )(h, h, ln_g, ln_b, sgu_wt, sgu_bs)


def _merge_kernel(x_ref, yap_ref, ybp_ref, ycp_ref, ydp_ref, yas_ref, ybs_ref, ycs_ref, yds_ref,
                  g0_ref, g1_ref, g2_ref, g3_ref, bg_ref, pa_ref, pb_ref, pc_ref, pd_ref, o_ref,
                  *, prompt_tiles):
    x = x_ref[...]
    is_prompt = pl.program_id(1) < prompt_tiles
    acc = None
    branches = ((yap_ref, yas_ref, g0_ref, pa_ref), (ybp_ref, ybs_ref, g1_ref, pb_ref),
                (ycp_ref, ycs_ref, g2_ref, pc_ref), (ydp_ref, yds_ref, g3_ref, pd_ref))
    for i, (yp_ref, ys_ref, g_ref, p_ref) in enumerate(branches):
        y = jnp.where(is_prompt, yp_ref[...], ys_ref[...])
        gate = jax.nn.sigmoid(jnp.dot(x, g_ref[...], preferred_element_type=F32) + bg_ref[i:i + 1, :])
        term = gate * jnp.dot(y, p_ref[...], preferred_element_type=F32)
        acc = term if acc is None else acc + term
    o_ref[...] = acc.astype(o_ref.dtype)


def _merge(xb, y_prompt, y_sample, w_in_b, b_gate, projs):
    m = xb.shape[0]
    mp, ms = y_prompt[0].shape[0], y_sample[0].shape[0]
    tm = _tile(np.gcd(mp, ms), 512)
    npt = mp // tm
    tn = 512
    nn = D_MODEL // tn
    widths = (D_A, Q_WIDTH, D_C, D_D)

    def act_p(width):
        return pl.BlockSpec((tm, width), lambda j, i: (jnp.minimum(i, npt - 1), 0))

    def act_s(width):
        return pl.BlockSpec((tm, width), lambda j, i: (jnp.maximum(i - npt, 0), 0))

    def gate_w(b):
        return pl.BlockSpec((D_MODEL, tn), lambda j, i: (0, (OFF_GATES + b * D_MODEL) // tn + j))

    def proj_w(k):
        return pl.BlockSpec((k, tn), lambda j, i: (0, j))

    return pl.pallas_call(
        functools.partial(_merge_kernel, prompt_tiles=npt),
        grid=(nn, m // tm),
        in_specs=[pl.BlockSpec((tm, D_MODEL), lambda j, i: (i, 0))]
                 + [act_p(wd) for wd in widths] + [act_s(wd) for wd in widths]
                 + [gate_w(0), gate_w(1), gate_w(2), gate_w(3),
                    pl.BlockSpec((N_BRANCH, tn), lambda j, i: (0, j))]
                 + [proj_w(wd) for wd in widths],
        out_specs=pl.BlockSpec((tm, tn), lambda j, i: (i, j)),
        out_shape=jax.ShapeDtypeStruct((m, D_MODEL), BF16),
        compiler_params=_params("parallel", "arbitrary"),
        name="merge",
    )(xb, *y_prompt, *y_sample, w_in_b, w_in_b, w_in_b, w_in_b, b_gate, *projs)


def _out_ln_kernel(mg_ref, *refs, prompt_tiles):
    if prompt_tiles is None:
        x_ref, w_ref, g_ref, b_ref, o_ref, ob_ref = refs
    else:
        xp_ref, xs_ref, w_ref, g_ref, b_ref, o_ref, ob_ref = refs
        is_prompt = pl.program_id(0) < prompt_tiles
    tm = mg_ref.shape[0]
    sub = tm // OUT_LN_SUBTILES
    for k in range(OUT_LN_SUBTILES):
        rows = slice(k * sub, (k + 1) * sub)
        if prompt_tiles is None:
            x = x_ref[rows, :]
        else:
            x = jnp.where(is_prompt, xp_ref[rows, :], xs_ref[rows, :])
        y = ALPHA * x + jnp.dot(mg_ref[rows, :], w_ref[...], preferred_element_type=F32)
        y = _layer_norm(y, g_ref[...], b_ref[...])
        o_ref[rows, :] = y
        ob_ref[rows, :] = y.astype(ob_ref.dtype)


def _out_ln(merged, x, w_out_b, g, b):
    m = merged.shape[0]
    if isinstance(x, tuple):
        mp, ms = x[0].shape[0], x[1].shape[0]
        tm = _tile(np.gcd(mp, ms), 512)
        npt = mp // tm
        x_specs = [pl.BlockSpec((tm, D_MODEL), lambda i: (jnp.minimum(i, npt - 1), 0)),
                   pl.BlockSpec((tm, D_MODEL), lambda i: (jnp.maximum(i - npt, 0), 0))]
    else:
        tm = _tile(m, 512)
        npt = None
        x, x_specs = (x,), [pl.BlockSpec((tm, D_MODEL), lambda i: (i, 0))]
    return pl.pallas_call(
        functools.partial(_out_ln_kernel, prompt_tiles=npt),
        grid=(m // tm,),
        in_specs=[pl.BlockSpec((tm, D_MODEL), lambda i: (i, 0))] + x_specs
                 + [pl.BlockSpec((D_MODEL, D_MODEL), lambda i: (0, 0), pipeline_mode=pl.Buffered(1)),
                    pl.BlockSpec((1, D_MODEL), lambda i: (0, 0)),
                    pl.BlockSpec((1, D_MODEL), lambda i: (0, 0))],
        out_specs=[pl.BlockSpec((tm, D_MODEL), lambda i: (i, 0)),
                   pl.BlockSpec((tm, D_MODEL), lambda i: (i, 0))],
        out_shape=[jax.ShapeDtypeStruct((m, D_MODEL), F32),
                   jax.ShapeDtypeStruct((m, D_MODEL), BF16)],
        compiler_params=_params("parallel"),
        name="out_ln",
    )(merged, *x, w_out_b, g, b)


def _swiglu_partial(xb, wg_refs, wu_refs, wd_refs):
    kb = xb.shape[1] // W_BANDS
    gate = up = None
    for j in range(W_BANDS):
        xs = xb[:, j * kb:(j + 1) * kb]
        gj = jnp.dot(xs, wg_refs[j][...], preferred_element_type=F32)
        uj = jnp.dot(xs, wu_refs[j][...], preferred_element_type=F32)
        gate, up = (gj, uj) if gate is None else (gate + gj, up + uj)
    hid = (_silu(gate) * up).astype(BF16)
    fb = hid.shape[1] // W_BANDS
    down = None
    for j in range(W_BANDS):
        dj = jnp.dot(hid[:, j * fb:(j + 1) * fb], wd_refs[j][...], preferred_element_type=F32)
        down = dj if down is None else down + dj
    return down


def _band_specs(lead, tf, col_index, row_index):
    kb, fb = D_MODEL // W_BANDS, tf // W_BANDS

    def col(j):
        def index(*args):
            *head, f = col_index(*args)
            return (*head, j, f)
        return pl.BlockSpec((*lead, kb, tf), index)

    def row(j):
        def index(*args):
            *head, f = row_index(*args)
            return (*head, W_BANDS * f + j, 0)
        return pl.BlockSpec((*lead, fb, D_MODEL), index)

    return ([col(j) for j in range(W_BANDS)] + [col(j) for j in range(W_BANDS)]
            + [row(j) for j in range(W_BANDS)])


def _ffn_kernel(x_ref, xb_ref, *refs):
    wg_refs, wu_refs, wd_refs = (refs[0:W_BANDS], refs[W_BANDS:2 * W_BANDS],
                                 refs[2 * W_BANDS:3 * W_BANDS])
    g_ref, b_ref, o_ref, ob_ref, acc_s = refs[3 * W_BANDS:]
    f = pl.program_id(1)
    last = pl.num_programs(1) - 1

    def partial_down():
        return _swiglu_partial(xb_ref[...], wg_refs, wu_refs, wd_refs)

    @pl.when(f == 0)
    def _():
        acc_s[...] = partial_down()

    @pl.when((f > 0) & (f < last))
    def _():
        acc_s[...] += partial_down()

    @pl.when(f == last)
    def _():
        acc_s[...] += partial_down()
        n_sub = max(k for k in (1, 2, OUT_LN_SUBTILES) if acc_s.shape[0] % (16 * k) == 0)
        sub = acc_s.shape[0] // n_sub
        for k in range(n_sub):
            rows = slice(k * sub, (k + 1) * sub)
            y = _layer_norm(ALPHA * x_ref[rows, :] + acc_s[rows, :], g_ref[...], b_ref[...])
            o_ref[rows, :] = y
            ob_ref[rows, :] = y.astype(ob_ref.dtype)


def _ffn_ln(x, xb, wg, wu, wd, g, b):
    m = x.shape[0]
    tm = _tile(m, FFN_ROWS)
    tf = 512
    return pl.pallas_call(
        _ffn_kernel,
        grid=(m // tm, D_FF // tf),
        in_specs=[pl.BlockSpec((tm, D_MODEL), lambda i, f: (i, 0)),
                  pl.BlockSpec((tm, D_MODEL), lambda i, f: (i, 0))]
                 + _band_specs((), tf, lambda i, f: (f,), lambda i, f: (f,))
                 + [pl.BlockSpec((1, D_MODEL), lambda i, f: (0, 0)),
                    pl.BlockSpec((1, D_MODEL), lambda i, f: (0, 0))],
        out_specs=[pl.BlockSpec((tm, D_MODEL), lambda i, f: (i, 0)),
                   pl.BlockSpec((tm, D_MODEL), lambda i, f: (i, 0))],
        out_shape=[jax.ShapeDtypeStruct((m, D_MODEL), F32),
                   jax.ShapeDtypeStruct((m, D_MODEL), BF16)],
        scratch_shapes=[pltpu.VMEM((tm, D_MODEL), F32)],
        compiler_params=_params("parallel", "arbitrary"),
        name="ffn_ln",
    )(x, xb, *([wg] * W_BANDS), *([wu] * W_BANDS), *([wd] * W_BANDS), g, b)


def _router_kernel(x_ref, w_ref, b_ref, idx_ref, prob_ref, rank_ref, cnt_ref, run_s):
    @pl.when(pl.program_id(0) == 0)
    def _():
        run_s[...] = jnp.zeros_like(run_s)

    x, w = x_ref[...], w_ref[...]
    x_hi, w_hi = x.astype(BF16), w.astype(BF16)
    x_lo = (x - x_hi.astype(F32)).astype(BF16)
    w_lo = (w - w_hi.astype(F32)).astype(BF16)
    logits = (jnp.dot(x_hi, w_hi, preferred_element_type=F32)
              + (jnp.dot(x_lo, w_hi, preferred_element_type=F32)
                 + jnp.dot(x_hi, w_lo, preferred_element_type=F32))) + b_ref[...]
    tm = logits.shape[0]
    idx = lax.broadcasted_iota(jnp.int32, logits.shape, 1)
    v1 = jnp.max(logits, axis=1, keepdims=True)
    i1 = jnp.min(jnp.where(logits == v1, idx, N_EXPERTS), axis=1, keepdims=True)
    rest = jnp.where(idx == i1, -jnp.inf, logits)
    v2 = jnp.max(rest, axis=1, keepdims=True)
    i2 = jnp.min(jnp.where(rest == v2, idx, N_EXPERTS), axis=1, keepdims=True)
    e2 = jnp.exp(v2 - v1)
    den = 1.0 + e2

    hit = ((idx == i1) | (idx == i2)).astype(F32)
    ri = lax.broadcasted_iota(jnp.int32, (tm, tm), 0)
    ci = lax.broadcasted_iota(jnp.int32, (tm, tm), 1)
    before = jnp.dot((ci < ri).astype(BF16), hit.astype(BF16), preferred_element_type=F32) + run_s[...]
    rank1 = jnp.sum(jnp.where(idx == i1, before, 0.0), axis=1, keepdims=True)
    rank2 = jnp.sum(jnp.where(idx == i2, before, 0.0), axis=1, keepdims=True)

    col = lax.broadcasted_iota(jnp.int32, (tm, 2), 1)
    idx_ref[...] = jnp.where(col == 0, i1, i2)
    prob_ref[...] = jnp.where(col == 0, 1.0 / den, e2 / den)
    rank_ref[...] = jnp.where(col == 0, rank1, rank2).astype(jnp.int32)
    total = run_s[...] + jnp.sum(hit, axis=0, keepdims=True)
    run_s[...] = total
    cnt_ref[...] = total.astype(jnp.int32)


def _router(x, router_w, router_b):
    m = x.shape[0]
    tm = _tile(m, 512)
    return pl.pallas_call(
        _router_kernel,
        grid=(m // tm,),
        in_specs=[pl.BlockSpec((tm, D_MODEL), lambda i: (i, 0)),
                  pl.BlockSpec((D_MODEL, N_EXPERTS), lambda i: (0, 0)),
                  pl.BlockSpec((1, N_EXPERTS), lambda i: (0, 0))],
        out_specs=[pl.BlockSpec((tm, 2), lambda i: (i, 0)),
                   pl.BlockSpec((tm, 2), lambda i: (i, 0)),
                   pl.BlockSpec((tm, 2), lambda i: (i, 0)),
                   pl.BlockSpec((1, N_EXPERTS), lambda i: (0, 0))],
        out_shape=[jax.ShapeDtypeStruct((m, 2), jnp.int32),
                   jax.ShapeDtypeStruct((m, 2), F32),
                   jax.ShapeDtypeStruct((m, 2), jnp.int32),
                   jax.ShapeDtypeStruct((1, N_EXPERTS), jnp.int32)],
        scratch_shapes=[pltpu.VMEM((1, N_EXPERTS), F32)],
        compiler_params=_params("arbitrary"),
        name="router",
    )(x, router_w, router_b)


def _route_tables(idx, rank, counts, tm, n_tiles):
    m = idx.shape[0]
    padded = ((counts + tm - 1) // tm) * tm
    ends = jnp.cumsum(padded)
    base = ends - padded
    pos = base[idx] + rank
    n_valid = ends[-1] // tm
    tiles = jnp.arange(n_tiles, dtype=jnp.int32)
    tile_valid = (tiles < n_valid).astype(jnp.int32)
    first_row = jnp.minimum(tiles, n_valid - 1) * tm
    owner = jnp.sum((ends[None, :] <= first_row[:, None]).astype(jnp.int32), axis=1)
    tile_expert = jnp.minimum(owner, N_EXPERTS - 1).astype(jnp.int32)
    flat = pos.reshape(-1)
    token = jnp.repeat(jnp.arange(m, dtype=jnp.int32), 2)
    src = jnp.zeros((n_tiles * tm,), jnp.int32).at[flat].set(token, unique_indices=True)
    return pos, src, tile_expert, tile_valid


def _row_copy(src_hbm, row, dst_vmem, dst_row, sem):
    return pltpu.make_async_copy(src_hbm.at[pl.ds(row, 1), :], dst_vmem.at[pl.ds(dst_row, 1), :], sem)


def _moe_group_kernel(te_ref, tv_ref, src_ref, src_next_ref, x_hbm, *refs):
    wg_refs, wu_refs, wd_refs = (refs[0:W_BANDS], refs[W_BANDS:2 * W_BANDS],
                                 refs[2 * W_BANDS:3 * W_BANDS])
    o_ref, xg_s, xb_s, sem = refs[3 * W_BANDS:]
    i, f = pl.program_id(0), pl.program_id(1)
    nf = pl.num_programs(1)
    tm = xb_s.shape[0]
    chunk = tm // (MOE_F_STEPS - 1)
    slot = i % 2
    valid = tv_ref[i] == 1

    @pl.when((i == 0) & (f == 0))
    def _():
        def body(r, c):
            _row_copy(x_hbm, src_ref[0, r], xg_s.at[0], r, sem.at[0]).start()
            return c
        lax.fori_loop(0, tm, body, 0, unroll=8)

    def wait_rows():
        pltpu.make_async_copy(xg_s.at[slot], xg_s.at[slot], sem.at[slot]).wait()

    @pl.when((f == 0) & jnp.logical_not(valid))
    def _():
        o_ref[...] = jnp.zeros_like(o_ref)

        @pl.when((i == 0) | (tv_ref[jnp.maximum(i - 1, 0)] == 1))
        def _():
            wait_rows()

    def swiglu_step(first, issue_next):
        if first:
            wait_rows()
            xb = xg_s[slot].astype(BF16)
            xb_s[...] = xb
        else:
            xb = xb_s[...]
        if issue_next:
            for r in range(chunk):
                row = f * chunk + r
                _row_copy(x_hbm, src_next_ref[0, row], xg_s.at[1 - slot], row, sem.at[1 - slot]).start()
        down = _swiglu_partial(xb, wg_refs, wu_refs, wd_refs)
        if first:
            o_ref[...] = down
        else:
            o_ref[...] += down

    @pl.when(valid & (f == 0))
    def _():
        swiglu_step(True, True)

    @pl.when(valid & (f > 0) & (f < nf - 1))
    def _():
        swiglu_step(False, True)

    @pl.when(valid & (f == nf - 1))
    def _():
        swiglu_step(False, False)


def _moe_group(x, src, tile_expert, tile_valid, wg, wu, wd, tm, n_tiles):
    tf = D_FF // MOE_F_STEPS
    nf = MOE_F_STEPS

    def w_tile(i, f, te, tv):
        return (te[i], jnp.where(tv[i] == 1, f, nf - 1))

    grid_spec = pltpu.PrefetchScalarGridSpec(
        num_scalar_prefetch=2,
        grid=(n_tiles, nf),
        in_specs=[pl.BlockSpec((None, 1, tm), lambda i, f, te, tv: (i, 0, 0), memory_space=pltpu.SMEM),
                  pl.BlockSpec((None, 1, tm), lambda i, f, te, tv: (jnp.minimum(i + 1, n_tiles - 1), 0, 0),
                               memory_space=pltpu.SMEM),
                  pl.BlockSpec(memory_space=pl.ANY)]
                 + _band_specs((None,), tf, w_tile, w_tile),
        out_specs=pl.BlockSpec((tm, D_MODEL), lambda i, f, te, tv: (i, 0)),
        scratch_shapes=[pltpu.VMEM((2, tm, D_MODEL), F32),
                        pltpu.VMEM((tm, D_MODEL), BF16),
                        pltpu.SemaphoreType.DMA((2,))],
    )
    src3 = src.reshape(n_tiles, 1, tm)
    return pl.pallas_call(
        _moe_group_kernel,
        grid_spec=grid_spec,
        out_shape=jax.ShapeDtypeStruct((n_tiles * tm, D_MODEL), F32),
        compiler_params=pltpu.CompilerParams(dimension_semantics=("arbitrary", "arbitrary"),
                                             vmem_limit_bytes=MOE_VMEM_LIMIT),
        name="moe_group",
    )(tile_expert, tile_valid, src3, src3, x, *([wg] * W_BANDS), *([wu] * W_BANDS), *([wd] * W_BANDS))

def _combine_kernel(pos_ref, pos_next_ref, x_ref, p_ref, y_hbm, g_ref, b_ref, op_ref, os_ref, ybuf, sem,
                    *, prompt_tiles):
    i, nt = pl.program_id(0), pl.num_programs(0)
    tm = x_ref.shape[0]
    slot = i % 2

    def start_pair(idx_ref, s, r):
        _row_copy(y_hbm, idx_ref[0, 2 * r], ybuf.at[s, 0], r, sem.at[s]).start()
        _row_copy(y_hbm, idx_ref[0, 2 * r + 1], ybuf.at[s, 1], r, sem.at[s]).start()

    @pl.when(i == 0)
    def _():
        def body(r, c):
            start_pair(pos_ref, 0, r)
            return c
        lax.fori_loop(0, tm, body, 0, unroll=4)

    pltpu.make_async_copy(ybuf.at[slot], ybuf.at[slot], sem.at[slot]).wait()

    def norm_tile(issue_next):
        if issue_next:
            for r in range(tm):
                start_pair(pos_next_ref, 1 - slot, r)
        moe = p_ref[:, 0:1] * ybuf[slot, 0] + p_ref[:, 1:2] * ybuf[slot, 1]
        y = _layer_norm(ALPHA * x_ref[...] + moe, g_ref[...], b_ref[...])

        @pl.when(i < prompt_tiles)
        def _():
            op_ref[...] = y

        @pl.when(i >= prompt_tiles)
        def _():
            os_ref[...] = y

    @pl.when(i + 1 < nt)
    def _():
        norm_tile(True)

    @pl.when(i + 1 >= nt)
    def _():
        norm_tile(False)


def _combine_ln(x, prob, y_sorted, pos, g, b, mp):
    m = x.shape[0]
    ms = m - mp
    tm = _tile(np.gcd(mp, ms), 256)
    nt, npt = m // tm, mp // tm
    pos3 = pos.reshape(nt, 1, 2 * tm)
    return pl.pallas_call(
        functools.partial(_combine_kernel, prompt_tiles=npt),
        grid=(nt,),
        in_specs=[pl.BlockSpec((None, 1, 2 * tm), lambda i: (i, 0, 0), memory_space=pltpu.SMEM),
                  pl.BlockSpec((None, 1, 2 * tm), lambda i: (jnp.minimum(i + 1, nt - 1), 0, 0),
                               memory_space=pltpu.SMEM),
                  pl.BlockSpec((tm, D_MODEL), lambda i: (i, 0)),
                  pl.BlockSpec((tm, 2), lambda i: (i, 0)),
                  pl.BlockSpec(memory_space=pl.ANY),
                  pl.BlockSpec((1, D_MODEL), lambda i: (0, 0)),
                  pl.BlockSpec((1, D_MODEL), lambda i: (0, 0))],
        out_specs=[pl.BlockSpec((tm, D_MODEL), lambda i: (jnp.minimum(i, npt - 1), 0)),
                   pl.BlockSpec((tm, D_MODEL), lambda i: (jnp.maximum(i - npt, 0), 0))],
        out_shape=[jax.ShapeDtypeStruct((mp, D_MODEL), F32),
                   jax.ShapeDtypeStruct((ms, D_MODEL), F32)],
        scratch_shapes=[pltpu.VMEM((2, 2, tm, D_MODEL), F32), pltpu.SemaphoreType.DMA((2,))],
        compiler_params=_params("arbitrary"),
        name="combine_ln",
    )(pos3, pos3, x, prob, y_sorted, g, b)


def _moe_ln(x, mp, router_w, router_b, wg, wu, wd, g, b):
    m = x.shape[0]
    tm = MOE_ROWS
    n_tiles = -(-2 * m // tm) + N_EXPERTS
    idx, prob, rank, counts = _router(x, router_w, router_b)
    pos, src, tile_expert, tile_valid = _route_tables(idx, rank, counts[0], tm, n_tiles)
    y_sorted = _moe_group(x, src, tile_expert, tile_valid, wg, wu, wd, tm, n_tiles)
    return _combine_ln(x, prob, y_sorted, pos, g, b, mp)


def kernel(x_prompt, x_sample, state_conv_a, cache_swa_k, cache_swa_v, state_conv_c, w_in, b_gate, conv_a_w, conv_a_b, ln_a_g, ln_a_b, w_branch_a, sinks, w_branch_b, conv_c_w, w_branch_c, ln_d_g, ln_d_b, sgu_w, sgu_b, w_branch_d, w_out, ln1_g, ln1_b, ffn_w_gate, ffn_w_up, ffn_w_down, router_w, router_b, exp_w_gate, exp_w_up, exp_w_down, ln2_g, ln2_b):
    nb, t_len, d = x_prompt.shape
    nseq, s_len, _ = x_sample.shape
    depth = w_in.shape[0]
    mp, ms = nb * t_len, nseq * s_len
    w = WINDOW

    x = (x_prompt.reshape(mp, d), x_sample.reshape(ms, d))
    xb = _stack_cast(*x)

    cos_p, sin_p = _rope_tables(jnp.arange(t_len))
    cos_s, sin_s = _rope_tables(PAST_LEN + jnp.arange(s_len))

    def row(v):
        return v.reshape(1, -1)

    outs = {k: [] for k in ("pa", "pk", "pv", "pc", "sa", "sk", "sv", "sc", "sd")}
    for l in range(depth):
        w_in_b = w_in[l].astype(BF16)
        h = _in_proj(xb, w_in_b)

        wa, ba = conv_a_w[l], row(conv_a_b[l])
        lag, lab = row(ln_a_g[l]), row(ln_a_b[l])
        ya_p, yc_p, pa, pc = _ac_prompt(h, nb, t_len, wa, ba, lag, lab, conv_c_w[l])
        ya_s, yc_s, sa, sc = _ac_sample(h, mp, nseq, s_len, state_conv_a[l], state_conv_c[l],
                                        wa, ba, lag, lab, conv_c_w[l])

        yb_p, pk, pv = _attn_prompt(h, nb, t_len, sinks[l], cos_p, sin_p)
        yb_s, sk, sv = _attn_sample(h, mp, nseq, s_len, sinks[l],
                                    cache_swa_k[l].reshape(nseq, w, KV_WIDTH),
                                    cache_swa_v[l].reshape(nseq, w, KV_WIDTH), cos_s, sin_s)

        ldg, ldb = row(ln_d_g[l]), row(ln_d_b[l])
        yd_p = _d_prompt(h, mp, ldg, ldb, sgu_w[l], jnp.transpose(sgu_b[l]))
        sgu_wt = jnp.repeat(jnp.transpose(sgu_w[l][:, :s_len, :s_len], (2, 1, 0)), GROUP_D, axis=2)
        sgu_bs = jnp.repeat(jnp.transpose(sgu_b[l][:, :s_len]), GROUP_D, axis=1)
        yd_s, sd = _d_sample(h, mp, nseq, s_len, ldg, ldb, sgu_wt, sgu_bs)

        merged = _merge(xb, (ya_p, yb_p, yc_p, yd_p), (ya_s, yb_s, yc_s, yd_s), w_in_b, b_gate[l],
                        (w_branch_a[l].astype(BF16), w_branch_b[l].astype(BF16),
                         w_branch_c[l].astype(BF16), w_branch_d[l].astype(BF16)))
        x, xb = _out_ln(merged, x, w_out[l].astype(BF16), row(ln1_g[l]), row(ln1_b[l]))

        i = l // 2
        if l % 2 == 0:
            x, xb = _ffn_ln(x, xb, ffn_w_gate[i].astype(BF16), ffn_w_up[i].astype(BF16),
                            ffn_w_down[i].astype(BF16), row(ln2_g[l]), row(ln2_b[l]))
            x_p, x_s = x[:mp], x[mp:]
        else:
            x_p, x_s = _moe_ln(x, mp, router_w[i], row(router_b[i]), exp_w_gate[i].astype(BF16),
                               exp_w_up[i].astype(BF16), exp_w_down[i].astype(BF16),
                               row(ln2_g[l]), row(ln2_b[l]))
            if l + 1 < depth:
                x = jnp.concatenate([x_p, x_s], axis=0)
                xb = x.astype(BF16)

        outs["pa"].append(pa)
        outs["pk"].append(pk.reshape(nb, w, N_KV, HEAD_DIM))
        outs["pv"].append(pv.reshape(nb, w, N_KV, HEAD_DIM))
        outs["pc"].append(pc)
        outs["sa"].append(sa)
        outs["sk"].append(sk.reshape(nseq, w, N_KV, HEAD_DIM))
        outs["sv"].append(sv.reshape(nseq, w, N_KV, HEAD_DIM))
        outs["sc"].append(sc)
        outs["sd"].append(sd)

    y_prompt = x_p.reshape(nb, t_len, d)
    y_sample = x_s.reshape(nseq, s_len, d)
    st = {k: jnp.stack(v) for k, v in outs.items()}
    return (y_prompt, y_sample, st["pa"], st["pk"], st["pv"], st["pc"],
            st["sa"], st["sk"], st["sv"], st["sc"], st["sd"])
```
